```python
import math
import jax
import jax.numpy as jnp
from jax import lax
import numpy as np

D_MODEL = 2048
BATCH = 4
SEQ = 2048
DEPTH = 2

HEAD_DIM = 128
N_HEADS = D_MODEL // 256
BRANCH_WIDTH = N_HEADS * HEAD_DIM
N_BRANCHES = 3
MLSTM_CHUNK = 64
CONV_WIDTH = 4
IDX_HEADS = 16
IDX_DIM = 64
TOPK_MAX = 256
N_BUCKETS = 32
MAX_DISTANCE = 128
Q_BLOCK = 128
N_GROUPS = 4
EXPERTS_PER_GROUP = 8
N_EXPERTS = N_GROUPS * EXPERTS_PER_GROUP
TOP_K_INNER = 2
D_EXPERT = D_MODEL // 4
EPS = 1e-6
N_IN = (4 * BRANCH_WIDTH + 2 * N_HEADS
        + 3 * BRANCH_WIDTH + IDX_HEADS * IDX_DIM + IDX_DIM + IDX_HEADS
        + 3 * BRANCH_WIDTH + N_BRANCHES * D_MODEL)

kernel_name = "hybrid_mlstm_dsa_stickbreak_hmoe"


def _in_sizes():
    bw = BRANCH_WIDTH
    return (bw, bw, bw, bw, N_HEADS, N_HEADS,
            bw, bw, bw, IDX_HEADS * IDX_DIM, IDX_DIM, IDX_HEADS,
            bw, bw, bw,
            N_BRANCHES * D_MODEL)


def rms_norm(x, g):
    xf = x.astype(jnp.float32)
    return xf * lax.rsqrt(jnp.mean(xf * xf, axis=-1, keepdims=True) + EPS) * g.astype(jnp.float32)


def causal_dwconv(x, w):
    c = x.shape[-1]
    return lax.conv_general_dilated(x, w[:, None, :].astype(x.dtype), window_strides=(1,),
                                    padding=[(CONV_WIDTH - 1, 0)],
                                    dimension_numbers=('NWC', 'WIO', 'NWC'),
                                    feature_group_count=c)


def t5_bucket(rel):
    n = jnp.maximum(rel, 0)
    max_exact = N_BUCKETS // 2
    nf = jnp.maximum(n, 1).astype(jnp.float32)
    large = max_exact + (jnp.log(nf / max_exact) / math.log(MAX_DISTANCE / max_exact)
                         * (N_BUCKETS - max_exact)).astype(jnp.int32)
    large = jnp.minimum(large, N_BUCKETS - 1)
    return jnp.where(n < max_exact, n, large)


def mlstm(q, k, v, i_pre, f_pre):
    B, S, H, d = q.shape
    L = MLSTM_CHUNK
    nc = S // L
    f32 = jnp.float32

    def chunks(t):
        t = t.astype(f32).reshape((B, nc, L, H) + t.shape[3:])
        return jnp.moveaxis(jnp.moveaxis(t, 1, 0), 3, 2)

    qc = chunks(q) * d ** -0.5
    kc, vc = chunks(k), chunks(v)
    ic = chunks(i_pre)
    lfc = jax.nn.log_sigmoid(chunks(f_pre))
    causal = jnp.tril(jnp.ones((L, L), dtype=bool))

    def step(carry, inp):
        C, n, m = carry
        qt, kt, vt, it, lft = inp
        b = jnp.cumsum(lft, axis=-1)
        dlog = jnp.where(causal, b[..., :, None] - b[..., None, :] + it[..., None, :], -jnp.inf)
        inter = b + m[..., None]
        m_t = jnp.maximum(inter, jnp.max(dlog, axis=-1))
        w_intra = jnp.exp(dlog - m_t[..., None])
        w_inter = jnp.exp(inter - m_t)
        s = jnp.einsum('bhtk,bhsk->bhts', qt, kt) * w_intra
        num = (w_inter[..., None] * jnp.einsum('bhvk,bhtk->bhtv', C, qt)
               + jnp.einsum('bhts,bhsv->bhtv', s, vt))
        den = w_inter * jnp.einsum('bhk,bhtk->bht', n, qt) + jnp.sum(s, axis=-1)
        h = num / jnp.maximum(jnp.abs(den), jnp.exp(-m_t))[..., None]
        b_end = b[..., -1]
        wlog = b_end[..., None] - b + it
        m_new = jnp.maximum(b_end + m, jnp.max(wlog, axis=-1))
        decay = jnp.exp(b_end + m - m_new)
        ws = jnp.exp(wlog - m_new[..., None])
        C_new = decay[..., None, None] * C + jnp.einsum('bhsv,bhsk->bhvk', vt * ws[..., None], kt)
        n_new = decay[..., None] * n + jnp.einsum('bhs,bhsk->bhk', ws, kt)
        return (C_new, n_new, m_new), h

    init = (jnp.zeros((B, H, d, d), f32), jnp.zeros((B, H, d), f32), jnp.zeros((B, H), f32))
    _, h = lax.scan(step, init, (qc, kc, vc, ic, lfc))
    return jnp.moveaxis(h, 0, 1).swapaxes(2, 3).reshape(B, S, H, d)


def dsa_attention(q, k, v, q_idx, k_idx, w_idx, rel_bias):
    B, S, H, d = q.shape
    topk = min(TOPK_MAX, S // 4)
    key_pos = jnp.arange(S)

    def block(start):
        qb = lax.dynamic_slice_in_dim(q, start, Q_BLOCK, axis=1)
        qib = lax.dynamic_slice_in_dim(q_idx, start, Q_BLOCK, axis=1)
        wib = lax.dynamic_slice_in_dim(w_idx, start, Q_BLOCK, axis=1)
        q_pos = start + jnp.arange(Q_BLOCK)
        r = jax.nn.relu(jnp.einsum('bthe,bse->bths', qib, k_idx) * IDX_DIM ** -0.5)
        score = jnp.einsum('bth,bths->bts', wib * IDX_HEADS ** -0.5, r).astype(jnp.float32)
        score = jnp.where(key_pos[None, :] <= q_pos[:, None], score, -jnp.inf)
        _, sel = lax.top_k(score, topk)
        valid = sel <= q_pos[None, :, None]
        k_sel = jax.vmap(lambda kk, ii: kk[ii])(k, sel)
        v_sel = jax.vmap(lambda vv, ii: vv[ii])(v, sel)
        logits = jnp.einsum('bthd,btkhd->bhtk', qb, k_sel).astype(jnp.float32) * d ** -0.5
        bias = rel_bias[t5_bucket(q_pos[None, :, None] - sel)]
        logits = logits + jnp.moveaxis(bias, 3, 1).astype(jnp.float32)
        logits = jnp.where(valid[:, None], logits, -jnp.inf)
        p = jax.nn.softmax(logits, axis=-1)
        return jnp.einsum('bhtk,btkhd->bthd', p, v_sel)

    out = lax.map(block, jnp.arange(S // Q_BLOCK) * Q_BLOCK)
    return jnp.moveaxis(out, 0, 1).reshape(B, S, H, d)


def stick_breaking(q, k, v):
    B, S, H, d = q.shape
    key_pos = jnp.arange(S)

    def block(start):
        qb = lax.dynamic_slice_in_dim(q, start, Q_BLOCK, axis=1)
        q_pos = start + jnp.arange(Q_BLOCK)
        z = jnp.einsum('bthd,bshd->bhts', qb, k).astype(jnp.float32) * d ** -0.5
        strict = key_pos[None, :] < q_pos[:, None]
        log_not = jnp.where(strict, jax.nn.log_sigmoid(-z), 0.0)
        suffix = lax.cumsum(log_not, axis=3, reverse=True) - log_not
        a = jnp.where(strict, jnp.exp(jax.nn.log_sigmoid(z) + suffix), 0.0)
        return jnp.einsum('bhts,bshd->bthd', a, v)

    out = lax.map(block, jnp.arange(S // Q_BLOCK) * Q_BLOCK)
    return jnp.moveaxis(out, 0, 1).reshape(B, S, H, d)


def token_mixer(xn, w_in, conv_w, b_i, b_f, mlstm_norm_g, q_norm_g, k_norm_g, w_branch, w_out, rel_bias):
    B, S, D = xn.shape
    proj = xn @ w_in
    offs = np.cumsum(_in_sizes())[:-1].tolist()
    (mq, mk, mv, mo, mi, mf, dq, dk, dv, iq, ik, iw, sq, sk, sv, gpre) = jnp.split(proj, offs, axis=-1)

    def heads(t):
        return t.reshape(B, S, N_HEADS, HEAD_DIM)

    qk = jax.nn.silu(causal_dwconv(jnp.concatenate([mq, mk], axis=-1), conv_w))
    mq, mk = jnp.split(qk, 2, axis=-1)
    h_m = mlstm(heads(mq), heads(mk), heads(mv), mi + b_i, mf + b_f)
    h_m = rms_norm(h_m, mlstm_norm_g.reshape(N_HEADS, HEAD_DIM)) * jax.nn.sigmoid(heads(mo))
    h_d = dsa_attention(rms_norm(heads(dq), q_norm_g), rms_norm(heads(dk), k_norm_g), heads(dv),
                        iq.reshape(B, S, IDX_HEADS, IDX_DIM), ik, iw, rel_bias)
    h_s = stick_breaking(heads(sq), heads(sk), heads(sv))

    branches = jnp.stack([h_m.reshape(B, S, BRANCH_WIDTH),
                          h_d.reshape(B, S, BRANCH_WIDTH).astype(jnp.float32),
                          h_s.reshape(B, S, BRANCH_WIDTH).astype(jnp.float32)], axis=0)
    up = jnp.einsum('nbsc,ncd->nbsd', branches, w_branch)
    gates = jax.nn.sigmoid(gpre.reshape(B, S, N_BRANCHES, D))
    merged = jnp.einsum('bsnd,nbsd->bsd', gates, up)
    return merged @ w_out


def hier_moe(xn, w_rg, b_rg, w_re, b_re, w_gate, w_up, w_down):
    B, S, D = xn.shape
    n_tok = B * S
    xf = xn.reshape(n_tok, D)
    tok = jnp.arange(n_tok)
    g_logits = (xf @ w_rg).astype(jnp.float32)
    g_sel = jnp.argmax(g_logits + b_rg, axis=-1)
    p_group = jax.nn.softmax(g_logits, axis=-1)[tok, g_sel][:, None]
    e_logits = (xf @ w_re).astype(jnp.float32).reshape(n_tok, N_GROUPS, EXPERTS_PER_GROUP)
    e_in = e_logits[tok, g_sel]
    _, top_e = lax.top_k(e_in + b_re[g_sel], TOP_K_INNER)
    w_top = jnp.take_along_axis(jax.nn.softmax(e_in, axis=-1), top_e, axis=-1)
    w_top = p_group * w_top / jnp.sum(w_top, axis=-1, keepdims=True)
    expert_id = g_sel[:, None] * EXPERTS_PER_GROUP + top_e
    comb = jnp.sum(jax.nn.one_hot(expert_id, N_EXPERTS, dtype=jnp.float32) * w_top[..., None], axis=1)
    h = jax.nn.silu(jnp.einsum('nd,edf->nef', xf, w_gate)) * jnp.einsum('nd,edf->nef', xf, w_up)
    y = jnp.einsum('nef,efd->nd', h * comb[..., None], w_down)
    return y.reshape(B, S, D)


def setup_inputs(seed: int = 0) -> dict:
    key = jax.random.key(seed)
    ks = jax.random.split(key, 20)
    f32 = jnp.float32
    bw = BRANCH_WIDTH

    def nrm(k, shape, scale):
        return jax.random.normal(k, shape, f32) * scale

    return {
        "x": nrm(ks[0], (BATCH, SEQ, D_MODEL), 1.0),
        "norm1_g": 1.0 + nrm(ks[1], (DEPTH, D_MODEL), 0.02),
        "w_in": nrm(ks[2], (DEPTH, D_MODEL, N_IN), D_MODEL ** -0.5),
        "conv_w": nrm(ks[3], (DEPTH, CONV_WIDTH, 2 * bw), CONV_WIDTH ** -0.5),
        "b_i": nrm(ks[4], (DEPTH, N_HEADS), 0.1),
        "b_f": jnp.linspace(3.0, 6.0, N_HEADS, dtype=f32)[None, :] + nrm(ks[5], (DEPTH, N_HEADS), 0.1),
        "mlstm_norm_g": 1.0 + nrm(ks[6], (DEPTH, bw), 0.02),
        "q_norm_g": 1.0 + nrm(ks[7], (DEPTH, HEAD_DIM), 0.02),
        "k_norm_g": 1.0 + nrm(ks[8], (DEPTH, HEAD_DIM), 0.02),
        "w_branch": nrm(ks[9], (DEPTH, N_BRANCHES, bw, D_MODEL), bw ** -0.5),
        "w_out": nrm(ks[10], (DEPTH, D_MODEL, D_MODEL), D_MODEL ** -0.5),
        "norm2_g": 1.0 + nrm(ks[11], (DEPTH, D_MODEL), 0.02),
        "w_router_g": nrm(ks[12], (DEPTH, D_MODEL, N_GROUPS), D_MODEL ** -0.5),
        "b_router_g": nrm(ks[13], (DEPTH, N_GROUPS), 0.01),
        "w_router_e": nrm(ks[14], (DEPTH, D_MODEL, N_EXPERTS), D_MODEL ** -0.5),
        "b_router_e": nrm(ks[15], (DEPTH, N_GROUPS, EXPERTS_PER_GROUP), 0.01),
        "w_gate": nrm(ks[16], (DEPTH, N_EXPERTS, D_MODEL, D_EXPERT), D_MODEL ** -0.5),
        "w_up": nrm(ks[17], (DEPTH, N_EXPERTS, D_MODEL, D_EXPERT), D_MODEL ** -0.5),
        "w_down": nrm(ks[18], (DEPTH, N_EXPERTS, D_EXPERT, D_MODEL), D_EXPERT ** -0.5),
        "rel_bias": nrm(ks[19], (N_BUCKETS, N_HEADS), 0.5),
    }


def reference(x, norm1_g, w_in, conv_w, b_i, b_f, mlstm_norm_g, q_norm_g, k_norm_g, w_branch, w_out,
              norm2_g, w_router_g, b_router_g, w_router_e, b_router_e, w_gate, w_up, w_down, rel_bias):
    for l in range(DEPTH):
        xn = rms_norm(x, norm1_g[l])
        mix = token_mixer(xn, w_in[l], conv_w[l], b_i[l], b_f[l], mlstm_norm_g[l], q_norm_g[l],
                          k_norm_g[l], w_branch[l], w_out[l], rel_bias)
        x = x + mix.astype(x.dtype)
        xn = rms_norm(x, norm2_g[l])
        ffn = hier_moe(xn, w_router_g[l], b_router_g[l], w_router_e[l], b_router_e[l],
                       w_gate[l], w_up[l], w_down[l])
        x = x + ffn.astype(x.dtype)
    return x
```

```python
import functools
import math

import numpy as np
import jax
import jax.numpy as jnp
from jax import lax
from jax.experimental import pallas as pl
from jax.experimental.pallas import tpu as pltpu

F32 = jnp.float32
BF16 = jnp.bfloat16
I32 = jnp.int32

LANES = 128
HEAD_DIM = 128
N_HEADS = 8
BRANCH_WIDTH = N_HEADS * HEAD_DIM
N_BRANCHES = 3
CONV_WIDTH = 4
IDX_HEADS = 16
IDX_DIM = 64
TOPK_MAX = 256
N_BUCKETS = 32
MAX_DISTANCE = 128
N_GROUPS = 4
EXPERTS_PER_GROUP = 8
N_EXPERTS = N_GROUPS * EXPERTS_PER_GROUP
EPS = 1e-6
NEG = -1e30
INT_MIN = -(2 ** 31)

MLSTM_CHUNK = 128
DSA_BLOCK = 128
SB_BLOCK = 256
MOE_TILE = 256
VMEM_LIMIT = 56 * 1024 * 1024

SL_MQ, SL_MK, SL_MV, SL_MO = 0, 8, 16, 24
SL_DQ, SL_DK, SL_DV, SL_IQ = 32, 40, 48, 56
SL_SQ, SL_SK, SL_SV, SL_GATE = 64, 72, 80, 88
N_MAIN_SLABS = 88 + 48


def _params(sem):
    return pltpu.CompilerParams(dimension_semantics=sem, vmem_limit_bytes=VMEM_LIMIT)


def _dot(a, b):
    return jnp.dot(a, b, preferred_element_type=F32)


def _dot_nt(a, b):
    return lax.dot_general(a, b, (((1,), (1,)), ((), ())), preferred_element_type=F32)


def _softplus(z):
    return jnp.maximum(z, 0.0) + jnp.log1p(jnp.exp(-jnp.abs(z)))


def _sigmoid(z):
    return 1.0 / (1.0 + jnp.exp(-z))


def _rmsnorm_kernel(x_ref, g_ref, o_ref):
    x = x_ref[...]
    ms = jnp.mean(x * x, axis=-1, keepdims=True)
    o_ref[...] = (x * lax.rsqrt(ms + EPS) * g_ref[...]).astype(o_ref.dtype)


def _rmsnorm(x2d, g, tm=512):
    n, d = x2d.shape
    return pl.pallas_call(
        _rmsnorm_kernel,
        grid=(n // tm,),
        in_specs=[pl.BlockSpec((tm, d), lambda i: (i, 0)), pl.BlockSpec((1, d), lambda i: (0, 0))],
        out_specs=pl.BlockSpec((tm, d), lambda i: (i, 0)),
        out_shape=jax.ShapeDtypeStruct((n, d), BF16),
        compiler_params=_params(("parallel",)),
        name="rmsnorm",
    )(x2d, g.reshape(1, d))


def _mm_slab_kernel(a_ref, w_ref, o_ref):
    acc = _dot(a_ref[...], w_ref[...])
    for j in range(o_ref.shape[0]):
        o_ref[j] = acc[:, j * LANES:(j + 1) * LANES].astype(o_ref.dtype)


def _matmul_slabs(a, w, out_dtype, tm, tn):
    m, k = a.shape
    n = w.shape[1]
    return pl.pallas_call(
        _mm_slab_kernel,
        grid=(m // tm, n // tn),
        in_specs=[pl.BlockSpec((tm, k), lambda i, j: (i, 0)), pl.BlockSpec((k, tn), lambda i, j: (0, j))],
        out_specs=pl.BlockSpec((tn // LANES, tm, LANES), lambda i, j: (j, i, 0)),
        out_shape=jax.ShapeDtypeStruct((n // LANES, m, LANES), out_dtype),
        compiler_params=_params(("parallel", "parallel")),
        name="in_proj",
    )(a, w)


def _mlstm_kernel(bias_ref, q_ref, k_ref, v_ref, o_ref, g_ref, cwq_ref, cwk_ref, ng_ref, out_ref,
                  qf, kf, qc, kc, st):
    h = pl.program_id(1)
    S = q_ref.shape[0]
    L = MLSTM_CHUNK
    nc = S // L
    d = HEAD_DIM
    PAD = 8

    qf[0:PAD, :] = jnp.zeros((PAD, d), F32)
    kf[0:PAD, :] = jnp.zeros((PAD, d), F32)
    qf[PAD:PAD + S, :] = q_ref[...].astype(F32)
    kf[PAD:PAD + S, :] = k_ref[...].astype(F32)
    R = min(S, 256)
    for r0 in range(0, S, R):
        aq = jnp.zeros((R, d), F32)
        ak = jnp.zeros((R, d), F32)
        for t in range(CONV_WIDTH):
            off = PAD - (CONV_WIDTH - 1) + t + r0
            aq = aq + cwq_ref[t:t + 1, :] * qf[off:off + R, :]
            ak = ak + cwk_ref[t:t + 1, :] * kf[off:off + R, :]
        qc[r0:r0 + R, :] = (aq * _sigmoid(aq) * (d ** -0.5)).astype(BF16)
        kc[r0:r0 + R, :] = ak * _sigmoid(ak)

    st[...] = jnp.zeros(st.shape, F32)
    b_i = bias_ref[0, h]
    b_f = bias_ref[1, h]
    row = lax.broadcasted_iota(I32, (L, L), 0)
    col = lax.broadcasted_iota(I32, (L, L), 1)
    causal = col <= row
    eye = col == row
    lane = lax.broadcasted_iota(I32, (L, d), 1)
    ones_col = jnp.where(lane == 0, 1.0, 0.0).astype(BF16)
    ng = ng_ref[...]

    def chunk(c, m):
        r = pl.multiple_of(c * L, L)
        q = qc[pl.ds(r, L), :]
        kT = kc[pl.ds(r, L), :].T
        v = v_ref[pl.ds(r, L), :]
        vaug = jnp.concatenate([v, ones_col], axis=1)
        gates = g_ref[c]
        i_row = gates[0:1, :] + b_i
        f_row = gates[1:2, :] + b_f
        lf_row = jnp.minimum(f_row, 0.0) - jnp.log1p(jnp.exp(-jnp.abs(f_row)))
        b_col = jnp.sum(jnp.where(causal, lf_row, 0.0), axis=1, keepdims=True)
        b_row = jnp.sum(jnp.where(eye, b_col, 0.0), axis=0, keepdims=True)
        dlog = jnp.where(causal, b_col - b_row + i_row, NEG)
        inter = b_col + m
        m_t = jnp.maximum(inter, jnp.max(dlog, axis=1, keepdims=True))
        w_intra = jnp.exp(dlog - m_t)
        w_inter = jnp.exp(inter - m_t)
        s = _dot(q, kT.astype(BF16)) * w_intra
        res = w_inter * _dot(q, st[...].astype(BF16)) + _dot(s.astype(BF16), vaug)
        num = res[:, :d]
        den = res[:, d:d + 1]
        hh = num / jnp.maximum(jnp.abs(den), jnp.exp(-m_t))
        hn = hh * lax.rsqrt(jnp.mean(hh * hh, axis=1, keepdims=True) + EPS) * ng
        og = o_ref[pl.ds(r, L), :].astype(F32)
        out_ref[pl.ds(r, L), :] = (hn * _sigmoid(og)).astype(out_ref.dtype)
        ws_row = w_intra[L - 1:L, :]
        decay = w_inter[L - 1:L, :]
        st[...] = decay * st[...] + _dot((kT * ws_row).astype(BF16), vaug)
        return m_t[L - 1:L, :]

    lax.fori_loop(0, nc, chunk, jnp.zeros((1, 1), F32))


def _mlstm(proj, gates_t, bias_if, conv_w, norm_g, B, S):
    L = MLSTM_CHUNK
    nc = S // L
    d = HEAD_DIM
    H = N_HEADS

    def slab(off):
        return pl.BlockSpec((None, S, d), lambda b, h: (off + h, b, 0))

    return pl.pallas_call(
        _mlstm_kernel,
        grid=(B, H),
        in_specs=[
            pl.BlockSpec(memory_space=pltpu.SMEM),
            slab(SL_MQ), slab(SL_MK), slab(SL_MV), slab(SL_MO),
            pl.BlockSpec((None, None, nc, 2, L), lambda b, h: (b, h, 0, 0, 0)),
            pl.BlockSpec((CONV_WIDTH, d), lambda b, h: (0, h)),
            pl.BlockSpec((CONV_WIDTH, d), lambda b, h: (0, H + h)),
            pl.BlockSpec((1, d), lambda b, h: (0, h)),
        ],
        out_specs=pl.BlockSpec((None, S, d), lambda b, h: (h, b, 0)),
        out_shape=jax.ShapeDtypeStruct((H, B * S, d), BF16),
        scratch_shapes=[
            pltpu.VMEM((S + 8, d), F32), pltpu.VMEM((S + 8, d), F32),
            pltpu.VMEM((S, d), BF16), pltpu.VMEM((S, d), F32),
            pltpu.VMEM((d, 2 * d), F32),
        ],
        compiler_params=_params(("parallel", "parallel")),
        name="mlstm",
    )(bias_if, proj, proj, proj, proj, gates_t, conv_w, conv_w, norm_g.reshape(1, H * d))


def _t5_thresholds():
    max_exact = N_BUCKETS // 2
    n = np.arange(0, 2 * MAX_DISTANCE)
    nf = np.maximum(n, 1).astype(np.float64)
    val = np.log(nf / max_exact) / math.log(MAX_DISTANCE / max_exact) * (N_BUCKETS - max_exact)
    frac = np.abs(val - np.round(val))
    frac_ok = (frac > 1e-4) | (n <= max_exact) | (n >= MAX_DISTANCE)
    assert frac_ok.all()
    large = np.minimum(max_exact + np.trunc(val).astype(np.int64), N_BUCKETS - 1)
    bucket = np.where(n < max_exact, n, large)
    assert (np.diff(bucket) >= 0).all() and bucket[MAX_DISTANCE] == N_BUCKETS - 1
    return [int(np.argmax(bucket >= j)) for j in range(1, N_BUCKETS)]


_T5_THR = _t5_thresholds()


def _dsa_kernel(rb_ref, q_ref, k_ref, v_ref, iq_ref, ik_ref, wt_ref, gq_ref, gk_ref, out_ref,
                kn, ikk, qm, wb, key_ref, mask_ref, lg_ref, bias_ref, *, topk):
    qi = pl.program_id(1)
    T = DSA_BLOCK
    d = HEAD_DIM
    H = N_HEADS
    S = k_ref.shape[1]
    nkb = qi + 1
    row = lax.broadcasted_iota(I32, (T, T), 0)
    col = lax.broadcasted_iota(I32, (T, T), 1)

    @pl.when(qi == 0)
    def _per_batch():
        gk = gk_ref[...]

        def norm_k(h, _):
            kh = k_ref[h].astype(F32)
            kn[h] = (kh * lax.rsqrt(jnp.mean(kh * kh, axis=1, keepdims=True) + EPS) * gk).astype(BF16)
            for o in range(2):
                n = o * T + row - col
                val = jnp.full((T, T), rb_ref[0, h], F32)
                for j, thr in enumerate(_T5_THR):
                    val = jnp.where(n >= thr, rb_ref[j + 1, h], val)
                bias_ref[h, o] = val
            bias_ref[h, 2] = jnp.full((T, T), rb_ref[N_BUCKETS - 1, h], F32)
            return 0

        lax.fori_loop(0, H, norm_k, 0)
        ikk[...] = ik_ref[...].astype(BF16)

    lane = lax.broadcasted_iota(I32, (T, LANES), 1)
    wsc = wt_ref[...] * (IDX_HEADS ** -0.5 * IDX_DIM ** -0.5)
    for hp in range(IDX_HEADS // 2):
        qp = iq_ref[hp]
        zero = jnp.zeros_like(qp)
        qm[2 * hp] = jnp.where(lane < IDX_DIM, qp, zero)
        qm[2 * hp + 1] = jnp.where(lane >= IDX_DIM, qp, zero)
    for h16 in range(IDX_HEADS):
        wb[h16] = jnp.broadcast_to(wsc[:, h16:h16 + 1], (T, LANES))

    def score_block(j, _):
        kk = ikk[pl.ds(pl.multiple_of(j * T, T), T), :]
        sc = jnp.zeros((T, T), F32)
        for h16 in range(IDX_HEADS):
            sc = sc + wb[h16] * jnp.maximum(_dot_nt(qm[h16], kk), 0.0)
        bits = lax.bitcast_convert_type(sc, I32)
        key = jnp.where(bits < 0, bits ^ jnp.int32(0x7FFFFFFF), bits)
        key = jnp.where(sc == 0.0, 0, key)
        key = jnp.where(j * T + col <= qi * T + row, key, INT_MIN)
        key_ref[j] = key
        return 0

    lax.fori_loop(0, nkb, score_block, 0)

    def count_ge(cand):
        cb = jnp.broadcast_to(cand, (T, LANES))

        def body(j, acc):
            return acc + jnp.where(key_ref[j] >= cb, 1.0, 0.0)

        acc = lax.fori_loop(0, nkb, body, jnp.zeros((T, LANES), F32))
        return jnp.sum(acc, axis=1, keepdims=True)

    kf = float(topk)
    thr0 = jnp.where(count_ge(jnp.zeros((T, 1), I32)) >= kf, 0, INT_MIN).astype(I32)

    def bisect(it, thr):
        cand = thr | jnp.left_shift(jnp.int32(1), 30 - it)
        return jnp.where(count_ge(cand) >= kf, cand, thr)

    thr = lax.fori_loop(0, 31, bisect, thr0)
    thrb = jnp.broadcast_to(thr, (T, LANES))
    need = kf - _count_gt(key_ref, thrb, nkb, T)

    tri = jnp.where(row <= col, 1.0, 0.0).astype(BF16)

    def select_block(j, carry):
        key = key_ref[j]
        tie = key == thrb
        tief = jnp.where(tie, 1.0, 0.0)
        pre = _dot(tief.astype(BF16), tri) + carry
        m = jnp.where(key > thrb, 0.0, jnp.where(tie, jnp.where(pre <= need, 0.0, NEG), NEG))
        mask_ref[j] = jnp.where(key == INT_MIN, NEG, m)
        return carry + jnp.sum(tief, axis=1, keepdims=True)

    lax.fori_loop(0, nkb, select_block, jnp.zeros((T, 1), F32))

    gq = gq_ref[...]

    def head(h, _):
        qh = q_ref[h].astype(F32)
        qn = (qh * lax.rsqrt(jnp.mean(qh * qh, axis=1, keepdims=True) + EPS) * gq * (d ** -0.5)).astype(BF16)

        def logits(j, mx):
            kj = kn[h, pl.ds(pl.multiple_of(j * T, T), T), :]
            lg = _dot_nt(qn, kj) + bias_ref[h, jnp.minimum(qi - j, 2)] + mask_ref[j]
            lg_ref[j] = lg
            return jnp.maximum(mx, lg)

        mx = lax.fori_loop(0, nkb, logits, jnp.full((T, T), NEG, F32))
        rowmax = jnp.max(mx, axis=1, keepdims=True)

        def pv(j, carry):
            l, acc = carry
            p = jnp.exp(lg_ref[j] - rowmax)
            vj = v_ref[h, pl.ds(pl.multiple_of(j * T, T), T), :]
            return l + p, acc + _dot(p.astype(BF16), vj)

        l, acc = lax.fori_loop(0, nkb, pv, (jnp.zeros((T, T), F32), jnp.zeros((T, d), F32)))
        out_ref[h] = (acc / jnp.sum(l, axis=1, keepdims=True)).astype(out_ref.dtype)
        return 0

    lax.fori_loop(0, H, head, 0)


def _count_gt(key_ref, thrb, nkb, T):
    def body(j, acc):
        return acc + jnp.where(key_ref[j] > thrb, 1.0, 0.0)

    acc = lax.fori_loop(0, nkb, body, jnp.zeros((T, LANES), F32))
    return jnp.sum(acc, axis=1, keepdims=True)


def _dsa(proj, tail, q_norm_g, k_norm_g, rel_bias, B, S):
    T = DSA_BLOCK
    nq = S // T
    d = HEAD_DIM
    H = N_HEADS
    topk = min(TOPK_MAX, S // 4)
    kern = functools.partial(_dsa_kernel, topk=topk)
    return pl.pallas_call(
        kern,
        grid=(B, nq),
        in_specs=[
            pl.BlockSpec(memory_space=pltpu.SMEM),
            pl.BlockSpec((H, T, d), lambda b, i: (SL_DQ // H, b * nq + i, 0)),
            pl.BlockSpec((H, S, d), lambda b, i: (SL_DK // H, b, 0)),
            pl.BlockSpec((H, S, d), lambda b, i: (SL_DV // H, b, 0)),
            pl.BlockSpec((H, T, d), lambda b, i: (SL_IQ // H, b * nq + i, 0)),
            pl.BlockSpec((None, S, LANES), lambda b, i: (0, b, 0)),
            pl.BlockSpec((None, T, LANES), lambda b, i: (1, b * nq + i, 0)),
            pl.BlockSpec((1, d), lambda b, i: (0, 0)),
            pl.BlockSpec((1, d), lambda b, i: (0, 0)),
        ],
        out_specs=pl.BlockSpec((H, T, d), lambda b, i: (0, b * nq + i, 0)),
        out_shape=jax.ShapeDtypeStruct((H, B * S, d), BF16),
        scratch_shapes=[
            pltpu.VMEM((H, S, d), BF16),
            pltpu.VMEM((S, LANES), BF16),
            pltpu.VMEM((IDX_HEADS, T, LANES), BF16),
            pltpu.VMEM((IDX_HEADS, T, LANES), F32),
            pltpu.VMEM((nq, T, T), I32),
            pltpu.VMEM((nq, T, T), F32),
            pltpu.VMEM((nq, T, T), F32),
            pltpu.VMEM((H, 3, T, T), F32),
        ],
        compiler_params=_params(("parallel", "arbitrary")),
        name="dsa",
    )(rel_bias, proj, proj, proj, proj, tail, tail, q_norm_g.reshape(1, d), k_norm_g.reshape(1, d))


def _sb_kernel(q_ref, k_ref, v_ref, out_ref):
    qi = pl.program_id(2)
    T = SB_BLOCK
    d = HEAD_DIM
    row = lax.broadcasted_iota(I32, (T, T), 0)
    col = lax.broadcasted_iota(I32, (T, T), 1)
    upper = jnp.where(row > col, 1.0, 0.0).astype(BF16)
    q = (q_ref[...].astype(F32) * (d ** -0.5)).astype(BF16)

    def block(i, carry):
        run, acc = carry
        j = qi - i
        r = pl.multiple_of(j * T, T)
        z = _dot_nt(q, k_ref[pl.ds(r, T), :])
        sp = _softplus(z)
        strict = j * T + col < qi * T + row
        ln = jnp.where(strict, -sp, 0.0)
        hi = ln.astype(BF16)
        lo = (ln - hi.astype(F32)).astype(BF16)
        suf = _dot(hi, upper) + _dot(lo, upper)
        a = jnp.where(strict, jnp.exp(z - sp + suf + run), 0.0)
        acc = acc + _dot(a.astype(BF16), v_ref[pl.ds(r, T), :])
        return run + suf[:, 0:1] + ln[:, 0:1], acc

    _, acc = lax.fori_loop(0, qi + 1, block, (jnp.zeros((T, 1), F32), jnp.zeros((T, d), F32)))
    out_ref[...] = acc.astype(out_ref.dtype)


def _stick_breaking(proj, B, S):
    T = SB_BLOCK
    nq = S // T
    d = HEAD_DIM
    H = N_HEADS
    return pl.pallas_call(
        _sb_kernel,
        grid=(B, H, nq),
        in_specs=[
            pl.BlockSpec((None, T, d), lambda b, h, i: (SL_SQ + h, b * nq + i, 0)),
            pl.BlockSpec((None, S, d), lambda b, h, i: (SL_SK + h, b, 0)),
            pl.BlockSpec((None, S, d), lambda b, h, i: (SL_SV + h, b, 0)),
        ],
        out_specs=pl.BlockSpec((None, T, d), lambda b, h, i: (h, b * nq + i, 0)),
        out_shape=jax.ShapeDtypeStruct((H, B * S, d), BF16),
        compiler_params=_params(("parallel", "parallel", "parallel")),
        name="stick_breaking",
    )(proj, proj, proj)


def _merge_kernel(hm_ref, hd_ref, hs_ref, g0_ref, g1_ref, g2_ref, w_ref, o_ref):
    H = N_HEADS
    acc = None
    ns = o_ref.shape[1] // LANES
    for n, (br, gr) in enumerate(((hm_ref, g0_ref), (hd_ref, g1_ref), (hs_ref, g2_ref))):
        a = jnp.concatenate([br[h] for h in range(H)], axis=1)
        up = _dot(a, w_ref[n])
        gate = jnp.concatenate([gr[s] for s in range(ns)], axis=1).astype(F32)
        term = _sigmoid(gate) * up
        acc = term if acc is None else acc + term
    o_ref[...] = acc.astype(o_ref.dtype)


def _merge(hm, hd, hs, proj, w_branch, tm=512, tn=512):
    H, n, d = hm.shape
    D = w_branch.shape[2]
    ns = tn // LANES
    br = pl.BlockSpec((H, tm, d), lambda i, j: (0, i, 0))

    def gate(b):
        first = (SL_GATE + b * (D // LANES)) // ns
        return pl.BlockSpec((ns, tm, LANES), lambda i, j: (first + j, i, 0))

    return pl.pallas_call(
        _merge_kernel,
        grid=(n // tm, D // tn),
        in_specs=[br, br, br, gate(0), gate(1), gate(2),
                  pl.BlockSpec((N_BRANCHES, H * d, tn), lambda i, j: (0, 0, j))],
        out_specs=pl.BlockSpec((tm, tn), lambda i, j: (i, j)),
        out_shape=jax.ShapeDtypeStruct((n, D), BF16),
        compiler_params=_params(("parallel", "parallel")),
        name="branch_merge",
    )(hm, hd, hs, proj, proj, proj, w_branch)


def _mm_res_kernel(a_ref, w_ref, x_ref, o_ref):
    o_ref[...] = x_ref[...] + _dot(a_ref[...], w_ref[...])


def _matmul_residual(a, w, x, tm=512, tn=512):
    m, k = a.shape
    n = w.shape[1]
    return pl.pallas_call(
        _mm_res_kernel,
        grid=(m // tm, n // tn),
        in_specs=[pl.BlockSpec((tm, k), lambda i, j: (i, 0)), pl.BlockSpec((k, tn), lambda i, j: (0, j)),
                  pl.BlockSpec((tm, tn), lambda i, j: (i, j))],
        out_specs=pl.BlockSpec((tm, tn), lambda i, j: (i, j)),
        out_shape=jax.ShapeDtypeStruct((m, n), F32),
        compiler_params=_params(("parallel", "parallel")),
        name="out_proj",
    )(a, w, x)


def _router_kernel(x_ref, g_ref, wr_ref, br_ref, xn_ref, ids_ref, wts_ref, cnt_ref, carry):
    i = pl.program_id(0)
    tm, D = x_ref.shape
    nchunk = D // LANES

    @pl.when(i == 0)
    def _():
        carry[...] = jnp.zeros(carry.shape, F32)

    x = x_ref[...]
    xn = x * lax.rsqrt(jnp.mean(x * x, axis=-1, keepdims=True) + EPS) * g_ref[...]
    for c in range(nchunk):
        xn_ref[:, c, :] = xn[:, c * LANES:(c + 1) * LANES]
    logits = jnp.dot(xn, wr_ref[...], preferred_element_type=F32, precision=lax.Precision.HIGHEST)
    biased = logits + br_ref[...]
    lane = lax.broadcasted_iota(I32, (tm, LANES), 1)
    lanef = lane.astype(F32)
    big = float(LANES)

    def first_lane(mask):
        return jnp.min(jnp.where(mask, lanef, big), axis=1, keepdims=True)

    gmask = lane < N_GROUPS
    gmax = jnp.max(jnp.where(gmask, biased, NEG), axis=1, keepdims=True)
    g_sel = first_lane(gmask & (biased == gmax))
    gm = jnp.max(jnp.where(gmask, logits, NEG), axis=1, keepdims=True)
    ge = jnp.where(gmask, jnp.exp(logits - gm), 0.0)
    p_group = jnp.sum(jnp.where(lanef == g_sel, ge, 0.0), axis=1, keepdims=True) / jnp.sum(ge, axis=1, keepdims=True)
    lo = N_GROUPS + EXPERTS_PER_GROUP * g_sel
    emask = (lanef >= lo) & (lanef < lo + EXPERTS_PER_GROUP)
    eb = jnp.where(emask, biased, NEG)
    e1 = first_lane(emask & (eb == jnp.max(eb, axis=1, keepdims=True)))
    emask2 = emask & (lanef != e1)
    eb2 = jnp.where(emask2, biased, NEG)
    e2 = first_lane(emask2 & (eb2 == jnp.max(eb2, axis=1, keepdims=True)))
    em = jnp.max(jnp.where(emask, logits, NEG), axis=1, keepdims=True)
    ee = jnp.where(emask, jnp.exp(logits - em), 0.0)
    s1 = jnp.sum(jnp.where(lanef == e1, ee, 0.0), axis=1, keepdims=True)
    s2 = jnp.sum(jnp.where(lanef == e2, ee, 0.0), axis=1, keepdims=True)
    se = jnp.sum(ee, axis=1, keepdims=True)
    w1 = s1 / se
    w2 = s2 / se
    wsum = w1 + w2
    w1 = p_group * w1 / wsum
    w2 = p_group * w2 / wsum
    x1 = e1 - N_GROUPS
    x2 = e2 - N_GROUPS

    onehot = jnp.where((lanef == x1) | (lanef == x2), 1.0, 0.0)
    r = lax.broadcasted_iota(I32, (tm, tm), 0)
    c = lax.broadcasted_iota(I32, (tm, tm), 1)
    lower = jnp.where(c < r, 1.0, 0.0).astype(BF16)
    prefix = _dot(lower, onehot.astype(BF16)) + carry[...]
    r1 = jnp.sum(jnp.where(lanef == x1, prefix, 0.0), axis=1, keepdims=True)
    r2 = jnp.sum(jnp.where(lanef == x2, prefix, 0.0), axis=1, keepdims=True)
    carry[...] = carry[...] + jnp.sum(onehot, axis=0, keepdims=True)
    cnt_ref[...] = carry[...].astype(I32)

    idsf = jnp.where(lane == 0, x1, jnp.where(lane == 1, x2, jnp.where(lane == 2, r1, jnp.where(lane == 3, r2, 0.0))))
    ids_ref[...] = idsf.astype(I32)
    wts_ref[...] = jnp.where(lane == 0, w1, jnp.where(lane == 1, w2, 0.0))


def _router(x2d, g, w_router, b_router, tm=512):
    n, D = x2d.shape
    return pl.pallas_call(
        _router_kernel,
        grid=(n // tm,),
        in_specs=[pl.BlockSpec((tm, D), lambda i: (i, 0)), pl.BlockSpec((1, D), lambda i: (0, 0)),
                  pl.BlockSpec((D, LANES), lambda i: (0, 0)), pl.BlockSpec((1, LANES), lambda i: (0, 0))],
        out_specs=[pl.BlockSpec((tm, D // LANES, LANES), lambda i: (i, 0, 0)),
                   pl.BlockSpec((tm, LANES), lambda i: (i, 0)),
                   pl.BlockSpec((tm, LANES), lambda i: (i, 0)),
                   pl.BlockSpec((1, LANES), lambda i: (0, 0))],
        out_shape=[jax.ShapeDtypeStruct((n, D // LANES, LANES), F32),
                   jax.ShapeDtypeStruct((n, LANES), I32),
                   jax.ShapeDtypeStruct((n, LANES), F32),
                   jax.ShapeDtypeStruct((1, LANES), I32)],
        scratch_shapes=[pltpu.VMEM((1, LANES), F32)],
        compiler_params=_params(("arbitrary",)),
        name="moe_router",
    )(x2d, g.reshape(1, D), w_router, b_router)


def _dispatch_kernel(p1_ref, p2_ref, xn_ref, xs_in_ref, xs_ref, sem, *, tb):
    del xs_in_ref
    base = pl.program_id(0) * tb

    def copies(t):
        return (pltpu.make_async_copy(xn_ref.at[t], xs_ref.at[p1_ref[t]], sem),
                pltpu.make_async_copy(xn_ref.at[t], xs_ref.at[p2_ref[t]], sem))

    def issue(t, _):
        for cp in copies(base + t):
            cp.start()
        return 0

    lax.fori_loop(0, tb, issue, 0)

    def drain(t, _):
        for cp in copies(base + t):
            cp.wait()
        return 0

    lax.fori_loop(0, tb, drain, 0)


def _dispatch(pos1, pos2, xn3, n_rows, tb=1024):
    n, nchunk, _ = xn3.shape
    tb = min(tb, n)
    xs0 = jnp.zeros((n_rows, nchunk, LANES), F32)
    grid_spec = pltpu.PrefetchScalarGridSpec(
        num_scalar_prefetch=2,
        grid=(n // tb,),
        in_specs=[pl.BlockSpec(memory_space=pl.ANY), pl.BlockSpec(memory_space=pl.ANY)],
        out_specs=pl.BlockSpec(memory_space=pl.ANY),
        scratch_shapes=[pltpu.SemaphoreType.DMA(())],
    )
    return pl.pallas_call(
        functools.partial(_dispatch_kernel, tb=tb),
        grid_spec=grid_spec,
        out_shape=jax.ShapeDtypeStruct((n_rows, nchunk, LANES), F32),
        input_output_aliases={3: 0},
        compiler_params=pltpu.CompilerParams(dimension_semantics=("arbitrary",), has_side_effects=True),
        name="moe_dispatch",
    )(pos1, pos2, xn3, xs0)


def _expert_kernel(te_ref, nv_ref, xs_ref, wg_ref, wu_ref, wd_ref, ys_ref, wgb, wub, wdb):
    i = pl.program_id(0)
    nchunk = xs_ref.shape[1]
    prev = te_ref[jnp.maximum(i - 1, 0)]

    @pl.when(i < nv_ref[0])
    def _():
        @pl.when((i == 0) | (te_ref[i] != prev))
        def _():
            wgb[...] = wg_ref[...].astype(BF16)
            wub[...] = wu_ref[...].astype(BF16)
            wdb[...] = wd_ref[...].astype(BF16)

        x = jnp.concatenate([xs_ref[:, c, :] for c in range(nchunk)], axis=1).astype(BF16)
        g = _dot(x, wgb[...])
        u = _dot(x, wub[...])
        hcur = (g * _sigmoid(g) * u).astype(BF16)
        y = _dot(hcur, wdb[...])
        for c in range(nchunk):
            ys_ref[:, c, :] = y[:, c * LANES:(c + 1) * LANES]

    @pl.when(i >= nv_ref[0])
    def _():
        ys_ref[...] = jnp.zeros(ys_ref.shape, F32)


def _experts(tile_expert, n_valid, xs, w_gate, w_up, w_down):
    n_rows, nchunk, _ = xs.shape
    E, D, Fe = w_gate.shape
    tm = MOE_TILE
    n_tiles = n_rows // tm

    def row_map(i, te, nv):
        return (jnp.minimum(i, nv[0] - 1), 0, 0)

    def w_map(i, te, nv):
        return (te[i], 0, 0)

    grid_spec = pltpu.PrefetchScalarGridSpec(
        num_scalar_prefetch=2,
        grid=(n_tiles,),
        in_specs=[pl.BlockSpec((tm, nchunk, LANES), row_map),
                  pl.BlockSpec((None, D, Fe), w_map), pl.BlockSpec((None, D, Fe), w_map),
                  pl.BlockSpec((None, Fe, D), w_map)],
        out_specs=pl.BlockSpec((tm, nchunk, LANES), lambda i, te, nv: (i, 0, 0)),
        scratch_shapes=[pltpu.VMEM((D, Fe), BF16), pltpu.VMEM((D, Fe), BF16), pltpu.VMEM((Fe, D), BF16)],
    )
    return pl.pallas_call(
        _expert_kernel,
        grid_spec=grid_spec,
        out_shape=jax.ShapeDtypeStruct((n_rows, nchunk, LANES), F32),
        compiler_params=_params(("arbitrary",)),
        name="moe_experts",
    )(tile_expert, n_valid, xs, w_gate, w_up, w_down)


def _combine_kernel(p1_ref, p2_ref, ys_ref, x_ref, w_ref, o_ref, buf, sem):
    i = pl.program_id(0)
    tc = x_ref.shape[0]
    nchunk = buf.shape[2]
    base = i * tc

    def copies(t):
        return (pltpu.make_async_copy(ys_ref.at[p1_ref[base + t]], buf.at[0, t], sem),
                pltpu.make_async_copy(ys_ref.at[p2_ref[base + t]], buf.at[1, t], sem))

    def issue(t, _):
        for cp in copies(t):
            cp.start()
        return 0

    lax.fori_loop(0, tc, issue, 0)

    def drain(t, _):
        for cp in copies(t):
            cp.wait()
        return 0

    lax.fori_loop(0, tc, drain, 0)
    w = w_ref[...]
    w1 = w[:, 0:1]
    w2 = w[:, 1:2]
    for c in range(nchunk):
        sl = slice(c * LANES, (c + 1) * LANES)
        o_ref[:, sl] = x_ref[:, sl] + w1 * buf[0, :, c, :] + w2 * buf[1, :, c, :]


def _combine(pos1, pos2, ys, x2d, wts, tc=256):
    n, D = x2d.shape
    nchunk = D // LANES
    grid_spec = pltpu.PrefetchScalarGridSpec(
        num_scalar_prefetch=2,
        grid=(n // tc,),
        in_specs=[pl.BlockSpec(memory_space=pl.ANY),
                  pl.BlockSpec((tc, D), lambda i, p1, p2: (i, 0)),
                  pl.BlockSpec((tc, LANES), lambda i, p1, p2: (i, 0))],
        out_specs=pl.BlockSpec((tc, D), lambda i, p1, p2: (i, 0)),
        scratch_shapes=[pltpu.VMEM((2, tc, nchunk, LANES), F32), pltpu.SemaphoreType.DMA(())],
    )
    return pl.pallas_call(
        _combine_kernel,
        grid_spec=grid_spec,
        out_shape=jax.ShapeDtypeStruct((n, D), F32),
        compiler_params=_params(("arbitrary",)),
        name="moe_combine",
    )(pos1, pos2, ys, x2d, wts)


def _hier_moe(x2d, norm_g, w_rg, b_rg, w_re, b_re, w_gate, w_up, w_down):
    n, D = x2d.shape
    tm = MOE_TILE
    pad = LANES - N_GROUPS - N_EXPERTS
    w_router = jnp.concatenate([w_rg, w_re, jnp.zeros((D, pad), F32)], axis=1)
    b_router = jnp.concatenate([b_rg, b_re.reshape(-1), jnp.zeros((pad,), F32)]).reshape(1, LANES)
    xn3, ids, wts, counts = _router(x2d, norm_g, w_router, b_router)
    cnt = counts[0, :N_EXPERTS]
    padded = ((cnt + tm - 1) // tm) * tm
    ends = jnp.cumsum(padded)
    offs = ends - padded
    n_rows = 2 * n + N_EXPERTS * tm
    n_tiles = n_rows // tm
    tile_start = jnp.arange(n_tiles, dtype=I32) * tm
    tile_expert = jnp.minimum(jnp.sum(tile_start[:, None] >= ends[None, :], axis=1), N_EXPERTS - 1).astype(I32)
    n_valid = (ends[-1] // tm).astype(I32).reshape(1)
    last_e = tile_expert[jnp.maximum(n_valid[0] - 1, 0)]
    tile_expert = jnp.where(jnp.arange(n_tiles) < n_valid[0], tile_expert, last_e)
    pos1 = (offs[ids[:, 0]] + ids[:, 2]).astype(I32)
    pos2 = (offs[ids[:, 1]] + ids[:, 3]).astype(I32)
    xs = _dispatch(pos1, pos2, xn3, n_rows)
    ys = _experts(tile_expert, n_valid, xs, w_gate, w_up, w_down)
    return _combine(pos1, pos2, ys, x2d, wts)


def _in_proj_weights(w_in):
    bw = BRANCH_WIDTH
    sizes = (bw, bw, bw, bw, N_HEADS, N_HEADS, bw, bw, bw, IDX_HEADS * IDX_DIM, IDX_DIM, IDX_HEADS,
             bw, bw, bw, N_BRANCHES * w_in.shape[0])
    offs = np.concatenate([[0], np.cumsum(sizes)]).tolist()
    (mq, mk, mv, mo, mi, mf, dq, dk, dv, iq, ik, iw, sq, sk, sv, gp) = [
        w_in[:, offs[i]:offs[i + 1]] for i in range(len(sizes))]
    main = jnp.concatenate([mq, mk, mv, mo, dq, dk, dv, iq, sq, sk, sv, gp], axis=1).astype(BF16)
    tail_pad = jnp.zeros((w_in.shape[0], LANES - IDX_HEADS - 2 * N_HEADS), w_in.dtype)
    tail = jnp.concatenate([ik, ik, iw, mi, mf, tail_pad], axis=1).astype(BF16)
    return main, tail


def _token_mixer(x2d, B, S, norm_g, w_in, conv_w, b_i, b_f, mlstm_norm_g, q_norm_g, k_norm_g,
                 w_branch, w_out, rel_bias):
    n, D = x2d.shape
    L = MLSTM_CHUNK
    H = N_HEADS
    xn = _rmsnorm(x2d, norm_g)
    w_main, w_tail = _in_proj_weights(w_in)
    proj = _matmul_slabs(xn, w_main, BF16, tm=min(n, 1024), tn=512)
    tail = _matmul_slabs(xn, w_tail, F32, tm=min(n, 1024), tn=2 * LANES)
    g = tail[1][:, IDX_HEADS:IDX_HEADS + 2 * H].reshape(B, S // L, L, 2, H)
    gates_t = jnp.transpose(g, (0, 4, 1, 3, 2))
    hm = _mlstm(proj, gates_t, jnp.stack([b_i, b_f]), conv_w, mlstm_norm_g, B, S)
    hd = _dsa(proj, tail, q_norm_g, k_norm_g, rel_bias, B, S)
    hs = _stick_breaking(proj, B, S)
    merged = _merge(hm, hd, hs, proj, w_branch.astype(BF16))
    return _matmul_residual(merged, w_out.astype(BF16), x2d)


def kernel(x, norm1_g, w_in, conv_w, b_i, b_f, mlstm_norm_g, q_norm_g, k_norm_g, w_branch, w_out, norm2_g,
           w_router_g, b_router_g, w_router_e, b_router_e, w_gate, w_up, w_down, rel_bias):
    B, S, D = x.shape
    x2d = x.reshape(B * S, D)
    for l in range(w_in.shape[0]):
        x2d = _token_mixer(x2d, B, S, norm1_g[l], w_in[l], conv_w[l], b_i[l], b_f[l], mlstm_norm_g[l],
                           q_norm_g[l], k_norm_g[l], w_branch[l], w_out[l], rel_bias)
        x2d = _hier_moe(x2d, norm2_g[l], w_router_g[l], b_router_g[l], w_router_e[l], b_router_e[l],
                        w_gate[l], w_up[l], w_down[l])
    return x2d.reshape(B, S, D)
```

```python
import functools
import math

import numpy as np
import jax
import jax.numpy as jnp
from jax import lax
from jax.experimental import pallas as pl
from jax.experimental.pallas import tpu as pltpu

F32 = jnp.float32
BF16 = jnp.bfloat16
I32 = jnp.int32

LANES = 128
HEAD_DIM = 128
N_HEADS = 8
BRANCH_WIDTH = N_HEADS * HEAD_DIM
N_BRANCHES = 3
CONV_WIDTH = 4
IDX_HEADS = 16
IDX_DIM = 64
TOPK_MAX = 256
N_BUCKETS = 32
MAX_DISTANCE = 128
N_GROUPS = 4
EXPERTS_PER_GROUP = 8
N_EXPERTS = N_GROUPS * EXPERTS_PER_GROUP
EPS = 1e-6
NEG = -1e30
INT_MIN = -(2 ** 31)

MLSTM_CHUNK = 128
DSA_BLOCK = 128
SB_BLOCK = 256
MOE_TILE = 256
VMEM_LIMIT = 56 * 1024 * 1024

SL_MQ, SL_MK, SL_MV, SL_MO = 0, 8, 16, 24
SL_DQ, SL_DK, SL_DV, SL_IQ = 32, 40, 48, 56
SL_SQ, SL_SK, SL_SV, SL_GATE = 64, 72, 80, 88
N_MAIN_SLABS = 88 + 48


def _params(sem):
    return pltpu.CompilerParams(dimension_semantics=sem, vmem_limit_bytes=VMEM_LIMIT)


def _dot(a, b):
    return jnp.dot(a, b, preferred_element_type=F32)


def _dot_nt(a, b):
    return lax.dot_general(a, b, (((1,), (1,)), ((), ())), preferred_element_type=F32)


def _softplus(z):
    return jnp.maximum(z, 0.0) + jnp.log1p(jnp.exp(-jnp.abs(z)))


def _sigmoid(z):
    return 1.0 / (1.0 + jnp.exp(-z))


def _rmsnorm_kernel(x_ref, g_ref, o_ref):
    x = x_ref[...]
    ms = jnp.mean(x * x, axis=-1, keepdims=True)
    o_ref[...] = (x * lax.rsqrt(ms + EPS) * g_ref[...]).astype(o_ref.dtype)


def _rmsnorm(x2d, g, tm=512):
    n, d = x2d.shape
    return pl.pallas_call(
        _rmsnorm_kernel,
        grid=(n // tm,),
        in_specs=[pl.BlockSpec((tm, d), lambda i: (i, 0)), pl.BlockSpec((1, d), lambda i: (0, 0))],
        out_specs=pl.BlockSpec((tm, d), lambda i: (i, 0)),
        out_shape=jax.ShapeDtypeStruct((n, d), BF16),
        compiler_params=_params(("parallel",)),
        name="rmsnorm",
    )(x2d, g.reshape(1, d))


def _mm_slab_kernel(a_ref, w_ref, o_ref):
    acc = _dot(a_ref[...], w_ref[...])
    for j in range(o_ref.shape[0]):
        o_ref[j] = acc[:, j * LANES:(j + 1) * LANES].astype(o_ref.dtype)


def _matmul_slabs(a, w, out_dtype, tm, tn):
    m, k = a.shape
    n = w.shape[1]
    return pl.pallas_call(
        _mm_slab_kernel,
        grid=(m // tm, n // tn),
        in_specs=[pl.BlockSpec((tm, k), lambda i, j: (i, 0)), pl.BlockSpec((k, tn), lambda i, j: (0, j))],
        out_specs=pl.BlockSpec((tn // LANES, tm, LANES), lambda i, j: (j, i, 0)),
        out_shape=jax.ShapeDtypeStruct((n // LANES, m, LANES), out_dtype),
        compiler_params=_params(("parallel", "parallel")),
        name="in_proj",
    )(a, w)


def _mlstm_kernel(bias_ref, q_ref, k_ref, v_ref, o_ref, g_ref, cwq_ref, cwk_ref, ng_ref, out_ref,
                  qf, kf, qc, kc, st):
    h = pl.program_id(1)
    S = q_ref.shape[0]
    L = MLSTM_CHUNK
    nc = S // L
    d = HEAD_DIM
    PAD = 8

    qf[0:PAD, :] = jnp.zeros((PAD, d), F32)
    kf[0:PAD, :] = jnp.zeros((PAD, d), F32)
    qf[PAD:PAD + S, :] = q_ref[...].astype(F32)
    kf[PAD:PAD + S, :] = k_ref[...].astype(F32)
    R = min(S, 256)
    for r0 in range(0, S, R):
        aq = jnp.zeros((R, d), F32)
        ak = jnp.zeros((R, d), F32)
        for t in range(CONV_WIDTH):
            off = PAD - (CONV_WIDTH - 1) + t + r0
            aq = aq + cwq_ref[t:t + 1, :] * qf[off:off + R, :]
            ak = ak + cwk_ref[t:t + 1, :] * kf[off:off + R, :]
        qc[r0:r0 + R, :] = (aq * _sigmoid(aq) * (d ** -0.5)).astype(BF16)
        kc[r0:r0 + R, :] = ak * _sigmoid(ak)

    st[...] = jnp.zeros(st.shape, F32)
    b_i = bias_ref[0, h]
    b_f = bias_ref[1, h]
    row = lax.broadcasted_iota(I32, (L, L), 0)
    col = lax.broadcasted_iota(I32, (L, L), 1)
    causal = col <= row
    eye = col == row
    lane = lax.broadcasted_iota(I32, (L, d), 1)
    ones_col = jnp.where(lane == 0, 1.0, 0.0).astype(BF16)
    ng = ng_ref[...]

    def chunk(c, m):
        r = pl.multiple_of(c * L, L)
        q = qc[pl.ds(r, L), :]
        kT = kc[pl.ds(r, L), :].T
        v = v_ref[pl.ds(r, L), :]
        vaug = jnp.concatenate([v, ones_col], axis=1)
        gates = g_ref[c]
        i_row = gates[0:1, :] + b_i
        f_row = gates[1:2, :] + b_f
        lf_row = jnp.minimum(f_row, 0.0) - jnp.log1p(jnp.exp(-jnp.abs(f_row)))
        b_col = jnp.sum(jnp.where(causal, lf_row, 0.0), axis=1, keepdims=True)
        b_row = jnp.sum(jnp.where(eye, b_col, 0.0), axis=0, keepdims=True)
        dlog = jnp.where(causal, b_col - b_row + i_row, NEG)
        inter = b_col + m
        m_t = jnp.maximum(inter, jnp.max(dlog, axis=1, keepdims=True))
        w_intra = jnp.exp(dlog - m_t)
        w_inter = jnp.exp(inter - m_t)
        s = _dot(q, kT.astype(BF16)) * w_intra
        res = w_inter * _dot(q, st[...].astype(BF16)) + _dot(s.astype(BF16), vaug)
        num = res[:, :d]
        den = res[:, d:d + 1]
        hh = num / jnp.maximum(jnp.abs(den), jnp.exp(-m_t))
        hn = hh * lax.rsqrt(jnp.mean(hh * hh, axis=1, keepdims=True) + EPS) * ng
        og = o_ref[pl.ds(r, L), :].astype(F32)
        out_ref[pl.ds(r, L), :] = (hn * _sigmoid(og)).astype(out_ref.dtype)
        ws_row = w_intra[L - 1:L, :]
        decay = w_inter[L - 1:L, :]
        st[...] = decay * st[...] + _dot((kT * ws_row).astype(BF16), vaug)
        return m_t[L - 1:L, :]

    lax.fori_loop(0, nc, chunk, jnp.zeros((1, 1), F32))


def _mlstm(proj, gates_t, bias_if, conv_w, norm_g, B, S):
    L = MLSTM_CHUNK
    nc = S // L
    d = HEAD_DIM
    H = N_HEADS

    def slab(off):
        return pl.BlockSpec((None, S, d), lambda b, h: (off + h, b, 0))

    return pl.pallas_call(
        _mlstm_kernel,
        grid=(B, H),
        in_specs=[
            pl.BlockSpec(memory_space=pltpu.SMEM),
            slab(SL_MQ), slab(SL_MK), slab(SL_MV), slab(SL_MO),
            pl.BlockSpec((None, None, nc, 2, L), lambda b, h: (b, h, 0, 0, 0)),
            pl.BlockSpec((CONV_WIDTH, d), lambda b, h: (0, h)),
            pl.BlockSpec((CONV_WIDTH, d), lambda b, h: (0, H + h)),
            pl.BlockSpec((1, d), lambda b, h: (0, h)),
        ],
        out_specs=pl.BlockSpec((None, S, d), lambda b, h: (h, b, 0)),
        out_shape=jax.ShapeDtypeStruct((H, B * S, d), BF16),
        scratch_shapes=[
            pltpu.VMEM((S + 8, d), F32), pltpu.VMEM((S + 8, d), F32),
            pltpu.VMEM((S, d), BF16), pltpu.VMEM((S, d), F32),
            pltpu.VMEM((d, 2 * d), F32),
        ],
        compiler_params=_params(("parallel", "parallel")),
        name="mlstm",
    )(bias_if, proj, proj, proj, proj, gates_t, conv_w, conv_w, norm_g.reshape(1, H * d))


def _t5_thresholds():
    max_exact = N_BUCKETS // 2
    n = np.arange(0, 2 * MAX_DISTANCE)
    nf = np.maximum(n, 1).astype(np.float64)
    val = np.log(nf / max_exact) / math.log(MAX_DISTANCE / max_exact) * (N_BUCKETS - max_exact)
    frac = np.abs(val - np.round(val))
    frac_ok = (frac > 1e-4) | (n <= max_exact) | (n >= MAX_DISTANCE)
    assert frac_ok.all()
    large = np.minimum(max_exact + np.trunc(val).astype(np.int64), N_BUCKETS - 1)
    bucket = np.where(n < max_exact, n, large)
    assert (np.diff(bucket) >= 0).all() and bucket[MAX_DISTANCE] == N_BUCKETS - 1
    return [int(np.argmax(bucket >= j)) for j in range(1, N_BUCKETS)]


_T5_THR = _t5_thresholds()


def _dsa_kernel(rb_ref, q_ref, k_ref, v_ref, iq_ref, ik_ref, wt_ref, gq_ref, gk_ref, out_ref,
                kn, ikk, qm, wb, key_ref, mask_ref, lg_ref, bias_ref, *, topk):
    qi = pl.program_id(1)
    T = DSA_BLOCK
    d = HEAD_DIM
    H = N_HEADS
    S = k_ref.shape[1]
    nkb = qi + 1
    row = lax.broadcasted_iota(I32, (T, T), 0)
    col = lax.broadcasted_iota(I32, (T, T), 1)

    @pl.when(qi == 0)
    def _per_batch():
        gk = gk_ref[...]

        def norm_k(h, _):
            kh = k_ref[h].astype(F32)
            kn[h] = (kh * lax.rsqrt(jnp.mean(kh * kh, axis=1, keepdims=True) + EPS) * gk).astype(BF16)
            for o in range(2):
                n = o * T + row - col
                val = jnp.full((T, T), rb_ref[0, h], F32)
                for j, thr in enumerate(_T5_THR):
                    val = jnp.where(n >= thr, rb_ref[j + 1, h], val)
                bias_ref[h, o] = val
            bias_ref[h, 2] = jnp.full((T, T), rb_ref[N_BUCKETS - 1, h], F32)
            return 0

        lax.fori_loop(0, H, norm_k, 0)
        ikk[...] = ik_ref[...].astype(BF16)

    lane = lax.broadcasted_iota(I32, (T, LANES), 1)
    wsc = wt_ref[...] * (IDX_HEADS ** -0.5 * IDX_DIM ** -0.5)
    for hp in range(IDX_HEADS // 2):
        qp = iq_ref[hp]
        zero = jnp.zeros_like(qp)
        qm[2 * hp] = jnp.where(lane < IDX_DIM, qp, zero)
        qm[2 * hp + 1] = jnp.where(lane >= IDX_DIM, qp, zero)
    for h16 in range(IDX_HEADS):
        wb[h16] = jnp.broadcast_to(wsc[:, h16:h16 + 1], (T, LANES))

    def score_block(j, _):
        kk = ikk[pl.ds(pl.multiple_of(j * T, T), T), :]
        sc = jnp.zeros((T, T), F32)
        for h16 in range(IDX_HEADS):
            sc = sc + wb[h16] * jnp.maximum(_dot_nt(qm[h16], kk), 0.0)
        bits = lax.bitcast_convert_type(sc, I32)
        key = jnp.where(bits < 0, bits ^ jnp.int32(0x7FFFFFFF), bits)
        key = jnp.where(sc == 0.0, 0, key)
        key = jnp.where(j * T + col <= qi * T + row, key, INT_MIN)
        key_ref[j] = key
        return 0

    lax.fori_loop(0, nkb, score_block, 0)

    def count_ge(cand):
        cb = jnp.broadcast_to(cand, (T, LANES))

        def body(j, acc):
            return acc + jnp.where(key_ref[j] >= cb, 1.0, 0.0)

        acc = lax.fori_loop(0, nkb, body, jnp.zeros((T, LANES), F32))
        return jnp.sum(acc, axis=1, keepdims=True)

    kf = float(topk)
    thr0 = jnp.where(count_ge(jnp.zeros((T, 1), I32)) >= kf, 0, INT_MIN).astype(I32)

    def bisect(it, thr):
        cand = thr | jnp.left_shift(jnp.int32(1), 30 - it)
        return jnp.where(count_ge(cand) >= kf, cand, thr)

    thr = lax.fori_loop(0, 31, bisect, thr0)
    thrb = jnp.broadcast_to(thr, (T, LANES))
    need = kf - _count_gt(key_ref, thrb, nkb, T)

    tri = jnp.where(row <= col, 1.0, 0.0).astype(BF16)

    def select_block(j, carry):
        key = key_ref[j]
        tie = key == thrb
        tief = jnp.where(tie, 1.0, 0.0)
        pre = _dot(tief.astype(BF16), tri) + carry
        m = jnp.where(key > thrb, 0.0, jnp.where(tie, jnp.where(pre <= need, 0.0, NEG), NEG))
        mask_ref[j] = jnp.where(key == INT_MIN, NEG, m)
        return carry + jnp.sum(tief, axis=1, keepdims=True)

    lax.fori_loop(0, nkb, select_block, jnp.zeros((T, 1), F32))

    gq = gq_ref[...]

    def head(h, _):
        qh = q_ref[h].astype(F32)
        qn = (qh * lax.rsqrt(jnp.mean(qh * qh, axis=1, keepdims=True) + EPS) * gq * (d ** -0.5)).astype(BF16)

        def logits(j, mx):
            kj = kn[h, pl.ds(pl.multiple_of(j * T, T), T), :]
            lg = _dot_nt(qn, kj) + bias_ref[h, jnp.minimum(qi - j, 2)] + mask_ref[j]
            lg_ref[j] = lg
            return jnp.maximum(mx, lg)

        mx = lax.fori_loop(0, nkb, logits, jnp.full((T, T), NEG, F32))
        rowmax = jnp.max(mx, axis=1, keepdims=True)

        def pv(j, carry):
            l, acc = carry
            p = jnp.exp(lg_ref[j] - rowmax)
            vj = v_ref[h, pl.ds(pl.multiple_of(j * T, T), T), :]
            return l + p, acc + _dot(p.astype(BF16), vj)

        l, acc = lax.fori_loop(0, nkb, pv, (jnp.zeros((T, T), F32), jnp.zeros((T, d), F32)))
        out_ref[h] = (acc / jnp.sum(l, axis=1, keepdims=True)).astype(out_ref.dtype)
        return 0

    lax.fori_loop(0, H, head, 0)


def _count_gt(key_ref, thrb, nkb, T):
    def body(j, acc):
        return acc + jnp.where(key_ref[j] > thrb, 1.0, 0.0)

    acc = lax.fori_loop(0, nkb, body, jnp.zeros((T, LANES), F32))
    return jnp.sum(acc, axis=1, keepdims=True)


def _dsa(proj, tail, q_norm_g, k_norm_g, rel_bias, B, S):
    T = DSA_BLOCK
    nq = S // T
    d = HEAD_DIM
    H = N_HEADS
    topk = min(TOPK_MAX, S // 4)
    kern = functools.partial(_dsa_kernel, topk=topk)
    return pl.pallas_call(
        kern,
        grid=(B, nq),
        in_specs=[
            pl.BlockSpec(memory_space=pltpu.SMEM),
            pl.BlockSpec((H, T, d), lambda b, i: (SL_DQ // H, b * nq + i, 0)),
            pl.BlockSpec((H, S, d), lambda b, i: (SL_DK // H, b, 0)),
            pl.BlockSpec((H, S, d), lambda b, i: (SL_DV // H, b, 0)),
            pl.BlockSpec((H, T, d), lambda b, i: (SL_IQ // H, b * nq + i, 0)),
            pl.BlockSpec((None, S, LANES), lambda b, i: (0, b, 0)),
            pl.BlockSpec((None, T, LANES), lambda b, i: (1, b * nq + i, 0)),
            pl.BlockSpec((1, d), lambda b, i: (0, 0)),
            pl.BlockSpec((1, d), lambda b, i: (0, 0)),
        ],
        out_specs=pl.BlockSpec((H, T, d), lambda b, i: (0, b * nq + i, 0)),
        out_shape=jax.ShapeDtypeStruct((H, B * S, d), BF16),
        scratch_shapes=[
            pltpu.VMEM((H, S, d), BF16),
            pltpu.VMEM((S, LANES), BF16),
            pltpu.VMEM((IDX_HEADS, T, LANES), BF16),
            pltpu.VMEM((IDX_HEADS, T, LANES), F32),
            pltpu.VMEM((nq, T, T), I32),
            pltpu.VMEM((nq, T, T), F32),
            pltpu.VMEM((nq, T, T), F32),
            pltpu.VMEM((H, 3, T, T), F32),
        ],
        compiler_params=_params(("parallel", "arbitrary")),
        name="dsa",
    )(rel_bias, proj, proj, proj, proj, tail, tail, q_norm_g.reshape(1, d), k_norm_g.reshape(1, d))


def _sb_kernel(q_ref, k_ref, v_ref, out_ref):
    qi = pl.program_id(2)
    T = SB_BLOCK
    d = HEAD_DIM
    row = lax.broadcasted_iota(I32, (T, T), 0)
    col = lax.broadcasted_iota(I32, (T, T), 1)
    upper = jnp.where(row > col, 1.0, 0.0).astype(BF16)
    q = (q_ref[...].astype(F32) * (d ** -0.5)).astype(BF16)

    def block(i, carry):
        run, acc = carry
        j = qi - i
        r = pl.multiple_of(j * T, T)
        z = _dot_nt(q, k_ref[pl.ds(r, T), :])
        sp = _softplus(z)
        strict = j * T + col < qi * T + row
        ln = jnp.where(strict, -sp, 0.0)
        hi = ln.astype(BF16)
        lo = (ln - hi.astype(F32)).astype(BF16)
        suf = _dot(hi, upper) + _dot(lo, upper)
        a = jnp.where(strict, jnp.exp(z - sp + suf + run), 0.0)
        acc = acc + _dot(a.astype(BF16), v_ref[pl.ds(r, T), :])
        return run + suf[:, 0:1] + ln[:, 0:1], acc

    _, acc = lax.fori_loop(0, qi + 1, block, (jnp.zeros((T, 1), F32), jnp.zeros((T, d), F32)))
    out_ref[...] = acc.astype(out_ref.dtype)


def _stick_breaking(proj, B, S):
    T = SB_BLOCK
    nq = S // T
    d = HEAD_DIM
    H = N_HEADS
    return pl.pallas_call(
        _sb_kernel,
        grid=(B, H, nq),
        in_specs=[
            pl.BlockSpec((None, T, d), lambda b, h, i: (SL_SQ + h, b * nq + i, 0)),
            pl.BlockSpec((None, S, d), lambda b, h, i: (SL_SK + h, b, 0)),
            pl.BlockSpec((None, S, d), lambda b, h, i: (SL_SV + h, b, 0)),
        ],
        out_specs=pl.BlockSpec((None, T, d), lambda b, h, i: (h, b * nq + i, 0)),
        out_shape=jax.ShapeDtypeStruct((H, B * S, d), BF16),
        compiler_params=_params(("parallel", "parallel", "parallel")),
        name="stick_breaking",
    )(proj, proj, proj)


def _merge_kernel(hm_ref, hd_ref, hs_ref, g0_ref, g1_ref, g2_ref, w_ref, o_ref):
    H = N_HEADS
    acc = None
    ns = o_ref.shape[1] // LANES
    for n, (br, gr) in enumerate(((hm_ref, g0_ref), (hd_ref, g1_ref), (hs_ref, g2_ref))):
        a = jnp.concatenate([br[h] for h in range(H)], axis=1)
        up = _dot(a, w_ref[n])
        gate = jnp.concatenate([gr[s] for s in range(ns)], axis=1).astype(F32)
        term = _sigmoid(gate) * up
        acc = term if acc is None else acc + term
    o_ref[...] = acc.astype(o_ref.dtype)


def _merge(hm, hd, hs, proj, w_branch, tm=512, tn=512):
    H, n, d = hm.shape
    D = w_branch.shape[2]
    ns = tn // LANES
    br = pl.BlockSpec((H, tm, d), lambda i, j: (0, i, 0))

    def gate(b):
        first = (SL_GATE + b * (D // LANES)) // ns
        return pl.BlockSpec((ns, tm, LANES), lambda i, j: (first + j, i, 0))

    return pl.pallas_call(
        _merge_kernel,
        grid=(n // tm, D // tn),
        in_specs=[br, br, br, gate(0), gate(1), gate(2),
                  pl.BlockSpec((N_BRANCHES, H * d, tn), lambda i, j: (0, 0, j))],
        out_specs=pl.BlockSpec((tm, tn), lambda i, j: (i, j)),
        out_shape=jax.ShapeDtypeStruct((n, D), BF16),
        compiler_params=_params(("parallel", "parallel")),
        name="branch_merge",
    )(hm, hd, hs, proj, proj, proj, w_branch)


def _mm_res_kernel(a_ref, w_ref, x_ref, o_ref):
    o_ref[...] = x_ref[...] + _dot(a_ref[...], w_ref[...])


def _matmul_residual(a, w, x, tm=512, tn=512):
    m, k = a.shape
    n = w.shape[1]
    return pl.pallas_call(
        _mm_res_kernel,
        grid=(m // tm, n // tn),
        in_specs=[pl.BlockSpec((tm, k), lambda i, j: (i, 0)), pl.BlockSpec((k, tn), lambda i, j: (0, j)),
                  pl.BlockSpec((tm, tn), lambda i, j: (i, j))],
        out_specs=pl.BlockSpec((tm, tn), lambda i, j: (i, j)),
        out_shape=jax.ShapeDtypeStruct((m, n), F32),
        compiler_params=_params(("parallel", "parallel")),
        name="out_proj",
    )(a, w, x)


def _router_kernel(x_ref, g_ref, wr_ref, br_ref, xn_ref, ids_ref, wts_ref, cnt_ref, carry):
    i = pl.program_id(0)
    tm, D = x_ref.shape
    nchunk = D // LANES

    @pl.when(i == 0)
    def _():
        carry[...] = jnp.zeros(carry.shape, F32)

    x = x_ref[...]
    xn = x * lax.rsqrt(jnp.mean(x * x, axis=-1, keepdims=True) + EPS) * g_ref[...]
    for c in range(nchunk):
        xn_ref[:, c, :] = xn[:, c * LANES:(c + 1) * LANES]
    logits = jnp.dot(xn, wr_ref[...], preferred_element_type=F32, precision=lax.Precision.HIGHEST)
    biased = logits + br_ref[...]
    lane = lax.broadcasted_iota(I32, (tm, LANES), 1)
    lanef = lane.astype(F32)
    big = float(LANES)

    def first_lane(mask):
        return jnp.min(jnp.where(mask, lanef, big), axis=1, keepdims=True)

    gmask = lane < N_GROUPS
    gmax = jnp.max(jnp.where(gmask, biased, NEG), axis=1, keepdims=True)
    g_sel = first_lane(gmask & (biased == gmax))
    gm = jnp.max(jnp.where(gmask, logits, NEG), axis=1, keepdims=True)
    ge = jnp.where(gmask, jnp.exp(logits - gm), 0.0)
    p_group = jnp.sum(jnp.where(lanef == g_sel, ge, 0.0), axis=1, keepdims=True) / jnp.sum(ge, axis=1, keepdims=True)
    lo = N_GROUPS + EXPERTS_PER_GROUP * g_sel
    emask = (lanef >= lo) & (lanef < lo + EXPERTS_PER_GROUP)
    eb = jnp.where(emask, biased, NEG)
    e1 = first_lane(emask & (eb == jnp.max(eb, axis=1, keepdims=True)))
    emask2 = emask & (lanef != e1)
    eb2 = jnp.where(emask2, biased, NEG)
    e2 = first_lane(emask2 & (eb2 == jnp.max(eb2, axis=1, keepdims=True)))
    em = jnp.max(jnp.where(emask, logits, NEG), axis=1, keepdims=True)
    ee = jnp.where(emask, jnp.exp(logits - em), 0.0)
    s1 = jnp.sum(jnp.where(lanef == e1, ee, 0.0), axis=1, keepdims=True)
    s2 = jnp.sum(jnp.where(lanef == e2, ee, 0.0), axis=1, keepdims=True)
    se = jnp.sum(ee, axis=1, keepdims=True)
    w1 = s1 / se
    w2 = s2 / se
    wsum = w1 + w2
    w1 = p_group * w1 / wsum
    w2 = p_group * w2 / wsum
    x1 = e1 - N_GROUPS
    x2 = e2 - N_GROUPS

    onehot = jnp.where((lanef == x1) | (lanef == x2), 1.0, 0.0)
    r = lax.broadcasted_iota(I32, (tm, tm), 0)
    c = lax.broadcasted_iota(I32, (tm, tm), 1)
    lower = jnp.where(c < r, 1.0, 0.0).astype(BF16)
    prefix = _dot(lower, onehot.astype(BF16)) + carry[...]
    r1 = jnp.sum(jnp.where(lanef == x1, prefix, 0.0), axis=1, keepdims=True)
    r2 = jnp.sum(jnp.where(lanef == x2, prefix, 0.0), axis=1, keepdims=True)
    carry[...] = carry[...] + jnp.sum(onehot, axis=0, keepdims=True)
    cnt_ref[...] = carry[...].astype(I32)

    idsf = jnp.where(lane == 0, x1, jnp.where(lane == 1, x2, jnp.where(lane == 2, r1, jnp.where(lane == 3, r2, 0.0))))
    ids_ref[...] = idsf.astype(I32)
    wts_ref[...] = jnp.where(lane == 0, w1, jnp.where(lane == 1, w2, 0.0))


def _router(x2d, g, w_router, b_router, tm=512):
    n, D = x2d.shape
    return pl.pallas_call(
        _router_kernel,
        grid=(n // tm,),
        in_specs=[pl.BlockSpec((tm, D), lambda i: (i, 0)), pl.BlockSpec((1, D), lambda i: (0, 0)),
                  pl.BlockSpec((D, LANES), lambda i: (0, 0)), pl.BlockSpec((1, LANES), lambda i: (0, 0))],
        out_specs=[pl.BlockSpec((tm, D // LANES, LANES), lambda i: (i, 0, 0)),
                   pl.BlockSpec((tm, LANES), lambda i: (i, 0)),
                   pl.BlockSpec((tm, LANES), lambda i: (i, 0)),
                   pl.BlockSpec((1, LANES), lambda i: (0, 0))],
        out_shape=[jax.ShapeDtypeStruct((n, D // LANES, LANES), F32),
                   jax.ShapeDtypeStruct((n, LANES), I32),
                   jax.ShapeDtypeStruct((n, LANES), F32),
                   jax.ShapeDtypeStruct((1, LANES), I32)],
        scratch_shapes=[pltpu.VMEM((1, LANES), F32)],
        compiler_params=_params(("arbitrary",)),
        name="moe_router",
    )(x2d, g.reshape(1, D), w_router, b_router)


def _dispatch_kernel(p1_ref, p2_ref, xn_ref, xs_in_ref, xs_ref, sem, *, tb):
    del xs_in_ref
    base = pl.program_id(0) * tb

    def copies(t):
        return (pltpu.make_async_copy(xn_ref.at[t], xs_ref.at[p1_ref[base + t]], sem),
                pltpu.make_async_copy(xn_ref.at[t], xs_ref.at[p2_ref[base + t]], sem))

    def issue(t, _):
        for cp in copies(t):
            cp.start()
        return 0

    lax.fori_loop(0, tb, issue, 0)

    def drain(t, _):
        for cp in copies(t):
            cp.wait()
        return 0

    lax.fori_loop(0, tb, drain, 0)


def _dispatch(pos1, pos2, xn3, n_rows, tb=512):
    n, nchunk, _ = xn3.shape
    tb = min(tb, n)
    xs0 = jnp.zeros((n_rows, nchunk, LANES), F32)
    grid_spec = pltpu.PrefetchScalarGridSpec(
        num_scalar_prefetch=2,
        grid=(n // tb,),
        in_specs=[pl.BlockSpec((tb, nchunk, LANES), lambda i, p1, p2: (i, 0, 0)),
                  pl.BlockSpec(memory_space=pl.ANY)],
        out_specs=pl.BlockSpec(memory_space=pl.ANY),
        scratch_shapes=[pltpu.SemaphoreType.DMA(())],
    )
    return pl.pallas_call(
        functools.partial(_dispatch_kernel, tb=tb),
        grid_spec=grid_spec,
        out_shape=jax.ShapeDtypeStruct((n_rows, nchunk, LANES), F32),
        input_output_aliases={3: 0},
        compiler_params=pltpu.CompilerParams(dimension_semantics=("arbitrary",), has_side_effects=True),
        name="moe_dispatch",
    )(pos1, pos2, xn3, xs0)


def _expert_kernel(te_ref, nv_ref, xs_ref, wg_ref, wu_ref, wd_ref, ys_ref, wgb, wub, wdb):
    i = pl.program_id(0)
    nchunk = xs_ref.shape[1]
    prev = te_ref[jnp.maximum(i - 1, 0)]

    @pl.when(i < nv_ref[0])
    def _():
        @pl.when((i == 0) | (te_ref[i] != prev))
        def _():
            wgb[...] = wg_ref[...].astype(BF16)
            wub[...] = wu_ref[...].astype(BF16)
            wdb[...] = wd_ref[...].astype(BF16)

        x = jnp.concatenate([xs_ref[:, c, :] for c in range(nchunk)], axis=1).astype(BF16)
        g = _dot(x, wgb[...])
        u = _dot(x, wub[...])
        hcur = (g * _sigmoid(g) * u).astype(BF16)
        y = _dot(hcur, wdb[...])
        for c in range(nchunk):
            ys_ref[:, c, :] = y[:, c * LANES:(c + 1) * LANES]

    @pl.when(i >= nv_ref[0])
    def _():
        ys_ref[...] = jnp.zeros(ys_ref.shape, F32)


def _experts(tile_expert, n_valid, xs, w_gate, w_up, w_down):
    n_rows, nchunk, _ = xs.shape
    E, D, Fe = w_gate.shape
    tm = MOE_TILE
    n_tiles = n_rows // tm

    def row_map(i, te, nv):
        return (jnp.minimum(i, nv[0] - 1), 0, 0)

    def w_map(i, te, nv):
        return (te[i], 0, 0)

    grid_spec = pltpu.PrefetchScalarGridSpec(
        num_scalar_prefetch=2,
        grid=(n_tiles,),
        in_specs=[pl.BlockSpec((tm, nchunk, LANES), row_map),
                  pl.BlockSpec((None, D, Fe), w_map), pl.BlockSpec((None, D, Fe), w_map),
                  pl.BlockSpec((None, Fe, D), w_map)],
        out_specs=pl.BlockSpec((tm, nchunk, LANES), lambda i, te, nv: (i, 0, 0)),
        scratch_shapes=[pltpu.VMEM((D, Fe), BF16), pltpu.VMEM((D, Fe), BF16), pltpu.VMEM((Fe, D), BF16)],
    )
    return pl.pallas_call(
        _expert_kernel,
        grid_spec=grid_spec,
        out_shape=jax.ShapeDtypeStruct((n_rows, nchunk, LANES), F32),
        compiler_params=_params(("arbitrary",)),
        name="moe_experts",
    )(tile_expert, n_valid, xs, w_gate, w_up, w_down)


def _combine_kernel(p1_ref, p2_ref, ys_ref, x_ref, w_ref, o_ref, buf, sem):
    i = pl.program_id(0)
    tc = x_ref.shape[0]
    nchunk = buf.shape[2]
    base = i * tc

    def copies(t):
        return (pltpu.make_async_copy(ys_ref.at[p1_ref[base + t]], buf.at[0, t], sem),
                pltpu.make_async_copy(ys_ref.at[p2_ref[base + t]], buf.at[1, t], sem))

    def issue(t, _):
        for cp in copies(t):
            cp.start()
        return 0

    lax.fori_loop(0, tc, issue, 0)

    def drain(t, _):
        for cp in copies(t):
            cp.wait()
        return 0

    lax.fori_loop(0, tc, drain, 0)
    w = w_ref[...]
    w1 = w[:, 0:1]
    w2 = w[:, 1:2]
    for c in range(nchunk):
        sl = slice(c * LANES, (c + 1) * LANES)
        o_ref[:, sl] = x_ref[:, sl] + w1 * buf[0, :, c, :] + w2 * buf[1, :, c, :]


def _combine(pos1, pos2, ys, x2d, wts, tc=256):
    n, D = x2d.shape
    nchunk = D // LANES
    grid_spec = pltpu.PrefetchScalarGridSpec(
        num_scalar_prefetch=2,
        grid=(n // tc,),
        in_specs=[pl.BlockSpec(memory_space=pl.ANY),
                  pl.BlockSpec((tc, D), lambda i, p1, p2: (i, 0)),
                  pl.BlockSpec((tc, LANES), lambda i, p1, p2: (i, 0))],
        out_specs=pl.BlockSpec((tc, D), lambda i, p1, p2: (i, 0)),
        scratch_shapes=[pltpu.VMEM((2, tc, nchunk, LANES), F32), pltpu.SemaphoreType.DMA(())],
    )
    return pl.pallas_call(
        _combine_kernel,
        grid_spec=grid_spec,
        out_shape=jax.ShapeDtypeStruct((n, D), F32),
        compiler_params=_params(("arbitrary",)),
        name="moe_combine",
    )(pos1, pos2, ys, x2d, wts)


def _hier_moe(x2d, norm_g, w_rg, b_rg, w_re, b_re, w_gate, w_up, w_down):
    n, D = x2d.shape
    tm = MOE_TILE
    pad = LANES - N_GROUPS - N_EXPERTS
    w_router = jnp.concatenate([w_rg, w_re, jnp.zeros((D, pad), F32)], axis=1)
    b_router = jnp.concatenate([b_rg, b_re.reshape(-1), jnp.zeros((pad,), F32)]).reshape(1, LANES)
    xn3, ids, wts, counts = _router(x2d, norm_g, w_router, b_router)
    cnt = counts[0, :N_EXPERTS]
    padded = ((cnt + tm - 1) // tm) * tm
    ends = jnp.cumsum(padded)
    offs = ends - padded
    n_rows = 2 * n + N_EXPERTS * tm
    n_tiles = n_rows // tm
    tile_start = jnp.arange(n_tiles, dtype=I32) * tm
    tile_expert = jnp.minimum(jnp.sum(tile_start[:, None] >= ends[None, :], axis=1), N_EXPERTS - 1).astype(I32)
    n_valid = (ends[-1] // tm).astype(I32).reshape(1)
    last_e = tile_expert[jnp.maximum(n_valid[0] - 1, 0)]
    tile_expert = jnp.where(jnp.arange(n_tiles) < n_valid[0], tile_expert, last_e)
    pos1 = (offs[ids[:, 0]] + ids[:, 2]).astype(I32)
    pos2 = (offs[ids[:, 1]] + ids[:, 3]).astype(I32)
    xs = _dispatch(pos1, pos2, xn3, n_rows)
    ys = _experts(tile_expert, n_valid, xs, w_gate, w_up, w_down)
    return _combine(pos1, pos2, ys, x2d, wts)


def _in_proj_weights(w_in):
    bw = BRANCH_WIDTH
    sizes = (bw, bw, bw, bw, N_HEADS, N_HEADS, bw, bw, bw, IDX_HEADS * IDX_DIM, IDX_DIM, IDX_HEADS,
             bw, bw, bw, N_BRANCHES * w_in.shape[0])
    offs = np.concatenate([[0], np.cumsum(sizes)]).tolist()
    (mq, mk, mv, mo, mi, mf, dq, dk, dv, iq, ik, iw, sq, sk, sv, gp) = [
        w_in[:, offs[i]:offs[i + 1]] for i in range(len(sizes))]
    main = jnp.concatenate([mq, mk, mv, mo, dq, dk, dv, iq, sq, sk, sv, gp], axis=1).astype(BF16)
    tail_pad = jnp.zeros((w_in.shape[0], LANES - IDX_HEADS - 2 * N_HEADS), w_in.dtype)
    tail = jnp.concatenate([ik, ik, iw, mi, mf, tail_pad], axis=1).astype(BF16)
    return main, tail


def _token_mixer(x2d, B, S, norm_g, w_in, conv_w, b_i, b_f, mlstm_norm_g, q_norm_g, k_norm_g,
                 w_branch, w_out, rel_bias):
    n, D = x2d.shape
    L = MLSTM_CHUNK
    H = N_HEADS
    xn = _rmsnorm(x2d, norm_g)
    w_main, w_tail = _in_proj_weights(w_in)
    proj = _matmul_slabs(xn, w_main, BF16, tm=min(n, 1024), tn=512)
    tail = _matmul_slabs(xn, w_tail, F32, tm=min(n, 1024), tn=2 * LANES)
    g = tail[1][:, IDX_HEADS:IDX_HEADS + 2 * H].reshape(B, S // L, L, 2, H)
    gates_t = jnp.transpose(g, (0, 4, 1, 3, 2))
    hm = _mlstm(proj, gates_t, jnp.stack([b_i, b_f]), conv_w, mlstm_norm_g, B, S)
    hd = _dsa(proj, tail, q_norm_g, k_norm_g, rel_bias, B, S)
    hs = _stick_breaking(proj, B, S)
    merged = _merge(hm, hd, hs, proj, w_branch.astype(BF16))
    return _matmul_residual(merged, w_out.astype(BF16), x2d)


def kernel(x, norm1_g, w_in, conv_w, b_i, b_f, mlstm_norm_g, q_norm_g, k_norm_g, w_branch, w_out, norm2_g,
           w_router_g, b_router_g, w_router_e, b_router_e, w_gate, w_up, w_down, rel_bias):
    B, S, D = x.shape
    x2d = x.reshape(B * S, D)
    for l in range(w_in.shape[0]):
        x2d = _token_mixer(x2d, B, S, norm1_g[l], w_in[l], conv_w[l], b_i[l], b_f[l], mlstm_norm_g[l],
                           q_norm_g[l], k_norm_g[l], w_branch[l], w_out[l], rel_bias)
        x2d = _hier_moe(x2d, norm2_g[l], w_router_g[l], b_router_g[l], w_router_e[l], b_router_e[l],
                        w_gate[l], w_up[l], w_down[l])
    return x2d.reshape(B, S, D)
```

```python
import functools
import math

import numpy as np
import jax
import jax.numpy as jnp
from jax import lax
from jax.experimental import pallas as pl
from jax.experimental.pallas import tpu as pltpu

F32 = jnp.float32
BF16 = jnp.bfloat16
I32 = jnp.int32

LANES = 128
HEAD_DIM = 128
N_HEADS = 8
BRANCH_WIDTH = N_HEADS * HEAD_DIM
N_BRANCHES = 3
CONV_WIDTH = 4
IDX_HEADS = 16
IDX_DIM = 64
TOPK_MAX = 256
N_BUCKETS = 32
MAX_DISTANCE = 128
N_GROUPS = 4
EXPERTS_PER_GROUP = 8
N_EXPERTS = N_GROUPS * EXPERTS_PER_GROUP
EPS = 1e-6
NEG = -1e30
INT_MIN = -(2 ** 31)

MLSTM_CHUNK = 128
DSA_BLOCK = 128
DSA_BANDS = 4
SB_BLOCK = 256
SB_ROWS = 256
HEADS_PER_STEP = 2
MOE_TILE = 256
PROJ_TN = 512
VMEM_LIMIT = 56 * 1024 * 1024

A_MQ, A_MK, A_MV, A_MO = 0, 8, 16, 24
B_DQ, B_DK, B_DV, B_IQ = 0, 8, 16, 24
C_SQ, C_SK, C_SV, C_GATE = 0, 8, 16, 24


def _params(sem):
    return pltpu.CompilerParams(dimension_semantics=sem, vmem_limit_bytes=VMEM_LIMIT)


def _dot(a, b):
    return jnp.dot(a, b, preferred_element_type=F32)


def _dot_nt(a, b):
    return lax.dot_general(a, b, (((1,), (1,)), ((), ())), preferred_element_type=F32)


def _sigmoid(z):
    return 1.0 / (1.0 + jnp.exp(-z))


def _rmsnorm_kernel(x_ref, g_ref, o_ref):
    x = x_ref[...]
    ms = jnp.mean(x * x, axis=-1, keepdims=True)
    o_ref[...] = (x * lax.rsqrt(ms + EPS) * g_ref[...]).astype(o_ref.dtype)


def _rmsnorm(x2d, g, tm=512):
    n, d = x2d.shape
    return pl.pallas_call(
        _rmsnorm_kernel,
        grid=(n // tm,),
        in_specs=[pl.BlockSpec((tm, d), lambda i: (i, 0)), pl.BlockSpec((1, d), lambda i: (0, 0))],
        out_specs=pl.BlockSpec((tm, d), lambda i: (i, 0)),
        out_shape=jax.ShapeDtypeStruct((n, d), BF16),
        compiler_params=_params(("parallel",)),
        name="rmsnorm",
    )(x2d, g.reshape(1, d))


def _in_proj_kernel(x_ref, *rest, shift, nblk):
    w_refs, o_ref, wb = rest[:nblk], rest[nblk], rest[nblk + 1]
    tn = wb.shape[1]

    @pl.when(pl.program_id(1) == 0)
    def _():
        w = jnp.concatenate([r[...] for r in w_refs], axis=1)
        wb[...] = w[:, shift:shift + tn].astype(BF16)

    acc = _dot(x_ref[...], wb[...])
    for j in range(o_ref.shape[0]):
        o_ref[j] = acc[:, j * LANES:(j + 1) * LANES].astype(o_ref.dtype)


def _in_proj(xn, w_in, layer, col0, ncols, tm):
    m, k = xn.shape
    tn = PROJ_TN
    base, shift = divmod(col0, LANES)
    nblk = tn // LANES + (1 if shift else 0)
    per = tn // LANES

    def wspec(r):
        return pl.BlockSpec((None, k, LANES), lambda j, i: (layer, 0, base + per * j + r))

    return pl.pallas_call(
        functools.partial(_in_proj_kernel, shift=shift, nblk=nblk),
        grid=(ncols // tn, m // tm),
        in_specs=[pl.BlockSpec((tm, k), lambda j, i: (i, 0))] + [wspec(r) for r in range(nblk)],
        out_specs=pl.BlockSpec((per, tm, LANES), lambda j, i: (j, i, 0)),
        out_shape=jax.ShapeDtypeStruct((ncols // LANES, m, LANES), BF16),
        scratch_shapes=[pltpu.VMEM((k, tn), BF16)],
        compiler_params=_params(("parallel", "arbitrary")),
        name="in_proj",
    )(xn, *([w_in] * nblk))


def _mm_slab_kernel(a_ref, w_ref, o_ref):
    acc = _dot(a_ref[...], w_ref[...])
    for j in range(o_ref.shape[0]):
        o_ref[j] = acc[:, j * LANES:(j + 1) * LANES].astype(o_ref.dtype)


def _matmul_slabs(a, w, out_dtype, tm, tn):
    m, k = a.shape
    n = w.shape[1]
    return pl.pallas_call(
        _mm_slab_kernel,
        grid=(m // tm, n // tn),
        in_specs=[pl.BlockSpec((tm, k), lambda i, j: (i, 0)), pl.BlockSpec((k, tn), lambda i, j: (0, j))],
        out_specs=pl.BlockSpec((tn // LANES, tm, LANES), lambda i, j: (j, i, 0)),
        out_shape=jax.ShapeDtypeStruct((n // LANES, m, LANES), out_dtype),
        compiler_params=_params(("parallel", "parallel")),
        name="tail_proj",
    )(a, w)


def _mlstm_kernel(bias_ref, q_ref, k_ref, v_ref, o_ref, g_ref, cwq_ref, cwk_ref, ng_ref, out_ref,
                  qf, kf, qc, kc, st):
    hp = pl.program_id(1)
    U = HEADS_PER_STEP
    S = q_ref.shape[1]
    L = MLSTM_CHUNK
    nc = S // L
    d = HEAD_DIM
    PAD = 8

    R = min(S, 256)
    for u in range(U):
        qf[u, 0:PAD, :] = jnp.zeros((PAD, d), F32)
        kf[u, 0:PAD, :] = jnp.zeros((PAD, d), F32)
        qf[u, PAD:PAD + S, :] = q_ref[u].astype(F32)
        kf[u, PAD:PAD + S, :] = k_ref[u].astype(F32)
        ls = slice(u * d, (u + 1) * d)
        for r0 in range(0, S, R):
            aq = jnp.zeros((R, d), F32)
            ak = jnp.zeros((R, d), F32)
            for t in range(CONV_WIDTH):
                off = PAD - (CONV_WIDTH - 1) + t + r0
                aq = aq + cwq_ref[t:t + 1, ls] * qf[u, off:off + R, :]
                ak = ak + cwk_ref[t:t + 1, ls] * kf[u, off:off + R, :]
            qc[u, r0:r0 + R, :] = (aq * _sigmoid(aq) * (d ** -0.5)).astype(BF16)
            kc[u, r0:r0 + R, :] = ak * _sigmoid(ak)

    st[...] = jnp.zeros(st.shape, F32)
    row = lax.broadcasted_iota(I32, (L, L), 0)
    col = lax.broadcasted_iota(I32, (L, L), 1)
    causal = col <= row
    eye = col == row
    lane = lax.broadcasted_iota(I32, (L, d), 1)
    ones_col = jnp.where(lane == 0, 1.0, 0.0).astype(BF16)
    ng = ng_ref[...]

    def chunk_one(u, c, r, m):
        q = qc[u, pl.ds(r, L), :]
        kT = kc[u, pl.ds(r, L), :].T
        v = v_ref[u, pl.ds(r, L), :]
        vaug = jnp.concatenate([v, ones_col], axis=1)
        gates = g_ref[u, c]
        i_row = gates[0:1, :] + bias_ref[0, hp * U + u]
        f_row = gates[1:2, :] + bias_ref[1, hp * U + u]
        lf_row = jnp.minimum(f_row, 0.0) - jnp.log1p(jnp.exp(-jnp.abs(f_row)))
        b_col = jnp.sum(jnp.where(causal, lf_row, 0.0), axis=1, keepdims=True)
        b_row = jnp.sum(jnp.where(eye, b_col, 0.0), axis=0, keepdims=True)
        dlog = jnp.where(causal, b_col - b_row + i_row, NEG)
        inter = b_col + m
        m_t = jnp.maximum(inter, jnp.max(dlog, axis=1, keepdims=True))
        w_intra = jnp.exp(dlog - m_t)
        w_inter = jnp.exp(inter - m_t)
        s = _dot(q, kT.astype(BF16)) * w_intra
        res = w_inter * _dot(q, st[u].astype(BF16)) + _dot(s.astype(BF16), vaug)
        num = res[:, :d]
        den = res[:, d:d + 1]
        hh = num / jnp.maximum(jnp.abs(den), jnp.exp(-m_t))
        hn = hh * lax.rsqrt(jnp.mean(hh * hh, axis=1, keepdims=True) + EPS) * ng[:, u * d:(u + 1) * d]
        og = o_ref[u, pl.ds(r, L), :].astype(F32)
        out_ref[u, pl.ds(r, L), :] = (hn * _sigmoid(og)).astype(out_ref.dtype)
        ws_row = w_intra[L - 1:L, :]
        decay = w_inter[L - 1:L, :]
        st[u] = decay * st[u] + _dot((kT * ws_row).astype(BF16), vaug)
        return m_t[L - 1:L, :]

    def chunk(c, ms):
        r = pl.multiple_of(c * L, L)
        return tuple(chunk_one(u, c, r, ms[u]) for u in range(U))

    lax.fori_loop(0, nc, chunk, tuple(jnp.zeros((1, 1), F32) for _ in range(U)))


def _mlstm(proj_a, gates_t, bias_if, conv_w, norm_g, B, S):
    L = MLSTM_CHUNK
    nc = S // L
    d = HEAD_DIM
    H = N_HEADS
    U = HEADS_PER_STEP

    def slab(off):
        return pl.BlockSpec((U, S, d), lambda b, h: (off // U + h, b, 0))

    return pl.pallas_call(
        _mlstm_kernel,
        grid=(B, H // U),
        in_specs=[
            pl.BlockSpec(memory_space=pltpu.SMEM),
            slab(A_MQ), slab(A_MK), slab(A_MV), slab(A_MO),
            pl.BlockSpec((None, U, nc, 2, L), lambda b, h: (b, h, 0, 0, 0)),
            pl.BlockSpec((CONV_WIDTH, U * d), lambda b, h: (0, h)),
            pl.BlockSpec((CONV_WIDTH, U * d), lambda b, h: (0, H // U + h)),
            pl.BlockSpec((1, U * d), lambda b, h: (0, h)),
        ],
        out_specs=pl.BlockSpec((U, S, d), lambda b, h: (h, b, 0)),
        out_shape=jax.ShapeDtypeStruct((H, B * S, d), BF16),
        scratch_shapes=[
            pltpu.VMEM((U, S + 8, d), F32), pltpu.VMEM((U, S + 8, d), F32),
            pltpu.VMEM((U, S, d), BF16), pltpu.VMEM((U, S, d), F32),
            pltpu.VMEM((U, d, 2 * d), F32),
        ],
        compiler_params=_params(("parallel", "parallel")),
        name="mlstm",
    )(bias_if, proj_a, proj_a, proj_a, proj_a, gates_t, conv_w, conv_w, norm_g.reshape(1, H * d))


def _t5_thresholds():
    max_exact = N_BUCKETS // 2
    n = np.arange(0, 2 * MAX_DISTANCE)
    nf = np.maximum(n, 1).astype(np.float64)
    val = np.log(nf / max_exact) / math.log(MAX_DISTANCE / max_exact) * (N_BUCKETS - max_exact)
    frac = np.abs(val - np.round(val))
    frac_ok = (frac > 1e-4) | (n <= max_exact) | (n >= MAX_DISTANCE)
    assert frac_ok.all()
    large = np.minimum(max_exact + np.trunc(val).astype(np.int64), N_BUCKETS - 1)
    bucket = np.where(n < max_exact, n, large)
    assert (np.diff(bucket) >= 0).all() and bucket[MAX_DISTANCE] == N_BUCKETS - 1
    return [int(np.argmax(bucket >= j)) for j in range(1, N_BUCKETS)]


_T5_THR = _t5_thresholds()


def _dsa_kernel(rb_ref, q_ref, k_ref, v_ref, iq_ref, ik_ref, wt_ref, gq_ref, gk_ref, *rest,
                topk, q0, nb, chained):
    out_ref, kn, ikk, qm, wb, key_ref, mask_ref, bias_ref = rest[1:] if chained else rest
    qi = q0 + pl.program_id(1)
    T = DSA_BLOCK
    d = HEAD_DIM
    H = N_HEADS
    HALF = T // 2
    row = lax.broadcasted_iota(I32, (T, T), 0)
    col = lax.broadcasted_iota(I32, (T, T), 1)

    @pl.when(pl.program_id(1) == 0)
    def _per_batch():
        gk = gk_ref[...]

        def norm_k(h, _):
            kh = k_ref[h].astype(F32)
            kn[h] = (kh * lax.rsqrt(jnp.mean(kh * kh, axis=1, keepdims=True) + EPS) * gk).astype(BF16)
            for o in range(2):
                n = o * T + row - col
                val = jnp.full((T, T), rb_ref[0, h], F32)
                for j, thr in enumerate(_T5_THR):
                    val = jnp.where(n >= thr, rb_ref[j + 1, h], val)
                bias_ref[h, o] = val
            bias_ref[h, 2] = jnp.full((T, T), rb_ref[N_BUCKETS - 1, h], F32)
            return 0

        lax.fori_loop(0, H, norm_k, 0)
        ikk[...] = ik_ref[...].astype(BF16)

    lane = lax.broadcasted_iota(I32, (T, LANES), 1)
    wsc = wt_ref[...] * (IDX_HEADS ** -0.5 * IDX_DIM ** -0.5)
    for hp in range(IDX_HEADS // 2):
        qp = iq_ref[hp]
        zero = jnp.zeros_like(qp)
        qm[2 * hp] = jnp.where(lane < IDX_DIM, qp, zero)
        qm[2 * hp + 1] = jnp.where(lane >= IDX_DIM, qp, zero)
    for h16 in range(IDX_HEADS):
        wb[h16] = jnp.broadcast_to(wsc[:, h16:h16 + 1], (T, LANES))
    qm_all = qm[...].reshape(IDX_HEADS * T, LANES)

    def score_block(j, _):
        kk = ikk[pl.ds(pl.multiple_of(j * T, T), T), :]
        dots = _dot_nt(qm_all, kk)
        sc = jnp.zeros((T, T), F32)
        for h16 in range(IDX_HEADS):
            sc = sc + wb[h16] * jnp.maximum(dots[h16 * T:(h16 + 1) * T, :], 0.0)
        bits = lax.bitcast_convert_type(sc, I32)
        key = jnp.where(bits < 0, bits ^ jnp.int32(0x7FFFFFFF), bits)
        key = jnp.where(sc == 0.0, 0, key)
        key = jnp.where(j * T + col <= qi * T + row, key, INT_MIN)
        key_ref[j] = key
        return 0

    lax.fori_loop(0, nb, score_block, 0)

    kf = float(topk)
    ones = jnp.ones((LANES, LANES), BF16)

    def count(lo, pred):
        acc = jnp.zeros((HALF, LANES), F32)
        for jb in range(nb):
            acc = acc + jnp.where(pred(key_ref[jb, lo:lo + HALF, :]), 1.0, 0.0)
        return _dot(acc.astype(BF16), ones)

    halves = (0, HALF)
    zero_i = jnp.zeros((HALF, LANES), I32)
    thr0 = tuple(jnp.where(count(lo, lambda k: k >= zero_i) >= kf, jnp.int32(0), jnp.int32(INT_MIN))
                 for lo in halves)

    def bisect_bit(thrs, bit):
        out = []
        for lo, t in zip(halves, thrs):
            cand = t | bit
            out.append(jnp.where(count(lo, lambda k, c=cand: k >= c) >= kf, cand, t))
        return tuple(out)

    def bisect_pair(it, thrs):
        hi_bit = jnp.left_shift(jnp.int32(1), 29 - 2 * it)
        lo_bit = jnp.left_shift(jnp.int32(1), 28 - 2 * it)
        out = []
        for lo, t in zip(halves, thrs):
            c01, c10, c11 = t | lo_bit, t | hi_bit, t | hi_bit | lo_bit
            n01, n10, n11 = (count(lo, lambda k, c=c: k >= c) >= kf for c in (c01, c10, c11))
            out.append(jnp.where(n11, c11, jnp.where(n10, c10, jnp.where(n01, c01, t))))
        return tuple(out)

    thrs = bisect_bit(thr0, jnp.int32(1 << 30))
    thrs = lax.fori_loop(0, 15, bisect_pair, thrs)
    need = jnp.concatenate([kf - count(lo, lambda k, t=t: k > t) for lo, t in zip(halves, thrs)], axis=0)
    thrb = jnp.concatenate(thrs, axis=0)

    tri = jnp.where(row <= col, 1.0, 0.0).astype(BF16)
    seen = jnp.zeros((T, LANES), F32)
    for jb in range(nb):
        key = key_ref[jb]
        tie = key == thrb
        tie16 = jnp.where(tie, 1.0, 0.0).astype(BF16)
        pre = _dot(tie16, tri) + seen
        m = jnp.where(key > thrb, 0.0, jnp.where(tie, jnp.where(pre <= need, 0.0, NEG), NEG))
        mask_ref[jb] = jnp.where(key == INT_MIN, NEG, m)
        seen = seen + _dot(tie16, ones)

    gq = gq_ref[...]

    def head(h):
        qh = q_ref[h].astype(F32)
        qn = (qh * lax.rsqrt(jnp.mean(qh * qh, axis=1, keepdims=True) + EPS) * gq * (d ** -0.5)).astype(BF16)
        lgs = _dot_nt(qn, kn[h])
        mx = None
        lg_blocks = []
        for jb in range(nb):
            lg = lgs[:, jb * T:(jb + 1) * T] + bias_ref[h, jnp.clip(qi - jb, 0, 2)] + mask_ref[jb]
            lg_blocks.append(lg)
            mx = lg if mx is None else jnp.maximum(mx, lg)
        rowmax = jnp.max(mx, axis=1, keepdims=True)
        l = jnp.zeros((T, T), F32)
        ps = []
        for jb in range(nb):
            p = jnp.exp(lg_blocks[jb] - rowmax)
            l = l + p
            ps.append(p.astype(BF16))
        acc = _dot(jnp.concatenate(ps, axis=1), v_ref[h])
        out_ref[h] = (acc / jnp.sum(l, axis=1, keepdims=True)).astype(out_ref.dtype)

    def head_group(g, _):
        for u in range(HEADS_PER_STEP):
            head(g * HEADS_PER_STEP + u)
        return 0

    lax.fori_loop(0, H // HEADS_PER_STEP, head_group, 0)


def _dsa(proj_b, tail, q_norm_g, k_norm_g, rel_bias, B, S):
    T = DSA_BLOCK
    nq = S // T
    d = HEAD_DIM
    H = N_HEADS
    topk = min(TOPK_MAX, S // 4)
    per = nq // DSA_BANDS
    pb = proj_b.reshape(proj_b.shape[0], B, S, d)
    tl = tail.reshape(tail.shape[0], B, S, LANES)
    out = None
    for band in range(DSA_BANDS):
        q0 = band * per
        nb = q0 + per
        W = nb * T
        chained = out is not None
        in_specs = [
            pl.BlockSpec(memory_space=pltpu.SMEM),
            pl.BlockSpec((H, None, T, d), lambda b, i: (B_DQ // H, b, q0 + i, 0)),
            pl.BlockSpec((H, None, W, d), lambda b, i: (B_DK // H, b, 0, 0)),
            pl.BlockSpec((H, None, W, d), lambda b, i: (B_DV // H, b, 0, 0)),
            pl.BlockSpec((H, None, T, d), lambda b, i: (B_IQ // H, b, q0 + i, 0)),
            pl.BlockSpec((None, None, W, LANES), lambda b, i: (0, b, 0, 0)),
            pl.BlockSpec((None, None, T, LANES), lambda b, i: (1, b, q0 + i, 0)),
            pl.BlockSpec((1, d), lambda b, i: (0, 0)),
            pl.BlockSpec((1, d), lambda b, i: (0, 0)),
        ]
        args = [rel_bias, pb, pb, pb, pb, tl, tl, q_norm_g.reshape(1, d), k_norm_g.reshape(1, d)]
        if chained:
            in_specs.append(pl.BlockSpec(memory_space=pl.ANY))
            args.append(out)
        out = pl.pallas_call(
            functools.partial(_dsa_kernel, topk=topk, q0=q0, nb=nb, chained=chained),
            grid=(B, per),
            in_specs=in_specs,
            out_specs=pl.BlockSpec((H, None, T, d), lambda b, i: (0, b, q0 + i, 0)),
            out_shape=jax.ShapeDtypeStruct((H, B, S, d), BF16),
            scratch_shapes=[
                pltpu.VMEM((H, W, d), BF16),
                pltpu.VMEM((W, LANES), BF16),
                pltpu.VMEM((IDX_HEADS, T, LANES), BF16),
                pltpu.VMEM((IDX_HEADS, T, LANES), F32),
                pltpu.VMEM((nb, T, T), I32),
                pltpu.VMEM((nb, T, T), F32),
                pltpu.VMEM((H, 3, T, T), F32),
            ],
            input_output_aliases={len(args) - 1: 0} if chained else {},
            compiler_params=_params(("parallel", "arbitrary")),
            name=f"dsa_band{band}",
        )(*args)
    return out.reshape(H, B * S, d)


def _sb_kernel(q_ref, k_ref, v_ref, out_ref):
    qi = pl.program_id(2)
    U = HEADS_PER_STEP
    T = SB_BLOCK
    d = HEAD_DIM
    R = SB_ROWS
    ns = T // R
    row = lax.broadcasted_iota(I32, (T, T), 0)
    col = lax.broadcasted_iota(I32, (T, T), 1)
    upper = jnp.where(row > col, 1.0, 0.0).astype(BF16)
    srow = lax.broadcasted_iota(I32, (R, T), 0)
    scol = lax.broadcasted_iota(I32, (R, T), 1)
    units = [(u, s) for u in range(U) for s in range(ns)]
    qs = [(q_ref[u, s * R:(s + 1) * R, :].astype(F32) * (d ** -0.5)).astype(BF16) for u, s in units]

    def block(n, j, diag, run, acc):
        u, s = units[n]
        r = pl.multiple_of(j * T, T)
        z = _dot_nt(qs[n], k_ref[u, pl.ds(r, T), :])
        sp = jnp.maximum(z, 0.0) + jnp.log(1.0 + jnp.exp(-jnp.abs(z)))
        strict = scol < srow + s * R
        ln = jnp.where(strict, -sp, 0.0) if diag else -sp
        hi = ln.astype(BF16)
        lo = (ln - hi.astype(F32)).astype(BF16)
        both = _dot(jnp.concatenate([hi, lo], axis=0), upper)
        suf = both[:R] + both[R:]
        a = jnp.exp(z - sp + suf + run)
        if diag:
            a = jnp.where(strict, a, 0.0)
        acc = acc + _dot(a.astype(BF16), v_ref[u, pl.ds(r, T), :])
        return run + suf[:, 0:1] + ln[:, 0:1], acc

    state = []
    for n in range(len(units)):
        state.extend(block(n, qi, True, jnp.zeros((R, 1), F32), jnp.zeros((R, d), F32)))

    def earlier(i, st):
        out = []
        for n in range(len(units)):
            out.extend(block(n, qi - i, False, st[2 * n], st[2 * n + 1]))
        return tuple(out)

    state = lax.fori_loop(1, qi + 1, earlier, tuple(state))
    for n, (u, s) in enumerate(units):
        out_ref[u, s * R:(s + 1) * R, :] = state[2 * n + 1].astype(out_ref.dtype)


def _stick_breaking(proj_c, B, S):
    T = SB_BLOCK
    nq = S // T
    d = HEAD_DIM
    H = N_HEADS
    U = HEADS_PER_STEP
    return pl.pallas_call(
        _sb_kernel,
        grid=(B, H // U, nq),
        in_specs=[
            pl.BlockSpec((U, T, d), lambda b, h, i: (C_SQ // U + h, b * nq + i, 0)),
            pl.BlockSpec((U, S, d), lambda b, h, i: (C_SK // U + h, b, 0)),
            pl.BlockSpec((U, S, d), lambda b, h, i: (C_SV // U + h, b, 0)),
        ],
        out_specs=pl.BlockSpec((U, T, d), lambda b, h, i: (h, b * nq + i, 0)),
        out_shape=jax.ShapeDtypeStruct((H, B * S, d), BF16),
        compiler_params=_params(("parallel", "parallel", "parallel")),
        name="stick_breaking",
    )(proj_c, proj_c, proj_c)


def _merge_kernel(hm_ref, hd_ref, hs_ref, g0_ref, g1_ref, g2_ref, w_ref, o_ref, wb):
    H = N_HEADS

    @pl.when(pl.program_id(1) == 0)
    def _():
        wb[...] = w_ref[...].astype(BF16)

    acc = None
    ns = o_ref.shape[1] // LANES
    for n, (br, gr) in enumerate(((hm_ref, g0_ref), (hd_ref, g1_ref), (hs_ref, g2_ref))):
        a = jnp.concatenate([br[h] for h in range(H)], axis=1)
        up = _dot(a, wb[n])
        gate = jnp.concatenate([gr[s] for s in range(ns)], axis=1).astype(F32)
        term = _sigmoid(gate) * up
        acc = term if acc is None else acc + term
    o_ref[...] = acc.astype(o_ref.dtype)


def _merge(hm, hd, hs, proj_c, w_branch, layer, tm=512, tn=512):
    H, n, d = hm.shape
    D = w_branch.shape[3]
    ns = tn // LANES
    br = pl.BlockSpec((H, tm, d), lambda j, i: (0, i, 0))

    def gate(b):
        first = (C_GATE + b * (D // LANES)) // ns
        return pl.BlockSpec((ns, tm, LANES), lambda j, i: (first + j, i, 0))

    return pl.pallas_call(
        _merge_kernel,
        grid=(D // tn, n // tm),
        in_specs=[br, br, br, gate(0), gate(1), gate(2),
                  pl.BlockSpec((None, N_BRANCHES, H * d, tn), lambda j, i: (layer, 0, 0, j))],
        out_specs=pl.BlockSpec((tm, tn), lambda j, i: (i, j)),
        out_shape=jax.ShapeDtypeStruct((n, D), BF16),
        scratch_shapes=[pltpu.VMEM((N_BRANCHES, H * d, tn), BF16)],
        compiler_params=_params(("parallel", "arbitrary")),
        name="branch_merge",
    )(hm, hd, hs, proj_c, proj_c, proj_c, w_branch)


def _mm_res_kernel(a_ref, w_ref, x_ref, o_ref, wb):
    @pl.when(pl.program_id(1) == 0)
    def _():
        wb[...] = w_ref[...].astype(BF16)

    o_ref[...] = x_ref[...] + _dot(a_ref[...], wb[...])


def _matmul_residual(a, w, layer, x, tm=512, tn=512):
    m, k = a.shape
    n = w.shape[2]
    return pl.pallas_call(
        _mm_res_kernel,
        grid=(n // tn, m // tm),
        in_specs=[pl.BlockSpec((tm, k), lambda j, i: (i, 0)),
                  pl.BlockSpec((None, k, tn), lambda j, i: (layer, 0, j)),
                  pl.BlockSpec((tm, tn), lambda j, i: (i, j))],
        out_specs=pl.BlockSpec((tm, tn), lambda j, i: (i, j)),
        out_shape=jax.ShapeDtypeStruct((m, n), F32),
        scratch_shapes=[pltpu.VMEM((k, tn), BF16)],
        compiler_params=_params(("parallel", "arbitrary")),
        name="out_proj",
    )(a, w, x)


def _router_kernel(x_ref, g_ref, wr_ref, br_ref, xn_ref, ids_ref, wts_ref, cnt_ref, carry):
    i = pl.program_id(0)
    tm, D = x_ref.shape
    nchunk = D // LANES

    @pl.when(i == 0)
    def _():
        carry[...] = jnp.zeros(carry.shape, F32)

    x = x_ref[...]
    xn = x * lax.rsqrt(jnp.mean(x * x, axis=-1, keepdims=True) + EPS) * g_ref[...]
    for c in range(nchunk):
        xn_ref[:, c, :] = xn[:, c * LANES:(c + 1) * LANES]
    logits = jnp.dot(xn, wr_ref[...], preferred_element_type=F32, precision=lax.Precision.HIGHEST)
    biased = logits + br_ref[...]
    lane = lax.broadcasted_iota(I32, (tm, LANES), 1)
    lanef = lane.astype(F32)
    big = float(LANES)

    def first_lane(mask):
        return jnp.min(jnp.where(mask, lanef, big), axis=1, keepdims=True)

    gmask = lane < N_GROUPS
    gmax = jnp.max(jnp.where(gmask, biased, NEG), axis=1, keepdims=True)
    g_sel = first_lane(gmask & (biased == gmax))
    gm = jnp.max(jnp.where(gmask, logits, NEG), axis=1, keepdims=True)
    ge = jnp.where(gmask, jnp.exp(logits - gm), 0.0)
    p_group = jnp.sum(jnp.where(lanef == g_sel, ge, 0.0), axis=1, keepdims=True) / jnp.sum(ge, axis=1, keepdims=True)
    lo = N_GROUPS + EXPERTS_PER_GROUP * g_sel
    emask = (lanef >= lo) & (lanef < lo + EXPERTS_PER_GROUP)
    eb = jnp.where(emask, biased, NEG)
    e1 = first_lane(emask & (eb == jnp.max(eb, axis=1, keepdims=True)))
    emask2 = emask & (lanef != e1)
    eb2 = jnp.where(emask2, biased, NEG)
    e2 = first_lane(emask2 & (eb2 == jnp.max(eb2, axis=1, keepdims=True)))
    em = jnp.max(jnp.where(emask, logits, NEG), axis=1, keepdims=True)
    ee = jnp.where(emask, jnp.exp(logits - em), 0.0)
    s1 = jnp.sum(jnp.where(lanef == e1, ee, 0.0), axis=1, keepdims=True)
    s2 = jnp.sum(jnp.where(lanef == e2, ee, 0.0), axis=1, keepdims=True)
    se = jnp.sum(ee, axis=1, keepdims=True)
    w1 = s1 / se
    w2 = s2 / se
    wsum = w1 + w2
    w1 = p_group * w1 / wsum
    w2 = p_group * w2 / wsum
    x1 = e1 - N_GROUPS
    x2 = e2 - N_GROUPS

    onehot = jnp.where((lanef == x1) | (lanef == x2), 1.0, 0.0)
    r = lax.broadcasted_iota(I32, (tm, tm), 0)
    c = lax.broadcasted_iota(I32, (tm, tm), 1)
    lower = jnp.where(c < r, 1.0, 0.0).astype(BF16)
    prefix = _dot(lower, onehot.astype(BF16)) + carry[...]
    r1 = jnp.sum(jnp.where(lanef == x1, prefix, 0.0), axis=1, keepdims=True)
    r2 = jnp.sum(jnp.where(lanef == x2, prefix, 0.0), axis=1, keepdims=True)
    carry[...] = carry[...] + jnp.sum(onehot, axis=0, keepdims=True)
    cnt_ref[...] = carry[...].astype(I32)

    idsf = jnp.where(lane == 0, x1, jnp.where(lane == 1, x2, jnp.where(lane == 2, r1, jnp.where(lane == 3, r2, 0.0))))
    ids_ref[...] = idsf.astype(I32)
    wts_ref[...] = jnp.where(lane == 0, w1, jnp.where(lane == 1, w2, 0.0))


def _router(x2d, g, w_router, b_router, tm=512):
    n, D = x2d.shape
    return pl.pallas_call(
        _router_kernel,
        grid=(n // tm,),
        in_specs=[pl.BlockSpec((tm, D), lambda i: (i, 0)), pl.BlockSpec((1, D), lambda i: (0, 0)),
                  pl.BlockSpec((D, LANES), lambda i: (0, 0)), pl.BlockSpec((1, LANES), lambda i: (0, 0))],
        out_specs=[pl.BlockSpec((tm, D // LANES, LANES), lambda i: (i, 0, 0)),
                   pl.BlockSpec((tm, LANES), lambda i: (i, 0)),
                   pl.BlockSpec((tm, LANES), lambda i: (i, 0)),
                   pl.BlockSpec((1, LANES), lambda i: (0, 0))],
        out_shape=[jax.ShapeDtypeStruct((n, D // LANES, LANES), F32),
                   jax.ShapeDtypeStruct((n, LANES), I32),
                   jax.ShapeDtypeStruct((n, LANES), F32),
                   jax.ShapeDtypeStruct((1, LANES), I32)],
        scratch_shapes=[pltpu.VMEM((1, LANES), F32)],
        compiler_params=_params(("arbitrary",)),
        name="moe_router",
    )(x2d, g.reshape(1, D), w_router, b_router)


def _dispatch_kernel(p1_ref, p2_ref, xn_ref, xs_in_ref, xs_ref, sem, *, tb):
    del xs_in_ref
    base = pl.program_id(0) * tb

    def copies(t):
        return (pltpu.make_async_copy(xn_ref.at[t], xs_ref.at[p1_ref[base + t]], sem),
                pltpu.make_async_copy(xn_ref.at[t], xs_ref.at[p2_ref[base + t]], sem))

    def issue(t, _):
        for cp in copies(t):
            cp.start()
        return 0

    lax.fori_loop(0, tb, issue, 0)

    def drain(t, _):
        for cp in copies(t):
            cp.wait()
        return 0

    lax.fori_loop(0, tb, drain, 0)


def _dispatch(pos1, pos2, xn3, n_rows, tb=512):
    n, nchunk, _ = xn3.shape
    tb = min(tb, n)
    xs0 = jnp.zeros((n_rows, nchunk, LANES), F32)
    grid_spec = pltpu.PrefetchScalarGridSpec(
        num_scalar_prefetch=2,
        grid=(n // tb,),
        in_specs=[pl.BlockSpec((tb, nchunk, LANES), lambda i, p1, p2: (i, 0, 0)),
                  pl.BlockSpec(memory_space=pl.ANY)],
        out_specs=pl.BlockSpec(memory_space=pl.ANY),
        scratch_shapes=[pltpu.SemaphoreType.DMA(())],
    )
    return pl.pallas_call(
        functools.partial(_dispatch_kernel, tb=tb),
        grid_spec=grid_spec,
        out_shape=jax.ShapeDtypeStruct((n_rows, nchunk, LANES), F32),
        input_output_aliases={3: 0},
        compiler_params=pltpu.CompilerParams(dimension_semantics=("arbitrary",), has_side_effects=True),
        name="moe_dispatch",
    )(pos1, pos2, xn3, xs0)


def _expert_kernel(te_ref, nv_ref, xs_ref, wg_ref, wu_ref, wd_ref, ys_ref, wgb, wub, wdb):
    i = pl.program_id(0)
    nchunk = xs_ref.shape[1]
    prev = te_ref[jnp.maximum(i - 1, 0)]

    @pl.when(i < nv_ref[0])
    def _():
        @pl.when((i == 0) | (te_ref[i] != prev))
        def _():
            wgb[...] = wg_ref[...].astype(BF16)
            wub[...] = wu_ref[...].astype(BF16)
            wdb[...] = wd_ref[...].astype(BF16)

        x = jnp.concatenate([xs_ref[:, c, :] for c in range(nchunk)], axis=1).astype(BF16)
        g = _dot(x, wgb[...])
        u = _dot(x, wub[...])
        hcur = (g * _sigmoid(g) * u).astype(BF16)
        y = _dot(hcur, wdb[...])
        for c in range(nchunk):
            ys_ref[:, c, :] = y[:, c * LANES:(c + 1) * LANES]

    @pl.when(i >= nv_ref[0])
    def _():
        ys_ref[...] = jnp.zeros(ys_ref.shape, F32)


def _experts(tile_expert, n_valid, xs, w_gate, w_up, w_down, layer):
    n_rows, nchunk, _ = xs.shape
    _, E, D, Fe = w_gate.shape
    tm = MOE_TILE
    n_tiles = n_rows // tm

    def row_map(i, te, nv):
        return (jnp.minimum(i, jnp.maximum(nv[0] - 1, 0)), 0, 0)

    def w_map(i, te, nv):
        return (layer, te[i], 0, 0)

    grid_spec = pltpu.PrefetchScalarGridSpec(
        num_scalar_prefetch=2,
        grid=(n_tiles,),
        in_specs=[pl.BlockSpec((tm, nchunk, LANES), row_map),
                  pl.BlockSpec((None, None, D, Fe), w_map), pl.BlockSpec((None, None, D, Fe), w_map),
                  pl.BlockSpec((None, None, Fe, D), w_map)],
        out_specs=pl.BlockSpec((tm, nchunk, LANES), lambda i, te, nv: (i, 0, 0)),
        scratch_shapes=[pltpu.VMEM((D, Fe), BF16), pltpu.VMEM((D, Fe), BF16), pltpu.VMEM((Fe, D), BF16)],
    )
    return pl.pallas_call(
        _expert_kernel,
        grid_spec=grid_spec,
        out_shape=jax.ShapeDtypeStruct((n_rows, nchunk, LANES), F32),
        compiler_params=_params(("arbitrary",)),
        name="moe_experts",
    )(tile_expert, n_valid, xs, w_gate, w_up, w_down)


def _combine_kernel(p1_ref, p2_ref, ys_ref, x_ref, w_ref, o_ref, buf, sem):
    i = pl.program_id(0)
    tc = x_ref.shape[0]
    nchunk = buf.shape[2]
    base = i * tc

    def copies(t):
        return (pltpu.make_async_copy(ys_ref.at[p1_ref[base + t]], buf.at[0, t], sem),
                pltpu.make_async_copy(ys_ref.at[p2_ref[base + t]], buf.at[1, t], sem))

    def issue(t, _):
        for cp in copies(t):
            cp.start()
        return 0

    lax.fori_loop(0, tc, issue, 0)

    def drain(t, _):
        for cp in copies(t):
            cp.wait()
        return 0

    lax.fori_loop(0, tc, drain, 0)
    w = w_ref[...]
    w1 = w[:, 0:1]
    w2 = w[:, 1:2]
    for c in range(nchunk):
        sl = slice(c * LANES, (c + 1) * LANES)
        o_ref[:, sl] = x_ref[:, sl] + w1 * buf[0, :, c, :] + w2 * buf[1, :, c, :]


def _combine(pos1, pos2, ys, x2d, wts, tc=256):
    n, D = x2d.shape
    nchunk = D // LANES
    grid_spec = pltpu.PrefetchScalarGridSpec(
        num_scalar_prefetch=2,
        grid=(n // tc,),
        in_specs=[pl.BlockSpec(memory_space=pl.ANY),
                  pl.BlockSpec((tc, D), lambda i, p1, p2: (i, 0)),
                  pl.BlockSpec((tc, LANES), lambda i, p1, p2: (i, 0))],
        out_specs=pl.BlockSpec((tc, D), lambda i, p1, p2: (i, 0)),
        scratch_shapes=[pltpu.VMEM((2, tc, nchunk, LANES), F32), pltpu.SemaphoreType.DMA(())],
    )
    return pl.pallas_call(
        _combine_kernel,
        grid_spec=grid_spec,
        out_shape=jax.ShapeDtypeStruct((n, D), F32),
        compiler_params=_params(("arbitrary",)),
        name="moe_combine",
    )(pos1, pos2, ys, x2d, wts)


def _hier_moe(x2d, norm_g, w_rg, b_rg, w_re, b_re, w_gate, w_up, w_down, layer):
    n, D = x2d.shape
    tm = MOE_TILE
    pad = LANES - N_GROUPS - N_EXPERTS
    w_router = jnp.concatenate([w_rg, w_re, jnp.zeros((D, pad), F32)], axis=1)
    b_router = jnp.concatenate([b_rg, b_re.reshape(-1), jnp.zeros((pad,), F32)]).reshape(1, LANES)
    xn3, ids, wts, counts = _router(x2d, norm_g, w_router, b_router)
    cnt = counts[0, :N_EXPERTS]
    padded = ((cnt + tm - 1) // tm) * tm
    ends = jnp.cumsum(padded)
    offs = ends - padded
    n_rows = 2 * n + N_EXPERTS * tm
    n_tiles = n_rows // tm
    tile_start = jnp.arange(n_tiles, dtype=I32) * tm
    tile_expert = jnp.minimum(jnp.sum(tile_start[:, None] >= ends[None, :], axis=1), N_EXPERTS - 1).astype(I32)
    n_valid = (ends[-1] // tm).astype(I32).reshape(1)
    last_e = tile_expert[jnp.maximum(n_valid[0] - 1, 0)]
    tile_expert = jnp.where(jnp.arange(n_tiles) < n_valid[0], tile_expert, last_e)
    pos1 = (offs[ids[:, 0]] + ids[:, 2]).astype(I32)
    pos2 = (offs[ids[:, 1]] + ids[:, 3]).astype(I32)
    xs = _dispatch(pos1, pos2, xn3, n_rows)
    ys = _experts(tile_expert, n_valid, xs, w_gate, w_up, w_down, layer)
    return _combine(pos1, pos2, ys, x2d, wts)


def _in_proj_regions(D):
    bw = BRANCH_WIDTH
    sizes = (bw, bw, bw, bw, N_HEADS, N_HEADS, bw, bw, bw, IDX_HEADS * IDX_DIM, IDX_DIM, IDX_HEADS,
             bw, bw, bw, N_BRANCHES * D)
    offs = np.concatenate([[0], np.cumsum(sizes)]).tolist()
    region_a = (offs[0], offs[4] - offs[0])
    region_b = (offs[6], offs[10] - offs[6])
    region_c = (offs[12], offs[16] - offs[12])
    small = dict(mi=offs[4], mf=offs[5], ik=offs[10], iw=offs[11])
    return region_a, region_b, region_c, small


def _token_mixer(x2d, B, S, layer, norm_g, w_in, conv_w, b_i, b_f, mlstm_norm_g, q_norm_g, k_norm_g,
                 w_branch, w_out, rel_bias):
    n, D = x2d.shape
    L = MLSTM_CHUNK
    H = N_HEADS
    tm = min(n, 1024)
    xn = _rmsnorm(x2d, norm_g)
    ra, rb, rc, small = _in_proj_regions(D)
    proj_a = _in_proj(xn, w_in, layer, ra[0], ra[1], tm)
    proj_b = _in_proj(xn, w_in, layer, rb[0], rb[1], tm)
    proj_c = _in_proj(xn, w_in, layer, rc[0], rc[1], tm)
    ik = w_in[layer, :, small["ik"]:small["ik"] + IDX_DIM]
    iw = w_in[layer, :, small["iw"]:small["iw"] + IDX_HEADS]
    mi = w_in[layer, :, small["mi"]:small["mi"] + H]
    mf = w_in[layer, :, small["mf"]:small["mf"] + H]
    tail_pad = jnp.zeros((D, LANES - IDX_HEADS - 2 * H), w_in.dtype)
    w_tail = jnp.concatenate([ik, ik, iw, mi, mf, tail_pad], axis=1).astype(BF16)
    tail = _matmul_slabs(xn, w_tail, F32, tm=tm, tn=2 * LANES)
    g = tail[1][:, IDX_HEADS:IDX_HEADS + 2 * H].reshape(B, S // L, L, 2, H)
    gates_t = jnp.transpose(g, (0, 4, 1, 3, 2))
    hm = _mlstm(proj_a, gates_t, jnp.stack([b_i, b_f]), conv_w, mlstm_norm_g, B, S)
    hd = _dsa(proj_b, tail, q_norm_g, k_norm_g, rel_bias, B, S)
    hs = _stick_breaking(proj_c, B, S)
    merged = _merge(hm, hd, hs, proj_c, w_branch, layer)
    return _matmul_residual(merged, w_out, layer, x2d)


def kernel(x, norm1_g, w_in, conv_w, b_i, b_f, mlstm_norm_g, q_norm_g, k_norm_g, w_branch, w_out, norm2_g,
           w_router_g, b_router_g, w_router_e, b_router_e, w_gate, w_up, w_down, rel_bias):
    B, S, D = x.shape
    x2d = x.reshape(B * S, D)
    for l in range(w_in.shape[0]):
        x2d = _token_mixer(x2d, B, S, l, norm1_g[l], w_in, conv_w[l], b_i[l], b_f[l], mlstm_norm_g[l],
                           q_norm_g[l], k_norm_g[l], w_branch, w_out, rel_bias)
        x2d = _hier_moe(x2d, norm2_g[l], w_router_g[l], b_router_g[l], w_router_e[l], b_router_e[l],
                        w_gate, w_up, w_down, l)
    return x2d.reshape(B, S, D)
```

```python
import functools
import math

import numpy as np
import jax
import jax.numpy as jnp
from jax import lax
from jax.experimental import pallas as pl
from jax.experimental.pallas import tpu as pltpu

F32 = jnp.float32
BF16 = jnp.bfloat16
I32 = jnp.int32

LANES = 128
HEAD_DIM = 128
N_HEADS = 8
BRANCH_WIDTH = N_HEADS * HEAD_DIM
N_BRANCHES = 3
CONV_WIDTH = 4
IDX_HEADS = 16
IDX_DIM = 64
TOPK_MAX = 256
N_BUCKETS = 32
MAX_DISTANCE = 128
N_GROUPS = 4
EXPERTS_PER_GROUP = 8
N_EXPERTS = N_GROUPS * EXPERTS_PER_GROUP
EPS = 1e-6
NEG = -1e30
INT_MIN = -(2 ** 31)

MLSTM_CHUNK = 256
DSA_BLOCK = 128
DSA_BANDS = 4
SB_BLOCK = 256
SB_ROWS = 256
HEADS_PER_STEP = 2
MOE_TILE = 256
PROJ_TN = 512
PROJ_TM = 2048
VMEM_LIMIT = 56 * 1024 * 1024

A_MQ, A_MK, A_MV, A_MO = 0, 8, 16, 24
B_DQ, B_DK, B_DV, B_IQ = 0, 8, 16, 24
C_SQ, C_SK, C_SV, C_GATE = 0, 8, 16, 24


def _params(sem):
    return pltpu.CompilerParams(dimension_semantics=sem, vmem_limit_bytes=VMEM_LIMIT)


def _dot(a, b):
    return jnp.dot(a, b, preferred_element_type=F32)


def _dot_nt(a, b):
    return lax.dot_general(a, b, (((1,), (1,)), ((), ())), preferred_element_type=F32)


def _sigmoid(z):
    return 1.0 / (1.0 + jnp.exp(-z))


def _rmsnorm_kernel(x_ref, g_ref, o_ref):
    x = x_ref[...]
    ms = jnp.mean(x * x, axis=-1, keepdims=True)
    o_ref[...] = (x * lax.rsqrt(ms + EPS) * g_ref[...]).astype(o_ref.dtype)


def _rmsnorm(x2d, g, tm=512):
    n, d = x2d.shape
    return pl.pallas_call(
        _rmsnorm_kernel,
        grid=(n // tm,),
        in_specs=[pl.BlockSpec((tm, d), lambda i: (i, 0)), pl.BlockSpec((1, d), lambda i: (0, 0))],
        out_specs=pl.BlockSpec((tm, d), lambda i: (i, 0)),
        out_shape=jax.ShapeDtypeStruct((n, d), BF16),
        compiler_params=_params(("parallel",)),
        name="rmsnorm",
    )(x2d, g.reshape(1, d))


def _in_proj_kernel(x_ref, *rest, shift, nblk):
    w_refs, o_ref, wb = rest[:nblk], rest[nblk], rest[nblk + 1]
    tn = wb.shape[1]

    @pl.when(pl.program_id(1) == 0)
    def _():
        w = jnp.concatenate([r[...] for r in w_refs], axis=1)
        wb[...] = w[:, shift:shift + tn].astype(BF16)

    acc = _dot(x_ref[...], wb[...])
    for j in range(o_ref.shape[0]):
        o_ref[j] = acc[:, j * LANES:(j + 1) * LANES].astype(o_ref.dtype)


def _in_proj(xn, w_in, layer, col0, ncols, tm):
    m, k = xn.shape
    tn = PROJ_TN
    base, shift = divmod(col0, LANES)
    nblk = tn // LANES + (1 if shift else 0)
    per = tn // LANES

    def wspec(r):
        return pl.BlockSpec((None, k, LANES), lambda j, i: (layer, 0, base + per * j + r))

    return pl.pallas_call(
        functools.partial(_in_proj_kernel, shift=shift, nblk=nblk),
        grid=(ncols // tn, m // tm),
        in_specs=[pl.BlockSpec((tm, k), lambda j, i: (i, 0))] + [wspec(r) for r in range(nblk)],
        out_specs=pl.BlockSpec((per, tm, LANES), lambda j, i: (j, i, 0)),
        out_shape=jax.ShapeDtypeStruct((ncols // LANES, m, LANES), BF16),
        scratch_shapes=[pltpu.VMEM((k, tn), BF16)],
        compiler_params=_params(("parallel", "arbitrary")),
        name="in_proj",
    )(xn, *([w_in] * nblk))


def _mm_slab_kernel(a_ref, w_ref, o_ref):
    acc = _dot(a_ref[...], w_ref[...].astype(BF16))
    for j in range(o_ref.shape[0]):
        o_ref[j] = acc[:, j * LANES:(j + 1) * LANES].astype(o_ref.dtype)


def _matmul_slabs(a, w, out_dtype, tm, tn):
    m, k = a.shape
    n = w.shape[1]
    return pl.pallas_call(
        _mm_slab_kernel,
        grid=(m // tm, n // tn),
        in_specs=[pl.BlockSpec((tm, k), lambda i, j: (i, 0)), pl.BlockSpec((k, tn), lambda i, j: (0, j))],
        out_specs=pl.BlockSpec((tn // LANES, tm, LANES), lambda i, j: (j, i, 0)),
        out_shape=jax.ShapeDtypeStruct((n // LANES, m, LANES), out_dtype),
        compiler_params=_params(("parallel", "parallel")),
        name="tail_proj",
    )(a, w)


def _mlstm_kernel(bias_ref, q_ref, k_ref, v_ref, o_ref, g_ref, cwq_ref, cwk_ref, ng_ref, out_ref,
                  qf, kf, qc, kc, st):
    hp = pl.program_id(1)
    U = HEADS_PER_STEP
    S = q_ref.shape[1]
    L = MLSTM_CHUNK
    nc = S // L
    d = HEAD_DIM
    PAD = 8

    R = min(S, 256)
    for u in range(U):
        qf[u, 0:PAD, :] = jnp.zeros((PAD, d), F32)
        kf[u, 0:PAD, :] = jnp.zeros((PAD, d), F32)
        qf[u, PAD:PAD + S, :] = q_ref[u].astype(F32)
        kf[u, PAD:PAD + S, :] = k_ref[u].astype(F32)
        ls = slice(u * d, (u + 1) * d)
        for r0 in range(0, S, R):
            aq = jnp.zeros((R, d), F32)
            ak = jnp.zeros((R, d), F32)
            for t in range(CONV_WIDTH):
                off = PAD - (CONV_WIDTH - 1) + t + r0
                aq = aq + cwq_ref[t:t + 1, ls] * qf[u, off:off + R, :]
                ak = ak + cwk_ref[t:t + 1, ls] * kf[u, off:off + R, :]
            qc[u, r0:r0 + R, :] = (aq * _sigmoid(aq) * (d ** -0.5)).astype(BF16)
            kc[u, r0:r0 + R, :] = ak * _sigmoid(ak)

    st[...] = jnp.zeros(st.shape, F32)
    row = lax.broadcasted_iota(I32, (L, L), 0)
    col = lax.broadcasted_iota(I32, (L, L), 1)
    causal = col <= row
    eye = col == row
    lane = lax.broadcasted_iota(I32, (L, d), 1)
    ones_col = jnp.where(lane == 0, 1.0, 0.0).astype(BF16)
    ng = ng_ref[...]

    def chunk_one(u, c, r, m):
        q = qc[u, pl.ds(r, L), :]
        kT = kc[u, pl.ds(r, L), :].T
        v = v_ref[u, pl.ds(r, L), :]
        vaug = jnp.concatenate([v, ones_col], axis=1)
        gates = g_ref[u, c]
        i_row = gates[0:1, :] + bias_ref[0, hp * U + u]
        f_row = gates[1:2, :] + bias_ref[1, hp * U + u]
        lf_row = jnp.minimum(f_row, 0.0) - jnp.log1p(jnp.exp(-jnp.abs(f_row)))
        b_col = jnp.sum(jnp.where(causal, lf_row, 0.0), axis=1, keepdims=True)
        b_row = jnp.sum(jnp.where(eye, b_col, 0.0), axis=0, keepdims=True)
        dlog = jnp.where(causal, b_col - b_row + i_row, NEG)
        inter = b_col + m
        m_t = jnp.maximum(inter, jnp.max(dlog, axis=1, keepdims=True))
        w_intra = jnp.exp(dlog - m_t)
        w_inter = jnp.exp(inter - m_t)
        s = _dot(q, kT.astype(BF16)) * w_intra
        res = w_inter * _dot(q, st[u].astype(BF16)) + _dot(s.astype(BF16), vaug)
        num = res[:, :d]
        den = res[:, d:d + 1]
        hh = num / jnp.maximum(jnp.abs(den), jnp.exp(-m_t))
        hn = hh * lax.rsqrt(jnp.mean(hh * hh, axis=1, keepdims=True) + EPS) * ng[:, u * d:(u + 1) * d]
        og = o_ref[u, pl.ds(r, L), :].astype(F32)
        out_ref[u, pl.ds(r, L), :] = (hn * _sigmoid(og)).astype(out_ref.dtype)
        ws_row = w_intra[L - 1:L, :]
        decay = w_inter[L - 1:L, :]
        st[u] = decay * st[u] + _dot((kT * ws_row).astype(BF16), vaug)
        return m_t[L - 1:L, :]

    def chunk(c, ms):
        r = pl.multiple_of(c * L, L)
        return tuple(chunk_one(u, c, r, ms[u]) for u in range(U))

    lax.fori_loop(0, nc, chunk, tuple(jnp.zeros((1, 1), F32) for _ in range(U)))


def _mlstm(proj_a, gates_t, bias_if, conv_w, norm_g, B, S):
    L = MLSTM_CHUNK
    nc = S // L
    d = HEAD_DIM
    H = N_HEADS
    U = HEADS_PER_STEP

    def slab(off):
        return pl.BlockSpec((U, S, d), lambda b, h: (off // U + h, b, 0))

    return pl.pallas_call(
        _mlstm_kernel,
        grid=(B, H // U),
        in_specs=[
            pl.BlockSpec(memory_space=pltpu.SMEM),
            slab(A_MQ), slab(A_MK), slab(A_MV), slab(A_MO),
            pl.BlockSpec((None, U, nc, 2, L), lambda b, h: (b, h, 0, 0, 0)),
            pl.BlockSpec((CONV_WIDTH, U * d), lambda b, h: (0, h)),
            pl.BlockSpec((CONV_WIDTH, U * d), lambda b, h: (0, H // U + h)),
            pl.BlockSpec((1, U * d), lambda b, h: (0, h)),
        ],
        out_specs=pl.BlockSpec((U, S, d), lambda b, h: (h, b, 0)),
        out_shape=jax.ShapeDtypeStruct((H, B * S, d), BF16),
        scratch_shapes=[
            pltpu.VMEM((U, S + 8, d), F32), pltpu.VMEM((U, S + 8, d), F32),
            pltpu.VMEM((U, S, d), BF16), pltpu.VMEM((U, S, d), F32),
            pltpu.VMEM((U, d, 2 * d), F32),
        ],
        compiler_params=_params(("parallel", "parallel")),
        name="mlstm",
    )(bias_if, proj_a, proj_a, proj_a, proj_a, gates_t, conv_w, conv_w, norm_g.reshape(1, H * d))


def _t5_thresholds():
    max_exact = N_BUCKETS // 2
    n = np.arange(0, 2 * MAX_DISTANCE)
    nf = np.maximum(n, 1).astype(np.float64)
    val = np.log(nf / max_exact) / math.log(MAX_DISTANCE / max_exact) * (N_BUCKETS - max_exact)
    frac = np.abs(val - np.round(val))
    frac_ok = (frac > 1e-4) | (n <= max_exact) | (n >= MAX_DISTANCE)
    assert frac_ok.all()
    large = np.minimum(max_exact + np.trunc(val).astype(np.int64), N_BUCKETS - 1)
    bucket = np.where(n < max_exact, n, large)
    assert (np.diff(bucket) >= 0).all() and bucket[MAX_DISTANCE] == N_BUCKETS - 1
    return [int(np.argmax(bucket >= j)) for j in range(1, N_BUCKETS)]


_T5_THR = _t5_thresholds()


def _dsa_kernel(rb_ref, q_ref, k_ref, v_ref, iq_ref, ik_ref, wt_ref, gq_ref, gk_ref, prev_ref, out_ref,
                kn, ikk, wb, key_ref, mask_ref, bias_ref, *, topk, q0, nb):
    del prev_ref
    qi = q0 + pl.program_id(1)
    T = DSA_BLOCK
    d = HEAD_DIM
    H = N_HEADS
    HALF = T // 2
    row = lax.broadcasted_iota(I32, (T, T), 0)
    col = lax.broadcasted_iota(I32, (T, T), 1)

    @pl.when(pl.program_id(1) == 0)
    def _per_batch():
        gk = gk_ref[...]

        def norm_k(h, _):
            kh = k_ref[h].astype(F32)
            kn[h] = (kh * lax.rsqrt(jnp.mean(kh * kh, axis=1, keepdims=True) + EPS) * gk).astype(BF16)
            for o in range(2):
                n = o * T + row - col
                val = jnp.full((T, T), rb_ref[0, h], F32)
                for j, thr in enumerate(_T5_THR):
                    val = jnp.where(n >= thr, rb_ref[j + 1, h], val)
                bias_ref[h, o] = val
            bias_ref[h, 2] = jnp.full((T, T), rb_ref[N_BUCKETS - 1, h], F32)
            return 0

        lax.fori_loop(0, H, norm_k, 0)
        ik = ik_ref[...]
        klane = lax.broadcasted_iota(I32, ik.shape, 1)
        ikk[0] = jnp.where(klane < IDX_DIM, ik, 0.0).astype(BF16)
        ikk[1] = jnp.where(klane >= IDX_DIM, ik, 0.0).astype(BF16)

    wsc = wt_ref[...] * (IDX_HEADS ** -0.5 * IDX_DIM ** -0.5)
    for h16 in range(IDX_HEADS):
        wb[h16] = jnp.broadcast_to(wsc[:, h16:h16 + 1], (T, LANES))
    q_pairs = iq_ref[...].reshape((IDX_HEADS // 2) * T, LANES)

    def score_block(j, _):
        r = pl.multiple_of(j * T, T)
        kk = jnp.concatenate([ikk[0, pl.ds(r, T), :], ikk[1, pl.ds(r, T), :]], axis=0)
        dots = _dot_nt(q_pairs, kk)
        sc = jnp.zeros((T, T), F32)
        for h16 in range(IDX_HEADS):
            hp, odd = divmod(h16, 2)
            sc = sc + wb[h16] * jnp.maximum(dots[hp * T:(hp + 1) * T, odd * T:(odd + 1) * T], 0.0)
        bits = lax.bitcast_convert_type(sc, I32)
        key = jnp.where(bits < 0, bits ^ jnp.int32(0x7FFFFFFF), bits)
        key = jnp.where(sc == 0.0, 0, key)
        key = jnp.where(j * T + col <= qi * T + row, key, INT_MIN)
        key_ref[j] = key
        return 0

    lax.fori_loop(0, nb, score_block, 0, unroll=2)

    kf = float(topk)
    ones = jnp.ones((LANES, LANES), BF16)

    def count(lo, pred):
        acc = jnp.zeros((HALF, LANES), F32)
        for jb in range(nb):
            acc = acc + jnp.where(pred(key_ref[jb, lo:lo + HALF, :]), 1.0, 0.0)
        return _dot(acc.astype(BF16), ones)

    halves = (0, HALF)
    zero_i = jnp.zeros((HALF, LANES), I32)
    thr0 = tuple(jnp.where(count(lo, lambda k: k >= zero_i) >= kf, jnp.int32(0), jnp.int32(INT_MIN))
                 for lo in halves)

    def bisect_bit(thrs, bit):
        out = []
        for lo, t in zip(halves, thrs):
            cand = t | bit
            out.append(jnp.where(count(lo, lambda k, c=cand: k >= c) >= kf, cand, t))
        return tuple(out)

    def bisect_pair(it, thrs):
        hi_bit = jnp.left_shift(jnp.int32(1), 29 - 2 * it)
        lo_bit = jnp.left_shift(jnp.int32(1), 28 - 2 * it)
        out = []
        for lo, t in zip(halves, thrs):
            c01, c10, c11 = t | lo_bit, t | hi_bit, t | hi_bit | lo_bit
            n01, n10, n11 = (count(lo, lambda k, c=c: k >= c) >= kf for c in (c01, c10, c11))
            out.append(jnp.where(n11, c11, jnp.where(n10, c10, jnp.where(n01, c01, t))))
        return tuple(out)

    thrs = bisect_bit(thr0, jnp.int32(1 << 30))
    thrs = lax.fori_loop(0, 15, bisect_pair, thrs)
    need = jnp.concatenate([kf - count(lo, lambda k, t=t: k > t) for lo, t in zip(halves, thrs)], axis=0)
    thrb = jnp.concatenate(thrs, axis=0)

    tri = jnp.where(row <= col, 1.0, 0.0).astype(BF16)
    seen = jnp.zeros((T, LANES), F32)
    for jb in range(nb):
        key = key_ref[jb]
        tie = key == thrb
        tie16 = jnp.where(tie, 1.0, 0.0).astype(BF16)
        pre = _dot(tie16, tri) + seen
        m = jnp.where(key > thrb, 0.0, jnp.where(tie, jnp.where(pre <= need, 0.0, NEG), NEG))
        mask_ref[jb] = jnp.where(key == INT_MIN, NEG, m)
        seen = seen + _dot(tie16, ones)

    gq = gq_ref[...]

    def head(h):
        qh = q_ref[h].astype(F32)
        qn = (qh * lax.rsqrt(jnp.mean(qh * qh, axis=1, keepdims=True) + EPS) * gq * (d ** -0.5)).astype(BF16)
        lgs = _dot_nt(qn, kn[h])
        mx = None
        lg_blocks = []
        for jb in range(nb):
            lg = lgs[:, jb * T:(jb + 1) * T] + bias_ref[h, jnp.clip(qi - jb, 0, 2)] + mask_ref[jb]
            lg_blocks.append(lg)
            mx = lg if mx is None else jnp.maximum(mx, lg)
        rowmax = jnp.max(mx, axis=1, keepdims=True)
        l = jnp.zeros((T, T), F32)
        ps = []
        for jb in range(nb):
            p = jnp.exp(lg_blocks[jb] - rowmax)
            l = l + p
            ps.append(p.astype(BF16))
        acc = _dot(jnp.concatenate(ps, axis=1), v_ref[h])
        out_ref[h] = (acc / jnp.sum(l, axis=1, keepdims=True)).astype(out_ref.dtype)

    def head_group(g, _):
        for u in range(HEADS_PER_STEP):
            head(g * HEADS_PER_STEP + u)
        return 0

    lax.fori_loop(0, H // HEADS_PER_STEP, head_group, 0)


def _dsa(proj_b, tail, q_norm_g, k_norm_g, rel_bias, B, S):
    T = DSA_BLOCK
    nq = S // T
    d = HEAD_DIM
    H = N_HEADS
    topk = min(TOPK_MAX, S // 4)
    per = nq // DSA_BANDS
    pb = proj_b.reshape(proj_b.shape[0], B, S, d)
    tl = tail.reshape(tail.shape[0], B, S, LANES)
    out = jnp.zeros((H, B, S, d), BF16)
    for band in range(DSA_BANDS):
        q0 = band * per
        nb = q0 + per
        W = nb * T
        in_specs = [
            pl.BlockSpec(memory_space=pltpu.SMEM),
            pl.BlockSpec((H, None, T, d), lambda b, i: (B_DQ // H, b, q0 + i, 0)),
            pl.BlockSpec((H, None, W, d), lambda b, i: (B_DK // H, b, 0, 0)),
            pl.BlockSpec((H, None, W, d), lambda b, i: (B_DV // H, b, 0, 0)),
            pl.BlockSpec((H, None, T, d), lambda b, i: (B_IQ // H, b, q0 + i, 0)),
            pl.BlockSpec((None, None, W, LANES), lambda b, i: (0, b, 0, 0)),
            pl.BlockSpec((None, None, T, LANES), lambda b, i: (1, b, q0 + i, 0)),
            pl.BlockSpec((1, d), lambda b, i: (0, 0)),
            pl.BlockSpec((1, d), lambda b, i: (0, 0)),
            pl.BlockSpec(memory_space=pl.ANY),
        ]
        args = [rel_bias, pb, pb, pb, pb, tl, tl, q_norm_g.reshape(1, d), k_norm_g.reshape(1, d), out]
        out = pl.pallas_call(
            functools.partial(_dsa_kernel, topk=topk, q0=q0, nb=nb),
            grid=(B, per),
            in_specs=in_specs,
            out_specs=pl.BlockSpec((H, None, T, d), lambda b, i: (0, b, q0 + i, 0)),
            out_shape=jax.ShapeDtypeStruct((H, B, S, d), BF16),
            scratch_shapes=[
                pltpu.VMEM((H, W, d), BF16),
                pltpu.VMEM((2, W, LANES), BF16),
                pltpu.VMEM((IDX_HEADS, T, LANES), F32),
                pltpu.VMEM((nb, T, T), I32),
                pltpu.VMEM((nb, T, T), F32),
                pltpu.VMEM((H, 3, T, T), F32),
            ],
            input_output_aliases={len(args) - 1: 0},
            compiler_params=_params(("parallel", "arbitrary")),
            name=f"dsa_band{band}",
        )(*args)
    return out.reshape(H, B * S, d)


def _sb_kernel(q_ref, k_ref, v_ref, out_ref):
    qi = pl.program_id(2)
    U = HEADS_PER_STEP
    T = SB_BLOCK
    d = HEAD_DIM
    R = SB_ROWS
    ns = T // R
    row = lax.broadcasted_iota(I32, (T, T), 0)
    col = lax.broadcasted_iota(I32, (T, T), 1)
    upper = jnp.where(row > col, 1.0, 0.0).astype(BF16)
    srow = lax.broadcasted_iota(I32, (R, T), 0)
    scol = lax.broadcasted_iota(I32, (R, T), 1)
    units = [(u, s) for u in range(U) for s in range(ns)]
    qs = [(q_ref[u, s * R:(s + 1) * R, :].astype(F32) * (d ** -0.5)).astype(BF16) for u, s in units]

    def block(n, j, diag, run, acc):
        u, s = units[n]
        r = pl.multiple_of(j * T, T)
        z = _dot_nt(qs[n], k_ref[u, pl.ds(r, T), :])
        sp = jnp.maximum(z, 0.0) + jnp.log(1.0 + jnp.exp(-jnp.abs(z)))
        strict = scol < srow + s * R
        ln = jnp.where(strict, -sp, 0.0) if diag else -sp
        hi = ln.astype(BF16)
        lo = (ln - hi.astype(F32)).astype(BF16)
        both = _dot(jnp.concatenate([hi, lo], axis=0), upper)
        suf = both[:R] + both[R:]
        a = jnp.exp(z - sp + suf + run)
        if diag:
            a = jnp.where(strict, a, 0.0)
        acc = acc + _dot(a.astype(BF16), v_ref[u, pl.ds(r, T), :])
        return run + suf[:, 0:1] + ln[:, 0:1], acc

    state = []
    for n in range(len(units)):
        state.extend(block(n, qi, True, jnp.zeros((R, 1), F32), jnp.zeros((R, d), F32)))

    def earlier(i, st):
        out = []
        for n in range(len(units)):
            out.extend(block(n, qi - i, False, st[2 * n], st[2 * n + 1]))
        return tuple(out)

    state = lax.fori_loop(1, qi + 1, earlier, tuple(state))
    for n, (u, s) in enumerate(units):
        out_ref[u, s * R:(s + 1) * R, :] = state[2 * n + 1].astype(out_ref.dtype)


def _stick_breaking(proj_c, B, S):
    T = SB_BLOCK
    nq = S // T
    d = HEAD_DIM
    H = N_HEADS
    U = HEADS_PER_STEP
    return pl.pallas_call(
        _sb_kernel,
        grid=(B, H // U, nq),
        in_specs=[
            pl.BlockSpec((U, T, d), lambda b, h, i: (C_SQ // U + h, b * nq + i, 0)),
            pl.BlockSpec((U, S, d), lambda b, h, i: (C_SK // U + h, b, 0)),
            pl.BlockSpec((U, S, d), lambda b, h, i: (C_SV // U + h, b, 0)),
        ],
        out_specs=pl.BlockSpec((U, T, d), lambda b, h, i: (h, b * nq + i, 0)),
        out_shape=jax.ShapeDtypeStruct((H, B * S, d), BF16),
        compiler_params=_params(("parallel", "parallel", "parallel")),
        name="stick_breaking",
    )(proj_c, proj_c, proj_c)


def _merge_kernel(hm_ref, hd_ref, hs_ref, g0_ref, g1_ref, g2_ref, w_ref, o_ref, wb):
    H = N_HEADS

    @pl.when(pl.program_id(1) == 0)
    def _():
        wb[...] = w_ref[...].astype(BF16)

    acc = None
    ns = o_ref.shape[1] // LANES
    for n, (br, gr) in enumerate(((hm_ref, g0_ref), (hd_ref, g1_ref), (hs_ref, g2_ref))):
        a = jnp.concatenate([br[h] for h in range(H)], axis=1)
        up = _dot(a, wb[n])
        gate = jnp.concatenate([gr[s] for s in range(ns)], axis=1).astype(F32)
        term = _sigmoid(gate) * up
        acc = term if acc is None else acc + term
    o_ref[...] = acc.astype(o_ref.dtype)


def _merge(hm, hd, hs, proj_c, w_branch, layer, tm=512, tn=512):
    H, n, d = hm.shape
    D = w_branch.shape[3]
    ns = tn // LANES
    br = pl.BlockSpec((H, tm, d), lambda j, i: (0, i, 0))

    def gate(b):
        first = (C_GATE + b * (D // LANES)) // ns
        return pl.BlockSpec((ns, tm, LANES), lambda j, i: (first + j, i, 0))

    return pl.pallas_call(
        _merge_kernel,
        grid=(D // tn, n // tm),
        in_specs=[br, br, br, gate(0), gate(1), gate(2),
                  pl.BlockSpec((None, N_BRANCHES, H * d, tn), lambda j, i: (layer, 0, 0, j))],
        out_specs=pl.BlockSpec((tm, tn), lambda j, i: (i, j)),
        out_shape=jax.ShapeDtypeStruct((n, D), BF16),
        scratch_shapes=[pltpu.VMEM((N_BRANCHES, H * d, tn), BF16)],
        compiler_params=_params(("parallel", "arbitrary")),
        name="branch_merge",
    )(hm, hd, hs, proj_c, proj_c, proj_c, w_branch)


def _mm_res_kernel(a_ref, w_ref, x_ref, o_ref, wb):
    @pl.when(pl.program_id(1) == 0)
    def _():
        wb[...] = w_ref[...].astype(BF16)

    o_ref[...] = x_ref[...] + _dot(a_ref[...], wb[...])


def _matmul_residual(a, w, layer, x, tm=512, tn=512):
    m, k = a.shape
    n = w.shape[2]
    return pl.pallas_call(
        _mm_res_kernel,
        grid=(n // tn, m // tm),
        in_specs=[pl.BlockSpec((tm, k), lambda j, i: (i, 0)),
                  pl.BlockSpec((None, k, tn), lambda j, i: (layer, 0, j)),
                  pl.BlockSpec((tm, tn), lambda j, i: (i, j))],
        out_specs=pl.BlockSpec((tm, tn), lambda j, i: (i, j)),
        out_shape=jax.ShapeDtypeStruct((m, n), F32),
        scratch_shapes=[pltpu.VMEM((k, tn), BF16)],
        compiler_params=_params(("parallel", "arbitrary")),
        name="out_proj",
    )(a, w, x)


def _router_kernel(x_ref, g_ref, wr_ref, br_ref, xn_ref, ids_ref, wts_ref, cnt_ref, carry):
    i = pl.program_id(0)
    tm, D = x_ref.shape
    nchunk = D // LANES

    @pl.when(i == 0)
    def _():
        carry[...] = jnp.zeros(carry.shape, F32)

    x = x_ref[...]
    xn = x * lax.rsqrt(jnp.mean(x * x, axis=-1, keepdims=True) + EPS) * g_ref[...]
    for c in range(nchunk):
        xn_ref[:, c, :] = xn[:, c * LANES:(c + 1) * LANES]
    logits = jnp.dot(xn, wr_ref[...], preferred_element_type=F32, precision=lax.Precision.HIGHEST)
    biased = logits + br_ref[...]
    lane = lax.broadcasted_iota(I32, (tm, LANES), 1)
    lanef = lane.astype(F32)
    big = float(LANES)

    def first_lane(mask):
        return jnp.min(jnp.where(mask, lanef, big), axis=1, keepdims=True)

    gmask = lane < N_GROUPS
    gmax = jnp.max(jnp.where(gmask, biased, NEG), axis=1, keepdims=True)
    g_sel = first_lane(gmask & (biased == gmax))
    gm = jnp.max(jnp.where(gmask, logits, NEG), axis=1, keepdims=True)
    ge = jnp.where(gmask, jnp.exp(logits - gm), 0.0)
    p_group = jnp.sum(jnp.where(lanef == g_sel, ge, 0.0), axis=1, keepdims=True) / jnp.sum(ge, axis=1, keepdims=True)
    lo = N_GROUPS + EXPERTS_PER_GROUP * g_sel
    emask = (lanef >= lo) & (lanef < lo + EXPERTS_PER_GROUP)
    eb = jnp.where(emask, biased, NEG)
    e1 = first_lane(emask & (eb == jnp.max(eb, axis=1, keepdims=True)))
    emask2 = emask & (lanef != e1)
    eb2 = jnp.where(emask2, biased, NEG)
    e2 = first_lane(emask2 & (eb2 == jnp.max(eb2, axis=1, keepdims=True)))
    em = jnp.max(jnp.where(emask, logits, NEG), axis=1, keepdims=True)
    ee = jnp.where(emask, jnp.exp(logits - em), 0.0)
    s1 = jnp.sum(jnp.where(lanef == e1, ee, 0.0), axis=1, keepdims=True)
    s2 = jnp.sum(jnp.where(lanef == e2, ee, 0.0), axis=1, keepdims=True)
    se = jnp.sum(ee, axis=1, keepdims=True)
    w1 = s1 / se
    w2 = s2 / se
    wsum = w1 + w2
    w1 = p_group * w1 / wsum
    w2 = p_group * w2 / wsum
    x1 = e1 - N_GROUPS
    x2 = e2 - N_GROUPS

    onehot = jnp.where((lanef == x1) | (lanef == x2), 1.0, 0.0)
    r = lax.broadcasted_iota(I32, (tm, tm), 0)
    c = lax.broadcasted_iota(I32, (tm, tm), 1)
    lower = jnp.where(c < r, 1.0, 0.0).astype(BF16)
    prefix = _dot(lower, onehot.astype(BF16)) + carry[...]
    r1 = jnp.sum(jnp.where(lanef == x1, prefix, 0.0), axis=1, keepdims=True)
    r2 = jnp.sum(jnp.where(lanef == x2, prefix, 0.0), axis=1, keepdims=True)
    carry[...] = carry[...] + jnp.sum(onehot, axis=0, keepdims=True)
    cnt_ref[...] = carry[...].astype(I32)

    idsf = jnp.where(lane == 0, x1, jnp.where(lane == 1, x2, jnp.where(lane == 2, r1, jnp.where(lane == 3, r2, 0.0))))
    ids_ref[...] = idsf.astype(I32)
    wts_ref[...] = jnp.where(lane == 0, w1, jnp.where(lane == 1, w2, 0.0))


def _router(x2d, g, w_router, b_router, tm=512):
    n, D = x2d.shape
    return pl.pallas_call(
        _router_kernel,
        grid=(n // tm,),
        in_specs=[pl.BlockSpec((tm, D), lambda i: (i, 0)), pl.BlockSpec((1, D), lambda i: (0, 0)),
                  pl.BlockSpec((D, LANES), lambda i: (0, 0)), pl.BlockSpec((1, LANES), lambda i: (0, 0))],
        out_specs=[pl.BlockSpec((tm, D // LANES, LANES), lambda i: (i, 0, 0)),
                   pl.BlockSpec((tm, LANES), lambda i: (i, 0)),
                   pl.BlockSpec((tm, LANES), lambda i: (i, 0)),
                   pl.BlockSpec((1, LANES), lambda i: (0, 0))],
        out_shape=[jax.ShapeDtypeStruct((n, D // LANES, LANES), F32),
                   jax.ShapeDtypeStruct((n, LANES), I32),
                   jax.ShapeDtypeStruct((n, LANES), F32),
                   jax.ShapeDtypeStruct((1, LANES), I32)],
        scratch_shapes=[pltpu.VMEM((1, LANES), F32)],
        compiler_params=_params(("arbitrary",)),
        name="moe_router",
    )(x2d, g.reshape(1, D), w_router, b_router)


def _dispatch_kernel(p1_ref, p2_ref, xn_ref, xs_in_ref, xs_ref, sem, *, tb):
    del xs_in_ref
    base = pl.program_id(0) * tb

    def copies(t):
        return (pltpu.make_async_copy(xn_ref.at[t], xs_ref.at[p1_ref[base + t]], sem),
                pltpu.make_async_copy(xn_ref.at[t], xs_ref.at[p2_ref[base + t]], sem))

    def issue(t, _):
        for cp in copies(t):
            cp.start()
        return 0

    lax.fori_loop(0, tb, issue, 0)

    def drain(t, _):
        for cp in copies(t):
            cp.wait()
        return 0

    lax.fori_loop(0, tb, drain, 0)


def _dispatch(pos1, pos2, xn3, n_rows, tb=512):
    n, nchunk, _ = xn3.shape
    tb = min(tb, n)
    xs0 = jnp.zeros((n_rows, nchunk, LANES), F32)
    grid_spec = pltpu.PrefetchScalarGridSpec(
        num_scalar_prefetch=2,
        grid=(n // tb,),
        in_specs=[pl.BlockSpec((tb, nchunk, LANES), lambda i, p1, p2: (i, 0, 0)),
                  pl.BlockSpec(memory_space=pl.ANY)],
        out_specs=pl.BlockSpec(memory_space=pl.ANY),
        scratch_shapes=[pltpu.SemaphoreType.DMA(())],
    )
    return pl.pallas_call(
        functools.partial(_dispatch_kernel, tb=tb),
        grid_spec=grid_spec,
        out_shape=jax.ShapeDtypeStruct((n_rows, nchunk, LANES), F32),
        input_output_aliases={3: 0},
        compiler_params=pltpu.CompilerParams(dimension_semantics=("arbitrary",), has_side_effects=True),
        name="moe_dispatch",
    )(pos1, pos2, xn3, xs0)


def _expert_kernel(te_ref, nv_ref, xs_ref, wg_ref, wu_ref, wd_ref, ys_ref, wgb, wub, wdb):
    i = pl.program_id(0)
    nchunk = xs_ref.shape[1]
    prev = te_ref[jnp.maximum(i - 1, 0)]

    @pl.when(i < nv_ref[0])
    def _():
        @pl.when((i == 0) | (te_ref[i] != prev))
        def _():
            wgb[...] = wg_ref[...].astype(BF16)
            wub[...] = wu_ref[...].astype(BF16)
            wdb[...] = wd_ref[...].astype(BF16)

        x = jnp.concatenate([xs_ref[:, c, :] for c in range(nchunk)], axis=1).astype(BF16)
        g = _dot(x, wgb[...])
        u = _dot(x, wub[...])
        hcur = (g * _sigmoid(g) * u).astype(BF16)
        y = _dot(hcur, wdb[...])
        for c in range(nchunk):
            ys_ref[:, c, :] = y[:, c * LANES:(c + 1) * LANES]

    @pl.when(i >= nv_ref[0])
    def _():
        ys_ref[...] = jnp.zeros(ys_ref.shape, F32)


def _experts(tile_expert, n_valid, xs, w_gate, w_up, w_down, layer):
    n_rows, nchunk, _ = xs.shape
    _, E, D, Fe = w_gate.shape
    tm = MOE_TILE
    n_tiles = n_rows // tm

    def row_map(i, te, nv):
        return (jnp.minimum(i, jnp.maximum(nv[0] - 1, 0)), 0, 0)

    def w_map(i, te, nv):
        return (layer, te[i], 0, 0)

    grid_spec = pltpu.PrefetchScalarGridSpec(
        num_scalar_prefetch=2,
        grid=(n_tiles,),
        in_specs=[pl.BlockSpec((tm, nchunk, LANES), row_map),
                  pl.BlockSpec((None, None, D, Fe), w_map), pl.BlockSpec((None, None, D, Fe), w_map),
                  pl.BlockSpec((None, None, Fe, D), w_map)],
        out_specs=pl.BlockSpec((tm, nchunk, LANES), lambda i, te, nv: (i, 0, 0)),
        scratch_shapes=[pltpu.VMEM((D, Fe), BF16), pltpu.VMEM((D, Fe), BF16), pltpu.VMEM((Fe, D), BF16)],
    )
    return pl.pallas_call(
        _expert_kernel,
        grid_spec=grid_spec,
        out_shape=jax.ShapeDtypeStruct((n_rows, nchunk, LANES), F32),
        compiler_params=_params(("arbitrary",)),
        name="moe_experts",
    )(tile_expert, n_valid, xs, w_gate, w_up, w_down)


def _combine_kernel(p1_ref, p2_ref, ys_ref, x_ref, w_ref, o_ref, buf, sem):
    i = pl.program_id(0)
    tc = x_ref.shape[0]
    nchunk = buf.shape[2]
    base = i * tc

    def copies(t):
        return (pltpu.make_async_copy(ys_ref.at[p1_ref[base + t]], buf.at[0, t], sem),
                pltpu.make_async_copy(ys_ref.at[p2_ref[base + t]], buf.at[1, t], sem))

    def issue(t, _):
        for cp in copies(t):
            cp.start()
        return 0

    lax.fori_loop(0, tc, issue, 0)

    def drain(t, _):
        for cp in copies(t):
            cp.wait()
        return 0

    lax.fori_loop(0, tc, drain, 0)
    w = w_ref[...]
    w1 = w[:, 0:1]
    w2 = w[:, 1:2]
    for c in range(nchunk):
        sl = slice(c * LANES, (c + 1) * LANES)
        o_ref[:, sl] = x_ref[:, sl] + w1 * buf[0, :, c, :] + w2 * buf[1, :, c, :]


def _combine(pos1, pos2, ys, x2d, wts, tc=256):
    n, D = x2d.shape
    nchunk = D // LANES
    grid_spec = pltpu.PrefetchScalarGridSpec(
        num_scalar_prefetch=2,
        grid=(n // tc,),
        in_specs=[pl.BlockSpec(memory_space=pl.ANY),
                  pl.BlockSpec((tc, D), lambda i, p1, p2: (i, 0)),
                  pl.BlockSpec((tc, LANES), lambda i, p1, p2: (i, 0))],
        out_specs=pl.BlockSpec((tc, D), lambda i, p1, p2: (i, 0)),
        scratch_shapes=[pltpu.VMEM((2, tc, nchunk, LANES), F32), pltpu.SemaphoreType.DMA(())],
    )
    return pl.pallas_call(
        _combine_kernel,
        grid_spec=grid_spec,
        out_shape=jax.ShapeDtypeStruct((n, D), F32),
        compiler_params=_params(("arbitrary",)),
        name="moe_combine",
    )(pos1, pos2, ys, x2d, wts)


def _hier_moe(x2d, norm_g, w_rg, b_rg, w_re, b_re, w_gate, w_up, w_down, layer):
    n, D = x2d.shape
    tm = MOE_TILE
    pad = LANES - N_GROUPS - N_EXPERTS
    w_router = jnp.concatenate([w_rg, w_re, jnp.zeros((D, pad), F32)], axis=1)
    b_router = jnp.concatenate([b_rg, b_re.reshape(-1), jnp.zeros((pad,), F32)]).reshape(1, LANES)
    xn3, ids, wts, counts = _router(x2d, norm_g, w_router, b_router)
    cnt = counts[0, :N_EXPERTS]
    padded = ((cnt + tm - 1) // tm) * tm
    ends = jnp.cumsum(padded)
    offs = ends - padded
    n_rows = 2 * n + N_EXPERTS * tm
    n_tiles = n_rows // tm
    tile_start = jnp.arange(n_tiles, dtype=I32) * tm
    tile_expert = jnp.minimum(jnp.sum(tile_start[:, None] >= ends[None, :], axis=1), N_EXPERTS - 1).astype(I32)
    n_valid = (ends[-1] // tm).astype(I32).reshape(1)
    last_e = tile_expert[jnp.maximum(n_valid[0] - 1, 0)]
    tile_expert = jnp.where(jnp.arange(n_tiles) < n_valid[0], tile_expert, last_e)
    pos1 = (offs[ids[:, 0]] + ids[:, 2]).astype(I32)
    pos2 = (offs[ids[:, 1]] + ids[:, 3]).astype(I32)
    xs = _dispatch(pos1, pos2, xn3, n_rows)
    ys = _experts(tile_expert, n_valid, xs, w_gate, w_up, w_down, layer)
    return _combine(pos1, pos2, ys, x2d, wts)


def _in_proj_regions(D):
    bw = BRANCH_WIDTH
    sizes = (bw, bw, bw, bw, N_HEADS, N_HEADS, bw, bw, bw, IDX_HEADS * IDX_DIM, IDX_DIM, IDX_HEADS,
             bw, bw, bw, N_BRANCHES * D)
    offs = np.concatenate([[0], np.cumsum(sizes)]).tolist()
    region_a = (offs[0], offs[4] - offs[0])
    region_b = (offs[6], offs[10] - offs[6])
    region_c = (offs[12], offs[16] - offs[12])
    small = dict(mi=offs[4], mf=offs[5], ik=offs[10], iw=offs[11])
    return region_a, region_b, region_c, small


def _token_mixer(x2d, B, S, layer, norm_g, w_in, conv_w, b_i, b_f, mlstm_norm_g, q_norm_g, k_norm_g,
                 w_branch, w_out, rel_bias):
    n, D = x2d.shape
    L = MLSTM_CHUNK
    H = N_HEADS
    tm = min(n, PROJ_TM)
    xn = _rmsnorm(x2d, norm_g)
    ra, rb, rc, small = _in_proj_regions(D)
    proj_a = _in_proj(xn, w_in, layer, ra[0], ra[1], tm)
    proj_b = _in_proj(xn, w_in, layer, rb[0], rb[1], tm)
    proj_c = _in_proj(xn, w_in, layer, rc[0], rc[1], tm)
    ik = w_in[layer, :, small["ik"]:small["ik"] + IDX_DIM]
    iw = w_in[layer, :, small["iw"]:small["iw"] + IDX_HEADS]
    mi = w_in[layer, :, small["mi"]:small["mi"] + H]
    mf = w_in[layer, :, small["mf"]:small["mf"] + H]
    tail_pad = jnp.zeros((D, LANES - IDX_HEADS - 2 * H), w_in.dtype)
    w_tail = jnp.concatenate([ik, ik, iw, mi, mf, tail_pad], axis=1)
    tail = _matmul_slabs(xn, w_tail, F32, tm=tm, tn=2 * LANES)
    g = tail[1][:, IDX_HEADS:IDX_HEADS + 2 * H].reshape(B, S // L, L, 2, H)
    gates_t = jnp.transpose(g, (0, 4, 1, 3, 2))
    hm = _mlstm(proj_a, gates_t, jnp.stack([b_i, b_f]), conv_w, mlstm_norm_g, B, S)
    hd = _dsa(proj_b, tail, q_norm_g, k_norm_g, rel_bias, B, S)
    hs = _stick_breaking(proj_c, B, S)
    merged = _merge(hm, hd, hs, proj_c, w_branch, layer)
    return _matmul_residual(merged, w_out, layer, x2d)


def kernel(x, norm1_g, w_in, conv_w, b_i, b_f, mlstm_norm_g, q_norm_g, k_norm_g, w_branch, w_out, norm2_g,
           w_router_g, b_router_g, w_router_e, b_router_e, w_gate, w_up, w_down, rel_bias):
    B, S, D = x.shape
    x2d = x.reshape(B * S, D)
    for l in range(w_in.shape[0]):
        x2d = _token_mixer(x2d, B, S, l, norm1_g[l], w_in, conv_w[l], b_i[l], b_f[l], mlstm_norm_g[l],
                           q_norm_g[l], k_norm_g[l], w_branch, w_out, rel_bias)
        x2d = _hier_moe(x2d, norm2_g[l], w_router_g[l], b_router_g[l], w_router_e[l], b_router_e[l],
                        w_gate, w_up, w_down, l)
    return x2d.reshape(B, S, D)
```

```python
import functools
import math

import numpy as np
import jax
import jax.numpy as jnp
from jax import lax
from jax.experimental import pallas as pl
from jax.experimental.pallas import tpu as pltpu

F32 = jnp.float32
BF16 = jnp.bfloat16
I32 = jnp.int32

LANES = 128
HEAD_DIM = 128
N_HEADS = 8
BRANCH_WIDTH = N_HEADS * HEAD_DIM
N_BRANCHES = 3
CONV_WIDTH = 4
IDX_HEADS = 16
IDX_DIM = 64
TOPK_MAX = 256
N_BUCKETS = 32
MAX_DISTANCE = 128
N_GROUPS = 4
EXPERTS_PER_GROUP = 8
N_EXPERTS = N_GROUPS * EXPERTS_PER_GROUP
EPS = 1e-6
NEG = -1e30
INT_MIN = -(2 ** 31)

MLSTM_CHUNK = 256
DSA_BLOCK = 128
DSA_BANDS = 4
SB_BLOCK = 256
SB_ROWS = 256
SB_PIECES = 1
SB_HEADS = 4
HEADS_PER_STEP = 2
MOE_TILE = 256
PROJ_TN = 512
PROJ_TM = 2048
VMEM_LIMIT = 56 * 1024 * 1024

A_MQ, A_MK, A_MV, A_MO = 0, 8, 16, 24
B_DQ, B_DK, B_DV, B_IQ = 0, 8, 16, 24
C_SQ, C_SK, C_SV, C_GATE = 0, 8, 16, 24


def _params(sem):
    return pltpu.CompilerParams(dimension_semantics=sem, vmem_limit_bytes=VMEM_LIMIT)


def _dot(a, b):
    return jnp.dot(a, b, preferred_element_type=F32)


def _dot_nt(a, b):
    return lax.dot_general(a, b, (((1,), (1,)), ((), ())), preferred_element_type=F32)


def _sigmoid(z):
    return 1.0 / (1.0 + jnp.exp(-z))


def _rmsnorm_kernel(x_ref, g_ref, o_ref):
    x = x_ref[...]
    ms = jnp.mean(x * x, axis=-1, keepdims=True)
    o_ref[...] = (x * lax.rsqrt(ms + EPS) * g_ref[...]).astype(o_ref.dtype)


def _rmsnorm(x2d, g, tm=512):
    n, d = x2d.shape
    return pl.pallas_call(
        _rmsnorm_kernel,
        grid=(n // tm,),
        in_specs=[pl.BlockSpec((tm, d), lambda i: (i, 0)), pl.BlockSpec((1, d), lambda i: (0, 0))],
        out_specs=pl.BlockSpec((tm, d), lambda i: (i, 0)),
        out_shape=jax.ShapeDtypeStruct((n, d), BF16),
        compiler_params=_params(("parallel",)),
        name="rmsnorm",
    )(x2d, g.reshape(1, d))


def _in_proj_kernel(x_ref, *rest, shift, nblk):
    w_refs, o_ref, wb = rest[:nblk], rest[nblk], rest[nblk + 1]
    tn = wb.shape[1]

    @pl.when(pl.program_id(1) == 0)
    def _():
        w = jnp.concatenate([r[...] for r in w_refs], axis=0)
        wb[...] = w[shift:shift + tn, :].T.astype(BF16)

    acc = _dot(x_ref[...], wb[...])
    for j in range(o_ref.shape[0]):
        o_ref[j] = acc[:, j * LANES:(j + 1) * LANES].astype(o_ref.dtype)


def _in_proj(xn, w_t, layer, col0, ncols, tm):
    m, k = xn.shape
    tn = PROJ_TN
    base, shift = divmod(col0, LANES)
    assert shift % 8 == 0
    nblk = tn // LANES + (1 if shift else 0)
    per = tn // LANES

    def wspec(r):
        return pl.BlockSpec((None, LANES, k), lambda j, i: (layer, base + per * j + r, 0))

    return pl.pallas_call(
        functools.partial(_in_proj_kernel, shift=shift, nblk=nblk),
        grid=(ncols // tn, m // tm),
        in_specs=[pl.BlockSpec((tm, k), lambda j, i: (i, 0))] + [wspec(r) for r in range(nblk)],
        out_specs=pl.BlockSpec((per, tm, LANES), lambda j, i: (j, i, 0)),
        out_shape=jax.ShapeDtypeStruct((ncols // LANES, m, LANES), BF16),
        scratch_shapes=[pltpu.VMEM((k, tn), BF16)],
        compiler_params=_params(("parallel", "arbitrary")),
        name="in_proj",
    )(xn, *([w_t] * nblk))


def _tail_kernel(x_ref, wg_ref, wi_ref, o_ref, wt, *, g_lane, ik_lane, iw_lane):
    @pl.when(pl.program_id(0) == 0)
    def _():
        wg = wg_ref[...]
        wi = wi_ref[...]
        ik = wi[ik_lane:ik_lane + IDX_DIM, :]
        iw = wi[iw_lane:iw_lane + IDX_HEADS, :]
        gates = wg[g_lane:g_lane + 2 * N_HEADS, :]
        pad = jnp.zeros((LANES - IDX_HEADS - 2 * N_HEADS, wg.shape[1]), F32)
        wt[...] = jnp.concatenate([ik, ik, iw, gates, pad], axis=0).T.astype(BF16)

    acc = _dot(x_ref[...], wt[...])
    o_ref[0] = acc[:, :LANES]
    o_ref[1] = acc[:, LANES:]


def _tail_proj(xn, w_t, layer, small, tm):
    m, k = xn.shape
    g_blk, g_lane = divmod(small["mi"], LANES)
    i_blk, ik_lane = divmod(small["ik"], LANES)
    iw_lane = small["iw"] - i_blk * LANES
    assert small["mf"] == small["mi"] + N_HEADS and g_lane + 2 * N_HEADS <= LANES
    assert ik_lane + IDX_DIM <= LANES and 0 <= iw_lane and iw_lane + IDX_HEADS <= LANES
    assert g_lane % 8 == 0 and ik_lane % 8 == 0 and iw_lane % 8 == 0
    return pl.pallas_call(
        functools.partial(_tail_kernel, g_lane=g_lane, ik_lane=ik_lane, iw_lane=iw_lane),
        grid=(m // tm,),
        in_specs=[pl.BlockSpec((tm, k), lambda i: (i, 0)),
                  pl.BlockSpec((None, LANES, k), lambda i: (layer, g_blk, 0)),
                  pl.BlockSpec((None, LANES, k), lambda i: (layer, i_blk, 0))],
        out_specs=pl.BlockSpec((2, tm, LANES), lambda i: (0, i, 0)),
        out_shape=jax.ShapeDtypeStruct((2, m, LANES), F32),
        scratch_shapes=[pltpu.VMEM((k, 2 * LANES), BF16)],
        compiler_params=_params(("arbitrary",)),
        name="tail_proj",
    )(xn, w_t, w_t)


def _mlstm_kernel(bias_ref, q_ref, k_ref, v_ref, o_ref, g_ref, cwq_ref, cwk_ref, ng_ref, out_ref,
                  qf, kf, qc, kc, st):
    hp = pl.program_id(1)
    U = HEADS_PER_STEP
    S = q_ref.shape[1]
    L = MLSTM_CHUNK
    nc = S // L
    d = HEAD_DIM
    PAD = 8

    R = min(S, 256)
    for u in range(U):
        qf[u, 0:PAD, :] = jnp.zeros((PAD, d), F32)
        kf[u, 0:PAD, :] = jnp.zeros((PAD, d), F32)
        qf[u, PAD:PAD + S, :] = q_ref[u].astype(F32)
        kf[u, PAD:PAD + S, :] = k_ref[u].astype(F32)
        ls = slice(u * d, (u + 1) * d)
        for r0 in range(0, S, R):
            aq = jnp.zeros((R, d), F32)
            ak = jnp.zeros((R, d), F32)
            for t in range(CONV_WIDTH):
                off = PAD - (CONV_WIDTH - 1) + t + r0
                aq = aq + cwq_ref[t:t + 1, ls] * qf[u, off:off + R, :]
                ak = ak + cwk_ref[t:t + 1, ls] * kf[u, off:off + R, :]
            qc[u, r0:r0 + R, :] = (aq * _sigmoid(aq) * (d ** -0.5)).astype(BF16)
            kc[u, r0:r0 + R, :] = ak * _sigmoid(ak)

    st[...] = jnp.zeros(st.shape, F32)
    row = lax.broadcasted_iota(I32, (L, L), 0)
    col = lax.broadcasted_iota(I32, (L, L), 1)
    causal = col <= row
    eye = col == row
    lane = lax.broadcasted_iota(I32, (L, d), 1)
    ones_col = jnp.where(lane == 0, 1.0, 0.0).astype(BF16)
    ng = ng_ref[...]

    def chunk_one(u, c, r, m):
        q = qc[u, pl.ds(r, L), :]
        kT = kc[u, pl.ds(r, L), :].T
        v = v_ref[u, pl.ds(r, L), :]
        vaug = jnp.concatenate([v, ones_col], axis=1)
        gates = g_ref[u, c]
        i_row = gates[0:1, :] + bias_ref[0, hp * U + u]
        f_row = gates[1:2, :] + bias_ref[1, hp * U + u]
        lf_row = jnp.minimum(f_row, 0.0) - jnp.log1p(jnp.exp(-jnp.abs(f_row)))
        b_col = jnp.sum(jnp.where(causal, lf_row, 0.0), axis=1, keepdims=True)
        b_row = jnp.sum(jnp.where(eye, b_col, 0.0), axis=0, keepdims=True)
        dlog = jnp.where(causal, b_col - b_row + i_row, NEG)
        inter = b_col + m
        m_t = jnp.maximum(inter, jnp.max(dlog, axis=1, keepdims=True))
        w_intra = jnp.exp(dlog - m_t)
        w_inter = jnp.exp(inter - m_t)
        s = _dot(q, kT.astype(BF16)) * w_intra
        res = w_inter * _dot(q, st[u].astype(BF16)) + _dot(s.astype(BF16), vaug)
        num = res[:, :d]
        den = res[:, d:d + 1]
        hh = num / jnp.maximum(jnp.abs(den), jnp.exp(-m_t))
        hn = hh * lax.rsqrt(jnp.mean(hh * hh, axis=1, keepdims=True) + EPS) * ng[:, u * d:(u + 1) * d]
        og = o_ref[u, pl.ds(r, L), :].astype(F32)
        out_ref[u, pl.ds(r, L), :] = (hn * _sigmoid(og)).astype(out_ref.dtype)
        ws_row = w_intra[L - 1:L, :]
        decay = w_inter[L - 1:L, :]
        st[u] = decay * st[u] + _dot((kT * ws_row).astype(BF16), vaug)
        return m_t[L - 1:L, :]

    def chunk(c, ms):
        r = pl.multiple_of(c * L, L)
        return tuple(chunk_one(u, c, r, ms[u]) for u in range(U))

    lax.fori_loop(0, nc, chunk, tuple(jnp.zeros((1, 1), F32) for _ in range(U)))


def _mlstm(proj_a, gates_t, bias_if, conv_w, norm_g, B, S):
    L = MLSTM_CHUNK
    nc = S // L
    d = HEAD_DIM
    H = N_HEADS
    U = HEADS_PER_STEP

    def slab(off):
        return pl.BlockSpec((U, S, d), lambda b, h: (off // U + h, b, 0))

    return pl.pallas_call(
        _mlstm_kernel,
        grid=(B, H // U),
        in_specs=[
            pl.BlockSpec(memory_space=pltpu.SMEM),
            slab(A_MQ), slab(A_MK), slab(A_MV), slab(A_MO),
            pl.BlockSpec((None, U, nc, 2, L), lambda b, h: (b, h, 0, 0, 0)),
            pl.BlockSpec((CONV_WIDTH, U * d), lambda b, h: (0, h)),
            pl.BlockSpec((CONV_WIDTH, U * d), lambda b, h: (0, H // U + h)),
            pl.BlockSpec((1, U * d), lambda b, h: (0, h)),
        ],
        out_specs=pl.BlockSpec((U, S, d), lambda b, h: (h, b, 0)),
        out_shape=jax.ShapeDtypeStruct((H, B * S, d), BF16),
        scratch_shapes=[
            pltpu.VMEM((U, S + 8, d), F32), pltpu.VMEM((U, S + 8, d), F32),
            pltpu.VMEM((U, S, d), BF16), pltpu.VMEM((U, S, d), F32),
            pltpu.VMEM((U, d, 2 * d), F32),
        ],
        compiler_params=_params(("parallel", "parallel")),
        name="mlstm",
    )(bias_if, proj_a, proj_a, proj_a, proj_a, gates_t, conv_w, conv_w, norm_g.reshape(1, H * d))


def _t5_thresholds():
    max_exact = N_BUCKETS // 2
    n = np.arange(0, 2 * MAX_DISTANCE)
    nf = np.maximum(n, 1).astype(np.float64)
    val = np.log(nf / max_exact) / math.log(MAX_DISTANCE / max_exact) * (N_BUCKETS - max_exact)
    frac = np.abs(val - np.round(val))
    frac_ok = (frac > 1e-4) | (n <= max_exact) | (n >= MAX_DISTANCE)
    assert frac_ok.all()
    large = np.minimum(max_exact + np.trunc(val).astype(np.int64), N_BUCKETS - 1)
    bucket = np.where(n < max_exact, n, large)
    assert (np.diff(bucket) >= 0).all() and bucket[MAX_DISTANCE] == N_BUCKETS - 1
    return [int(np.argmax(bucket >= j)) for j in range(1, N_BUCKETS)]


_T5_THR = _t5_thresholds()


def _dsa_kernel(rb_ref, q_ref, k_ref, v_ref, iq_ref, ik_ref, wt_ref, gq_ref, gk_ref, prev_ref, out_ref,
                kn, ikk, wb, key_ref, mask_ref, bias_ref, *, topk, q0, nb):
    del prev_ref
    qi = q0 + pl.program_id(1)
    T = DSA_BLOCK
    d = HEAD_DIM
    H = N_HEADS
    HALF = T // 2
    row = lax.broadcasted_iota(I32, (T, T), 0)
    col = lax.broadcasted_iota(I32, (T, T), 1)

    @pl.when(pl.program_id(1) == 0)
    def _per_batch():
        gk = gk_ref[...]

        def norm_k(h, _):
            kh = k_ref[h].astype(F32)
            kn[h] = (kh * lax.rsqrt(jnp.mean(kh * kh, axis=1, keepdims=True) + EPS) * gk).astype(BF16)
            for o in range(2):
                n = o * T + row - col
                val = jnp.full((T, T), rb_ref[0, h], F32)
                for j, thr in enumerate(_T5_THR):
                    val = jnp.where(n >= thr, rb_ref[j + 1, h], val)
                bias_ref[h, o] = val
            bias_ref[h, 2] = jnp.full((T, T), rb_ref[N_BUCKETS - 1, h], F32)
            return 0

        lax.fori_loop(0, H, norm_k, 0)
        ik = ik_ref[...]
        klane = lax.broadcasted_iota(I32, ik.shape, 1)
        ikk[0] = jnp.where(klane < IDX_DIM, ik, 0.0).astype(BF16)
        ikk[1] = jnp.where(klane >= IDX_DIM, ik, 0.0).astype(BF16)

    wsc = wt_ref[...] * (IDX_HEADS ** -0.5 * IDX_DIM ** -0.5)
    for h16 in range(IDX_HEADS):
        wb[h16] = jnp.broadcast_to(wsc[:, h16:h16 + 1], (T, LANES))
    q_pairs = iq_ref[...].reshape((IDX_HEADS // 2) * T, LANES)

    def score_block(j, _):
        r = pl.multiple_of(j * T, T)
        kk = jnp.concatenate([ikk[0, pl.ds(r, T), :], ikk[1, pl.ds(r, T), :]], axis=0)
        dots = _dot_nt(q_pairs, kk)
        sc = jnp.zeros((T, T), F32)
        for h16 in range(IDX_HEADS):
            hp, odd = divmod(h16, 2)
            sc = sc + wb[h16] * jnp.maximum(dots[hp * T:(hp + 1) * T, odd * T:(odd + 1) * T], 0.0)
        bits = lax.bitcast_convert_type(sc, I32)
        key = jnp.where(bits < 0, bits ^ jnp.int32(0x7FFFFFFF), bits)
        key = jnp.where(sc == 0.0, 0, key)
        key = jnp.where(j * T + col <= qi * T + row, key, INT_MIN)
        key_ref[j] = key
        return 0

    lax.fori_loop(0, nb, score_block, 0, unroll=2)

    kf = float(topk)
    ones = jnp.ones((LANES, LANES), BF16)

    def count(lo, pred):
        acc = jnp.zeros((HALF, LANES), F32)
        for jb in range(nb):
            acc = acc + jnp.where(pred(key_ref[jb, lo:lo + HALF, :]), 1.0, 0.0)
        return _dot(acc.astype(BF16), ones)

    halves = (0, HALF)
    zero_i = jnp.zeros((HALF, LANES), I32)
    thr0 = tuple(jnp.where(count(lo, lambda k: k >= zero_i) >= kf, jnp.int32(0), jnp.int32(INT_MIN))
                 for lo in halves)

    def bisect_bit(thrs, bit):
        out = []
        for lo, t in zip(halves, thrs):
            cand = t | bit
            out.append(jnp.where(count(lo, lambda k, c=cand: k >= c) >= kf, cand, t))
        return tuple(out)

    def bisect_pair(it, thrs):
        hi_bit = jnp.left_shift(jnp.int32(1), 29 - 2 * it)
        lo_bit = jnp.left_shift(jnp.int32(1), 28 - 2 * it)
        out = []
        for lo, t in zip(halves, thrs):
            c01, c10, c11 = t | lo_bit, t | hi_bit, t | hi_bit | lo_bit
            n01, n10, n11 = (count(lo, lambda k, c=c: k >= c) >= kf for c in (c01, c10, c11))
            out.append(jnp.where(n11, c11, jnp.where(n10, c10, jnp.where(n01, c01, t))))
        return tuple(out)

    thrs = bisect_bit(thr0, jnp.int32(1 << 30))
    thrs = lax.fori_loop(0, 15, bisect_pair, thrs)
    need = jnp.concatenate([kf - count(lo, lambda k, t=t: k > t) for lo, t in zip(halves, thrs)], axis=0)
    thrb = jnp.concatenate(thrs, axis=0)

    tri = jnp.where(row <= col, 1.0, 0.0).astype(BF16)
    seen = jnp.zeros((T, LANES), F32)
    for jb in range(nb):
        key = key_ref[jb]
        tie = key == thrb
        tie16 = jnp.where(tie, 1.0, 0.0).astype(BF16)
        pre = _dot(tie16, tri) + seen
        m = jnp.where(key > thrb, 0.0, jnp.where(tie, jnp.where(pre <= need, 0.0, NEG), NEG))
        mask_ref[jb] = jnp.where(key == INT_MIN, NEG, m)
        seen = seen + _dot(tie16, ones)

    gq = gq_ref[...]

    def head(h):
        qh = q_ref[h].astype(F32)
        qn = (qh * lax.rsqrt(jnp.mean(qh * qh, axis=1, keepdims=True) + EPS) * gq * (d ** -0.5)).astype(BF16)
        lgs = _dot_nt(qn, kn[h])
        mx = None
        lg_blocks = []
        for jb in range(nb):
            lg = lgs[:, jb * T:(jb + 1) * T] + bias_ref[h, jnp.clip(qi - jb, 0, 2)] + mask_ref[jb]
            lg_blocks.append(lg)
            mx = lg if mx is None else jnp.maximum(mx, lg)
        rowmax = jnp.max(mx, axis=1, keepdims=True)
        l = jnp.zeros((T, T), F32)
        ps = []
        for jb in range(nb):
            p = jnp.exp(lg_blocks[jb] - rowmax)
            l = l + p
            ps.append(p.astype(BF16))
        acc = _dot(jnp.concatenate(ps, axis=1), v_ref[h])
        out_ref[h] = (acc / jnp.sum(l, axis=1, keepdims=True)).astype(out_ref.dtype)

    def head_group(g, _):
        for u in range(HEADS_PER_STEP):
            head(g * HEADS_PER_STEP + u)
        return 0

    lax.fori_loop(0, H // HEADS_PER_STEP, head_group, 0)


def _dsa(proj_b, tail, q_norm_g, k_norm_g, rel_bias, B, S):
    T = DSA_BLOCK
    nq = S // T
    d = HEAD_DIM
    H = N_HEADS
    topk = min(TOPK_MAX, S // 4)
    per = nq // DSA_BANDS
    pb = proj_b.reshape(proj_b.shape[0], B, S, d)
    tl = tail.reshape(tail.shape[0], B, S, LANES)
    out = jnp.zeros((H, B, S, d), BF16)
    for band in range(DSA_BANDS):
        q0 = band * per
        nb = q0 + per
        W = nb * T
        in_specs = [
            pl.BlockSpec(memory_space=pltpu.SMEM),
            pl.BlockSpec((H, None, T, d), lambda b, i: (B_DQ // H, b, q0 + i, 0)),
            pl.BlockSpec((H, None, W, d), lambda b, i: (B_DK // H, b, 0, 0)),
            pl.BlockSpec((H, None, W, d), lambda b, i: (B_DV // H, b, 0, 0)),
            pl.BlockSpec((H, None, T, d), lambda b, i: (B_IQ // H, b, q0 + i, 0)),
            pl.BlockSpec((None, None, W, LANES), lambda b, i: (0, b, 0, 0)),
            pl.BlockSpec((None, None, T, LANES), lambda b, i: (1, b, q0 + i, 0)),
            pl.BlockSpec((1, d), lambda b, i: (0, 0)),
            pl.BlockSpec((1, d), lambda b, i: (0, 0)),
            pl.BlockSpec(memory_space=pl.ANY),
        ]
        args = [rel_bias, pb, pb, pb, pb, tl, tl, q_norm_g.reshape(1, d), k_norm_g.reshape(1, d), out]
        out = pl.pallas_call(
            functools.partial(_dsa_kernel, topk=topk, q0=q0, nb=nb),
            grid=(B, per),
            in_specs=in_specs,
            out_specs=pl.BlockSpec((H, None, T, d), lambda b, i: (0, b, q0 + i, 0)),
            out_shape=jax.ShapeDtypeStruct((H, B, S, d), BF16),
            scratch_shapes=[
                pltpu.VMEM((H, W, d), BF16),
                pltpu.VMEM((2, W, LANES), BF16),
                pltpu.VMEM((IDX_HEADS, T, LANES), F32),
                pltpu.VMEM((nb, T, T), I32),
                pltpu.VMEM((nb, T, T), F32),
                pltpu.VMEM((H, 3, T, T), F32),
            ],
            input_output_aliases={len(args) - 1: 0},
            compiler_params=_params(("parallel", "arbitrary")),
            name=f"dsa_band{band}",
        )(*args)
    return out.reshape(H, B * S, d)


def _sb_kernel(q_ref, k_ref, v_ref, out_ref):
    qi = pl.program_id(2)
    U = SB_HEADS
    T = SB_BLOCK
    d = HEAD_DIM
    R = SB_ROWS
    ns = T // R
    row = lax.broadcasted_iota(I32, (T, T), 0)
    col = lax.broadcasted_iota(I32, (T, T), 1)
    upper = jnp.where(row > col, 1.0, 0.0).astype(BF16)
    srow = lax.broadcasted_iota(I32, (R, T), 0)
    scol = lax.broadcasted_iota(I32, (R, T), 1)
    units = [(u, s) for u in range(U) for s in range(ns)]
    qs = [(q_ref[u, s * R:(s + 1) * R, :].astype(F32) * (d ** -0.5)).astype(BF16) for u, s in units]

    def block(n, j, diag, run, acc):
        u, s = units[n]
        r = pl.multiple_of(j * T, T)
        z = _dot_nt(qs[n], k_ref[u, pl.ds(r, T), :])
        sp = jnp.maximum(z, 0.0) + jnp.log(1.0 + jnp.exp(-jnp.abs(z)))
        strict = scol < srow + s * R
        ln = jnp.where(strict, -sp, 0.0) if diag else -sp
        parts = [ln.astype(BF16)]
        for _ in range(SB_PIECES - 1):
            parts.append((ln - sum(p.astype(F32) for p in parts)).astype(BF16))
        sums = _dot(jnp.concatenate(parts, axis=0), upper)
        suf = sum(sums[p * R:(p + 1) * R] for p in range(SB_PIECES))
        a = jnp.exp(z - sp + suf + run)
        if diag:
            a = jnp.where(strict, a, 0.0)
        acc = acc + _dot(a.astype(BF16), v_ref[u, pl.ds(r, T), :])
        return run + suf[:, 0:1] + ln[:, 0:1], acc

    state = []
    for n in range(len(units)):
        state.extend(block(n, qi, True, jnp.zeros((R, 1), F32), jnp.zeros((R, d), F32)))

    def earlier(i, st):
        out = []
        for n in range(len(units)):
            out.extend(block(n, qi - i, False, st[2 * n], st[2 * n + 1]))
        return tuple(out)

    state = lax.fori_loop(1, qi + 1, earlier, tuple(state))
    for n, (u, s) in enumerate(units):
        out_ref[u, s * R:(s + 1) * R, :] = state[2 * n + 1].astype(out_ref.dtype)


def _stick_breaking(proj_c, B, S):
    T = SB_BLOCK
    nq = S // T
    d = HEAD_DIM
    H = N_HEADS
    U = SB_HEADS
    return pl.pallas_call(
        _sb_kernel,
        grid=(B, H // U, nq),
        in_specs=[
            pl.BlockSpec((U, T, d), lambda b, h, i: (C_SQ // U + h, b * nq + i, 0)),
            pl.BlockSpec((U, S, d), lambda b, h, i: (C_SK // U + h, b, 0)),
            pl.BlockSpec((U, S, d), lambda b, h, i: (C_SV // U + h, b, 0)),
        ],
        out_specs=pl.BlockSpec((U, T, d), lambda b, h, i: (h, b * nq + i, 0)),
        out_shape=jax.ShapeDtypeStruct((H, B * S, d), BF16),
        compiler_params=_params(("parallel", "parallel", "parallel")),
        name="stick_breaking",
    )(proj_c, proj_c, proj_c)


def _merge_kernel(hm_ref, hd_ref, hs_ref, g0_ref, g1_ref, g2_ref, w_ref, o_ref, wb):
    H = N_HEADS

    @pl.when(pl.program_id(1) == 0)
    def _():
        wb[...] = w_ref[...].astype(BF16)

    acc = None
    ns = o_ref.shape[1] // LANES
    for n, (br, gr) in enumerate(((hm_ref, g0_ref), (hd_ref, g1_ref), (hs_ref, g2_ref))):
        a = jnp.concatenate([br[h] for h in range(H)], axis=1)
        up = _dot(a, wb[n])
        gate = jnp.concatenate([gr[s] for s in range(ns)], axis=1).astype(F32)
        term = _sigmoid(gate) * up
        acc = term if acc is None else acc + term
    o_ref[...] = acc.astype(o_ref.dtype)


def _merge(hm, hd, hs, proj_c, w_branch, layer, tm=512, tn=512):
    H, n, d = hm.shape
    D = w_branch.shape[3]
    ns = tn // LANES
    br = pl.BlockSpec((H, tm, d), lambda j, i: (0, i, 0))

    def gate(b):
        first = (C_GATE + b * (D // LANES)) // ns
        return pl.BlockSpec((ns, tm, LANES), lambda j, i: (first + j, i, 0))

    return pl.pallas_call(
        _merge_kernel,
        grid=(D // tn, n // tm),
        in_specs=[br, br, br, gate(0), gate(1), gate(2),
                  pl.BlockSpec((None, N_BRANCHES, H * d, tn), lambda j, i: (layer, 0, 0, j))],
        out_specs=pl.BlockSpec((tm, tn), lambda j, i: (i, j)),
        out_shape=jax.ShapeDtypeStruct((n, D), BF16),
        scratch_shapes=[pltpu.VMEM((N_BRANCHES, H * d, tn), BF16)],
        compiler_params=_params(("parallel", "arbitrary")),
        name="branch_merge",
    )(hm, hd, hs, proj_c, proj_c, proj_c, w_branch)


def _mm_res_kernel(a_ref, w_ref, x_ref, o_ref, wb):
    @pl.when(pl.program_id(1) == 0)
    def _():
        wb[...] = w_ref[...].astype(BF16)

    o_ref[...] = x_ref[...] + _dot(a_ref[...], wb[...])


def _matmul_residual(a, w, layer, x, tm=512, tn=512):
    m, k = a.shape
    n = w.shape[2]
    return pl.pallas_call(
        _mm_res_kernel,
        grid=(n // tn, m // tm),
        in_specs=[pl.BlockSpec((tm, k), lambda j, i: (i, 0)),
                  pl.BlockSpec((None, k, tn), lambda j, i: (layer, 0, j)),
                  pl.BlockSpec((tm, tn), lambda j, i: (i, j))],
        out_specs=pl.BlockSpec((tm, tn), lambda j, i: (i, j)),
        out_shape=jax.ShapeDtypeStruct((m, n), F32),
        scratch_shapes=[pltpu.VMEM((k, tn), BF16)],
        compiler_params=_params(("parallel", "arbitrary")),
        name="out_proj",
    )(a, w, x)


def _router_kernel(x_ref, g_ref, wr_ref, br_ref, xn_ref, ids_ref, wts_ref, cnt_ref, carry):
    i = pl.program_id(0)
    tm, D = x_ref.shape
    nchunk = D // LANES

    @pl.when(i == 0)
    def _():
        carry[...] = jnp.zeros(carry.shape, F32)

    x = x_ref[...]
    xn = x * lax.rsqrt(jnp.mean(x * x, axis=-1, keepdims=True) + EPS) * g_ref[...]
    for c in range(nchunk):
        xn_ref[:, c, :] = xn[:, c * LANES:(c + 1) * LANES]
    logits = jnp.dot(xn, wr_ref[...], preferred_element_type=F32, precision=lax.Precision.HIGHEST)
    biased = logits + br_ref[...]
    lane = lax.broadcasted_iota(I32, (tm, LANES), 1)
    lanef = lane.astype(F32)
    big = float(LANES)

    def first_lane(mask):
        return jnp.min(jnp.where(mask, lanef, big), axis=1, keepdims=True)

    gmask = lane < N_GROUPS
    gmax = jnp.max(jnp.where(gmask, biased, NEG), axis=1, keepdims=True)
    g_sel = first_lane(gmask & (biased == gmax))
    gm = jnp.max(jnp.where(gmask, logits, NEG), axis=1, keepdims=True)
    ge = jnp.where(gmask, jnp.exp(logits - gm), 0.0)
    p_group = jnp.sum(jnp.where(lanef == g_sel, ge, 0.0), axis=1, keepdims=True) / jnp.sum(ge, axis=1, keepdims=True)
    lo = N_GROUPS + EXPERTS_PER_GROUP * g_sel
    emask = (lanef >= lo) & (lanef < lo + EXPERTS_PER_GROUP)
    eb = jnp.where(emask, biased, NEG)
    e1 = first_lane(emask & (eb == jnp.max(eb, axis=1, keepdims=True)))
    emask2 = emask & (lanef != e1)
    eb2 = jnp.where(emask2, biased, NEG)
    e2 = first_lane(emask2 & (eb2 == jnp.max(eb2, axis=1, keepdims=True)))
    em = jnp.max(jnp.where(emask, logits, NEG), axis=1, keepdims=True)
    ee = jnp.where(emask, jnp.exp(logits - em), 0.0)
    s1 = jnp.sum(jnp.where(lanef == e1, ee, 0.0), axis=1, keepdims=True)
    s2 = jnp.sum(jnp.where(lanef == e2, ee, 0.0), axis=1, keepdims=True)
    se = jnp.sum(ee, axis=1, keepdims=True)
    w1 = s1 / se
    w2 = s2 / se
    wsum = w1 + w2
    w1 = p_group * w1 / wsum
    w2 = p_group * w2 / wsum
    x1 = e1 - N_GROUPS
    x2 = e2 - N_GROUPS

    onehot = jnp.where((lanef == x1) | (lanef == x2), 1.0, 0.0)
    r = lax.broadcasted_iota(I32, (tm, tm), 0)
    c = lax.broadcasted_iota(I32, (tm, tm), 1)
    lower = jnp.where(c < r, 1.0, 0.0).astype(BF16)
    prefix = _dot(lower, onehot.astype(BF16)) + carry[...]
    r1 = jnp.sum(jnp.where(lanef == x1, prefix, 0.0), axis=1, keepdims=True)
    r2 = jnp.sum(jnp.where(lanef == x2, prefix, 0.0), axis=1, keepdims=True)
    carry[...] = carry[...] + jnp.sum(onehot, axis=0, keepdims=True)
    cnt_ref[...] = carry[...].astype(I32)

    idsf = jnp.where(lane == 0, x1, jnp.where(lane == 1, x2, jnp.where(lane == 2, r1, jnp.where(lane == 3, r2, 0.0))))
    ids_ref[...] = idsf.astype(I32)
    wts_ref[...] = jnp.where(lane == 0, w1, jnp.where(lane == 1, w2, 0.0))


def _router(x2d, g, w_router, b_router, tm=512):
    n, D = x2d.shape
    return pl.pallas_call(
        _router_kernel,
        grid=(n // tm,),
        in_specs=[pl.BlockSpec((tm, D), lambda i: (i, 0)), pl.BlockSpec((1, D), lambda i: (0, 0)),
                  pl.BlockSpec((D, LANES), lambda i: (0, 0)), pl.BlockSpec((1, LANES), lambda i: (0, 0))],
        out_specs=[pl.BlockSpec((tm, D // LANES, LANES), lambda i: (i, 0, 0)),
                   pl.BlockSpec((tm, LANES), lambda i: (i, 0)),
                   pl.BlockSpec((tm, LANES), lambda i: (i, 0)),
                   pl.BlockSpec((1, LANES), lambda i: (0, 0))],
        out_shape=[jax.ShapeDtypeStruct((n, D // LANES, LANES), F32),
                   jax.ShapeDtypeStruct((n, LANES), I32),
                   jax.ShapeDtypeStruct((n, LANES), F32),
                   jax.ShapeDtypeStruct((1, LANES), I32)],
        scratch_shapes=[pltpu.VMEM((1, LANES), F32)],
        compiler_params=_params(("arbitrary",)),
        name="moe_router",
    )(x2d, g.reshape(1, D), w_router, b_router)


def _dispatch_kernel(p1_ref, p2_ref, xn_ref, xs_in_ref, xs_ref, sem, *, tb):
    del xs_in_ref
    base = pl.program_id(0) * tb

    def copies(t):
        return (pltpu.make_async_copy(xn_ref.at[t], xs_ref.at[p1_ref[base + t]], sem),
                pltpu.make_async_copy(xn_ref.at[t], xs_ref.at[p2_ref[base + t]], sem))

    def issue(t, _):
        for cp in copies(t):
            cp.start()
        return 0

    lax.fori_loop(0, tb, issue, 0)

    def drain(t, _):
        for cp in copies(t):
            cp.wait()
        return 0

    lax.fori_loop(0, tb, drain, 0)


def _dispatch(pos1, pos2, xn3, xs0, tb=512):
    n, nchunk, _ = xn3.shape
    n_rows = xs0.shape[0]
    tb = min(tb, n)
    grid_spec = pltpu.PrefetchScalarGridSpec(
        num_scalar_prefetch=2,
        grid=(n // tb,),
        in_specs=[pl.BlockSpec((tb, nchunk, LANES), lambda i, p1, p2: (i, 0, 0)),
                  pl.BlockSpec(memory_space=pl.ANY)],
        out_specs=pl.BlockSpec(memory_space=pl.ANY),
        scratch_shapes=[pltpu.SemaphoreType.DMA(())],
    )
    return pl.pallas_call(
        functools.partial(_dispatch_kernel, tb=tb),
        grid_spec=grid_spec,
        out_shape=jax.ShapeDtypeStruct((n_rows, nchunk, LANES), F32),
        input_output_aliases={3: 0},
        compiler_params=pltpu.CompilerParams(dimension_semantics=("arbitrary",), has_side_effects=True),
        name="moe_dispatch",
    )(pos1, pos2, xn3, xs0)


def _expert_kernel(te_ref, nv_ref, xs_ref, wg_ref, wu_ref, wd_ref, ys_ref, wgb, wub, wdb):
    i = pl.program_id(0)
    nchunk = xs_ref.shape[1]
    prev = te_ref[jnp.maximum(i - 1, 0)]

    @pl.when(i < nv_ref[0])
    def _():
        @pl.when((i == 0) | (te_ref[i] != prev))
        def _():
            wgb[...] = wg_ref[...].astype(BF16)
            wub[...] = wu_ref[...].astype(BF16)
            wdb[...] = wd_ref[...].astype(BF16)

        x = jnp.concatenate([xs_ref[:, c, :] for c in range(nchunk)], axis=1).astype(BF16)
        g = _dot(x, wgb[...])
        u = _dot(x, wub[...])
        hcur = (g * _sigmoid(g) * u).astype(BF16)
        y = _dot(hcur, wdb[...])
        for c in range(nchunk):
            ys_ref[:, c, :] = y[:, c * LANES:(c + 1) * LANES]

    @pl.when(i >= nv_ref[0])
    def _():
        ys_ref[...] = jnp.zeros(ys_ref.shape, F32)


def _experts(tile_expert, n_valid, xs, w_gate, w_up, w_down, layer):
    n_rows, nchunk, _ = xs.shape
    _, E, D, Fe = w_gate.shape
    tm = MOE_TILE
    n_tiles = n_rows // tm

    def row_map(i, te, nv):
        return (jnp.minimum(i, jnp.maximum(nv[0] - 1, 0)), 0, 0)

    def w_map(i, te, nv):
        return (layer, te[i], 0, 0)

    grid_spec = pltpu.PrefetchScalarGridSpec(
        num_scalar_prefetch=2,
        grid=(n_tiles,),
        in_specs=[pl.BlockSpec((tm, nchunk, LANES), row_map),
                  pl.BlockSpec((None, None, D, Fe), w_map), pl.BlockSpec((None, None, D, Fe), w_map),
                  pl.BlockSpec((None, None, Fe, D), w_map)],
        out_specs=pl.BlockSpec((tm, nchunk, LANES), lambda i, te, nv: (i, 0, 0)),
        scratch_shapes=[pltpu.VMEM((D, Fe), BF16), pltpu.VMEM((D, Fe), BF16), pltpu.VMEM((Fe, D), BF16)],
    )
    return pl.pallas_call(
        _expert_kernel,
        grid_spec=grid_spec,
        out_shape=jax.ShapeDtypeStruct((n_rows, nchunk, LANES), F32),
        compiler_params=_params(("arbitrary",)),
        name="moe_experts",
    )(tile_expert, n_valid, xs, w_gate, w_up, w_down)


def _combine_kernel(p1_ref, p2_ref, ys_ref, x_ref, w_ref, o_ref, buf, sem):
    i = pl.program_id(0)
    tc = x_ref.shape[0]
    nchunk = buf.shape[2]
    base = i * tc

    def copies(t):
        return (pltpu.make_async_copy(ys_ref.at[p1_ref[base + t]], buf.at[0, t], sem),
                pltpu.make_async_copy(ys_ref.at[p2_ref[base + t]], buf.at[1, t], sem))

    def issue(t, _):
        for cp in copies(t):
            cp.start()
        return 0

    lax.fori_loop(0, tc, issue, 0)

    def drain(t, _):
        for cp in copies(t):
            cp.wait()
        return 0

    lax.fori_loop(0, tc, drain, 0)
    w = w_ref[...]
    w1 = w[:, 0:1]
    w2 = w[:, 1:2]
    for c in range(nchunk):
        sl = slice(c * LANES, (c + 1) * LANES)
        o_ref[:, sl] = x_ref[:, sl] + w1 * buf[0, :, c, :] + w2 * buf[1, :, c, :]


def _combine(pos1, pos2, ys, x2d, wts, tc=256):
    n, D = x2d.shape
    nchunk = D // LANES
    grid_spec = pltpu.PrefetchScalarGridSpec(
        num_scalar_prefetch=2,
        grid=(n // tc,),
        in_specs=[pl.BlockSpec(memory_space=pl.ANY),
                  pl.BlockSpec((tc, D), lambda i, p1, p2: (i, 0)),
                  pl.BlockSpec((tc, LANES), lambda i, p1, p2: (i, 0))],
        out_specs=pl.BlockSpec((tc, D), lambda i, p1, p2: (i, 0)),
        scratch_shapes=[pltpu.VMEM((2, tc, nchunk, LANES), F32), pltpu.SemaphoreType.DMA(())],
    )
    return pl.pallas_call(
        _combine_kernel,
        grid_spec=grid_spec,
        out_shape=jax.ShapeDtypeStruct((n, D), F32),
        compiler_params=_params(("arbitrary",)),
        name="moe_combine",
    )(pos1, pos2, ys, x2d, wts)


def _pos_kernel(ids_ref, cnt_ref, pos_ref):
    tm = ids_ref.shape[0]
    tiles = jnp.floor((cnt_ref[...].astype(F32) + (MOE_TILE - 1)) / MOE_TILE)
    r = lax.broadcasted_iota(I32, (LANES, LANES), 0)
    c = lax.broadcasted_iota(I32, (LANES, LANES), 1)
    before = jnp.where(r < c, 1.0, 0.0).astype(BF16)
    first_tile = _dot(jnp.broadcast_to(tiles, (8, LANES)).astype(BF16), before)
    offs = first_tile[0:1, :] * MOE_TILE
    ids = ids_ref[...].astype(F32)
    lane = lax.broadcasted_iota(I32, (tm, LANES), 1)
    lanef = lane.astype(F32)
    p1 = jnp.sum(jnp.where(lanef == ids[:, 0:1], offs, 0.0), axis=1, keepdims=True) + ids[:, 2:3]
    p2 = jnp.sum(jnp.where(lanef == ids[:, 1:2], offs, 0.0), axis=1, keepdims=True) + ids[:, 3:4]
    packed = jnp.where(lane == 0, p1, jnp.where(lane == 1, p2, 0.0))
    pos_ref[...] = packed.T[0:8, :].astype(I32)


def _positions(ids, counts, tm=512):
    n = ids.shape[0]
    assert MOE_TILE & (MOE_TILE - 1) == 0
    return pl.pallas_call(
        _pos_kernel,
        grid=(n // tm,),
        in_specs=[pl.BlockSpec((tm, LANES), lambda i: (i, 0)), pl.BlockSpec((1, LANES), lambda i: (0, 0))],
        out_specs=pl.BlockSpec((8, tm), lambda i: (0, i)),
        out_shape=jax.ShapeDtypeStruct((8, n), I32),
        compiler_params=_params(("parallel",)),
        name="moe_positions",
    )(ids, counts)


def _hier_moe(x2d, norm_g, w_rg, b_rg, w_re, b_re, w_gate, w_up, w_down, layer, xs_buf):
    n, D = x2d.shape
    tm = MOE_TILE
    pad = LANES - N_GROUPS - N_EXPERTS
    w_router = jnp.concatenate([w_rg, w_re, jnp.zeros((D, pad), F32)], axis=1)
    b_router = jnp.concatenate([b_rg, b_re.reshape(-1), jnp.zeros((pad,), F32)]).reshape(1, LANES)
    xn3, ids, wts, counts = _router(x2d, norm_g, w_router, b_router)
    cnt = counts[0, :N_EXPERTS]
    padded = ((cnt + tm - 1) // tm) * tm
    ends = jnp.cumsum(padded)
    n_tiles = xs_buf.shape[0] // tm
    tile_start = jnp.arange(n_tiles, dtype=I32) * tm
    tile_expert = jnp.minimum(jnp.sum(tile_start[:, None] >= ends[None, :], axis=1), N_EXPERTS - 1).astype(I32)
    n_valid = (ends[-1] // tm).astype(I32).reshape(1)
    last_e = tile_expert[jnp.maximum(n_valid[0] - 1, 0)]
    tile_expert = jnp.where(jnp.arange(n_tiles) < n_valid[0], tile_expert, last_e)
    pos = _positions(ids, counts)
    pos1, pos2 = pos[0], pos[1]
    xs = _dispatch(pos1, pos2, xn3, xs_buf)
    ys = _experts(tile_expert, n_valid, xs, w_gate, w_up, w_down, layer)
    return _combine(pos1, pos2, ys, x2d, wts), xs


def _in_proj_regions(D):
    bw = BRANCH_WIDTH
    sizes = (bw, bw, bw, bw, N_HEADS, N_HEADS, bw, bw, bw, IDX_HEADS * IDX_DIM, IDX_DIM, IDX_HEADS,
             bw, bw, bw, N_BRANCHES * D)
    offs = np.concatenate([[0], np.cumsum(sizes)]).tolist()
    region_a = (offs[0], offs[4] - offs[0])
    region_b = (offs[6], offs[10] - offs[6])
    region_c = (offs[12], offs[16] - offs[12])
    small = dict(mi=offs[4], mf=offs[5], ik=offs[10], iw=offs[11])
    return region_a, region_b, region_c, small


def _token_mixer(x2d, B, S, layer, norm_g, w_in, conv_w, b_i, b_f, mlstm_norm_g, q_norm_g, k_norm_g,
                 w_branch, w_out, rel_bias):
    n, D = x2d.shape
    L = MLSTM_CHUNK
    H = N_HEADS
    tm = min(n, PROJ_TM)
    xn = _rmsnorm(x2d, norm_g)
    ra, rb, rc, small = _in_proj_regions(D)
    w_t = jnp.swapaxes(w_in, 1, 2)
    proj_a = _in_proj(xn, w_t, layer, ra[0], ra[1], tm)
    proj_b = _in_proj(xn, w_t, layer, rb[0], rb[1], tm)
    proj_c = _in_proj(xn, w_t, layer, rc[0], rc[1], tm)
    tail = _tail_proj(xn, w_t, layer, small, tm)
    g = tail[1][:, IDX_HEADS:IDX_HEADS + 2 * H].reshape(B, S // L, L, 2, H)
    gates_t = jnp.transpose(g, (0, 4, 1, 3, 2))
    hm = _mlstm(proj_a, gates_t, jnp.stack([b_i, b_f]), conv_w, mlstm_norm_g, B, S)
    hd = _dsa(proj_b, tail, q_norm_g, k_norm_g, rel_bias, B, S)
    hs = _stick_breaking(proj_c, B, S)
    merged = _merge(hm, hd, hs, proj_c, w_branch, layer)
    return _matmul_residual(merged, w_out, layer, x2d)


def kernel(x, norm1_g, w_in, conv_w, b_i, b_f, mlstm_norm_g, q_norm_g, k_norm_g, w_branch, w_out, norm2_g,
           w_router_g, b_router_g, w_router_e, b_router_e, w_gate, w_up, w_down, rel_bias):
    B, S, D = x.shape
    x2d = x.reshape(B * S, D)
    xs_buf = jnp.zeros((2 * B * S + N_EXPERTS * MOE_TILE, D // LANES, LANES), F32)
    for l in range(w_in.shape[0]):
        x2d = _token_mixer(x2d, B, S, l, norm1_g[l], w_in, conv_w[l], b_i[l], b_f[l], mlstm_norm_g[l],
                           q_norm_g[l], k_norm_g[l], w_branch, w_out, rel_bias)
        x2d, xs_buf = _hier_moe(x2d, norm2_g[l], w_router_g[l], b_router_g[l], w_router_e[l], b_router_e[l],
                                w_gate, w_up, w_down, l, xs_buf)
    return x2d.reshape(B, S, D)
```

```python
import functools
import math

import numpy as np
import jax
import jax.numpy as jnp
from jax import lax
from jax.experimental import pallas as pl
from jax.experimental.pallas import tpu as pltpu

F32 = jnp.float32
BF16 = jnp.bfloat16
I32 = jnp.int32

LANES = 128
HEAD_DIM = 128
N_HEADS = 8
BRANCH_WIDTH = N_HEADS * HEAD_DIM
N_BRANCHES = 3
CONV_WIDTH = 4
IDX_HEADS = 16
IDX_DIM = 64
TOPK_MAX = 256
N_BUCKETS = 32
MAX_DISTANCE = 128
N_GROUPS = 4
EXPERTS_PER_GROUP = 8
N_EXPERTS = N_GROUPS * EXPERTS_PER_GROUP
EPS = 1e-6
NEG = -1e30
INT_MIN = -(2 ** 31)

MLSTM_CHUNK = 256
DSA_BLOCK = 128
DSA_BANDS = 4
SB_BLOCK = 256
SB_ROWS = 256
SB_PIECES = 1
SB_HEADS = 4
HEADS_PER_STEP = 4
MOE_TILE = 256
PROJ_TN = 512
PROJ_TM = 2048
VMEM_LIMIT = 56 * 1024 * 1024

A_MQ, A_MK, A_MV, A_MO = 0, 8, 16, 24
B_DQ, B_DK, B_DV, B_IQ = 0, 8, 16, 24
C_SQ, C_SK, C_SV, C_GATE = 0, 8, 16, 24


def _params(sem):
    return pltpu.CompilerParams(dimension_semantics=sem, vmem_limit_bytes=VMEM_LIMIT)


def _dot(a, b):
    return jnp.dot(a, b, preferred_element_type=F32)


def _dot_nt(a, b):
    return lax.dot_general(a, b, (((1,), (1,)), ((), ())), preferred_element_type=F32)


def _sigmoid(z):
    return 1.0 / (1.0 + jnp.exp(-z))


def _rmsnorm_kernel(x_ref, g_ref, o_ref):
    x = x_ref[...]
    ms = jnp.mean(x * x, axis=-1, keepdims=True)
    o_ref[...] = (x * lax.rsqrt(ms + EPS) * g_ref[...]).astype(o_ref.dtype)


def _rmsnorm(x2d, g, tm=512):
    n, d = x2d.shape
    return pl.pallas_call(
        _rmsnorm_kernel,
        grid=(n // tm,),
        in_specs=[pl.BlockSpec((tm, d), lambda i: (i, 0)), pl.BlockSpec((1, d), lambda i: (0, 0))],
        out_specs=pl.BlockSpec((tm, d), lambda i: (i, 0)),
        out_shape=jax.ShapeDtypeStruct((n, d), BF16),
        compiler_params=_params(("parallel",)),
        name="rmsnorm",
    )(x2d, g.reshape(1, d))


def _in_proj_kernel(x_ref, *rest, shift, nblk):
    w_refs, o_ref, wb = rest[:nblk], rest[nblk], rest[nblk + 1]
    tn = wb.shape[1]

    @pl.when(pl.program_id(1) == 0)
    def _():
        w = jnp.concatenate([r[...] for r in w_refs], axis=0)
        wb[...] = w[shift:shift + tn, :].T.astype(BF16)

    acc = _dot(x_ref[...], wb[...])
    for j in range(o_ref.shape[0]):
        o_ref[j] = acc[:, j * LANES:(j + 1) * LANES].astype(o_ref.dtype)


def _in_proj(xn, w_t, layer, col0, ncols, tm):
    m, k = xn.shape
    tn = PROJ_TN
    base, shift = divmod(col0, LANES)
    assert shift % 8 == 0
    nblk = tn // LANES + (1 if shift else 0)
    per = tn // LANES

    def wspec(r):
        return pl.BlockSpec((None, LANES, k), lambda j, i: (layer, base + per * j + r, 0))

    return pl.pallas_call(
        functools.partial(_in_proj_kernel, shift=shift, nblk=nblk),
        grid=(ncols // tn, m // tm),
        in_specs=[pl.BlockSpec((tm, k), lambda j, i: (i, 0))] + [wspec(r) for r in range(nblk)],
        out_specs=pl.BlockSpec((per, tm, LANES), lambda j, i: (j, i, 0)),
        out_shape=jax.ShapeDtypeStruct((ncols // LANES, m, LANES), BF16),
        scratch_shapes=[pltpu.VMEM((k, tn), BF16)],
        compiler_params=_params(("parallel", "arbitrary")),
        name="in_proj",
    )(xn, *([w_t] * nblk))


def _tail_kernel(x_ref, wg_ref, wi_ref, o_ref, wt, *, g_lane, ik_lane, iw_lane):
    @pl.when(pl.program_id(0) == 0)
    def _():
        wg = wg_ref[...]
        wi = wi_ref[...]
        ik = wi[ik_lane:ik_lane + IDX_DIM, :]
        iw = wi[iw_lane:iw_lane + IDX_HEADS, :]
        gates = wg[g_lane:g_lane + 2 * N_HEADS, :]
        pad = jnp.zeros((LANES - IDX_HEADS - 2 * N_HEADS, wg.shape[1]), F32)
        wt[...] = jnp.concatenate([ik, ik, iw, gates, pad], axis=0).T.astype(BF16)

    acc = _dot(x_ref[...], wt[...])
    o_ref[0] = acc[:, :LANES]
    o_ref[1] = acc[:, LANES:]


def _tail_proj(xn, w_t, layer, small, tm):
    m, k = xn.shape
    g_blk, g_lane = divmod(small["mi"], LANES)
    i_blk, ik_lane = divmod(small["ik"], LANES)
    iw_lane = small["iw"] - i_blk * LANES
    assert small["mf"] == small["mi"] + N_HEADS and g_lane + 2 * N_HEADS <= LANES
    assert ik_lane + IDX_DIM <= LANES and 0 <= iw_lane and iw_lane + IDX_HEADS <= LANES
    assert g_lane % 8 == 0 and ik_lane % 8 == 0 and iw_lane % 8 == 0
    return pl.pallas_call(
        functools.partial(_tail_kernel, g_lane=g_lane, ik_lane=ik_lane, iw_lane=iw_lane),
        grid=(m // tm,),
        in_specs=[pl.BlockSpec((tm, k), lambda i: (i, 0)),
                  pl.BlockSpec((None, LANES, k), lambda i: (layer, g_blk, 0)),
                  pl.BlockSpec((None, LANES, k), lambda i: (layer, i_blk, 0))],
        out_specs=pl.BlockSpec((2, tm, LANES), lambda i: (0, i, 0)),
        out_shape=jax.ShapeDtypeStruct((2, m, LANES), F32),
        scratch_shapes=[pltpu.VMEM((k, 2 * LANES), BF16)],
        compiler_params=_params(("arbitrary",)),
        name="tail_proj",
    )(xn, w_t, w_t)


def _mlstm_kernel(bias_ref, q_ref, k_ref, v_ref, o_ref, g_ref, cwq_ref, cwk_ref, ng_ref, out_ref,
                  qf, kf, qc, kc, st):
    hp = pl.program_id(1)
    U = HEADS_PER_STEP
    S = q_ref.shape[1]
    L = MLSTM_CHUNK
    nc = S // L
    d = HEAD_DIM
    PAD = 8

    R = min(S, 256)
    for u in range(U):
        qf[u, 0:PAD, :] = jnp.zeros((PAD, d), F32)
        kf[u, 0:PAD, :] = jnp.zeros((PAD, d), F32)
        qf[u, PAD:PAD + S, :] = q_ref[u].astype(F32)
        kf[u, PAD:PAD + S, :] = k_ref[u].astype(F32)
        ls = slice(u * d, (u + 1) * d)
        for r0 in range(0, S, R):
            aq = jnp.zeros((R, d), F32)
            ak = jnp.zeros((R, d), F32)
            for t in range(CONV_WIDTH):
                off = PAD - (CONV_WIDTH - 1) + t + r0
                aq = aq + cwq_ref[t:t + 1, ls] * qf[u, off:off + R, :]
                ak = ak + cwk_ref[t:t + 1, ls] * kf[u, off:off + R, :]
            qc[u, r0:r0 + R, :] = (aq * _sigmoid(aq) * (d ** -0.5)).astype(BF16)
            kc[u, r0:r0 + R, :] = ak * _sigmoid(ak)

    st[...] = jnp.zeros(st.shape, F32)
    row = lax.broadcasted_iota(I32, (L, L), 0)
    col = lax.broadcasted_iota(I32, (L, L), 1)
    causal = col <= row
    eye = col == row
    lane = lax.broadcasted_iota(I32, (L, d), 1)
    ones_col = jnp.where(lane == 0, 1.0, 0.0).astype(BF16)
    ng = ng_ref[...]

    def chunk_one(u, c, r, m):
        q = qc[u, pl.ds(r, L), :]
        kT = kc[u, pl.ds(r, L), :].T
        v = v_ref[u, pl.ds(r, L), :]
        vaug = jnp.concatenate([v, ones_col], axis=1)
        gates = g_ref[u, c]
        i_row = gates[0:1, :] + bias_ref[0, hp * U + u]
        f_row = gates[1:2, :] + bias_ref[1, hp * U + u]
        lf_row = jnp.minimum(f_row, 0.0) - jnp.log1p(jnp.exp(-jnp.abs(f_row)))
        b_col = jnp.sum(jnp.where(causal, lf_row, 0.0), axis=1, keepdims=True)
        b_row = jnp.sum(jnp.where(eye, b_col, 0.0), axis=0, keepdims=True)
        dlog = jnp.where(causal, b_col - b_row + i_row, NEG)
        inter = b_col + m
        m_t = jnp.maximum(inter, jnp.max(dlog, axis=1, keepdims=True))
        w_intra = jnp.exp(dlog - m_t)
        w_inter = jnp.exp(inter - m_t)
        s = _dot(q, kT.astype(BF16)) * w_intra
        res = w_inter * _dot(q, st[u].astype(BF16)) + _dot(s.astype(BF16), vaug)
        num = res[:, :d]
        den = res[:, d:d + 1]
        hh = num / jnp.maximum(jnp.abs(den), jnp.exp(-m_t))
        hn = hh * lax.rsqrt(jnp.mean(hh * hh, axis=1, keepdims=True) + EPS) * ng[:, u * d:(u + 1) * d]
        og = o_ref[u, pl.ds(r, L), :].astype(F32)
        out_ref[u, pl.ds(r, L), :] = (hn * _sigmoid(og)).astype(out_ref.dtype)
        ws_row = w_intra[L - 1:L, :]
        decay = w_inter[L - 1:L, :]
        st[u] = decay * st[u] + _dot((kT * ws_row).astype(BF16), vaug)
        return m_t[L - 1:L, :]

    def chunk(c, ms):
        r = pl.multiple_of(c * L, L)
        return tuple(chunk_one(u, c, r, ms[u]) for u in range(U))

    lax.fori_loop(0, nc, chunk, tuple(jnp.zeros((1, 1), F32) for _ in range(U)))


def _mlstm(proj_a, gates_t, bias_if, conv_w, norm_g, B, S):
    L = MLSTM_CHUNK
    nc = S // L
    d = HEAD_DIM
    H = N_HEADS
    U = HEADS_PER_STEP

    def slab(off):
        return pl.BlockSpec((U, S, d), lambda b, h: (off // U + h, b, 0))

    return pl.pallas_call(
        _mlstm_kernel,
        grid=(B, H // U),
        in_specs=[
            pl.BlockSpec(memory_space=pltpu.SMEM),
            slab(A_MQ), slab(A_MK), slab(A_MV), slab(A_MO),
            pl.BlockSpec((None, U, nc, 2, L), lambda b, h: (b, h, 0, 0, 0)),
            pl.BlockSpec((CONV_WIDTH, U * d), lambda b, h: (0, h)),
            pl.BlockSpec((CONV_WIDTH, U * d), lambda b, h: (0, H // U + h)),
            pl.BlockSpec((1, U * d), lambda b, h: (0, h)),
        ],
        out_specs=pl.BlockSpec((U, S, d), lambda b, h: (h, b, 0)),
        out_shape=jax.ShapeDtypeStruct((H, B * S, d), BF16),
        scratch_shapes=[
            pltpu.VMEM((U, S + 8, d), F32), pltpu.VMEM((U, S + 8, d), F32),
            pltpu.VMEM((U, S, d), BF16), pltpu.VMEM((U, S, d), F32),
            pltpu.VMEM((U, d, 2 * d), F32),
        ],
        compiler_params=_params(("parallel", "parallel")),
        name="mlstm",
    )(bias_if, proj_a, proj_a, proj_a, proj_a, gates_t, conv_w, conv_w, norm_g.reshape(1, H * d))


def _t5_thresholds():
    max_exact = N_BUCKETS // 2
    n = np.arange(0, 2 * MAX_DISTANCE)
    nf = np.maximum(n, 1).astype(np.float64)
    val = np.log(nf / max_exact) / math.log(MAX_DISTANCE / max_exact) * (N_BUCKETS - max_exact)
    frac = np.abs(val - np.round(val))
    frac_ok = (frac > 1e-4) | (n <= max_exact) | (n >= MAX_DISTANCE)
    assert frac_ok.all()
    large = np.minimum(max_exact + np.trunc(val).astype(np.int64), N_BUCKETS - 1)
    bucket = np.where(n < max_exact, n, large)
    assert (np.diff(bucket) >= 0).all() and bucket[MAX_DISTANCE] == N_BUCKETS - 1
    return [int(np.argmax(bucket >= j)) for j in range(1, N_BUCKETS)]


_T5_THR = _t5_thresholds()


def _dsa_kernel(rb_ref, q_ref, k_ref, v_ref, iq_ref, ik_ref, wt_ref, gq_ref, gk_ref, prev_ref, out_ref,
                kn, ikk, wb, key_ref, mask_ref, bias_ref, *, topk, q0, nb):
    del prev_ref
    qi = q0 + pl.program_id(1)
    T = DSA_BLOCK
    d = HEAD_DIM
    H = N_HEADS
    HALF = T // 2
    row = lax.broadcasted_iota(I32, (T, T), 0)
    col = lax.broadcasted_iota(I32, (T, T), 1)

    @pl.when(pl.program_id(1) == 0)
    def _per_batch():
        gk = gk_ref[...]

        def norm_k(h, _):
            kh = k_ref[h].astype(F32)
            kn[h] = (kh * lax.rsqrt(jnp.mean(kh * kh, axis=1, keepdims=True) + EPS) * gk).astype(BF16)
            for o in range(2):
                n = o * T + row - col
                val = jnp.full((T, T), rb_ref[0, h], F32)
                for j, thr in enumerate(_T5_THR):
                    val = jnp.where(n >= thr, rb_ref[j + 1, h], val)
                bias_ref[h, o] = val
            bias_ref[h, 2] = jnp.full((T, T), rb_ref[N_BUCKETS - 1, h], F32)
            return 0

        lax.fori_loop(0, H, norm_k, 0)
        ik = ik_ref[...]
        klane = lax.broadcasted_iota(I32, ik.shape, 1)
        ikk[0] = jnp.where(klane < IDX_DIM, ik, 0.0).astype(BF16)
        ikk[1] = jnp.where(klane >= IDX_DIM, ik, 0.0).astype(BF16)

    wsc = wt_ref[...] * (IDX_HEADS ** -0.5 * IDX_DIM ** -0.5)
    for h16 in range(IDX_HEADS):
        wb[h16] = jnp.broadcast_to(wsc[:, h16:h16 + 1], (T, LANES))
    q_pairs = iq_ref[...].reshape((IDX_HEADS // 2) * T, LANES)

    def score_block(j, _):
        r = pl.multiple_of(j * T, T)
        kk = jnp.concatenate([ikk[0, pl.ds(r, T), :], ikk[1, pl.ds(r, T), :]], axis=0)
        dots = _dot_nt(q_pairs, kk)
        sc = jnp.zeros((T, T), F32)
        for h16 in range(IDX_HEADS):
            hp, odd = divmod(h16, 2)
            sc = sc + wb[h16] * jnp.maximum(dots[hp * T:(hp + 1) * T, odd * T:(odd + 1) * T], 0.0)
        bits = lax.bitcast_convert_type(sc, I32)
        key = jnp.where(bits < 0, bits ^ jnp.int32(0x7FFFFFFF), bits)
        key = jnp.where(sc == 0.0, 0, key)
        key = jnp.where(j * T + col <= qi * T + row, key, INT_MIN)
        key_ref[j] = key
        return 0

    lax.fori_loop(0, nb, score_block, 0, unroll=2)

    kf = float(topk)
    ones = jnp.ones((LANES, LANES), BF16)

    def count(lo, pred):
        acc = jnp.zeros((HALF, LANES), F32)
        for jb in range(nb):
            acc = acc + jnp.where(pred(key_ref[jb, lo:lo + HALF, :]), 1.0, 0.0)
        return _dot(acc.astype(BF16), ones)

    halves = (0, HALF)
    zero_i = jnp.zeros((HALF, LANES), I32)
    thr0 = tuple(jnp.where(count(lo, lambda k: k >= zero_i) >= kf, jnp.int32(0), jnp.int32(INT_MIN))
                 for lo in halves)

    def bisect_bit(thrs, bit):
        out = []
        for lo, t in zip(halves, thrs):
            cand = t | bit
            out.append(jnp.where(count(lo, lambda k, c=cand: k >= c) >= kf, cand, t))
        return tuple(out)

    def bisect_pair(it, thrs):
        hi_bit = jnp.left_shift(jnp.int32(1), 29 - 2 * it)
        lo_bit = jnp.left_shift(jnp.int32(1), 28 - 2 * it)
        out = []
        for lo, t in zip(halves, thrs):
            c01, c10, c11 = t | lo_bit, t | hi_bit, t | hi_bit | lo_bit
            n01, n10, n11 = (count(lo, lambda k, c=c: k >= c) >= kf for c in (c01, c10, c11))
            out.append(jnp.where(n11, c11, jnp.where(n10, c10, jnp.where(n01, c01, t))))
        return tuple(out)

    thrs = bisect_bit(thr0, jnp.int32(1 << 30))
    thrs = lax.fori_loop(0, 15, bisect_pair, thrs)
    need = jnp.concatenate([kf - count(lo, lambda k, t=t: k > t) for lo, t in zip(halves, thrs)], axis=0)
    thrb = jnp.concatenate(thrs, axis=0)

    tri = jnp.where(row <= col, 1.0, 0.0).astype(BF16)
    seen = jnp.zeros((T, LANES), F32)
    for jb in range(nb):
        key = key_ref[jb]
        tie = key == thrb
        tie16 = jnp.where(tie, 1.0, 0.0).astype(BF16)
        pre = _dot(tie16, tri) + seen
        m = jnp.where(key > thrb, 0.0, jnp.where(tie, jnp.where(pre <= need, 0.0, NEG), NEG))
        mask_ref[jb] = jnp.where(key == INT_MIN, NEG, m)
        seen = seen + _dot(tie16, ones)

    gq = gq_ref[...]

    def head(h):
        qh = q_ref[h].astype(F32)
        qn = (qh * lax.rsqrt(jnp.mean(qh * qh, axis=1, keepdims=True) + EPS) * gq * (d ** -0.5)).astype(BF16)
        lgs = _dot_nt(qn, kn[h])
        mx = None
        lg_blocks = []
        for jb in range(nb):
            lg = lgs[:, jb * T:(jb + 1) * T] + bias_ref[h, jnp.clip(qi - jb, 0, 2)] + mask_ref[jb]
            lg_blocks.append(lg)
            mx = lg if mx is None else jnp.maximum(mx, lg)
        rowmax = jnp.max(mx, axis=1, keepdims=True)
        l = jnp.zeros((T, T), F32)
        ps = []
        for jb in range(nb):
            p = jnp.exp(lg_blocks[jb] - rowmax)
            l = l + p
            ps.append(p.astype(BF16))
        acc = _dot(jnp.concatenate(ps, axis=1), v_ref[h])
        out_ref[h] = (acc / jnp.sum(l, axis=1, keepdims=True)).astype(out_ref.dtype)

    def head_group(g, _):
        for u in range(HEADS_PER_STEP):
            head(g * HEADS_PER_STEP + u)
        return 0

    lax.fori_loop(0, H // HEADS_PER_STEP, head_group, 0)


def _dsa(proj_b, tail, q_norm_g, k_norm_g, rel_bias, B, S):
    T = DSA_BLOCK
    nq = S // T
    d = HEAD_DIM
    H = N_HEADS
    topk = min(TOPK_MAX, S // 4)
    per = nq // DSA_BANDS
    pb = proj_b.reshape(proj_b.shape[0], B, S, d)
    tl = tail.reshape(tail.shape[0], B, S, LANES)
    out = jnp.zeros((H, B, S, d), BF16)
    for band in range(DSA_BANDS):
        q0 = band * per
        nb = q0 + per
        W = nb * T
        in_specs = [
            pl.BlockSpec(memory_space=pltpu.SMEM),
            pl.BlockSpec((H, None, T, d), lambda b, i: (B_DQ // H, b, q0 + i, 0)),
            pl.BlockSpec((H, None, W, d), lambda b, i: (B_DK // H, b, 0, 0)),
            pl.BlockSpec((H, None, W, d), lambda b, i: (B_DV // H, b, 0, 0)),
            pl.BlockSpec((H, None, T, d), lambda b, i: (B_IQ // H, b, q0 + i, 0)),
            pl.BlockSpec((None, None, W, LANES), lambda b, i: (0, b, 0, 0)),
            pl.BlockSpec((None, None, T, LANES), lambda b, i: (1, b, q0 + i, 0)),
            pl.BlockSpec((1, d), lambda b, i: (0, 0)),
            pl.BlockSpec((1, d), lambda b, i: (0, 0)),
            pl.BlockSpec(memory_space=pl.ANY),
        ]
        args = [rel_bias, pb, pb, pb, pb, tl, tl, q_norm_g.reshape(1, d), k_norm_g.reshape(1, d), out]
        out = pl.pallas_call(
            functools.partial(_dsa_kernel, topk=topk, q0=q0, nb=nb),
            grid=(B, per),
            in_specs=in_specs,
            out_specs=pl.BlockSpec((H, None, T, d), lambda b, i: (0, b, q0 + i, 0)),
            out_shape=jax.ShapeDtypeStruct((H, B, S, d), BF16),
            scratch_shapes=[
                pltpu.VMEM((H, W, d), BF16),
                pltpu.VMEM((2, W, LANES), BF16),
                pltpu.VMEM((IDX_HEADS, T, LANES), F32),
                pltpu.VMEM((nb, T, T), I32),
                pltpu.VMEM((nb, T, T), F32),
                pltpu.VMEM((H, 3, T, T), F32),
            ],
            input_output_aliases={len(args) - 1: 0},
            compiler_params=_params(("parallel", "arbitrary")),
            name=f"dsa_band{band}",
        )(*args)
    return out.reshape(H, B * S, d)


def _sb_kernel(q_ref, k_ref, v_ref, out_ref):
    qi = pl.program_id(2)
    U = SB_HEADS
    T = SB_BLOCK
    d = HEAD_DIM
    R = SB_ROWS
    ns = T // R
    row = lax.broadcasted_iota(I32, (T, T), 0)
    col = lax.broadcasted_iota(I32, (T, T), 1)
    upper = jnp.where(row > col, 1.0, 0.0).astype(BF16)
    srow = lax.broadcasted_iota(I32, (R, T), 0)
    scol = lax.broadcasted_iota(I32, (R, T), 1)
    units = [(u, s) for u in range(U) for s in range(ns)]
    qs = [(q_ref[u, s * R:(s + 1) * R, :].astype(F32) * (d ** -0.5)).astype(BF16) for u, s in units]

    def block(n, j, diag, run, acc):
        u, s = units[n]
        r = pl.multiple_of(j * T, T)
        z = _dot_nt(qs[n], k_ref[u, pl.ds(r, T), :])
        sp = jnp.maximum(z, 0.0) + jnp.log(1.0 + jnp.exp(-jnp.abs(z)))
        strict = scol < srow + s * R
        ln = jnp.where(strict, -sp, 0.0) if diag else -sp
        parts = [ln.astype(BF16)]
        for _ in range(SB_PIECES - 1):
            parts.append((ln - sum(p.astype(F32) for p in parts)).astype(BF16))
        sums = _dot(jnp.concatenate(parts, axis=0), upper)
        suf = sum(sums[p * R:(p + 1) * R] for p in range(SB_PIECES))
        a = jnp.exp(z - sp + suf + run)
        if diag:
            a = jnp.where(strict, a, 0.0)
        acc = acc + _dot(a.astype(BF16), v_ref[u, pl.ds(r, T), :])
        return run + suf[:, 0:1] + ln[:, 0:1], acc

    state = []
    for n in range(len(units)):
        state.extend(block(n, qi, True, jnp.zeros((R, 1), F32), jnp.zeros((R, d), F32)))

    def earlier(i, st):
        out = []
        for n in range(len(units)):
            out.extend(block(n, qi - i, False, st[2 * n], st[2 * n + 1]))
        return tuple(out)

    state = lax.fori_loop(1, qi + 1, earlier, tuple(state))
    for n, (u, s) in enumerate(units):
        out_ref[u, s * R:(s + 1) * R, :] = state[2 * n + 1].astype(out_ref.dtype)


def _stick_breaking(proj_c, B, S):
    T = SB_BLOCK
    nq = S // T
    d = HEAD_DIM
    H = N_HEADS
    U = SB_HEADS
    return pl.pallas_call(
        _sb_kernel,
        grid=(B, H // U, nq),
        in_specs=[
            pl.BlockSpec((U, T, d), lambda b, h, i: (C_SQ // U + h, b * nq + i, 0)),
            pl.BlockSpec((U, S, d), lambda b, h, i: (C_SK // U + h, b, 0)),
            pl.BlockSpec((U, S, d), lambda b, h, i: (C_SV // U + h, b, 0)),
        ],
        out_specs=pl.BlockSpec((U, T, d), lambda b, h, i: (h, b * nq + i, 0)),
        out_shape=jax.ShapeDtypeStruct((H, B * S, d), BF16),
        compiler_params=_params(("parallel", "parallel", "parallel")),
        name="stick_breaking",
    )(proj_c, proj_c, proj_c)


def _merge_kernel(hm_ref, hd_ref, hs_ref, g0_ref, g1_ref, g2_ref, w_ref, o_ref, wb):
    H = N_HEADS

    @pl.when(pl.program_id(1) == 0)
    def _():
        wb[...] = w_ref[...].astype(BF16)

    acc = None
    ns = o_ref.shape[1] // LANES
    for n, (br, gr) in enumerate(((hm_ref, g0_ref), (hd_ref, g1_ref), (hs_ref, g2_ref))):
        a = jnp.concatenate([br[h] for h in range(H)], axis=1)
        up = _dot(a, wb[n])
        gate = jnp.concatenate([gr[s] for s in range(ns)], axis=1).astype(F32)
        term = _sigmoid(gate) * up
        acc = term if acc is None else acc + term
    o_ref[...] = acc.astype(o_ref.dtype)


def _merge(hm, hd, hs, proj_c, w_branch, layer, tm=512, tn=512):
    H, n, d = hm.shape
    D = w_branch.shape[3]
    ns = tn // LANES
    br = pl.BlockSpec((H, tm, d), lambda j, i: (0, i, 0))

    def gate(b):
        first = (C_GATE + b * (D // LANES)) // ns
        return pl.BlockSpec((ns, tm, LANES), lambda j, i: (first + j, i, 0))

    return pl.pallas_call(
        _merge_kernel,
        grid=(D // tn, n // tm),
        in_specs=[br, br, br, gate(0), gate(1), gate(2),
                  pl.BlockSpec((None, N_BRANCHES, H * d, tn), lambda j, i: (layer, 0, 0, j))],
        out_specs=pl.BlockSpec((tm, tn), lambda j, i: (i, j)),
        out_shape=jax.ShapeDtypeStruct((n, D), BF16),
        scratch_shapes=[pltpu.VMEM((N_BRANCHES, H * d, tn), BF16)],
        compiler_params=_params(("parallel", "arbitrary")),
        name="branch_merge",
    )(hm, hd, hs, proj_c, proj_c, proj_c, w_branch)


def _mm_res_kernel(a_ref, w_ref, x_ref, o_ref, wb):
    @pl.when(pl.program_id(1) == 0)
    def _():
        wb[...] = w_ref[...].astype(BF16)

    o_ref[...] = x_ref[...] + _dot(a_ref[...], wb[...])


def _matmul_residual(a, w, layer, x, tm=512, tn=512):
    m, k = a.shape
    n = w.shape[2]
    return pl.pallas_call(
        _mm_res_kernel,
        grid=(n // tn, m // tm),
        in_specs=[pl.BlockSpec((tm, k), lambda j, i: (i, 0)),
                  pl.BlockSpec((None, k, tn), lambda j, i: (layer, 0, j)),
                  pl.BlockSpec((tm, tn), lambda j, i: (i, j))],
        out_specs=pl.BlockSpec((tm, tn), lambda j, i: (i, j)),
        out_shape=jax.ShapeDtypeStruct((m, n), F32),
        scratch_shapes=[pltpu.VMEM((k, tn), BF16)],
        compiler_params=_params(("parallel", "arbitrary")),
        name="out_proj",
    )(a, w, x)


def _router_kernel(x_ref, g_ref, wr_ref, br_ref, xn_ref, ids_ref, wts_ref, cnt_ref, carry):
    i = pl.program_id(0)
    tm, D = x_ref.shape

    @pl.when(i == 0)
    def _():
        carry[...] = jnp.zeros(carry.shape, F32)

    x = x_ref[...]
    xn = x * lax.rsqrt(jnp.mean(x * x, axis=-1, keepdims=True) + EPS) * g_ref[...]
    xn_ref[...] = xn
    logits = jnp.dot(xn, wr_ref[...], preferred_element_type=F32, precision=lax.Precision.HIGHEST)
    biased = logits + br_ref[...]
    lane = lax.broadcasted_iota(I32, (tm, LANES), 1)
    lanef = lane.astype(F32)
    big = float(LANES)

    def first_lane(mask):
        return jnp.min(jnp.where(mask, lanef, big), axis=1, keepdims=True)

    gmask = lane < N_GROUPS
    gmax = jnp.max(jnp.where(gmask, biased, NEG), axis=1, keepdims=True)
    g_sel = first_lane(gmask & (biased == gmax))
    gm = jnp.max(jnp.where(gmask, logits, NEG), axis=1, keepdims=True)
    ge = jnp.where(gmask, jnp.exp(logits - gm), 0.0)
    p_group = jnp.sum(jnp.where(lanef == g_sel, ge, 0.0), axis=1, keepdims=True) / jnp.sum(ge, axis=1, keepdims=True)
    lo = N_GROUPS + EXPERTS_PER_GROUP * g_sel
    emask = (lanef >= lo) & (lanef < lo + EXPERTS_PER_GROUP)
    eb = jnp.where(emask, biased, NEG)
    e1 = first_lane(emask & (eb == jnp.max(eb, axis=1, keepdims=True)))
    emask2 = emask & (lanef != e1)
    eb2 = jnp.where(emask2, biased, NEG)
    e2 = first_lane(emask2 & (eb2 == jnp.max(eb2, axis=1, keepdims=True)))
    em = jnp.max(jnp.where(emask, logits, NEG), axis=1, keepdims=True)
    ee = jnp.where(emask, jnp.exp(logits - em), 0.0)
    s1 = jnp.sum(jnp.where(lanef == e1, ee, 0.0), axis=1, keepdims=True)
    s2 = jnp.sum(jnp.where(lanef == e2, ee, 0.0), axis=1, keepdims=True)
    se = jnp.sum(ee, axis=1, keepdims=True)
    w1 = s1 / se
    w2 = s2 / se
    wsum = w1 + w2
    w1 = p_group * w1 / wsum
    w2 = p_group * w2 / wsum
    x1 = e1 - N_GROUPS
    x2 = e2 - N_GROUPS

    onehot = jnp.where((lanef == x1) | (lanef == x2), 1.0, 0.0)
    r = lax.broadcasted_iota(I32, (tm, tm), 0)
    c = lax.broadcasted_iota(I32, (tm, tm), 1)
    lower = jnp.where(c < r, 1.0, 0.0).astype(BF16)
    prefix = _dot(lower, onehot.astype(BF16)) + carry[...]
    r1 = jnp.sum(jnp.where(lanef == x1, prefix, 0.0), axis=1, keepdims=True)
    r2 = jnp.sum(jnp.where(lanef == x2, prefix, 0.0), axis=1, keepdims=True)
    carry[...] = carry[...] + jnp.sum(onehot, axis=0, keepdims=True)
    cnt_ref[...] = carry[...].astype(I32)

    idsf = jnp.where(lane == 0, x1, jnp.where(lane == 1, x2, jnp.where(lane == 2, r1, jnp.where(lane == 3, r2, 0.0))))
    ids_ref[...] = idsf.astype(I32)
    wts_ref[...] = jnp.where(lane == 0, w1, jnp.where(lane == 1, w2, 0.0))


def _router(x2d, g, w_router, b_router, tm=512):
    n, D = x2d.shape
    return pl.pallas_call(
        _router_kernel,
        grid=(n // tm,),
        in_specs=[pl.BlockSpec((tm, D), lambda i: (i, 0)), pl.BlockSpec((1, D), lambda i: (0, 0)),
                  pl.BlockSpec((D, LANES), lambda i: (0, 0)), pl.BlockSpec((1, LANES), lambda i: (0, 0))],
        out_specs=[pl.BlockSpec((tm, D), lambda i: (i, 0)),
                   pl.BlockSpec((tm, LANES), lambda i: (i, 0)),
                   pl.BlockSpec((tm, LANES), lambda i: (i, 0)),
                   pl.BlockSpec((1, LANES), lambda i: (0, 0))],
        out_shape=[jax.ShapeDtypeStruct((n, D), F32),
                   jax.ShapeDtypeStruct((n, LANES), I32),
                   jax.ShapeDtypeStruct((n, LANES), F32),
                   jax.ShapeDtypeStruct((1, LANES), I32)],
        scratch_shapes=[pltpu.VMEM((1, LANES), F32)],
        compiler_params=_params(("arbitrary",)),
        name="moe_router",
    )(x2d, g.reshape(1, D), w_router, b_router)


def _dispatch_kernel(p1_ref, p2_ref, xn_ref, xs_in_ref, xs_ref, sem, *, tb):
    del xs_in_ref
    base = pl.program_id(0) * tb

    def copies(t):
        src = xn_ref.at[pl.ds(t, 1), :]
        return (pltpu.make_async_copy(src, xs_ref.at[pl.ds(p1_ref[base + t], 1), :], sem),
                pltpu.make_async_copy(src, xs_ref.at[pl.ds(p2_ref[base + t], 1), :], sem))

    def issue(t, _):
        for cp in copies(t):
            cp.start()
        return 0

    lax.fori_loop(0, tb, issue, 0)

    def drain(t, _):
        for cp in copies(t):
            cp.wait()
        return 0

    lax.fori_loop(0, tb, drain, 0)


def _dispatch(pos1, pos2, xn, xs0, tb=512):
    n, D = xn.shape
    n_rows = xs0.shape[0]
    tb = min(tb, n)
    grid_spec = pltpu.PrefetchScalarGridSpec(
        num_scalar_prefetch=2,
        grid=(n // tb,),
        in_specs=[pl.BlockSpec((tb, D), lambda i, p1, p2: (i, 0)),
                  pl.BlockSpec(memory_space=pl.ANY)],
        out_specs=pl.BlockSpec(memory_space=pl.ANY),
        scratch_shapes=[pltpu.SemaphoreType.DMA(())],
    )
    return pl.pallas_call(
        functools.partial(_dispatch_kernel, tb=tb),
        grid_spec=grid_spec,
        out_shape=jax.ShapeDtypeStruct((n_rows, D), F32),
        input_output_aliases={3: 0},
        compiler_params=pltpu.CompilerParams(dimension_semantics=("arbitrary",), has_side_effects=True),
        name="moe_dispatch",
    )(pos1, pos2, xn, xs0)


def _expert_kernel(te_ref, nv_ref, xs_ref, wg_ref, wu_ref, wd_ref, ys_ref, wgb, wub, wdb):
    i = pl.program_id(0)
    prev = te_ref[jnp.maximum(i - 1, 0)]

    @pl.when(i < nv_ref[0])
    def _():
        @pl.when((i == 0) | (te_ref[i] != prev))
        def _():
            wgb[...] = wg_ref[...].astype(BF16)
            wub[...] = wu_ref[...].astype(BF16)
            wdb[...] = wd_ref[...].astype(BF16)

        x = xs_ref[...].astype(BF16)
        g = _dot(x, wgb[...])
        u = _dot(x, wub[...])
        hcur = (g * _sigmoid(g) * u).astype(BF16)
        ys_ref[...] = _dot(hcur, wdb[...])

    @pl.when(i >= nv_ref[0])
    def _():
        ys_ref[...] = jnp.zeros(ys_ref.shape, F32)


def _experts(tile_expert, n_valid, xs, w_gate, w_up, w_down, layer):
    n_rows = xs.shape[0]
    _, E, D, Fe = w_gate.shape
    tm = MOE_TILE
    n_tiles = n_rows // tm

    def row_map(i, te, nv):
        return (jnp.minimum(i, jnp.maximum(nv[0] - 1, 0)), 0)

    def w_map(i, te, nv):
        return (layer, te[i], 0, 0)

    grid_spec = pltpu.PrefetchScalarGridSpec(
        num_scalar_prefetch=2,
        grid=(n_tiles,),
        in_specs=[pl.BlockSpec((tm, D), row_map),
                  pl.BlockSpec((None, None, D, Fe), w_map), pl.BlockSpec((None, None, D, Fe), w_map),
                  pl.BlockSpec((None, None, Fe, D), w_map)],
        out_specs=pl.BlockSpec((tm, D), lambda i, te, nv: (i, 0)),
        scratch_shapes=[pltpu.VMEM((D, Fe), BF16), pltpu.VMEM((D, Fe), BF16), pltpu.VMEM((Fe, D), BF16)],
    )
    return pl.pallas_call(
        _expert_kernel,
        grid_spec=grid_spec,
        out_shape=jax.ShapeDtypeStruct((n_rows, D), F32),
        compiler_params=_params(("arbitrary",)),
        name="moe_experts",
    )(tile_expert, n_valid, xs, w_gate, w_up, w_down)


def _combine_kernel(p1_ref, p2_ref, ys_ref, x_ref, w_ref, o_ref, buf, sem):
    i = pl.program_id(0)
    tc = x_ref.shape[0]
    slot = i % 2

    def copies(tile, slot, t):
        tok = tile * tc + t
        return (pltpu.make_async_copy(ys_ref.at[pl.ds(p1_ref[tok], 1), :], buf.at[slot, 0, pl.ds(t, 1), :],
                                      sem.at[slot]),
                pltpu.make_async_copy(ys_ref.at[pl.ds(p2_ref[tok], 1), :], buf.at[slot, 1, pl.ds(t, 1), :],
                                      sem.at[slot]))

    def issue(tile, slot):
        def body(t, _):
            for cp in copies(tile, slot, t):
                cp.start()
            return 0

        lax.fori_loop(0, tc, body, 0)

    @pl.when(i == 0)
    def _():
        issue(0, 0)

    @pl.when(i + 1 < pl.num_programs(0))
    def _():
        issue(i + 1, 1 - slot)

    def drain(t, _):
        for cp in copies(i, slot, t):
            cp.wait()
        return 0

    lax.fori_loop(0, tc, drain, 0)
    w = w_ref[...]
    o_ref[...] = x_ref[...] + w[:, 0:1] * buf[slot, 0] + w[:, 1:2] * buf[slot, 1]


def _combine(pos1, pos2, ys, x2d, wts, tc=256):
    n, D = x2d.shape
    grid_spec = pltpu.PrefetchScalarGridSpec(
        num_scalar_prefetch=2,
        grid=(n // tc,),
        in_specs=[pl.BlockSpec(memory_space=pl.ANY),
                  pl.BlockSpec((tc, D), lambda i, p1, p2: (i, 0)),
                  pl.BlockSpec((tc, LANES), lambda i, p1, p2: (i, 0))],
        out_specs=pl.BlockSpec((tc, D), lambda i, p1, p2: (i, 0)),
        scratch_shapes=[pltpu.VMEM((2, 2, tc, D), F32), pltpu.SemaphoreType.DMA((2,))],
    )
    return pl.pallas_call(
        _combine_kernel,
        grid_spec=grid_spec,
        out_shape=jax.ShapeDtypeStruct((n, D), F32),
        compiler_params=_params(("arbitrary",)),
        name="moe_combine",
    )(pos1, pos2, ys, x2d, wts)


def _pos_kernel(ids_ref, cnt_ref, pos_ref):
    tm = ids_ref.shape[0]
    tiles = jnp.floor((cnt_ref[...].astype(F32) + (MOE_TILE - 1)) / MOE_TILE)
    r = lax.broadcasted_iota(I32, (LANES, LANES), 0)
    c = lax.broadcasted_iota(I32, (LANES, LANES), 1)
    before = jnp.where(r < c, 1.0, 0.0).astype(BF16)
    first_tile = _dot(jnp.broadcast_to(tiles, (8, LANES)).astype(BF16), before)
    offs = first_tile[0:1, :] * MOE_TILE
    ids = ids_ref[...].astype(F32)
    lane = lax.broadcasted_iota(I32, (tm, LANES), 1)
    lanef = lane.astype(F32)
    p1 = jnp.sum(jnp.where(lanef == ids[:, 0:1], offs, 0.0), axis=1, keepdims=True) + ids[:, 2:3]
    p2 = jnp.sum(jnp.where(lanef == ids[:, 1:2], offs, 0.0), axis=1, keepdims=True) + ids[:, 3:4]
    packed = jnp.where(lane == 0, p1, jnp.where(lane == 1, p2, 0.0))
    pos_ref[...] = packed.T[0:8, :].astype(I32)


def _positions(ids, counts, tm=512):
    n = ids.shape[0]
    assert MOE_TILE & (MOE_TILE - 1) == 0
    return pl.pallas_call(
        _pos_kernel,
        grid=(n // tm,),
        in_specs=[pl.BlockSpec((tm, LANES), lambda i: (i, 0)), pl.BlockSpec((1, LANES), lambda i: (0, 0))],
        out_specs=pl.BlockSpec((8, tm), lambda i: (0, i)),
        out_shape=jax.ShapeDtypeStruct((8, n), I32),
        compiler_params=_params(("parallel",)),
        name="moe_positions",
    )(ids, counts)


def _hier_moe(x2d, norm_g, w_rg, b_rg, w_re, b_re, w_gate, w_up, w_down, layer, xs_buf):
    n, D = x2d.shape
    tm = MOE_TILE
    pad = LANES - N_GROUPS - N_EXPERTS
    w_router = jnp.concatenate([w_rg, w_re, jnp.zeros((D, pad), F32)], axis=1)
    b_router = jnp.concatenate([b_rg, b_re.reshape(-1), jnp.zeros((pad,), F32)]).reshape(1, LANES)
    xn3, ids, wts, counts = _router(x2d, norm_g, w_router, b_router)
    cnt = counts[0, :N_EXPERTS]
    padded = ((cnt + tm - 1) // tm) * tm
    ends = jnp.cumsum(padded)
    n_tiles = xs_buf.shape[0] // tm
    tile_start = jnp.arange(n_tiles, dtype=I32) * tm
    tile_expert = jnp.minimum(jnp.sum(tile_start[:, None] >= ends[None, :], axis=1), N_EXPERTS - 1).astype(I32)
    n_valid = (ends[-1] // tm).astype(I32).reshape(1)
    last_e = tile_expert[jnp.maximum(n_valid[0] - 1, 0)]
    tile_expert = jnp.where(jnp.arange(n_tiles) < n_valid[0], tile_expert, last_e)
    pos = _positions(ids, counts)
    pos1, pos2 = pos[0], pos[1]
    xs = _dispatch(pos1, pos2, xn3, xs_buf)
    ys = _experts(tile_expert, n_valid, xs, w_gate, w_up, w_down, layer)
    return _combine(pos1, pos2, ys, x2d, wts), xs


def _in_proj_regions(D):
    bw = BRANCH_WIDTH
    sizes = (bw, bw, bw, bw, N_HEADS, N_HEADS, bw, bw, bw, IDX_HEADS * IDX_DIM, IDX_DIM, IDX_HEADS,
             bw, bw, bw, N_BRANCHES * D)
    offs = np.concatenate([[0], np.cumsum(sizes)]).tolist()
    region_a = (offs[0], offs[4] - offs[0])
    region_b = (offs[6], offs[10] - offs[6])
    region_c = (offs[12], offs[16] - offs[12])
    small = dict(mi=offs[4], mf=offs[5], ik=offs[10], iw=offs[11])
    return region_a, region_b, region_c, small


def _token_mixer(x2d, B, S, layer, norm_g, w_in, conv_w, b_i, b_f, mlstm_norm_g, q_norm_g, k_norm_g,
                 w_branch, w_out, rel_bias):
    n, D = x2d.shape
    L = MLSTM_CHUNK
    H = N_HEADS
    tm = min(n, PROJ_TM)
    xn = _rmsnorm(x2d, norm_g)
    ra, rb, rc, small = _in_proj_regions(D)
    w_t = jnp.swapaxes(w_in, 1, 2)
    proj_a = _in_proj(xn, w_t, layer, ra[0], ra[1], tm)
    proj_b = _in_proj(xn, w_t, layer, rb[0], rb[1], tm)
    proj_c = _in_proj(xn, w_t, layer, rc[0], rc[1], tm)
    tail = _tail_proj(xn, w_t, layer, small, tm)
    g = tail[1][:, IDX_HEADS:IDX_HEADS + 2 * H].reshape(B, S // L, L, 2, H)
    gates_t = jnp.transpose(g, (0, 4, 1, 3, 2))
    hm = _mlstm(proj_a, gates_t, jnp.stack([b_i, b_f]), conv_w, mlstm_norm_g, B, S)
    hd = _dsa(proj_b, tail, q_norm_g, k_norm_g, rel_bias, B, S)
    hs = _stick_breaking(proj_c, B, S)
    merged = _merge(hm, hd, hs, proj_c, w_branch, layer)
    return _matmul_residual(merged, w_out, layer, x2d)


def kernel(x, norm1_g, w_in, conv_w, b_i, b_f, mlstm_norm_g, q_norm_g, k_norm_g, w_branch, w_out, norm2_g,
           w_router_g, b_router_g, w_router_e, b_router_e, w_gate, w_up, w_down, rel_bias):
    B, S, D = x.shape
    x2d = x.reshape(B * S, D)
    xs_buf = jnp.zeros((2 * B * S + N_EXPERTS * MOE_TILE, D), F32)
    for l in range(w_in.shape[0]):
        x2d = _token_mixer(x2d, B, S, l, norm1_g[l], w_in, conv_w[l], b_i[l], b_f[l], mlstm_norm_g[l],
                           q_norm_g[l], k_norm_g[l], w_branch, w_out, rel_bias)
        x2d, xs_buf = _hier_moe(x2d, norm2_g[l], w_router_g[l], b_router_g[l], w_router_e[l], b_router_e[l],
                                w_gate, w_up, w_down, l, xs_buf)
    return x2d.reshape(B, S, D)
```

```python
import functools
import math

import numpy as np
import jax
import jax.numpy as jnp
from jax import lax
from jax.experimental import pallas as pl
from jax.experimental.pallas import tpu as pltpu

F32 = jnp.float32
BF16 = jnp.bfloat16
I32 = jnp.int32

LANES = 128
HEAD_DIM = 128
N_HEADS = 8
BRANCH_WIDTH = N_HEADS * HEAD_DIM
N_BRANCHES = 3
CONV_WIDTH = 4
IDX_HEADS = 16
IDX_DIM = 64
TOPK_MAX = 256
N_BUCKETS = 32
MAX_DISTANCE = 128
N_GROUPS = 4
EXPERTS_PER_GROUP = 8
N_EXPERTS = N_GROUPS * EXPERTS_PER_GROUP
EPS = 1e-6
NEG = -1e30
INT_MIN = -(2 ** 31)

MLSTM_CHUNK = 256
DSA_BLOCK = 128
DSA_BANDS = 4
SB_BLOCK = 512
SB_ROWS = 512
SB_PIECES = 1
SB_HEADS = 4
HEADS_PER_STEP = 4
MOE_TILE = 256
DMA_UNROLL = 8
PROJ_TN = 512
PROJ_TM = 2048
VMEM_LIMIT = 56 * 1024 * 1024

A_MQ, A_MK, A_MV, A_MO = 0, 8, 16, 24
B_DQ, B_DK, B_DV, B_IQ = 0, 8, 16, 24
C_SQ, C_SK, C_SV, C_GATE = 0, 8, 16, 24


def _params(sem):
    return pltpu.CompilerParams(dimension_semantics=sem, vmem_limit_bytes=VMEM_LIMIT)


def _dot(a, b):
    return jnp.dot(a, b, preferred_element_type=F32)


def _dot_nt(a, b):
    return lax.dot_general(a, b, (((1,), (1,)), ((), ())), preferred_element_type=F32)


def _sigmoid(z):
    return 1.0 / (1.0 + jnp.exp(-z))


def _rmsnorm_kernel(x_ref, g_ref, o_ref):
    x = x_ref[...]
    ms = jnp.mean(x * x, axis=-1, keepdims=True)
    o_ref[...] = (x * lax.rsqrt(ms + EPS) * g_ref[...]).astype(o_ref.dtype)


def _rmsnorm(x2d, g, tm=512):
    n, d = x2d.shape
    return pl.pallas_call(
        _rmsnorm_kernel,
        grid=(n // tm,),
        in_specs=[pl.BlockSpec((tm, d), lambda i: (i, 0)), pl.BlockSpec((1, d), lambda i: (0, 0))],
        out_specs=pl.BlockSpec((tm, d), lambda i: (i, 0)),
        out_shape=jax.ShapeDtypeStruct((n, d), BF16),
        compiler_params=_params(("parallel",)),
        name="rmsnorm",
    )(x2d, g.reshape(1, d))


def _in_proj_kernel(x_ref, *rest, shift, nblk):
    w_refs, o_ref, wb = rest[:nblk], rest[nblk], rest[nblk + 1]
    tn = wb.shape[1]

    @pl.when(pl.program_id(1) == 0)
    def _():
        w = jnp.concatenate([r[...] for r in w_refs], axis=0)
        wb[...] = w[shift:shift + tn, :].T.astype(BF16)

    acc = _dot(x_ref[...], wb[...])
    for j in range(o_ref.shape[0]):
        o_ref[j] = acc[:, j * LANES:(j + 1) * LANES].astype(o_ref.dtype)


def _in_proj(xn, w_t, layer, col0, ncols, tm):
    m, k = xn.shape
    tn = PROJ_TN
    base, shift = divmod(col0, LANES)
    assert shift % 8 == 0
    nblk = tn // LANES + (1 if shift else 0)
    per = tn // LANES

    def wspec(r):
        return pl.BlockSpec((None, LANES, k), lambda j, i: (layer, base + per * j + r, 0))

    return pl.pallas_call(
        functools.partial(_in_proj_kernel, shift=shift, nblk=nblk),
        grid=(ncols // tn, m // tm),
        in_specs=[pl.BlockSpec((tm, k), lambda j, i: (i, 0))] + [wspec(r) for r in range(nblk)],
        out_specs=pl.BlockSpec((per, tm, LANES), lambda j, i: (j, i, 0)),
        out_shape=jax.ShapeDtypeStruct((ncols // LANES, m, LANES), BF16),
        scratch_shapes=[pltpu.VMEM((k, tn), BF16)],
        compiler_params=_params(("parallel", "arbitrary")),
        name="in_proj",
    )(xn, *([w_t] * nblk))


def _tail_kernel(x_ref, wg_ref, wi_ref, o_ref, wt, *, g_lane, ik_lane, iw_lane):
    @pl.when(pl.program_id(0) == 0)
    def _():
        wg = wg_ref[...]
        wi = wi_ref[...]
        ik = wi[ik_lane:ik_lane + IDX_DIM, :]
        iw = wi[iw_lane:iw_lane + IDX_HEADS, :]
        gates = wg[g_lane:g_lane + 2 * N_HEADS, :]
        pad = jnp.zeros((LANES - IDX_HEADS - 2 * N_HEADS, wg.shape[1]), F32)
        wt[...] = jnp.concatenate([ik, ik, iw, gates, pad], axis=0).T.astype(BF16)

    acc = _dot(x_ref[...], wt[...])
    o_ref[0] = acc[:, :LANES]
    o_ref[1] = acc[:, LANES:]


def _tail_proj(xn, w_t, layer, small, tm):
    m, k = xn.shape
    g_blk, g_lane = divmod(small["mi"], LANES)
    i_blk, ik_lane = divmod(small["ik"], LANES)
    iw_lane = small["iw"] - i_blk * LANES
    assert small["mf"] == small["mi"] + N_HEADS and g_lane + 2 * N_HEADS <= LANES
    assert ik_lane + IDX_DIM <= LANES and 0 <= iw_lane and iw_lane + IDX_HEADS <= LANES
    assert g_lane % 8 == 0 and ik_lane % 8 == 0 and iw_lane % 8 == 0
    return pl.pallas_call(
        functools.partial(_tail_kernel, g_lane=g_lane, ik_lane=ik_lane, iw_lane=iw_lane),
        grid=(m // tm,),
        in_specs=[pl.BlockSpec((tm, k), lambda i: (i, 0)),
                  pl.BlockSpec((None, LANES, k), lambda i: (layer, g_blk, 0)),
                  pl.BlockSpec((None, LANES, k), lambda i: (layer, i_blk, 0))],
        out_specs=pl.BlockSpec((2, tm, LANES), lambda i: (0, i, 0)),
        out_shape=jax.ShapeDtypeStruct((2, m, LANES), F32),
        scratch_shapes=[pltpu.VMEM((k, 2 * LANES), BF16)],
        compiler_params=_params(("arbitrary",)),
        name="tail_proj",
    )(xn, w_t, w_t)


def _mlstm_kernel(bias_ref, q_ref, k_ref, v_ref, o_ref, g_ref, cwq_ref, cwk_ref, ng_ref, out_ref,
                  qf, kf, qc, kc, st):
    hp = pl.program_id(1)
    U = HEADS_PER_STEP
    S = q_ref.shape[1]
    L = MLSTM_CHUNK
    nc = S // L
    d = HEAD_DIM
    PAD = 8

    R = min(S, 256)
    for u in range(U):
        qf[u, 0:PAD, :] = jnp.zeros((PAD, d), F32)
        kf[u, 0:PAD, :] = jnp.zeros((PAD, d), F32)
        qf[u, PAD:PAD + S, :] = q_ref[u].astype(F32)
        kf[u, PAD:PAD + S, :] = k_ref[u].astype(F32)
        ls = slice(u * d, (u + 1) * d)
        for r0 in range(0, S, R):
            aq = jnp.zeros((R, d), F32)
            ak = jnp.zeros((R, d), F32)
            for t in range(CONV_WIDTH):
                off = PAD - (CONV_WIDTH - 1) + t + r0
                aq = aq + cwq_ref[t:t + 1, ls] * qf[u, off:off + R, :]
                ak = ak + cwk_ref[t:t + 1, ls] * kf[u, off:off + R, :]
            qc[u, r0:r0 + R, :] = (aq * _sigmoid(aq) * (d ** -0.5)).astype(BF16)
            kc[u, r0:r0 + R, :] = ak * _sigmoid(ak)

    st[...] = jnp.zeros(st.shape, F32)
    row = lax.broadcasted_iota(I32, (L, L), 0)
    col = lax.broadcasted_iota(I32, (L, L), 1)
    causal = col <= row
    eye = col == row
    lane = lax.broadcasted_iota(I32, (L, d), 1)
    ones_col = jnp.where(lane == 0, 1.0, 0.0).astype(BF16)
    ng = ng_ref[...]

    def chunk_one(u, c, r, m):
        q = qc[u, pl.ds(r, L), :]
        kT = kc[u, pl.ds(r, L), :].T
        v = v_ref[u, pl.ds(r, L), :]
        vaug = jnp.concatenate([v, ones_col], axis=1)
        gates = g_ref[u, c]
        i_row = gates[0:1, :] + bias_ref[0, hp * U + u]
        f_row = gates[1:2, :] + bias_ref[1, hp * U + u]
        lf_row = jnp.minimum(f_row, 0.0) - jnp.log1p(jnp.exp(-jnp.abs(f_row)))
        b_col = jnp.sum(jnp.where(causal, lf_row, 0.0), axis=1, keepdims=True)
        b_row = jnp.sum(jnp.where(eye, b_col, 0.0), axis=0, keepdims=True)
        dlog = jnp.where(causal, b_col - b_row + i_row, NEG)
        inter = b_col + m
        m_t = jnp.maximum(inter, jnp.max(dlog, axis=1, keepdims=True))
        w_intra = jnp.exp(dlog - m_t)
        w_inter = jnp.exp(inter - m_t)
        s = _dot(q, kT.astype(BF16)) * w_intra
        res = w_inter * _dot(q, st[u].astype(BF16)) + _dot(s.astype(BF16), vaug)
        num = res[:, :d]
        den = res[:, d:d + 1]
        hh = num / jnp.maximum(jnp.abs(den), jnp.exp(-m_t))
        hn = hh * lax.rsqrt(jnp.mean(hh * hh, axis=1, keepdims=True) + EPS) * ng[:, u * d:(u + 1) * d]
        og = o_ref[u, pl.ds(r, L), :].astype(F32)
        out_ref[u, pl.ds(r, L), :] = (hn * _sigmoid(og)).astype(out_ref.dtype)
        ws_row = w_intra[L - 1:L, :]
        decay = w_inter[L - 1:L, :]
        st[u] = decay * st[u] + _dot((kT * ws_row).astype(BF16), vaug)
        return m_t[L - 1:L, :]

    def chunk(c, ms):
        r = pl.multiple_of(c * L, L)
        return tuple(chunk_one(u, c, r, ms[u]) for u in range(U))

    lax.fori_loop(0, nc, chunk, tuple(jnp.zeros((1, 1), F32) for _ in range(U)))


def _mlstm(proj_a, gates_t, bias_if, conv_w, norm_g, B, S):
    L = MLSTM_CHUNK
    nc = S // L
    d = HEAD_DIM
    H = N_HEADS
    U = HEADS_PER_STEP

    def slab(off):
        return pl.BlockSpec((U, S, d), lambda b, h: (off // U + h, b, 0))

    return pl.pallas_call(
        _mlstm_kernel,
        grid=(B, H // U),
        in_specs=[
            pl.BlockSpec(memory_space=pltpu.SMEM),
            slab(A_MQ), slab(A_MK), slab(A_MV), slab(A_MO),
            pl.BlockSpec((None, U, nc, 2, L), lambda b, h: (b, h, 0, 0, 0)),
            pl.BlockSpec((CONV_WIDTH, U * d), lambda b, h: (0, h)),
            pl.BlockSpec((CONV_WIDTH, U * d), lambda b, h: (0, H // U + h)),
            pl.BlockSpec((1, U * d), lambda b, h: (0, h)),
        ],
        out_specs=pl.BlockSpec((U, S, d), lambda b, h: (h, b, 0)),
        out_shape=jax.ShapeDtypeStruct((H, B * S, d), BF16),
        scratch_shapes=[
            pltpu.VMEM((U, S + 8, d), F32), pltpu.VMEM((U, S + 8, d), F32),
            pltpu.VMEM((U, S, d), BF16), pltpu.VMEM((U, S, d), F32),
            pltpu.VMEM((U, d, 2 * d), F32),
        ],
        compiler_params=_params(("parallel", "parallel")),
        name="mlstm",
    )(bias_if, proj_a, proj_a, proj_a, proj_a, gates_t, conv_w, conv_w, norm_g.reshape(1, H * d))


def _t5_thresholds():
    max_exact = N_BUCKETS // 2
    n = np.arange(0, 2 * MAX_DISTANCE)
    nf = np.maximum(n, 1).astype(np.float64)
    val = np.log(nf / max_exact) / math.log(MAX_DISTANCE / max_exact) * (N_BUCKETS - max_exact)
    frac = np.abs(val - np.round(val))
    frac_ok = (frac > 1e-4) | (n <= max_exact) | (n >= MAX_DISTANCE)
    assert frac_ok.all()
    large = np.minimum(max_exact + np.trunc(val).astype(np.int64), N_BUCKETS - 1)
    bucket = np.where(n < max_exact, n, large)
    assert (np.diff(bucket) >= 0).all() and bucket[MAX_DISTANCE] == N_BUCKETS - 1
    return [int(np.argmax(bucket >= j)) for j in range(1, N_BUCKETS)]


_T5_THR = _t5_thresholds()


def _dsa_kernel(rb_ref, q_ref, k_ref, v_ref, iq_ref, ik_ref, wt_ref, gq_ref, gk_ref, prev_ref, out_ref,
                kn, ikk, wb, key_ref, mask_ref, bias_ref, *, topk, q0, nb):
    del prev_ref
    qi = q0 + pl.program_id(1)
    T = DSA_BLOCK
    d = HEAD_DIM
    H = N_HEADS
    HALF = T // 2
    row = lax.broadcasted_iota(I32, (T, T), 0)
    col = lax.broadcasted_iota(I32, (T, T), 1)

    @pl.when(pl.program_id(1) == 0)
    def _per_batch():
        gk = gk_ref[...]

        def norm_k(h, _):
            kh = k_ref[h].astype(F32)
            kn[h] = (kh * lax.rsqrt(jnp.mean(kh * kh, axis=1, keepdims=True) + EPS) * gk).astype(BF16)
            for o in range(2):
                n = o * T + row - col
                val = jnp.full((T, T), rb_ref[0, h], F32)
                for j, thr in enumerate(_T5_THR):
                    val = jnp.where(n >= thr, rb_ref[j + 1, h], val)
                bias_ref[h, o] = val
            bias_ref[h, 2] = jnp.full((T, T), rb_ref[N_BUCKETS - 1, h], F32)
            return 0

        lax.fori_loop(0, H, norm_k, 0)
        ik = ik_ref[...]
        klane = lax.broadcasted_iota(I32, ik.shape, 1)
        ikk[0] = jnp.where(klane < IDX_DIM, ik, 0.0).astype(BF16)
        ikk[1] = jnp.where(klane >= IDX_DIM, ik, 0.0).astype(BF16)

    wsc = wt_ref[...] * (IDX_HEADS ** -0.5 * IDX_DIM ** -0.5)
    for h16 in range(IDX_HEADS):
        wb[h16] = jnp.broadcast_to(wsc[:, h16:h16 + 1], (T, LANES))
    q_pairs = iq_ref[...].reshape((IDX_HEADS // 2) * T, LANES)

    def score_block(j, _):
        r = pl.multiple_of(j * T, T)
        kk = jnp.concatenate([ikk[0, pl.ds(r, T), :], ikk[1, pl.ds(r, T), :]], axis=0)
        dots = _dot_nt(q_pairs, kk)
        sc = jnp.zeros((T, T), F32)
        for h16 in range(IDX_HEADS):
            hp, odd = divmod(h16, 2)
            sc = sc + wb[h16] * jnp.maximum(dots[hp * T:(hp + 1) * T, odd * T:(odd + 1) * T], 0.0)
        bits = lax.bitcast_convert_type(sc, I32)
        key = jnp.where(bits < 0, bits ^ jnp.int32(0x7FFFFFFF), bits)
        key = jnp.where(sc == 0.0, 0, key)
        key = jnp.where(j * T + col <= qi * T + row, key, INT_MIN)
        key_ref[j] = key
        return 0

    lax.fori_loop(0, nb, score_block, 0, unroll=2)

    kf = float(topk)
    ones = jnp.ones((LANES, LANES), BF16)

    def count(lo, pred):
        acc = jnp.zeros((HALF, LANES), F32)
        for jb in range(nb):
            acc = acc + jnp.where(pred(key_ref[jb, lo:lo + HALF, :]), 1.0, 0.0)
        return _dot(acc.astype(BF16), ones)

    halves = (0, HALF)
    zero_i = jnp.zeros((HALF, LANES), I32)
    thr0 = tuple(jnp.where(count(lo, lambda k: k >= zero_i) >= kf, jnp.int32(0), jnp.int32(INT_MIN))
                 for lo in halves)

    def bisect_bit(thrs, bit):
        out = []
        for lo, t in zip(halves, thrs):
            cand = t | bit
            out.append(jnp.where(count(lo, lambda k, c=cand: k >= c) >= kf, cand, t))
        return tuple(out)

    def bisect_pair(it, thrs):
        hi_bit = jnp.left_shift(jnp.int32(1), 29 - 2 * it)
        lo_bit = jnp.left_shift(jnp.int32(1), 28 - 2 * it)
        out = []
        for lo, t in zip(halves, thrs):
            c01, c10, c11 = t | lo_bit, t | hi_bit, t | hi_bit | lo_bit
            n01, n10, n11 = (count(lo, lambda k, c=c: k >= c) >= kf for c in (c01, c10, c11))
            out.append(jnp.where(n11, c11, jnp.where(n10, c10, jnp.where(n01, c01, t))))
        return tuple(out)

    thrs = bisect_bit(thr0, jnp.int32(1 << 30))
    thrs = lax.fori_loop(0, 15, bisect_pair, thrs)
    need = jnp.concatenate([kf - count(lo, lambda k, t=t: k > t) for lo, t in zip(halves, thrs)], axis=0)
    thrb = jnp.concatenate(thrs, axis=0)

    tri = jnp.where(row <= col, 1.0, 0.0).astype(BF16)
    seen = jnp.zeros((T, LANES), F32)
    for jb in range(nb):
        key = key_ref[jb]
        tie = key == thrb
        tie16 = jnp.where(tie, 1.0, 0.0).astype(BF16)
        pre = _dot(tie16, tri) + seen
        m = jnp.where(key > thrb, 0.0, jnp.where(tie, jnp.where(pre <= need, 0.0, NEG), NEG))
        mask_ref[jb] = jnp.where(key == INT_MIN, NEG, m)
        seen = seen + _dot(tie16, ones)

    gq = gq_ref[...]

    def head(h):
        qh = q_ref[h].astype(F32)
        qn = (qh * lax.rsqrt(jnp.mean(qh * qh, axis=1, keepdims=True) + EPS) * gq * (d ** -0.5)).astype(BF16)
        lgs = _dot_nt(qn, kn[h])
        mx = None
        lg_blocks = []
        for jb in range(nb):
            lg = lgs[:, jb * T:(jb + 1) * T] + bias_ref[h, jnp.clip(qi - jb, 0, 2)] + mask_ref[jb]
            lg_blocks.append(lg)
            mx = lg if mx is None else jnp.maximum(mx, lg)
        rowmax = jnp.max(mx, axis=1, keepdims=True)
        l = jnp.zeros((T, T), F32)
        ps = []
        for jb in range(nb):
            p = jnp.exp(lg_blocks[jb] - rowmax)
            l = l + p
            ps.append(p.astype(BF16))
        acc = _dot(jnp.concatenate(ps, axis=1), v_ref[h])
        out_ref[h] = (acc / jnp.sum(l, axis=1, keepdims=True)).astype(out_ref.dtype)

    def head_group(g, _):
        for u in range(HEADS_PER_STEP):
            head(g * HEADS_PER_STEP + u)
        return 0

    lax.fori_loop(0, H // HEADS_PER_STEP, head_group, 0)


def _dsa(proj_b, tail, q_norm_g, k_norm_g, rel_bias, B, S):
    T = DSA_BLOCK
    nq = S // T
    d = HEAD_DIM
    H = N_HEADS
    topk = min(TOPK_MAX, S // 4)
    per = nq // DSA_BANDS
    pb = proj_b.reshape(proj_b.shape[0], B, S, d)
    tl = tail.reshape(tail.shape[0], B, S, LANES)
    out = jnp.zeros((H, B, S, d), BF16)
    for band in range(DSA_BANDS):
        q0 = band * per
        nb = q0 + per
        W = nb * T
        in_specs = [
            pl.BlockSpec(memory_space=pltpu.SMEM),
            pl.BlockSpec((H, None, T, d), lambda b, i: (B_DQ // H, b, q0 + i, 0)),
            pl.BlockSpec((H, None, W, d), lambda b, i: (B_DK // H, b, 0, 0)),
            pl.BlockSpec((H, None, W, d), lambda b, i: (B_DV // H, b, 0, 0)),
            pl.BlockSpec((H, None, T, d), lambda b, i: (B_IQ // H, b, q0 + i, 0)),
            pl.BlockSpec((None, None, W, LANES), lambda b, i: (0, b, 0, 0)),
            pl.BlockSpec((None, None, T, LANES), lambda b, i: (1, b, q0 + i, 0)),
            pl.BlockSpec((1, d), lambda b, i: (0, 0)),
            pl.BlockSpec((1, d), lambda b, i: (0, 0)),
            pl.BlockSpec(memory_space=pl.ANY),
        ]
        args = [rel_bias, pb, pb, pb, pb, tl, tl, q_norm_g.reshape(1, d), k_norm_g.reshape(1, d), out]
        out = pl.pallas_call(
            functools.partial(_dsa_kernel, topk=topk, q0=q0, nb=nb),
            grid=(B, per),
            in_specs=in_specs,
            out_specs=pl.BlockSpec((H, None, T, d), lambda b, i: (0, b, q0 + i, 0)),
            out_shape=jax.ShapeDtypeStruct((H, B, S, d), BF16),
            scratch_shapes=[
                pltpu.VMEM((H, W, d), BF16),
                pltpu.VMEM((2, W, LANES), BF16),
                pltpu.VMEM((IDX_HEADS, T, LANES), F32),
                pltpu.VMEM((nb, T, T), I32),
                pltpu.VMEM((nb, T, T), F32),
                pltpu.VMEM((H, 3, T, T), F32),
            ],
            input_output_aliases={len(args) - 1: 0},
            compiler_params=_params(("parallel", "arbitrary")),
            name=f"dsa_band{band}",
        )(*args)
    return out.reshape(H, B * S, d)


def _sb_kernel(q_ref, k_ref, v_ref, out_ref):
    qi = pl.program_id(2)
    U = SB_HEADS
    T = SB_BLOCK
    d = HEAD_DIM
    R = SB_ROWS
    ns = T // R
    row = lax.broadcasted_iota(I32, (T, T), 0)
    col = lax.broadcasted_iota(I32, (T, T), 1)
    upper = jnp.where(row > col, 1.0, 0.0).astype(BF16)
    srow = lax.broadcasted_iota(I32, (R, T), 0)
    scol = lax.broadcasted_iota(I32, (R, T), 1)
    units = [(u, s) for u in range(U) for s in range(ns)]
    qs = [(q_ref[u, s * R:(s + 1) * R, :].astype(F32) * (d ** -0.5)).astype(BF16) for u, s in units]

    def block(n, j, diag, run, acc):
        u, s = units[n]
        r = pl.multiple_of(j * T, T)
        z = _dot_nt(qs[n], k_ref[u, pl.ds(r, T), :])
        sp = jnp.maximum(z, 0.0) + jnp.log(1.0 + jnp.exp(-jnp.abs(z)))
        strict = scol < srow + s * R
        ln = jnp.where(strict, -sp, 0.0) if diag else -sp
        parts = [ln.astype(BF16)]
        for _ in range(SB_PIECES - 1):
            parts.append((ln - sum(p.astype(F32) for p in parts)).astype(BF16))
        sums = _dot(jnp.concatenate(parts, axis=0), upper)
        suf = sum(sums[p * R:(p + 1) * R] for p in range(SB_PIECES))
        a = jnp.exp(z - sp + suf + run)
        if diag:
            a = jnp.where(strict, a, 0.0)
        acc = acc + _dot(a.astype(BF16), v_ref[u, pl.ds(r, T), :])
        return run + suf[:, 0:1] + ln[:, 0:1], acc

    state = []
    for n in range(len(units)):
        state.extend(block(n, qi, True, jnp.zeros((R, 1), F32), jnp.zeros((R, d), F32)))

    def earlier(i, st):
        out = []
        for n in range(len(units)):
            out.extend(block(n, qi - i, False, st[2 * n], st[2 * n + 1]))
        return tuple(out)

    state = lax.fori_loop(1, qi + 1, earlier, tuple(state))
    for n, (u, s) in enumerate(units):
        out_ref[u, s * R:(s + 1) * R, :] = state[2 * n + 1].astype(out_ref.dtype)


def _stick_breaking(proj_c, B, S):
    T = SB_BLOCK
    nq = S // T
    d = HEAD_DIM
    H = N_HEADS
    U = SB_HEADS
    return pl.pallas_call(
        _sb_kernel,
        grid=(B, H // U, nq),
        in_specs=[
            pl.BlockSpec((U, T, d), lambda b, h, i: (C_SQ // U + h, b * nq + i, 0)),
            pl.BlockSpec((U, S, d), lambda b, h, i: (C_SK // U + h, b, 0)),
            pl.BlockSpec((U, S, d), lambda b, h, i: (C_SV // U + h, b, 0)),
        ],
        out_specs=pl.BlockSpec((U, T, d), lambda b, h, i: (h, b * nq + i, 0)),
        out_shape=jax.ShapeDtypeStruct((H, B * S, d), BF16),
        compiler_params=_params(("parallel", "parallel", "parallel")),
        name="stick_breaking",
    )(proj_c, proj_c, proj_c)


def _merge_kernel(hm_ref, hd_ref, hs_ref, g0_ref, g1_ref, g2_ref, w_ref, o_ref, wb):
    H = N_HEADS

    @pl.when(pl.program_id(1) == 0)
    def _():
        wb[...] = w_ref[...].astype(BF16)

    acc = None
    ns = o_ref.shape[1] // LANES
    for n, (br, gr) in enumerate(((hm_ref, g0_ref), (hd_ref, g1_ref), (hs_ref, g2_ref))):
        a = jnp.concatenate([br[h] for h in range(H)], axis=1)
        up = _dot(a, wb[n])
        gate = jnp.concatenate([gr[s] for s in range(ns)], axis=1).astype(F32)
        term = _sigmoid(gate) * up
        acc = term if acc is None else acc + term
    o_ref[...] = acc.astype(o_ref.dtype)


def _merge(hm, hd, hs, proj_c, w_branch, layer, tm=1024, tn=512):
    H, n, d = hm.shape
    tm = min(tm, n)
    D = w_branch.shape[3]
    ns = tn // LANES
    br = pl.BlockSpec((H, tm, d), lambda j, i: (0, i, 0))

    def gate(b):
        first = (C_GATE + b * (D // LANES)) // ns
        return pl.BlockSpec((ns, tm, LANES), lambda j, i: (first + j, i, 0))

    return pl.pallas_call(
        _merge_kernel,
        grid=(D // tn, n // tm),
        in_specs=[br, br, br, gate(0), gate(1), gate(2),
                  pl.BlockSpec((None, N_BRANCHES, H * d, tn), lambda j, i: (layer, 0, 0, j))],
        out_specs=pl.BlockSpec((tm, tn), lambda j, i: (i, j)),
        out_shape=jax.ShapeDtypeStruct((n, D), BF16),
        scratch_shapes=[pltpu.VMEM((N_BRANCHES, H * d, tn), BF16)],
        compiler_params=_params(("parallel", "arbitrary")),
        name="branch_merge",
    )(hm, hd, hs, proj_c, proj_c, proj_c, w_branch)


def _mm_res_kernel(a_ref, w_ref, x_ref, o_ref, wb):
    @pl.when(pl.program_id(1) == 0)
    def _():
        wb[...] = w_ref[...].astype(BF16)

    o_ref[...] = x_ref[...] + _dot(a_ref[...], wb[...])


def _matmul_residual(a, w, layer, x, tm=1024, tn=512):
    m, k = a.shape
    tm = min(tm, m)
    n = w.shape[2]
    return pl.pallas_call(
        _mm_res_kernel,
        grid=(n // tn, m // tm),
        in_specs=[pl.BlockSpec((tm, k), lambda j, i: (i, 0)),
                  pl.BlockSpec((None, k, tn), lambda j, i: (layer, 0, j)),
                  pl.BlockSpec((tm, tn), lambda j, i: (i, j))],
        out_specs=pl.BlockSpec((tm, tn), lambda j, i: (i, j)),
        out_shape=jax.ShapeDtypeStruct((m, n), F32),
        scratch_shapes=[pltpu.VMEM((k, tn), BF16)],
        compiler_params=_params(("parallel", "arbitrary")),
        name="out_proj",
    )(a, w, x)


def _router_kernel(x_ref, g_ref, wr_ref, br_ref, xn_ref, ids_ref, wts_ref, cnt_ref, carry):
    i = pl.program_id(0)
    tm, D = x_ref.shape

    @pl.when(i == 0)
    def _():
        carry[...] = jnp.zeros(carry.shape, F32)

    x = x_ref[...]
    xn = x * lax.rsqrt(jnp.mean(x * x, axis=-1, keepdims=True) + EPS) * g_ref[...]
    xn_ref[...] = xn
    logits = jnp.dot(xn, wr_ref[...], preferred_element_type=F32, precision=lax.Precision.HIGHEST)
    biased = logits + br_ref[...]
    lane = lax.broadcasted_iota(I32, (tm, LANES), 1)
    lanef = lane.astype(F32)
    big = float(LANES)

    def first_lane(mask):
        return jnp.min(jnp.where(mask, lanef, big), axis=1, keepdims=True)

    gmask = lane < N_GROUPS
    gmax = jnp.max(jnp.where(gmask, biased, NEG), axis=1, keepdims=True)
    g_sel = first_lane(gmask & (biased == gmax))
    gm = jnp.max(jnp.where(gmask, logits, NEG), axis=1, keepdims=True)
    ge = jnp.where(gmask, jnp.exp(logits - gm), 0.0)
    p_group = jnp.sum(jnp.where(lanef == g_sel, ge, 0.0), axis=1, keepdims=True) / jnp.sum(ge, axis=1, keepdims=True)
    lo = N_GROUPS + EXPERTS_PER_GROUP * g_sel
    emask = (lanef >= lo) & (lanef < lo + EXPERTS_PER_GROUP)
    eb = jnp.where(emask, biased, NEG)
    e1 = first_lane(emask & (eb == jnp.max(eb, axis=1, keepdims=True)))
    emask2 = emask & (lanef != e1)
    eb2 = jnp.where(emask2, biased, NEG)
    e2 = first_lane(emask2 & (eb2 == jnp.max(eb2, axis=1, keepdims=True)))
    em = jnp.max(jnp.where(emask, logits, NEG), axis=1, keepdims=True)
    ee = jnp.where(emask, jnp.exp(logits - em), 0.0)
    s1 = jnp.sum(jnp.where(lanef == e1, ee, 0.0), axis=1, keepdims=True)
    s2 = jnp.sum(jnp.where(lanef == e2, ee, 0.0), axis=1, keepdims=True)
    se = jnp.sum(ee, axis=1, keepdims=True)
    w1 = s1 / se
    w2 = s2 / se
    wsum = w1 + w2
    w1 = p_group * w1 / wsum
    w2 = p_group * w2 / wsum
    x1 = e1 - N_GROUPS
    x2 = e2 - N_GROUPS

    onehot = jnp.where((lanef == x1) | (lanef == x2), 1.0, 0.0)
    r = lax.broadcasted_iota(I32, (tm, tm), 0)
    c = lax.broadcasted_iota(I32, (tm, tm), 1)
    lower = jnp.where(c < r, 1.0, 0.0).astype(BF16)
    prefix = _dot(lower, onehot.astype(BF16)) + carry[...]
    r1 = jnp.sum(jnp.where(lanef == x1, prefix, 0.0), axis=1, keepdims=True)
    r2 = jnp.sum(jnp.where(lanef == x2, prefix, 0.0), axis=1, keepdims=True)
    carry[...] = carry[...] + jnp.sum(onehot, axis=0, keepdims=True)
    cnt_ref[...] = carry[...].astype(I32)

    idsf = jnp.where(lane == 0, x1, jnp.where(lane == 1, x2, jnp.where(lane == 2, r1, jnp.where(lane == 3, r2, 0.0))))
    ids_ref[...] = idsf.astype(I32)
    wts_ref[...] = jnp.where(lane == 0, w1, jnp.where(lane == 1, w2, 0.0))


def _router(x2d, g, w_router, b_router, tm=512):
    n, D = x2d.shape
    return pl.pallas_call(
        _router_kernel,
        grid=(n // tm,),
        in_specs=[pl.BlockSpec((tm, D), lambda i: (i, 0)), pl.BlockSpec((1, D), lambda i: (0, 0)),
                  pl.BlockSpec((D, LANES), lambda i: (0, 0)), pl.BlockSpec((1, LANES), lambda i: (0, 0))],
        out_specs=[pl.BlockSpec((tm, D), lambda i: (i, 0)),
                   pl.BlockSpec((tm, LANES), lambda i: (i, 0)),
                   pl.BlockSpec((tm, LANES), lambda i: (i, 0)),
                   pl.BlockSpec((1, LANES), lambda i: (0, 0))],
        out_shape=[jax.ShapeDtypeStruct((n, D), F32),
                   jax.ShapeDtypeStruct((n, LANES), I32),
                   jax.ShapeDtypeStruct((n, LANES), F32),
                   jax.ShapeDtypeStruct((1, LANES), I32)],
        scratch_shapes=[pltpu.VMEM((1, LANES), F32)],
        compiler_params=_params(("arbitrary",)),
        name="moe_router",
    )(x2d, g.reshape(1, D), w_router, b_router)


def _dispatch_kernel(p1_ref, p2_ref, xn_ref, xs_in_ref, xs_ref, sem, *, tb):
    del xs_in_ref
    base = pl.program_id(0) * tb

    def copies(t):
        src = xn_ref.at[pl.ds(t, 1), :]
        return (pltpu.make_async_copy(src, xs_ref.at[pl.ds(p1_ref[base + t], 1), :], sem),
                pltpu.make_async_copy(src, xs_ref.at[pl.ds(p2_ref[base + t], 1), :], sem))

    def issue(t, _):
        for cp in copies(t):
            cp.start()
        return 0

    lax.fori_loop(0, tb, issue, 0, unroll=DMA_UNROLL)

    def drain(t, _):
        for cp in copies(t):
            cp.wait()
        return 0

    lax.fori_loop(0, tb, drain, 0, unroll=DMA_UNROLL)


def _dispatch(pos1, pos2, xn, xs0, tb=512):
    n, D = xn.shape
    n_rows = xs0.shape[0]
    tb = min(tb, n)
    grid_spec = pltpu.PrefetchScalarGridSpec(
        num_scalar_prefetch=2,
        grid=(n // tb,),
        in_specs=[pl.BlockSpec((tb, D), lambda i, p1, p2: (i, 0)),
                  pl.BlockSpec(memory_space=pl.ANY)],
        out_specs=pl.BlockSpec(memory_space=pl.ANY),
        scratch_shapes=[pltpu.SemaphoreType.DMA(())],
    )
    return pl.pallas_call(
        functools.partial(_dispatch_kernel, tb=tb),
        grid_spec=grid_spec,
        out_shape=jax.ShapeDtypeStruct((n_rows, D), F32),
        input_output_aliases={3: 0},
        compiler_params=pltpu.CompilerParams(dimension_semantics=("arbitrary",), has_side_effects=True),
        name="moe_dispatch",
    )(pos1, pos2, xn, xs0)


def _expert_kernel(te_ref, nv_ref, xs_ref, wg_ref, wu_ref, wd_ref, ys_ref, wgb, wub, wdb):
    i = pl.program_id(0)
    prev = te_ref[jnp.maximum(i - 1, 0)]

    @pl.when(i < nv_ref[0])
    def _():
        @pl.when((i == 0) | (te_ref[i] != prev))
        def _():
            wgb[...] = wg_ref[...].astype(BF16)
            wub[...] = wu_ref[...].astype(BF16)
            wdb[...] = wd_ref[...].astype(BF16)

        x = xs_ref[...].astype(BF16)
        g = _dot(x, wgb[...])
        u = _dot(x, wub[...])
        hcur = (g * _sigmoid(g) * u).astype(BF16)
        ys_ref[...] = _dot(hcur, wdb[...])

    @pl.when(i >= nv_ref[0])
    def _():
        ys_ref[...] = jnp.zeros(ys_ref.shape, F32)


def _experts(tile_expert, n_valid, xs, w_gate, w_up, w_down, layer):
    n_rows = xs.shape[0]
    _, E, D, Fe = w_gate.shape
    tm = MOE_TILE
    n_tiles = n_rows // tm

    def row_map(i, te, nv):
        return (jnp.minimum(i, jnp.maximum(nv[0] - 1, 0)), 0)

    def w_map(i, te, nv):
        return (layer, te[i], 0, 0)

    grid_spec = pltpu.PrefetchScalarGridSpec(
        num_scalar_prefetch=2,
        grid=(n_tiles,),
        in_specs=[pl.BlockSpec((tm, D), row_map),
                  pl.BlockSpec((None, None, D, Fe), w_map), pl.BlockSpec((None, None, D, Fe), w_map),
                  pl.BlockSpec((None, None, Fe, D), w_map)],
        out_specs=pl.BlockSpec((tm, D), lambda i, te, nv: (i, 0)),
        scratch_shapes=[pltpu.VMEM((D, Fe), BF16), pltpu.VMEM((D, Fe), BF16), pltpu.VMEM((Fe, D), BF16)],
    )
    return pl.pallas_call(
        _expert_kernel,
        grid_spec=grid_spec,
        out_shape=jax.ShapeDtypeStruct((n_rows, D), F32),
        compiler_params=_params(("arbitrary",)),
        name="moe_experts",
    )(tile_expert, n_valid, xs, w_gate, w_up, w_down)


def _combine_kernel(p1_ref, p2_ref, ys_ref, x_ref, w_ref, o_ref, buf, sem):
    i = pl.program_id(0)
    tc = x_ref.shape[0]
    slot = i % 2

    def copies(tile, slot, t):
        tok = tile * tc + t
        return (pltpu.make_async_copy(ys_ref.at[pl.ds(p1_ref[tok], 1), :], buf.at[slot, 0, pl.ds(t, 1), :],
                                      sem.at[slot]),
                pltpu.make_async_copy(ys_ref.at[pl.ds(p2_ref[tok], 1), :], buf.at[slot, 1, pl.ds(t, 1), :],
                                      sem.at[slot]))

    def issue(tile, slot):
        def body(t, _):
            for cp in copies(tile, slot, t):
                cp.start()
            return 0

        lax.fori_loop(0, tc, body, 0, unroll=DMA_UNROLL)

    @pl.when(i == 0)
    def _():
        issue(0, 0)

    @pl.when(i + 1 < pl.num_programs(0))
    def _():
        issue(i + 1, 1 - slot)

    def drain(t, _):
        for cp in copies(i, slot, t):
            cp.wait()
        return 0

    lax.fori_loop(0, tc, drain, 0, unroll=DMA_UNROLL)
    w = w_ref[...]
    o_ref[...] = x_ref[...] + w[:, 0:1] * buf[slot, 0] + w[:, 1:2] * buf[slot, 1]


def _combine(pos1, pos2, ys, x2d, wts, tc=256):
    n, D = x2d.shape
    grid_spec = pltpu.PrefetchScalarGridSpec(
        num_scalar_prefetch=2,
        grid=(n // tc,),
        in_specs=[pl.BlockSpec(memory_space=pl.ANY),
                  pl.BlockSpec((tc, D), lambda i, p1, p2: (i, 0)),
                  pl.BlockSpec((tc, LANES), lambda i, p1, p2: (i, 0))],
        out_specs=pl.BlockSpec((tc, D), lambda i, p1, p2: (i, 0)),
        scratch_shapes=[pltpu.VMEM((2, 2, tc, D), F32), pltpu.SemaphoreType.DMA((2,))],
    )
    return pl.pallas_call(
        _combine_kernel,
        grid_spec=grid_spec,
        out_shape=jax.ShapeDtypeStruct((n, D), F32),
        compiler_params=_params(("arbitrary",)),
        name="moe_combine",
    )(pos1, pos2, ys, x2d, wts)


def _pos_kernel(ids_ref, cnt_ref, pos_ref):
    tm = ids_ref.shape[0]
    tiles = jnp.floor((cnt_ref[...].astype(F32) + (MOE_TILE - 1)) / MOE_TILE)
    r = lax.broadcasted_iota(I32, (LANES, LANES), 0)
    c = lax.broadcasted_iota(I32, (LANES, LANES), 1)
    before = jnp.where(r < c, 1.0, 0.0).astype(BF16)
    first_tile = _dot(jnp.broadcast_to(tiles, (8, LANES)).astype(BF16), before)
    offs = first_tile[0:1, :] * MOE_TILE
    ids = ids_ref[...].astype(F32)
    lane = lax.broadcasted_iota(I32, (tm, LANES), 1)
    lanef = lane.astype(F32)
    p1 = jnp.sum(jnp.where(lanef == ids[:, 0:1], offs, 0.0), axis=1, keepdims=True) + ids[:, 2:3]
    p2 = jnp.sum(jnp.where(lanef == ids[:, 1:2], offs, 0.0), axis=1, keepdims=True) + ids[:, 3:4]
    packed = jnp.where(lane == 0, p1, jnp.where(lane == 1, p2, 0.0))
    pos_ref[...] = packed.T[0:8, :].astype(I32)


def _positions(ids, counts, tm=512):
    n = ids.shape[0]
    assert MOE_TILE & (MOE_TILE - 1) == 0
    return pl.pallas_call(
        _pos_kernel,
        grid=(n // tm,),
        in_specs=[pl.BlockSpec((tm, LANES), lambda i: (i, 0)), pl.BlockSpec((1, LANES), lambda i: (0, 0))],
        out_specs=pl.BlockSpec((8, tm), lambda i: (0, i)),
        out_shape=jax.ShapeDtypeStruct((8, n), I32),
        compiler_params=_params(("parallel",)),
        name="moe_positions",
    )(ids, counts)


def _hier_moe(x2d, norm_g, w_rg, b_rg, w_re, b_re, w_gate, w_up, w_down, layer, xs_buf):
    n, D = x2d.shape
    tm = MOE_TILE
    pad = LANES - N_GROUPS - N_EXPERTS
    w_router = jnp.concatenate([w_rg, w_re, jnp.zeros((D, pad), F32)], axis=1)
    b_router = jnp.concatenate([b_rg, b_re.reshape(-1), jnp.zeros((pad,), F32)]).reshape(1, LANES)
    xn3, ids, wts, counts = _router(x2d, norm_g, w_router, b_router)
    cnt = counts[0, :N_EXPERTS]
    padded = ((cnt + tm - 1) // tm) * tm
    ends = jnp.cumsum(padded)
    n_tiles = xs_buf.shape[0] // tm
    tile_start = jnp.arange(n_tiles, dtype=I32) * tm
    tile_expert = jnp.minimum(jnp.sum(tile_start[:, None] >= ends[None, :], axis=1), N_EXPERTS - 1).astype(I32)
    n_valid = (ends[-1] // tm).astype(I32).reshape(1)
    last_e = tile_expert[jnp.maximum(n_valid[0] - 1, 0)]
    tile_expert = jnp.where(jnp.arange(n_tiles) < n_valid[0], tile_expert, last_e)
    pos = _positions(ids, counts)
    pos1, pos2 = pos[0], pos[1]
    xs = _dispatch(pos1, pos2, xn3, xs_buf)
    ys = _experts(tile_expert, n_valid, xs, w_gate, w_up, w_down, layer)
    return _combine(pos1, pos2, ys, x2d, wts), xs


def _in_proj_regions(D):
    bw = BRANCH_WIDTH
    sizes = (bw, bw, bw, bw, N_HEADS, N_HEADS, bw, bw, bw, IDX_HEADS * IDX_DIM, IDX_DIM, IDX_HEADS,
             bw, bw, bw, N_BRANCHES * D)
    offs = np.concatenate([[0], np.cumsum(sizes)]).tolist()
    region_a = (offs[0], offs[4] - offs[0])
    region_b = (offs[6], offs[10] - offs[6])
    region_c = (offs[12], offs[16] - offs[12])
    small = dict(mi=offs[4], mf=offs[5], ik=offs[10], iw=offs[11])
    return region_a, region_b, region_c, small


def _token_mixer(x2d, B, S, layer, norm_g, w_in, conv_w, b_i, b_f, mlstm_norm_g, q_norm_g, k_norm_g,
                 w_branch, w_out, rel_bias):
    n, D = x2d.shape
    L = MLSTM_CHUNK
    H = N_HEADS
    tm = min(n, PROJ_TM)
    xn = _rmsnorm(x2d, norm_g)
    ra, rb, rc, small = _in_proj_regions(D)
    w_t = jnp.swapaxes(w_in, 1, 2)
    proj_a = _in_proj(xn, w_t, layer, ra[0], ra[1], tm)
    proj_b = _in_proj(xn, w_t, layer, rb[0], rb[1], tm)
    proj_c = _in_proj(xn, w_t, layer, rc[0], rc[1], tm)
    tail = _tail_proj(xn, w_t, layer, small, tm)
    g = tail[1][:, IDX_HEADS:IDX_HEADS + 2 * H].reshape(B, S // L, L, 2, H)
    gates_t = jnp.transpose(g, (0, 4, 1, 3, 2))
    hm = _mlstm(proj_a, gates_t, jnp.stack([b_i, b_f]), conv_w, mlstm_norm_g, B, S)
    hd = _dsa(proj_b, tail, q_norm_g, k_norm_g, rel_bias, B, S)
    hs = _stick_breaking(proj_c, B, S)
    merged = _merge(hm, hd, hs, proj_c, w_branch, layer)
    return _matmul_residual(merged, w_out, layer, x2d)


def kernel(x, norm1_g, w_in, conv_w, b_i, b_f, mlstm_norm_g, q_norm_g, k_norm_g, w_branch, w_out, norm2_g,
           w_router_g, b_router_g, w_router_e, b_router_e, w_gate, w_up, w_down, rel_bias):
    B, S, D = x.shape
    x2d = x.reshape(B * S, D)
    xs_buf = jnp.zeros((2 * B * S + N_EXPERTS * MOE_TILE, D), F32)
    for l in range(w_in.shape[0]):
        x2d = _token_mixer(x2d, B, S, l, norm1_g[l], w_in, conv_w[l], b_i[l], b_f[l], mlstm_norm_g[l],
                           q_norm_g[l], k_norm_g[l], w_branch, w_out, rel_bias)
        x2d, xs_buf = _hier_moe(x2d, norm2_g[l], w_router_g[l], b_router_g[l], w_router_e[l], b_router_e[l],
                                w_gate, w_up, w_down, l, xs_buf)
    return x2d.reshape(B, S, D)
```

```python
import functools
import math

import numpy as np
import jax
import jax.numpy as jnp
from jax import lax
from jax.experimental import pallas as pl
from jax.experimental.pallas import tpu as pltpu

F32 = jnp.float32
BF16 = jnp.bfloat16
I32 = jnp.int32

LANES = 128
HEAD_DIM = 128
N_HEADS = 8
BRANCH_WIDTH = N_HEADS * HEAD_DIM
N_BRANCHES = 3
CONV_WIDTH = 4
IDX_HEADS = 16
IDX_DIM = 64
TOPK_MAX = 256
N_BUCKETS = 32
MAX_DISTANCE = 128
N_GROUPS = 4
EXPERTS_PER_GROUP = 8
N_EXPERTS = N_GROUPS * EXPERTS_PER_GROUP
EPS = 1e-6
NEG = -1e30
INT_MIN = -(2 ** 31)

MLSTM_CHUNK = 256
DSA_BLOCK = 128
DSA_BANDS = 4
SB_BLOCK = 512
SB_ROWS = 512
SB_PIECES = 1
SB_HEADS = 4
HEADS_PER_STEP = 4
MOE_TILE = 256
DMA_UNROLL = 8
PROJ_TN = 512
PROJ_TM = 2048
VMEM_LIMIT = 56 * 1024 * 1024

A_MQ, A_MK, A_MV, A_MO = 0, 8, 16, 24
B_DQ, B_DK, B_DV, B_IQ = 0, 8, 16, 24
C_SQ, C_SK, C_SV, C_GATE = 0, 8, 16, 24


def _params(sem):
    return pltpu.CompilerParams(dimension_semantics=sem, vmem_limit_bytes=VMEM_LIMIT)


def _dot(a, b):
    return jnp.dot(a, b, preferred_element_type=F32)


def _dot_nt(a, b):
    return lax.dot_general(a, b, (((1,), (1,)), ((), ())), preferred_element_type=F32)


def _sigmoid(z):
    return 1.0 / (1.0 + jnp.exp(-z))


def _rmsnorm_kernel(x_ref, g_ref, o_ref):
    x = x_ref[...]
    ms = jnp.mean(x * x, axis=-1, keepdims=True)
    o_ref[...] = (x * lax.rsqrt(ms + EPS) * g_ref[...]).astype(o_ref.dtype)


def _rmsnorm(x2d, g, tm=512):
    n, d = x2d.shape
    return pl.pallas_call(
        _rmsnorm_kernel,
        grid=(n // tm,),
        in_specs=[pl.BlockSpec((tm, d), lambda i: (i, 0)), pl.BlockSpec((1, d), lambda i: (0, 0))],
        out_specs=pl.BlockSpec((tm, d), lambda i: (i, 0)),
        out_shape=jax.ShapeDtypeStruct((n, d), BF16),
        compiler_params=_params(("parallel",)),
        name="rmsnorm",
    )(x2d, g.reshape(1, d))


def _in_proj_kernel(x_ref, *rest, shift, nblk):
    w_refs, o_ref, wb = rest[:nblk], rest[nblk], rest[nblk + 1]
    tn = wb.shape[1]

    @pl.when(pl.program_id(1) == 0)
    def _():
        w = jnp.concatenate([r[...] for r in w_refs], axis=0)
        wb[...] = w[shift:shift + tn, :].T.astype(BF16)

    acc = _dot(x_ref[...], wb[...])
    for j in range(o_ref.shape[0]):
        o_ref[j] = acc[:, j * LANES:(j + 1) * LANES].astype(o_ref.dtype)


def _in_proj(xn, w_t, layer, col0, ncols, tm):
    m, k = xn.shape
    tn = PROJ_TN
    base, shift = divmod(col0, LANES)
    assert shift % 8 == 0
    nblk = tn // LANES + (1 if shift else 0)
    per = tn // LANES

    def wspec(r):
        return pl.BlockSpec((None, LANES, k), lambda j, i: (layer, base + per * j + r, 0))

    return pl.pallas_call(
        functools.partial(_in_proj_kernel, shift=shift, nblk=nblk),
        grid=(ncols // tn, m // tm),
        in_specs=[pl.BlockSpec((tm, k), lambda j, i: (i, 0))] + [wspec(r) for r in range(nblk)],
        out_specs=pl.BlockSpec((per, tm, LANES), lambda j, i: (j, i, 0)),
        out_shape=jax.ShapeDtypeStruct((ncols // LANES, m, LANES), BF16),
        scratch_shapes=[pltpu.VMEM((k, tn), BF16)],
        compiler_params=_params(("parallel", "arbitrary")),
        name="in_proj",
    )(xn, *([w_t] * nblk))


def _tail_kernel(x_ref, wg_ref, wi_ref, o_ref, wt, *, g_lane, ik_lane, iw_lane):
    @pl.when(pl.program_id(0) == 0)
    def _():
        wg = wg_ref[...]
        wi = wi_ref[...]
        ik = wi[ik_lane:ik_lane + IDX_DIM, :]
        iw = wi[iw_lane:iw_lane + IDX_HEADS, :]
        gates = wg[g_lane:g_lane + 2 * N_HEADS, :]
        pad = jnp.zeros((LANES - IDX_HEADS - 2 * N_HEADS, wg.shape[1]), F32)
        wt[...] = jnp.concatenate([ik, ik, iw, gates, pad], axis=0).T.astype(BF16)

    acc = _dot(x_ref[...], wt[...])
    o_ref[0] = acc[:, :LANES]
    o_ref[1] = acc[:, LANES:]


def _tail_proj(xn, w_t, layer, small, tm):
    m, k = xn.shape
    g_blk, g_lane = divmod(small["mi"], LANES)
    i_blk, ik_lane = divmod(small["ik"], LANES)
    iw_lane = small["iw"] - i_blk * LANES
    assert small["mf"] == small["mi"] + N_HEADS and g_lane + 2 * N_HEADS <= LANES
    assert ik_lane + IDX_DIM <= LANES and 0 <= iw_lane and iw_lane + IDX_HEADS <= LANES
    assert g_lane % 8 == 0 and ik_lane % 8 == 0 and iw_lane % 8 == 0
    return pl.pallas_call(
        functools.partial(_tail_kernel, g_lane=g_lane, ik_lane=ik_lane, iw_lane=iw_lane),
        grid=(m // tm,),
        in_specs=[pl.BlockSpec((tm, k), lambda i: (i, 0)),
                  pl.BlockSpec((None, LANES, k), lambda i: (layer, g_blk, 0)),
                  pl.BlockSpec((None, LANES, k), lambda i: (layer, i_blk, 0))],
        out_specs=pl.BlockSpec((2, tm, LANES), lambda i: (0, i, 0)),
        out_shape=jax.ShapeDtypeStruct((2, m, LANES), F32),
        scratch_shapes=[pltpu.VMEM((k, 2 * LANES), BF16)],
        compiler_params=_params(("arbitrary",)),
        name="tail_proj",
    )(xn, w_t, w_t)


def _mlstm_kernel(bias_ref, q_ref, k_ref, v_ref, o_ref, g_ref, cwq_ref, cwk_ref, ng_ref, out_ref,
                  qf, kf, qc, kc, st):
    hp = pl.program_id(1)
    U = HEADS_PER_STEP
    S = q_ref.shape[1]
    L = MLSTM_CHUNK
    nc = S // L
    d = HEAD_DIM
    PAD = 8

    R = min(S, 256)
    for u in range(U):
        qf[u, 0:PAD, :] = jnp.zeros((PAD, d), F32)
        kf[u, 0:PAD, :] = jnp.zeros((PAD, d), F32)
        qf[u, PAD:PAD + S, :] = q_ref[u].astype(F32)
        kf[u, PAD:PAD + S, :] = k_ref[u].astype(F32)
        ls = slice(u * d, (u + 1) * d)
        for r0 in range(0, S, R):
            aq = jnp.zeros((R, d), F32)
            ak = jnp.zeros((R, d), F32)
            for t in range(CONV_WIDTH):
                off = PAD - (CONV_WIDTH - 1) + t + r0
                aq = aq + cwq_ref[t:t + 1, ls] * qf[u, off:off + R, :]
                ak = ak + cwk_ref[t:t + 1, ls] * kf[u, off:off + R, :]
            qc[u, r0:r0 + R, :] = (aq * _sigmoid(aq) * (d ** -0.5)).astype(BF16)
            kc[u, r0:r0 + R, :] = ak * _sigmoid(ak)

    st[...] = jnp.zeros(st.shape, F32)
    row = lax.broadcasted_iota(I32, (L, L), 0)
    col = lax.broadcasted_iota(I32, (L, L), 1)
    causal = col <= row
    eye = col == row
    lane = lax.broadcasted_iota(I32, (L, d), 1)
    ones_col = jnp.where(lane == 0, 1.0, 0.0).astype(BF16)
    ng = ng_ref[...]

    def chunk_one(u, c, r, m):
        q = qc[u, pl.ds(r, L), :]
        kT = kc[u, pl.ds(r, L), :].T
        v = v_ref[u, pl.ds(r, L), :]
        vaug = jnp.concatenate([v, ones_col], axis=1)
        gates = g_ref[u, c]
        i_row = gates[0:1, :] + bias_ref[0, hp * U + u]
        f_row = gates[1:2, :] + bias_ref[1, hp * U + u]
        lf_row = jnp.minimum(f_row, 0.0) - jnp.log1p(jnp.exp(-jnp.abs(f_row)))
        b_col = jnp.sum(jnp.where(causal, lf_row, 0.0), axis=1, keepdims=True)
        b_row = jnp.sum(jnp.where(eye, b_col, 0.0), axis=0, keepdims=True)
        dlog = jnp.where(causal, b_col - b_row + i_row, NEG)
        inter = b_col + m
        m_t = jnp.maximum(inter, jnp.max(dlog, axis=1, keepdims=True))
        w_intra = jnp.exp(dlog - m_t)
        w_inter = jnp.exp(inter - m_t)
        s = _dot(q, kT.astype(BF16)) * w_intra
        res = w_inter * _dot(q, st[u].astype(BF16)) + _dot(s.astype(BF16), vaug)
        num = res[:, :d]
        den = res[:, d:d + 1]
        hh = num / jnp.maximum(jnp.abs(den), jnp.exp(-m_t))
        hn = hh * lax.rsqrt(jnp.mean(hh * hh, axis=1, keepdims=True) + EPS) * ng[:, u * d:(u + 1) * d]
        og = o_ref[u, pl.ds(r, L), :].astype(F32)
        out_ref[u, pl.ds(r, L), :] = (hn * _sigmoid(og)).astype(out_ref.dtype)
        ws_row = w_intra[L - 1:L, :]
        decay = w_inter[L - 1:L, :]
        st[u] = decay * st[u] + _dot((kT * ws_row).astype(BF16), vaug)
        return m_t[L - 1:L, :]

    def chunk(c, ms):
        r = pl.multiple_of(c * L, L)
        return tuple(chunk_one(u, c, r, ms[u]) for u in range(U))

    lax.fori_loop(0, nc, chunk, tuple(jnp.zeros((1, 1), F32) for _ in range(U)))


def _mlstm(proj_a, gates_t, bias_if, conv_w, norm_g, B, S):
    L = MLSTM_CHUNK
    nc = S // L
    d = HEAD_DIM
    H = N_HEADS
    U = HEADS_PER_STEP

    def slab(off):
        return pl.BlockSpec((U, S, d), lambda b, h: (off // U + h, b, 0))

    return pl.pallas_call(
        _mlstm_kernel,
        grid=(B, H // U),
        in_specs=[
            pl.BlockSpec(memory_space=pltpu.SMEM),
            slab(A_MQ), slab(A_MK), slab(A_MV), slab(A_MO),
            pl.BlockSpec((None, U, nc, 2, L), lambda b, h: (b, h, 0, 0, 0)),
            pl.BlockSpec((CONV_WIDTH, U * d), lambda b, h: (0, h)),
            pl.BlockSpec((CONV_WIDTH, U * d), lambda b, h: (0, H // U + h)),
            pl.BlockSpec((1, U * d), lambda b, h: (0, h)),
        ],
        out_specs=pl.BlockSpec((U, S, d), lambda b, h: (h, b, 0)),
        out_shape=jax.ShapeDtypeStruct((H, B * S, d), BF16),
        scratch_shapes=[
            pltpu.VMEM((U, S + 8, d), F32), pltpu.VMEM((U, S + 8, d), F32),
            pltpu.VMEM((U, S, d), BF16), pltpu.VMEM((U, S, d), F32),
            pltpu.VMEM((U, d, 2 * d), F32),
        ],
        compiler_params=_params(("parallel", "parallel")),
        name="mlstm",
    )(bias_if, proj_a, proj_a, proj_a, proj_a, gates_t, conv_w, conv_w, norm_g.reshape(1, H * d))


def _t5_thresholds():
    max_exact = N_BUCKETS // 2
    n = np.arange(0, 2 * MAX_DISTANCE)
    nf = np.maximum(n, 1).astype(np.float64)
    val = np.log(nf / max_exact) / math.log(MAX_DISTANCE / max_exact) * (N_BUCKETS - max_exact)
    frac = np.abs(val - np.round(val))
    frac_ok = (frac > 1e-4) | (n <= max_exact) | (n >= MAX_DISTANCE)
    assert frac_ok.all()
    large = np.minimum(max_exact + np.trunc(val).astype(np.int64), N_BUCKETS - 1)
    bucket = np.where(n < max_exact, n, large)
    assert (np.diff(bucket) >= 0).all() and bucket[MAX_DISTANCE] == N_BUCKETS - 1
    return [int(np.argmax(bucket >= j)) for j in range(1, N_BUCKETS)]


_T5_THR = _t5_thresholds()


def _dsa_kernel(rb_ref, q_ref, k_ref, v_ref, iq_ref, ik_ref, wt_ref, gq_ref, gk_ref, prev_ref, out_ref,
                kn, ikk, wb, key_ref, mask_ref, bias_ref, *, topk, q0, nb, nqb):
    del prev_ref
    qis = [q0 + pl.program_id(1) * nqb + qb for qb in range(nqb)]
    T = DSA_BLOCK
    d = HEAD_DIM
    H = N_HEADS
    HALF = T // 2
    row = lax.broadcasted_iota(I32, (T, T), 0)
    col = lax.broadcasted_iota(I32, (T, T), 1)

    @pl.when(pl.program_id(1) == 0)
    def _per_batch():
        gk = gk_ref[...]

        def norm_k(h, _):
            kh = k_ref[h].astype(F32)
            kn[h] = (kh * lax.rsqrt(jnp.mean(kh * kh, axis=1, keepdims=True) + EPS) * gk).astype(BF16)
            for o in range(2):
                n = o * T + row - col
                val = jnp.full((T, T), rb_ref[0, h], F32)
                for j, thr in enumerate(_T5_THR):
                    val = jnp.where(n >= thr, rb_ref[j + 1, h], val)
                bias_ref[h, o] = val
            bias_ref[h, 2] = jnp.full((T, T), rb_ref[N_BUCKETS - 1, h], F32)
            return 0

        lax.fori_loop(0, H, norm_k, 0)
        ik = ik_ref[...]
        klane = lax.broadcasted_iota(I32, ik.shape, 1)
        ikk[0] = jnp.where(klane < IDX_DIM, ik, 0.0).astype(BF16)
        ikk[1] = jnp.where(klane >= IDX_DIM, ik, 0.0).astype(BF16)

    wsc = wt_ref[...] * (IDX_HEADS ** -0.5 * IDX_DIM ** -0.5)
    for qb in range(nqb):
        for h16 in range(IDX_HEADS):
            wb[qb * IDX_HEADS + h16] = jnp.broadcast_to(wsc[qb * T:(qb + 1) * T, h16:h16 + 1], (T, LANES))
    q_pairs = iq_ref[...].reshape((IDX_HEADS // 2) * nqb * T, LANES)

    def score_block(j, _):
        r = pl.multiple_of(j * T, T)
        kk = jnp.concatenate([ikk[0, pl.ds(r, T), :], ikk[1, pl.ds(r, T), :]], axis=0)
        dots = _dot_nt(q_pairs, kk)
        for qb in range(nqb):
            sc = jnp.zeros((T, T), F32)
            for h16 in range(IDX_HEADS):
                hp, odd = divmod(h16, 2)
                r0 = (hp * nqb + qb) * T
                sc = sc + wb[qb * IDX_HEADS + h16] * jnp.maximum(dots[r0:r0 + T, odd * T:(odd + 1) * T], 0.0)
            bits = lax.bitcast_convert_type(sc, I32)
            key = jnp.where(bits < 0, bits ^ jnp.int32(0x7FFFFFFF), bits)
            key = jnp.where(sc == 0.0, 0, key)
            key = jnp.where(j * T + col <= qis[qb] * T + row, key, INT_MIN)
            key_ref[qb, j] = key
        return 0

    lax.fori_loop(0, nb, score_block, 0, unroll=2)

    kf = float(topk)
    ones = jnp.ones((LANES, LANES), BF16)

    def count(chain, pred):
        qb, lo = chain
        acc = jnp.zeros((HALF, LANES), F32)
        for jb in range(nb):
            acc = acc + jnp.where(pred(key_ref[qb, jb, lo:lo + HALF, :]), 1.0, 0.0)
        return _dot(acc.astype(BF16), ones)

    chains = [(qb, lo) for qb in range(nqb) for lo in (0, HALF)]
    zero_i = jnp.zeros((HALF, LANES), I32)
    thr0 = tuple(jnp.where(count(ch, lambda k: k >= zero_i) >= kf, jnp.int32(0), jnp.int32(INT_MIN))
                 for ch in chains)

    def bisect_bit(thrs, bit):
        out = []
        for ch, t in zip(chains, thrs):
            cand = t | bit
            out.append(jnp.where(count(ch, lambda k, c=cand: k >= c) >= kf, cand, t))
        return tuple(out)

    def bisect_pair(it, thrs):
        hi_bit = jnp.left_shift(jnp.int32(1), 29 - 2 * it)
        lo_bit = jnp.left_shift(jnp.int32(1), 28 - 2 * it)
        out = []
        for ch, t in zip(chains, thrs):
            c01, c10, c11 = t | lo_bit, t | hi_bit, t | hi_bit | lo_bit
            n01, n10, n11 = (count(ch, lambda k, c=c: k >= c) >= kf for c in (c01, c10, c11))
            out.append(jnp.where(n11, c11, jnp.where(n10, c10, jnp.where(n01, c01, t))))
        return tuple(out)

    thrs = bisect_bit(thr0, jnp.int32(1 << 30))
    thrs = lax.fori_loop(0, 15, bisect_pair, thrs)
    needs = [kf - count(ch, lambda k, t=t: k > t) for ch, t in zip(chains, thrs)]

    tri = jnp.where(row <= col, 1.0, 0.0).astype(BF16)
    for qb in range(nqb):
        need = jnp.concatenate(needs[2 * qb:2 * qb + 2], axis=0)
        thrb = jnp.concatenate(thrs[2 * qb:2 * qb + 2], axis=0)
        seen = jnp.zeros((T, LANES), F32)
        for jb in range(nb):
            key = key_ref[qb, jb]
            tie = key == thrb
            tie16 = jnp.where(tie, 1.0, 0.0).astype(BF16)
            pre = _dot(tie16, tri) + seen
            m = jnp.where(key > thrb, 0.0, jnp.where(tie, jnp.where(pre <= need, 0.0, NEG), NEG))
            mask_ref[qb, jb] = jnp.where(key == INT_MIN, NEG, m)
            seen = seen + _dot(tie16, ones)

    gq = gq_ref[...]

    def head(h):
        qh = q_ref[h].astype(F32)
        qn = (qh * lax.rsqrt(jnp.mean(qh * qh, axis=1, keepdims=True) + EPS) * gq * (d ** -0.5)).astype(BF16)
        lgs = _dot_nt(qn, kn[h])
        p_rows, sums = [], []
        for qb in range(nqb):
            mx = None
            lg_blocks = []
            for jb in range(nb):
                lg = (lgs[qb * T:(qb + 1) * T, jb * T:(jb + 1) * T]
                      + bias_ref[h, jnp.clip(qis[qb] - jb, 0, 2)] + mask_ref[qb, jb])
                lg_blocks.append(lg)
                mx = lg if mx is None else jnp.maximum(mx, lg)
            rowmax = jnp.max(mx, axis=1, keepdims=True)
            l = jnp.zeros((T, T), F32)
            ps = []
            for jb in range(nb):
                p = jnp.exp(lg_blocks[jb] - rowmax)
                l = l + p
                ps.append(p.astype(BF16))
            p_rows.append(jnp.concatenate(ps, axis=1))
            sums.append(jnp.sum(l, axis=1, keepdims=True))
        acc = _dot(jnp.concatenate(p_rows, axis=0), v_ref[h])
        out_ref[h] = (acc / jnp.concatenate(sums, axis=0)).astype(out_ref.dtype)

    def head_group(g, _):
        for u in range(HEADS_PER_STEP):
            head(g * HEADS_PER_STEP + u)
        return 0

    lax.fori_loop(0, H // HEADS_PER_STEP, head_group, 0)


def _dsa(proj_b, tail, q_norm_g, k_norm_g, rel_bias, B, S):
    T = DSA_BLOCK
    nq = S // T
    d = HEAD_DIM
    H = N_HEADS
    topk = min(TOPK_MAX, S // 4)
    per = nq // DSA_BANDS
    nqb = 2 if per % 2 == 0 else 1
    TQ = nqb * T
    pb = proj_b.reshape(proj_b.shape[0], B, S, d)
    tl = tail.reshape(tail.shape[0], B, S, LANES)
    out = jnp.zeros((H, B, S, d), BF16)
    for band in range(DSA_BANDS):
        q0 = band * per
        nb = q0 + per
        W = nb * T
        in_specs = [
            pl.BlockSpec(memory_space=pltpu.SMEM),
            pl.BlockSpec((H, None, TQ, d), lambda b, i: (B_DQ // H, b, q0 // nqb + i, 0)),
            pl.BlockSpec((H, None, W, d), lambda b, i: (B_DK // H, b, 0, 0)),
            pl.BlockSpec((H, None, W, d), lambda b, i: (B_DV // H, b, 0, 0)),
            pl.BlockSpec((H, None, TQ, d), lambda b, i: (B_IQ // H, b, q0 // nqb + i, 0)),
            pl.BlockSpec((None, None, W, LANES), lambda b, i: (0, b, 0, 0)),
            pl.BlockSpec((None, None, TQ, LANES), lambda b, i: (1, b, q0 // nqb + i, 0)),
            pl.BlockSpec((1, d), lambda b, i: (0, 0)),
            pl.BlockSpec((1, d), lambda b, i: (0, 0)),
            pl.BlockSpec(memory_space=pl.ANY),
        ]
        args = [rel_bias, pb, pb, pb, pb, tl, tl, q_norm_g.reshape(1, d), k_norm_g.reshape(1, d), out]
        out = pl.pallas_call(
            functools.partial(_dsa_kernel, topk=topk, q0=q0, nb=nb, nqb=nqb),
            grid=(B, per // nqb),
            in_specs=in_specs,
            out_specs=pl.BlockSpec((H, None, TQ, d), lambda b, i: (0, b, q0 // nqb + i, 0)),
            out_shape=jax.ShapeDtypeStruct((H, B, S, d), BF16),
            scratch_shapes=[
                pltpu.VMEM((H, W, d), BF16),
                pltpu.VMEM((2, W, LANES), BF16),
                pltpu.VMEM((nqb * IDX_HEADS, T, LANES), F32),
                pltpu.VMEM((nqb, nb, T, T), I32),
                pltpu.VMEM((nqb, nb, T, T), F32),
                pltpu.VMEM((H, 3, T, T), F32),
            ],
            input_output_aliases={len(args) - 1: 0},
            compiler_params=_params(("parallel", "arbitrary")),
            name=f"dsa_band{band}",
        )(*args)
    return out.reshape(H, B * S, d)


def _sb_kernel(q_ref, k_ref, v_ref, out_ref):
    qi = pl.program_id(2)
    U = SB_HEADS
    T = SB_BLOCK
    d = HEAD_DIM
    R = SB_ROWS
    ns = T // R
    row = lax.broadcasted_iota(I32, (T, T), 0)
    col = lax.broadcasted_iota(I32, (T, T), 1)
    upper = jnp.where(row > col, 1.0, 0.0).astype(BF16)
    srow = lax.broadcasted_iota(I32, (R, T), 0)
    scol = lax.broadcasted_iota(I32, (R, T), 1)
    units = [(u, s) for u in range(U) for s in range(ns)]
    qs = [(q_ref[u, s * R:(s + 1) * R, :].astype(F32) * (d ** -0.5)).astype(BF16) for u, s in units]

    def block(n, j, diag, run, acc):
        u, s = units[n]
        r = pl.multiple_of(j * T, T)
        z = _dot_nt(qs[n], k_ref[u, pl.ds(r, T), :])
        sp = jnp.maximum(z, 0.0) + jnp.log(1.0 + jnp.exp(-jnp.abs(z)))
        strict = scol < srow + s * R
        ln = jnp.where(strict, -sp, 0.0) if diag else -sp
        parts = [ln.astype(BF16)]
        for _ in range(SB_PIECES - 1):
            parts.append((ln - sum(p.astype(F32) for p in parts)).astype(BF16))
        sums = _dot(jnp.concatenate(parts, axis=0), upper)
        suf = sum(sums[p * R:(p + 1) * R] for p in range(SB_PIECES))
        a = jnp.exp(z - sp + suf + run)
        if diag:
            a = jnp.where(strict, a, 0.0)
        acc = acc + _dot(a.astype(BF16), v_ref[u, pl.ds(r, T), :])
        return run + suf[:, 0:1] + ln[:, 0:1], acc

    state = []
    for n in range(len(units)):
        state.extend(block(n, qi, True, jnp.zeros((R, 1), F32), jnp.zeros((R, d), F32)))

    def earlier(i, st):
        out = []
        for n in range(len(units)):
            out.extend(block(n, qi - i, False, st[2 * n], st[2 * n + 1]))
        return tuple(out)

    state = lax.fori_loop(1, qi + 1, earlier, tuple(state))
    for n, (u, s) in enumerate(units):
        out_ref[u, s * R:(s + 1) * R, :] = state[2 * n + 1].astype(out_ref.dtype)


def _stick_breaking(proj_c, B, S):
    T = SB_BLOCK
    nq = S // T
    d = HEAD_DIM
    H = N_HEADS
    U = SB_HEADS
    return pl.pallas_call(
        _sb_kernel,
        grid=(B, H // U, nq),
        in_specs=[
            pl.BlockSpec((U, T, d), lambda b, h, i: (C_SQ // U + h, b * nq + i, 0)),
            pl.BlockSpec((U, S, d), lambda b, h, i: (C_SK // U + h, b, 0)),
            pl.BlockSpec((U, S, d), lambda b, h, i: (C_SV // U + h, b, 0)),
        ],
        out_specs=pl.BlockSpec((U, T, d), lambda b, h, i: (h, b * nq + i, 0)),
        out_shape=jax.ShapeDtypeStruct((H, B * S, d), BF16),
        compiler_params=_params(("parallel", "parallel", "parallel")),
        name="stick_breaking",
    )(proj_c, proj_c, proj_c)


def _merge_kernel(hm_ref, hd_ref, hs_ref, g0_ref, g1_ref, g2_ref, w_ref, o_ref, wb):
    H = N_HEADS

    @pl.when(pl.program_id(1) == 0)
    def _():
        wb[...] = w_ref[...].astype(BF16)

    acc = None
    ns = o_ref.shape[1] // LANES
    for n, (br, gr) in enumerate(((hm_ref, g0_ref), (hd_ref, g1_ref), (hs_ref, g2_ref))):
        a = jnp.concatenate([br[h] for h in range(H)], axis=1)
        up = _dot(a, wb[n])
        gate = jnp.concatenate([gr[s] for s in range(ns)], axis=1).astype(F32)
        term = _sigmoid(gate) * up
        acc = term if acc is None else acc + term
    o_ref[...] = acc.astype(o_ref.dtype)


def _merge(hm, hd, hs, proj_c, w_branch, layer, tm=1024, tn=512):
    H, n, d = hm.shape
    tm = min(tm, n)
    D = w_branch.shape[3]
    ns = tn // LANES
    br = pl.BlockSpec((H, tm, d), lambda j, i: (0, i, 0))

    def gate(b):
        first = (C_GATE + b * (D // LANES)) // ns
        return pl.BlockSpec((ns, tm, LANES), lambda j, i: (first + j, i, 0))

    return pl.pallas_call(
        _merge_kernel,
        grid=(D // tn, n // tm),
        in_specs=[br, br, br, gate(0), gate(1), gate(2),
                  pl.BlockSpec((None, N_BRANCHES, H * d, tn), lambda j, i: (layer, 0, 0, j))],
        out_specs=pl.BlockSpec((tm, tn), lambda j, i: (i, j)),
        out_shape=jax.ShapeDtypeStruct((n, D), BF16),
        scratch_shapes=[pltpu.VMEM((N_BRANCHES, H * d, tn), BF16)],
        compiler_params=_params(("parallel", "arbitrary")),
        name="branch_merge",
    )(hm, hd, hs, proj_c, proj_c, proj_c, w_branch)


def _mm_res_kernel(a_ref, w_ref, x_ref, o_ref, wb):
    @pl.when(pl.program_id(1) == 0)
    def _():
        wb[...] = w_ref[...].astype(BF16)

    o_ref[...] = x_ref[...] + _dot(a_ref[...], wb[...])


def _matmul_residual(a, w, layer, x, tm=1024, tn=512):
    m, k = a.shape
    tm = min(tm, m)
    n = w.shape[2]
    return pl.pallas_call(
        _mm_res_kernel,
        grid=(n // tn, m // tm),
        in_specs=[pl.BlockSpec((tm, k), lambda j, i: (i, 0)),
                  pl.BlockSpec((None, k, tn), lambda j, i: (layer, 0, j)),
                  pl.BlockSpec((tm, tn), lambda j, i: (i, j))],
        out_specs=pl.BlockSpec((tm, tn), lambda j, i: (i, j)),
        out_shape=jax.ShapeDtypeStruct((m, n), F32),
        scratch_shapes=[pltpu.VMEM((k, tn), BF16)],
        compiler_params=_params(("parallel", "arbitrary")),
        name="out_proj",
    )(a, w, x)


def _router_kernel(x_ref, g_ref, wr_ref, br_ref, xn_ref, ids_ref, wts_ref, cnt_ref, carry):
    i = pl.program_id(0)
    tm, D = x_ref.shape

    @pl.when(i == 0)
    def _():
        carry[...] = jnp.zeros(carry.shape, F32)

    x = x_ref[...]
    xn = x * lax.rsqrt(jnp.mean(x * x, axis=-1, keepdims=True) + EPS) * g_ref[...]
    xn_ref[...] = xn
    logits = jnp.dot(xn, wr_ref[...], preferred_element_type=F32, precision=lax.Precision.HIGHEST)
    biased = logits + br_ref[...]
    lane = lax.broadcasted_iota(I32, (tm, LANES), 1)
    lanef = lane.astype(F32)
    big = float(LANES)

    def first_lane(mask):
        return jnp.min(jnp.where(mask, lanef, big), axis=1, keepdims=True)

    gmask = lane < N_GROUPS
    gmax = jnp.max(jnp.where(gmask, biased, NEG), axis=1, keepdims=True)
    g_sel = first_lane(gmask & (biased == gmax))
    gm = jnp.max(jnp.where(gmask, logits, NEG), axis=1, keepdims=True)
    ge = jnp.where(gmask, jnp.exp(logits - gm), 0.0)
    p_group = jnp.sum(jnp.where(lanef == g_sel, ge, 0.0), axis=1, keepdims=True) / jnp.sum(ge, axis=1, keepdims=True)
    lo = N_GROUPS + EXPERTS_PER_GROUP * g_sel
    emask = (lanef >= lo) & (lanef < lo + EXPERTS_PER_GROUP)
    eb = jnp.where(emask, biased, NEG)
    e1 = first_lane(emask & (eb == jnp.max(eb, axis=1, keepdims=True)))
    emask2 = emask & (lanef != e1)
    eb2 = jnp.where(emask2, biased, NEG)
    e2 = first_lane(emask2 & (eb2 == jnp.max(eb2, axis=1, keepdims=True)))
    em = jnp.max(jnp.where(emask, logits, NEG), axis=1, keepdims=True)
    ee = jnp.where(emask, jnp.exp(logits - em), 0.0)
    s1 = jnp.sum(jnp.where(lanef == e1, ee, 0.0), axis=1, keepdims=True)
    s2 = jnp.sum(jnp.where(lanef == e2, ee, 0.0), axis=1, keepdims=True)
    se = jnp.sum(ee, axis=1, keepdims=True)
    w1 = s1 / se
    w2 = s2 / se
    wsum = w1 + w2
    w1 = p_group * w1 / wsum
    w2 = p_group * w2 / wsum
    x1 = e1 - N_GROUPS
    x2 = e2 - N_GROUPS

    onehot = jnp.where((lanef == x1) | (lanef == x2), 1.0, 0.0)
    r = lax.broadcasted_iota(I32, (tm, tm), 0)
    c = lax.broadcasted_iota(I32, (tm, tm), 1)
    lower = jnp.where(c < r, 1.0, 0.0).astype(BF16)
    prefix = _dot(lower, onehot.astype(BF16)) + carry[...]
    r1 = jnp.sum(jnp.where(lanef == x1, prefix, 0.0), axis=1, keepdims=True)
    r2 = jnp.sum(jnp.where(lanef == x2, prefix, 0.0), axis=1, keepdims=True)
    carry[...] = carry[...] + jnp.sum(onehot, axis=0, keepdims=True)
    cnt_ref[...] = carry[...].astype(I32)

    idsf = jnp.where(lane == 0, x1, jnp.where(lane == 1, x2, jnp.where(lane == 2, r1, jnp.where(lane == 3, r2, 0.0))))
    ids_ref[...] = idsf.astype(I32)
    wts_ref[...] = jnp.where(lane == 0, w1, jnp.where(lane == 1, w2, 0.0))


def _router(x2d, g, w_router, b_router, tm=512):
    n, D = x2d.shape
    return pl.pallas_call(
        _router_kernel,
        grid=(n // tm,),
        in_specs=[pl.BlockSpec((tm, D), lambda i: (i, 0)), pl.BlockSpec((1, D), lambda i: (0, 0)),
                  pl.BlockSpec((D, LANES), lambda i: (0, 0)), pl.BlockSpec((1, LANES), lambda i: (0, 0))],
        out_specs=[pl.BlockSpec((tm, D), lambda i: (i, 0)),
                   pl.BlockSpec((tm, LANES), lambda i: (i, 0)),
                   pl.BlockSpec((tm, LANES), lambda i: (i, 0)),
                   pl.BlockSpec((1, LANES), lambda i: (0, 0))],
        out_shape=[jax.ShapeDtypeStruct((n, D), F32),
                   jax.ShapeDtypeStruct((n, LANES), I32),
                   jax.ShapeDtypeStruct((n, LANES), F32),
                   jax.ShapeDtypeStruct((1, LANES), I32)],
        scratch_shapes=[pltpu.VMEM((1, LANES), F32)],
        compiler_params=_params(("arbitrary",)),
        name="moe_router",
    )(x2d, g.reshape(1, D), w_router, b_router)


def _dispatch_kernel(p1_ref, p2_ref, xn_ref, xs_in_ref, xs_ref, sem, *, tb):
    del xs_in_ref
    base = pl.program_id(0) * tb

    def copies(t):
        src = xn_ref.at[pl.ds(t, 1), :]
        return (pltpu.make_async_copy(src, xs_ref.at[pl.ds(p1_ref[base + t], 1), :], sem),
                pltpu.make_async_copy(src, xs_ref.at[pl.ds(p2_ref[base + t], 1), :], sem))

    def issue(t, _):
        for cp in copies(t):
            cp.start()
        return 0

    lax.fori_loop(0, tb, issue, 0, unroll=DMA_UNROLL)

    def drain(t, _):
        for cp in copies(t):
            cp.wait()
        return 0

    lax.fori_loop(0, tb, drain, 0, unroll=DMA_UNROLL)


def _dispatch(pos1, pos2, xn, xs0, tb=512):
    n, D = xn.shape
    n_rows = xs0.shape[0]
    tb = min(tb, n)
    grid_spec = pltpu.PrefetchScalarGridSpec(
        num_scalar_prefetch=2,
        grid=(n // tb,),
        in_specs=[pl.BlockSpec((tb, D), lambda i, p1, p2: (i, 0)),
                  pl.BlockSpec(memory_space=pl.ANY)],
        out_specs=pl.BlockSpec(memory_space=pl.ANY),
        scratch_shapes=[pltpu.SemaphoreType.DMA(())],
    )
    return pl.pallas_call(
        functools.partial(_dispatch_kernel, tb=tb),
        grid_spec=grid_spec,
        out_shape=jax.ShapeDtypeStruct((n_rows, D), F32),
        input_output_aliases={3: 0},
        compiler_params=pltpu.CompilerParams(dimension_semantics=("arbitrary",), has_side_effects=True),
        name="moe_dispatch",
    )(pos1, pos2, xn, xs0)


def _expert_kernel(te_ref, nv_ref, xs_ref, wg_ref, wu_ref, wd_ref, ys_ref, wgb, wub, wdb):
    i = pl.program_id(0)
    prev = te_ref[jnp.maximum(i - 1, 0)]

    @pl.when(i < nv_ref[0])
    def _():
        @pl.when((i == 0) | (te_ref[i] != prev))
        def _():
            wgb[...] = wg_ref[...].astype(BF16)
            wub[...] = wu_ref[...].astype(BF16)
            wdb[...] = wd_ref[...].astype(BF16)

        x = xs_ref[...].astype(BF16)
        g = _dot(x, wgb[...])
        u = _dot(x, wub[...])
        hcur = (g * _sigmoid(g) * u).astype(BF16)
        ys_ref[...] = _dot(hcur, wdb[...])

    @pl.when(i >= nv_ref[0])
    def _():
        ys_ref[...] = jnp.zeros(ys_ref.shape, F32)


def _experts(tile_expert, n_valid, xs, w_gate, w_up, w_down, layer):
    n_rows = xs.shape[0]
    _, E, D, Fe = w_gate.shape
    tm = MOE_TILE
    n_tiles = n_rows // tm

    def row_map(i, te, nv):
        return (jnp.minimum(i, jnp.maximum(nv[0] - 1, 0)), 0)

    def w_map(i, te, nv):
        return (layer, te[i], 0, 0)

    grid_spec = pltpu.PrefetchScalarGridSpec(
        num_scalar_prefetch=2,
        grid=(n_tiles,),
        in_specs=[pl.BlockSpec((tm, D), row_map),
                  pl.BlockSpec((None, None, D, Fe), w_map), pl.BlockSpec((None, None, D, Fe), w_map),
                  pl.BlockSpec((None, None, Fe, D), w_map)],
        out_specs=pl.BlockSpec((tm, D), lambda i, te, nv: (i, 0)),
        scratch_shapes=[pltpu.VMEM((D, Fe), BF16), pltpu.VMEM((D, Fe), BF16), pltpu.VMEM((Fe, D), BF16)],
    )
    return pl.pallas_call(
        _expert_kernel,
        grid_spec=grid_spec,
        out_shape=jax.ShapeDtypeStruct((n_rows, D), F32),
        compiler_params=_params(("arbitrary",)),
        name="moe_experts",
    )(tile_expert, n_valid, xs, w_gate, w_up, w_down)


def _combine_kernel(p1_ref, p2_ref, ys_ref, x_ref, w_ref, o_ref, buf, sem):
    i = pl.program_id(0)
    tc = x_ref.shape[0]
    slot = i % 2

    def copies(tile, slot, t):
        tok = tile * tc + t
        return (pltpu.make_async_copy(ys_ref.at[pl.ds(p1_ref[tok], 1), :], buf.at[slot, 0, pl.ds(t, 1), :],
                                      sem.at[slot]),
                pltpu.make_async_copy(ys_ref.at[pl.ds(p2_ref[tok], 1), :], buf.at[slot, 1, pl.ds(t, 1), :],
                                      sem.at[slot]))

    def issue(tile, slot):
        def body(t, _):
            for cp in copies(tile, slot, t):
                cp.start()
            return 0

        lax.fori_loop(0, tc, body, 0, unroll=DMA_UNROLL)

    @pl.when(i == 0)
    def _():
        issue(0, 0)

    @pl.when(i + 1 < pl.num_programs(0))
    def _():
        issue(i + 1, 1 - slot)

    def drain(t, _):
        for cp in copies(i, slot, t):
            cp.wait()
        return 0

    lax.fori_loop(0, tc, drain, 0, unroll=DMA_UNROLL)
    w = w_ref[...]
    o_ref[...] = x_ref[...] + w[:, 0:1] * buf[slot, 0] + w[:, 1:2] * buf[slot, 1]


def _combine(pos1, pos2, ys, x2d, wts, tc=256):
    n, D = x2d.shape
    grid_spec = pltpu.PrefetchScalarGridSpec(
        num_scalar_prefetch=2,
        grid=(n // tc,),
        in_specs=[pl.BlockSpec(memory_space=pl.ANY),
                  pl.BlockSpec((tc, D), lambda i, p1, p2: (i, 0)),
                  pl.BlockSpec((tc, LANES), lambda i, p1, p2: (i, 0))],
        out_specs=pl.BlockSpec((tc, D), lambda i, p1, p2: (i, 0)),
        scratch_shapes=[pltpu.VMEM((2, 2, tc, D), F32), pltpu.SemaphoreType.DMA((2,))],
    )
    return pl.pallas_call(
        _combine_kernel,
        grid_spec=grid_spec,
        out_shape=jax.ShapeDtypeStruct((n, D), F32),
        compiler_params=_params(("arbitrary",)),
        name="moe_combine",
    )(pos1, pos2, ys, x2d, wts)


def _pos_kernel(ids_ref, cnt_ref, pos_ref):
    tm = ids_ref.shape[0]
    tiles = jnp.floor((cnt_ref[...].astype(F32) + (MOE_TILE - 1)) / MOE_TILE)
    r = lax.broadcasted_iota(I32, (LANES, LANES), 0)
    c = lax.broadcasted_iota(I32, (LANES, LANES), 1)
    before = jnp.where(r < c, 1.0, 0.0).astype(BF16)
    first_tile = _dot(jnp.broadcast_to(tiles, (8, LANES)).astype(BF16), before)
    offs = first_tile[0:1, :] * MOE_TILE
    ids = ids_ref[...].astype(F32)
    lane = lax.broadcasted_iota(I32, (tm, LANES), 1)
    lanef = lane.astype(F32)
    p1 = jnp.sum(jnp.where(lanef == ids[:, 0:1], offs, 0.0), axis=1, keepdims=True) + ids[:, 2:3]
    p2 = jnp.sum(jnp.where(lanef == ids[:, 1:2], offs, 0.0), axis=1, keepdims=True) + ids[:, 3:4]
    packed = jnp.where(lane == 0, p1, jnp.where(lane == 1, p2, 0.0))
    pos_ref[...] = packed.T[0:8, :].astype(I32)


def _positions(ids, counts, tm=512):
    n = ids.shape[0]
    assert MOE_TILE & (MOE_TILE - 1) == 0
    return pl.pallas_call(
        _pos_kernel,
        grid=(n // tm,),
        in_specs=[pl.BlockSpec((tm, LANES), lambda i: (i, 0)), pl.BlockSpec((1, LANES), lambda i: (0, 0))],
        out_specs=pl.BlockSpec((8, tm), lambda i: (0, i)),
        out_shape=jax.ShapeDtypeStruct((8, n), I32),
        compiler_params=_params(("parallel",)),
        name="moe_positions",
    )(ids, counts)


def _hier_moe(x2d, norm_g, w_rg, b_rg, w_re, b_re, w_gate, w_up, w_down, layer, xs_buf):
    n, D = x2d.shape
    tm = MOE_TILE
    pad = LANES - N_GROUPS - N_EXPERTS
    w_router = jnp.concatenate([w_rg, w_re, jnp.zeros((D, pad), F32)], axis=1)
    b_router = jnp.concatenate([b_rg, b_re.reshape(-1), jnp.zeros((pad,), F32)]).reshape(1, LANES)
    xn3, ids, wts, counts = _router(x2d, norm_g, w_router, b_router)
    cnt = counts[0, :N_EXPERTS]
    padded = ((cnt + tm - 1) // tm) * tm
    ends = jnp.cumsum(padded)
    n_tiles = xs_buf.shape[0] // tm
    tile_start = jnp.arange(n_tiles, dtype=I32) * tm
    tile_expert = jnp.minimum(jnp.sum(tile_start[:, None] >= ends[None, :], axis=1), N_EXPERTS - 1).astype(I32)
    n_valid = (ends[-1] // tm).astype(I32).reshape(1)
    last_e = tile_expert[jnp.maximum(n_valid[0] - 1, 0)]
    tile_expert = jnp.where(jnp.arange(n_tiles) < n_valid[0], tile_expert, last_e)
    pos = _positions(ids, counts)
    pos1, pos2 = pos[0], pos[1]
    xs = _dispatch(pos1, pos2, xn3, xs_buf)
    ys = _experts(tile_expert, n_valid, xs, w_gate, w_up, w_down, layer)
    return _combine(pos1, pos2, ys, x2d, wts), xs


def _in_proj_regions(D):
    bw = BRANCH_WIDTH
    sizes = (bw, bw, bw, bw, N_HEADS, N_HEADS, bw, bw, bw, IDX_HEADS * IDX_DIM, IDX_DIM, IDX_HEADS,
             bw, bw, bw, N_BRANCHES * D)
    offs = np.concatenate([[0], np.cumsum(sizes)]).tolist()
    region_a = (offs[0], offs[4] - offs[0])
    region_b = (offs[6], offs[10] - offs[6])
    region_c = (offs[12], offs[16] - offs[12])
    small = dict(mi=offs[4], mf=offs[5], ik=offs[10], iw=offs[11])
    return region_a, region_b, region_c, small


def _token_mixer(x2d, B, S, layer, norm_g, w_in, conv_w, b_i, b_f, mlstm_norm_g, q_norm_g, k_norm_g,
                 w_branch, w_out, rel_bias):
    n, D = x2d.shape
    L = MLSTM_CHUNK
    H = N_HEADS
    tm = min(n, PROJ_TM)
    xn = _rmsnorm(x2d, norm_g)
    ra, rb, rc, small = _in_proj_regions(D)
    w_t = jnp.swapaxes(w_in, 1, 2)
    proj_a = _in_proj(xn, w_t, layer, ra[0], ra[1], tm)
    proj_b = _in_proj(xn, w_t, layer, rb[0], rb[1], tm)
    proj_c = _in_proj(xn, w_t, layer, rc[0], rc[1], tm)
    tail = _tail_proj(xn, w_t, layer, small, tm)
    g = tail[1][:, IDX_HEADS:IDX_HEADS + 2 * H].reshape(B, S // L, L, 2, H)
    gates_t = jnp.transpose(g, (0, 4, 1, 3, 2))
    hm = _mlstm(proj_a, gates_t, jnp.stack([b_i, b_f]), conv_w, mlstm_norm_g, B, S)
    hd = _dsa(proj_b, tail, q_norm_g, k_norm_g, rel_bias, B, S)
    hs = _stick_breaking(proj_c, B, S)
    merged = _merge(hm, hd, hs, proj_c, w_branch, layer)
    return _matmul_residual(merged, w_out, layer, x2d)


def kernel(x, norm1_g, w_in, conv_w, b_i, b_f, mlstm_norm_g, q_norm_g, k_norm_g, w_branch, w_out, norm2_g,
           w_router_g, b_router_g, w_router_e, b_router_e, w_gate, w_up, w_down, rel_bias):
    B, S, D = x.shape
    x2d = x.reshape(B * S, D)
    xs_buf = jnp.zeros((2 * B * S + N_EXPERTS * MOE_TILE, D), F32)
    for l in range(w_in.shape[0]):
        x2d = _token_mixer(x2d, B, S, l, norm1_g[l], w_in, conv_w[l], b_i[l], b_f[l], mlstm_norm_g[l],
                           q_norm_g[l], k_norm_g[l], w_branch, w_out, rel_bias)
        x2d, xs_buf = _hier_moe(x2d, norm2_g[l], w_router_g[l], b_router_g[l], w_router_e[l], b_router_e[l],
                                w_gate, w_up, w_down, l, xs_buf)
    return x2d.reshape(B, S, D)
```

```python
import functools
import math

import numpy as np
import jax
import jax.numpy as jnp
from jax import lax
from jax.experimental import pallas as pl
from jax.experimental.pallas import tpu as pltpu

F32 = jnp.float32
BF16 = jnp.bfloat16
I32 = jnp.int32

LANES = 128
HEAD_DIM = 128
N_HEADS = 8
BRANCH_WIDTH = N_HEADS * HEAD_DIM
N_BRANCHES = 3
CONV_WIDTH = 4
IDX_HEADS = 16
IDX_DIM = 64
TOPK_MAX = 256
N_BUCKETS = 32
MAX_DISTANCE = 128
N_GROUPS = 4
EXPERTS_PER_GROUP = 8
N_EXPERTS = N_GROUPS * EXPERTS_PER_GROUP
EPS = 1e-6
NEG = -1e30
INT_MIN = -(2 ** 31)

MLSTM_CHUNK = 256
DSA_BLOCK = 128
DSA_BANDS = 4
DSA_QBLOCKS = 2
SB_BLOCK = 512
SB_ROWS = 512
SB_PIECES = 1
SB_HEADS = 4
HEADS_PER_STEP = 4
MOE_TILE = 256
DMA_UNROLL = 8
PROJ_TN = 512
PROJ_TM = 2048
VMEM_LIMIT = 56 * 1024 * 1024

A_MQ, A_MK, A_MV, A_MO = 0, 8, 16, 24
B_DQ, B_DK, B_DV, B_IQ = 0, 8, 16, 24
C_SQ, C_SK, C_SV, C_GATE = 0, 8, 16, 24


def _params(sem):
    return pltpu.CompilerParams(dimension_semantics=sem, vmem_limit_bytes=VMEM_LIMIT)


def _dot(a, b):
    return jnp.dot(a, b, preferred_element_type=F32)


def _dot_nt(a, b):
    return lax.dot_general(a, b, (((1,), (1,)), ((), ())), preferred_element_type=F32)


def _sigmoid(z):
    return 1.0 / (1.0 + jnp.exp(-z))


def _rmsnorm_kernel(x_ref, g_ref, o_ref):
    x = x_ref[...]
    ms = jnp.mean(x * x, axis=-1, keepdims=True)
    o_ref[...] = (x * lax.rsqrt(ms + EPS) * g_ref[...]).astype(o_ref.dtype)


def _rmsnorm(x2d, g, tm=512):
    n, d = x2d.shape
    return pl.pallas_call(
        _rmsnorm_kernel,
        grid=(n // tm,),
        in_specs=[pl.BlockSpec((tm, d), lambda i: (i, 0)), pl.BlockSpec((1, d), lambda i: (0, 0))],
        out_specs=pl.BlockSpec((tm, d), lambda i: (i, 0)),
        out_shape=jax.ShapeDtypeStruct((n, d), BF16),
        compiler_params=_params(("parallel",)),
        name="rmsnorm",
    )(x2d, g.reshape(1, d))


def _in_proj_kernel(x_ref, *rest, shift, nblk):
    w_refs, o_ref, wb = rest[:nblk], rest[nblk], rest[nblk + 1]
    tn = wb.shape[1]

    @pl.when(pl.program_id(1) == 0)
    def _():
        w = jnp.concatenate([r[...] for r in w_refs], axis=0)
        wb[...] = w[shift:shift + tn, :].T.astype(BF16)

    acc = _dot(x_ref[...], wb[...])
    for j in range(o_ref.shape[0]):
        o_ref[j] = acc[:, j * LANES:(j + 1) * LANES].astype(o_ref.dtype)


def _in_proj(xn, w_t, layer, col0, ncols, tm):
    m, k = xn.shape
    tn = PROJ_TN
    base, shift = divmod(col0, LANES)
    assert shift % 8 == 0
    nblk = tn // LANES + (1 if shift else 0)
    per = tn // LANES

    def wspec(r):
        return pl.BlockSpec((None, LANES, k), lambda j, i: (layer, base + per * j + r, 0))

    return pl.pallas_call(
        functools.partial(_in_proj_kernel, shift=shift, nblk=nblk),
        grid=(ncols // tn, m // tm),
        in_specs=[pl.BlockSpec((tm, k), lambda j, i: (i, 0))] + [wspec(r) for r in range(nblk)],
        out_specs=pl.BlockSpec((per, tm, LANES), lambda j, i: (j, i, 0)),
        out_shape=jax.ShapeDtypeStruct((ncols // LANES, m, LANES), BF16),
        scratch_shapes=[pltpu.VMEM((k, tn), BF16)],
        compiler_params=_params(("parallel", "arbitrary")),
        name="in_proj",
    )(xn, *([w_t] * nblk))


def _tail_kernel(x_ref, wg_ref, wi_ref, o_ref, wt, *, g_lane, ik_lane, iw_lane):
    @pl.when(pl.program_id(0) == 0)
    def _():
        wg = wg_ref[...]
        wi = wi_ref[...]
        ik = wi[ik_lane:ik_lane + IDX_DIM, :]
        iw = wi[iw_lane:iw_lane + IDX_HEADS, :]
        gates = wg[g_lane:g_lane + 2 * N_HEADS, :]
        pad = jnp.zeros((LANES - IDX_HEADS - 2 * N_HEADS, wg.shape[1]), F32)
        wt[...] = jnp.concatenate([ik, ik, iw, gates, pad], axis=0).T.astype(BF16)

    acc = _dot(x_ref[...], wt[...])
    o_ref[0] = acc[:, :LANES]
    o_ref[1] = acc[:, LANES:]


def _tail_proj(xn, w_t, layer, small, tm):
    m, k = xn.shape
    g_blk, g_lane = divmod(small["mi"], LANES)
    i_blk, ik_lane = divmod(small["ik"], LANES)
    iw_lane = small["iw"] - i_blk * LANES
    assert small["mf"] == small["mi"] + N_HEADS and g_lane + 2 * N_HEADS <= LANES
    assert ik_lane + IDX_DIM <= LANES and 0 <= iw_lane and iw_lane + IDX_HEADS <= LANES
    assert g_lane % 8 == 0 and ik_lane % 8 == 0 and iw_lane % 8 == 0
    return pl.pallas_call(
        functools.partial(_tail_kernel, g_lane=g_lane, ik_lane=ik_lane, iw_lane=iw_lane),
        grid=(m // tm,),
        in_specs=[pl.BlockSpec((tm, k), lambda i: (i, 0)),
                  pl.BlockSpec((None, LANES, k), lambda i: (layer, g_blk, 0)),
                  pl.BlockSpec((None, LANES, k), lambda i: (layer, i_blk, 0))],
        out_specs=pl.BlockSpec((2, tm, LANES), lambda i: (0, i, 0)),
        out_shape=jax.ShapeDtypeStruct((2, m, LANES), F32),
        scratch_shapes=[pltpu.VMEM((k, 2 * LANES), BF16)],
        compiler_params=_params(("arbitrary",)),
        name="tail_proj",
    )(xn, w_t, w_t)


def _mlstm_kernel(bias_ref, q_ref, k_ref, v_ref, o_ref, g_ref, cwq_ref, cwk_ref, ng_ref, out_ref,
                  qf, kf, qc, kc, st):
    hp = pl.program_id(1)
    U = HEADS_PER_STEP
    S = q_ref.shape[1]
    L = MLSTM_CHUNK
    nc = S // L
    d = HEAD_DIM
    PAD = 8

    R = min(S, 256)
    for u in range(U):
        qf[u, 0:PAD, :] = jnp.zeros((PAD, d), F32)
        kf[u, 0:PAD, :] = jnp.zeros((PAD, d), F32)
        qf[u, PAD:PAD + S, :] = q_ref[u].astype(F32)
        kf[u, PAD:PAD + S, :] = k_ref[u].astype(F32)
        ls = slice(u * d, (u + 1) * d)
        for r0 in range(0, S, R):
            aq = jnp.zeros((R, d), F32)
            ak = jnp.zeros((R, d), F32)
            for t in range(CONV_WIDTH):
                off = PAD - (CONV_WIDTH - 1) + t + r0
                aq = aq + cwq_ref[t:t + 1, ls] * qf[u, off:off + R, :]
                ak = ak + cwk_ref[t:t + 1, ls] * kf[u, off:off + R, :]
            qc[u, r0:r0 + R, :] = (aq * _sigmoid(aq) * (d ** -0.5)).astype(BF16)
            kc[u, r0:r0 + R, :] = ak * _sigmoid(ak)

    st[...] = jnp.zeros(st.shape, F32)
    row = lax.broadcasted_iota(I32, (L, L), 0)
    col = lax.broadcasted_iota(I32, (L, L), 1)
    causal = col <= row
    eye = col == row
    lane = lax.broadcasted_iota(I32, (L, d), 1)
    ones_col = jnp.where(lane == 0, 1.0, 0.0).astype(BF16)
    ng = ng_ref[...]

    def chunk_one(u, c, r, m):
        q = qc[u, pl.ds(r, L), :]
        kT = kc[u, pl.ds(r, L), :].T
        v = v_ref[u, pl.ds(r, L), :]
        vaug = jnp.concatenate([v, ones_col], axis=1)
        gates = g_ref[u, c]
        i_row = gates[0:1, :] + bias_ref[0, hp * U + u]
        f_row = gates[1:2, :] + bias_ref[1, hp * U + u]
        lf_row = jnp.minimum(f_row, 0.0) - jnp.log1p(jnp.exp(-jnp.abs(f_row)))
        b_col = jnp.sum(jnp.where(causal, lf_row, 0.0), axis=1, keepdims=True)
        b_row = jnp.sum(jnp.where(eye, b_col, 0.0), axis=0, keepdims=True)
        dlog = jnp.where(causal, b_col - b_row + i_row, NEG)
        inter = b_col + m
        m_t = jnp.maximum(inter, jnp.max(dlog, axis=1, keepdims=True))
        w_intra = jnp.exp(dlog - m_t)
        w_inter = jnp.exp(inter - m_t)
        s = _dot(q, kT.astype(BF16)) * w_intra
        res = w_inter * _dot(q, st[u].astype(BF16)) + _dot(s.astype(BF16), vaug)
        num = res[:, :d]
        den = res[:, d:d + 1]
        hh = num / jnp.maximum(jnp.abs(den), jnp.exp(-m_t))
        hn = hh * lax.rsqrt(jnp.mean(hh * hh, axis=1, keepdims=True) + EPS) * ng[:, u * d:(u + 1) * d]
        og = o_ref[u, pl.ds(r, L), :].astype(F32)
        out_ref[u, pl.ds(r, L), :] = (hn * _sigmoid(og)).astype(out_ref.dtype)
        ws_row = w_intra[L - 1:L, :]
        decay = w_inter[L - 1:L, :]
        st[u] = decay * st[u] + _dot((kT * ws_row).astype(BF16), vaug)
        return m_t[L - 1:L, :]

    def chunk(c, ms):
        r = pl.multiple_of(c * L, L)
        return tuple(chunk_one(u, c, r, ms[u]) for u in range(U))

    lax.fori_loop(0, nc, chunk, tuple(jnp.zeros((1, 1), F32) for _ in range(U)))


def _mlstm(proj_a, gates_t, bias_if, conv_w, norm_g, B, S):
    L = MLSTM_CHUNK
    nc = S // L
    d = HEAD_DIM
    H = N_HEADS
    U = HEADS_PER_STEP

    def slab(off):
        return pl.BlockSpec((U, S, d), lambda b, h: (off // U + h, b, 0))

    return pl.pallas_call(
        _mlstm_kernel,
        grid=(B, H // U),
        in_specs=[
            pl.BlockSpec(memory_space=pltpu.SMEM),
            slab(A_MQ), slab(A_MK), slab(A_MV), slab(A_MO),
            pl.BlockSpec((None, U, nc, 2, L), lambda b, h: (b, h, 0, 0, 0)),
            pl.BlockSpec((CONV_WIDTH, U * d), lambda b, h: (0, h)),
            pl.BlockSpec((CONV_WIDTH, U * d), lambda b, h: (0, H // U + h)),
            pl.BlockSpec((1, U * d), lambda b, h: (0, h)),
        ],
        out_specs=pl.BlockSpec((U, S, d), lambda b, h: (h, b, 0)),
        out_shape=jax.ShapeDtypeStruct((H, B * S, d), BF16),
        scratch_shapes=[
            pltpu.VMEM((U, S + 8, d), F32), pltpu.VMEM((U, S + 8, d), F32),
            pltpu.VMEM((U, S, d), BF16), pltpu.VMEM((U, S, d), F32),
            pltpu.VMEM((U, d, 2 * d), F32),
        ],
        compiler_params=_params(("parallel", "parallel")),
        name="mlstm",
    )(bias_if, proj_a, proj_a, proj_a, proj_a, gates_t, conv_w, conv_w, norm_g.reshape(1, H * d))


def _t5_thresholds():
    max_exact = N_BUCKETS // 2
    n = np.arange(0, 2 * MAX_DISTANCE)
    nf = np.maximum(n, 1).astype(np.float64)
    val = np.log(nf / max_exact) / math.log(MAX_DISTANCE / max_exact) * (N_BUCKETS - max_exact)
    frac = np.abs(val - np.round(val))
    frac_ok = (frac > 1e-4) | (n <= max_exact) | (n >= MAX_DISTANCE)
    assert frac_ok.all()
    large = np.minimum(max_exact + np.trunc(val).astype(np.int64), N_BUCKETS - 1)
    bucket = np.where(n < max_exact, n, large)
    assert (np.diff(bucket) >= 0).all() and bucket[MAX_DISTANCE] == N_BUCKETS - 1
    return [int(np.argmax(bucket >= j)) for j in range(1, N_BUCKETS)]


_T5_THR = _t5_thresholds()


def _dsa_kernel(rb_ref, q_ref, k_ref, v_ref, iq_ref, ik_ref, wt_ref, gq_ref, gk_ref, prev_ref, out_ref,
                kn, ikk, wb, key_ref, mask_ref, bias_ref, *, topk, q0, nb, nqb):
    del prev_ref
    qis = [q0 + pl.program_id(1) * nqb + qb for qb in range(nqb)]
    T = DSA_BLOCK
    d = HEAD_DIM
    H = N_HEADS
    HALF = T // 2
    row = lax.broadcasted_iota(I32, (T, T), 0)
    col = lax.broadcasted_iota(I32, (T, T), 1)

    @pl.when(pl.program_id(1) == 0)
    def _per_batch():
        gk = gk_ref[...]

        def norm_k(h, _):
            kh = k_ref[h].astype(F32)
            kn[h] = (kh * lax.rsqrt(jnp.mean(kh * kh, axis=1, keepdims=True) + EPS) * gk).astype(BF16)
            for o in range(2):
                n = o * T + row - col
                val = jnp.full((T, T), rb_ref[0, h], F32)
                for j, thr in enumerate(_T5_THR):
                    val = jnp.where(n >= thr, rb_ref[j + 1, h], val)
                bias_ref[h, o] = val
            bias_ref[h, 2] = jnp.full((T, T), rb_ref[N_BUCKETS - 1, h], F32)
            return 0

        lax.fori_loop(0, H, norm_k, 0)
        ik = ik_ref[...]
        klane = lax.broadcasted_iota(I32, ik.shape, 1)
        ikk[0] = jnp.where(klane < IDX_DIM, ik, 0.0).astype(BF16)
        ikk[1] = jnp.where(klane >= IDX_DIM, ik, 0.0).astype(BF16)

    wsc = wt_ref[...] * (IDX_HEADS ** -0.5 * IDX_DIM ** -0.5)
    for qb in range(nqb):
        for h16 in range(IDX_HEADS):
            wb[qb * IDX_HEADS + h16] = jnp.broadcast_to(wsc[qb * T:(qb + 1) * T, h16:h16 + 1], (T, LANES))
    q_pairs = iq_ref[...].reshape((IDX_HEADS // 2) * nqb * T, LANES)

    def score_block(j, _):
        r = pl.multiple_of(j * T, T)
        kk = jnp.concatenate([ikk[0, pl.ds(r, T), :], ikk[1, pl.ds(r, T), :]], axis=0)
        dots = _dot_nt(q_pairs, kk)
        for qb in range(nqb):
            sc = jnp.zeros((T, T), F32)
            for h16 in range(IDX_HEADS):
                hp, odd = divmod(h16, 2)
                r0 = (hp * nqb + qb) * T
                sc = sc + wb[qb * IDX_HEADS + h16] * jnp.maximum(dots[r0:r0 + T, odd * T:(odd + 1) * T], 0.0)
            bits = lax.bitcast_convert_type(sc, I32)
            key = jnp.where(bits < 0, bits ^ jnp.int32(0x7FFFFFFF), bits)
            key = jnp.where(sc == 0.0, 0, key)
            key = jnp.where(j * T + col <= qis[qb] * T + row, key, INT_MIN)
            key_ref[qb, j] = key
        return 0

    lax.fori_loop(0, nb, score_block, 0, unroll=2)

    kf = float(topk)
    ones = jnp.ones((LANES, LANES), BF16)

    def count(chain, pred):
        qb, lo = chain
        acc = jnp.zeros((HALF, LANES), F32)
        for jb in range(nb):
            acc = acc + jnp.where(pred(key_ref[qb, jb, lo:lo + HALF, :]), 1.0, 0.0)
        return _dot(acc.astype(BF16), ones)

    chains = [(qb, lo) for qb in range(nqb) for lo in (0, HALF)]
    zero_i = jnp.zeros((HALF, LANES), I32)
    thr0 = tuple(jnp.where(count(ch, lambda k: k >= zero_i) >= kf, jnp.int32(0), jnp.int32(INT_MIN))
                 for ch in chains)

    def bisect_bit(thrs, bit):
        out = []
        for ch, t in zip(chains, thrs):
            cand = t | bit
            out.append(jnp.where(count(ch, lambda k, c=cand: k >= c) >= kf, cand, t))
        return tuple(out)

    def bisect_pair(it, thrs):
        hi_bit = jnp.left_shift(jnp.int32(1), 29 - 2 * it)
        lo_bit = jnp.left_shift(jnp.int32(1), 28 - 2 * it)
        out = []
        for ch, t in zip(chains, thrs):
            c01, c10, c11 = t | lo_bit, t | hi_bit, t | hi_bit | lo_bit
            n01, n10, n11 = (count(ch, lambda k, c=c: k >= c) >= kf for c in (c01, c10, c11))
            out.append(jnp.where(n11, c11, jnp.where(n10, c10, jnp.where(n01, c01, t))))
        return tuple(out)

    thrs = bisect_bit(thr0, jnp.int32(1 << 30))
    thrs = lax.fori_loop(0, 15, bisect_pair, thrs)
    needs = [kf - count(ch, lambda k, t=t: k > t) for ch, t in zip(chains, thrs)]

    tri = jnp.where(row <= col, 1.0, 0.0).astype(BF16)
    for qb in range(nqb):
        need = jnp.concatenate(needs[2 * qb:2 * qb + 2], axis=0)
        thrb = jnp.concatenate(thrs[2 * qb:2 * qb + 2], axis=0)
        seen = jnp.zeros((T, LANES), F32)
        for jb in range(nb):
            key = key_ref[qb, jb]
            tie = key == thrb
            tie16 = jnp.where(tie, 1.0, 0.0).astype(BF16)
            pre = _dot(tie16, tri) + seen
            m = jnp.where(key > thrb, 0.0, jnp.where(tie, jnp.where(pre <= need, 0.0, NEG), NEG))
            mask_ref[qb, jb] = jnp.where(key == INT_MIN, NEG, m)
            seen = seen + _dot(tie16, ones)

    gq = gq_ref[...]

    def head(h):
        qh = q_ref[h].astype(F32)
        qn = (qh * lax.rsqrt(jnp.mean(qh * qh, axis=1, keepdims=True) + EPS) * gq * (d ** -0.5)).astype(BF16)
        lgs = _dot_nt(qn, kn[h])
        p_rows, sums = [], []
        for qb in range(nqb):
            mx = None
            lg_blocks = []
            for jb in range(nb):
                lg = (lgs[qb * T:(qb + 1) * T, jb * T:(jb + 1) * T]
                      + bias_ref[h, jnp.clip(qis[qb] - jb, 0, 2)] + mask_ref[qb, jb])
                lg_blocks.append(lg)
                mx = lg if mx is None else jnp.maximum(mx, lg)
            rowmax = jnp.max(mx, axis=1, keepdims=True)
            l = jnp.zeros((T, T), F32)
            ps = []
            for jb in range(nb):
                p = jnp.exp(lg_blocks[jb] - rowmax)
                l = l + p
                ps.append(p.astype(BF16))
            p_rows.append(jnp.concatenate(ps, axis=1))
            sums.append(jnp.sum(l, axis=1, keepdims=True))
        acc = _dot(jnp.concatenate(p_rows, axis=0), v_ref[h])
        out_ref[h] = (acc / jnp.concatenate(sums, axis=0)).astype(out_ref.dtype)

    def head_group(g, _):
        for u in range(HEADS_PER_STEP):
            head(g * HEADS_PER_STEP + u)
        return 0

    lax.fori_loop(0, H // HEADS_PER_STEP, head_group, 0)


def _dsa(proj_b, tail, q_norm_g, k_norm_g, rel_bias, B, S):
    T = DSA_BLOCK
    nq = S // T
    d = HEAD_DIM
    H = N_HEADS
    topk = min(TOPK_MAX, S // 4)
    per = nq // DSA_BANDS
    nqb = math.gcd(per, DSA_QBLOCKS)
    TQ = nqb * T
    pb = proj_b.reshape(proj_b.shape[0], B, S, d)
    tl = tail.reshape(tail.shape[0], B, S, LANES)
    out = jnp.zeros((H, B, S, d), BF16)
    for band in range(DSA_BANDS):
        q0 = band * per
        nb = q0 + per
        W = nb * T
        in_specs = [
            pl.BlockSpec(memory_space=pltpu.SMEM),
            pl.BlockSpec((H, None, TQ, d), lambda b, i: (B_DQ // H, b, q0 // nqb + i, 0)),
            pl.BlockSpec((H, None, W, d), lambda b, i: (B_DK // H, b, 0, 0)),
            pl.BlockSpec((H, None, W, d), lambda b, i: (B_DV // H, b, 0, 0)),
            pl.BlockSpec((H, None, TQ, d), lambda b, i: (B_IQ // H, b, q0 // nqb + i, 0)),
            pl.BlockSpec((None, None, W, LANES), lambda b, i: (0, b, 0, 0)),
            pl.BlockSpec((None, None, TQ, LANES), lambda b, i: (1, b, q0 // nqb + i, 0)),
            pl.BlockSpec((1, d), lambda b, i: (0, 0)),
            pl.BlockSpec((1, d), lambda b, i: (0, 0)),
            pl.BlockSpec(memory_space=pl.ANY),
        ]
        args = [rel_bias, pb, pb, pb, pb, tl, tl, q_norm_g.reshape(1, d), k_norm_g.reshape(1, d), out]
        out = pl.pallas_call(
            functools.partial(_dsa_kernel, topk=topk, q0=q0, nb=nb, nqb=nqb),
            grid=(B, per // nqb),
            in_specs=in_specs,
            out_specs=pl.BlockSpec((H, None, TQ, d), lambda b, i: (0, b, q0 // nqb + i, 0)),
            out_shape=jax.ShapeDtypeStruct((H, B, S, d), BF16),
            scratch_shapes=[
                pltpu.VMEM((H, W, d), BF16),
                pltpu.VMEM((2, W, LANES), BF16),
                pltpu.VMEM((nqb * IDX_HEADS, T, LANES), F32),
                pltpu.VMEM((nqb, nb, T, T), I32),
                pltpu.VMEM((nqb, nb, T, T), F32),
                pltpu.VMEM((H, 3, T, T), F32),
            ],
            input_output_aliases={len(args) - 1: 0},
            compiler_params=_params(("parallel", "arbitrary")),
            name=f"dsa_band{band}",
        )(*args)
    return out.reshape(H, B * S, d)


def _sb_kernel(q_ref, k_ref, v_ref, out_ref):
    qi = pl.program_id(2)
    U = SB_HEADS
    T = SB_BLOCK
    d = HEAD_DIM
    R = SB_ROWS
    ns = T // R
    row = lax.broadcasted_iota(I32, (T, T), 0)
    col = lax.broadcasted_iota(I32, (T, T), 1)
    upper = jnp.where(row > col, 1.0, 0.0).astype(BF16)
    srow = lax.broadcasted_iota(I32, (R, T), 0)
    scol = lax.broadcasted_iota(I32, (R, T), 1)
    units = [(u, s) for u in range(U) for s in range(ns)]
    qs = [(q_ref[u, s * R:(s + 1) * R, :].astype(F32) * (d ** -0.5)).astype(BF16) for u, s in units]

    def block(n, j, diag, run, acc):
        u, s = units[n]
        r = pl.multiple_of(j * T, T)
        z = _dot_nt(qs[n], k_ref[u, pl.ds(r, T), :])
        sp = jnp.maximum(z, 0.0) + jnp.log(1.0 + jnp.exp(-jnp.abs(z)))
        strict = scol < srow + s * R
        ln = jnp.where(strict, -sp, 0.0) if diag else -sp
        parts = [ln.astype(BF16)]
        for _ in range(SB_PIECES - 1):
            parts.append((ln - sum(p.astype(F32) for p in parts)).astype(BF16))
        sums = _dot(jnp.concatenate(parts, axis=0), upper)
        suf = sum(sums[p * R:(p + 1) * R] for p in range(SB_PIECES))
        a = jnp.exp(z - sp + suf + run)
        if diag:
            a = jnp.where(strict, a, 0.0)
        acc = acc + _dot(a.astype(BF16), v_ref[u, pl.ds(r, T), :])
        return run + suf[:, 0:1] + ln[:, 0:1], acc

    state = []
    for n in range(len(units)):
        state.extend(block(n, qi, True, jnp.zeros((R, 1), F32), jnp.zeros((R, d), F32)))

    def earlier(i, st):
        out = []
        for n in range(len(units)):
            out.extend(block(n, qi - i, False, st[2 * n], st[2 * n + 1]))
        return tuple(out)

    state = lax.fori_loop(1, qi + 1, earlier, tuple(state))
    for n, (u, s) in enumerate(units):
        out_ref[u, s * R:(s + 1) * R, :] = state[2 * n + 1].astype(out_ref.dtype)


def _stick_breaking(proj_c, B, S):
    T = SB_BLOCK
    nq = S // T
    d = HEAD_DIM
    H = N_HEADS
    U = SB_HEADS
    return pl.pallas_call(
        _sb_kernel,
        grid=(B, H // U, nq),
        in_specs=[
            pl.BlockSpec((U, T, d), lambda b, h, i: (C_SQ // U + h, b * nq + i, 0)),
            pl.BlockSpec((U, S, d), lambda b, h, i: (C_SK // U + h, b, 0)),
            pl.BlockSpec((U, S, d), lambda b, h, i: (C_SV // U + h, b, 0)),
        ],
        out_specs=pl.BlockSpec((U, T, d), lambda b, h, i: (h, b * nq + i, 0)),
        out_shape=jax.ShapeDtypeStruct((H, B * S, d), BF16),
        compiler_params=_params(("parallel", "parallel", "parallel")),
        name="stick_breaking",
    )(proj_c, proj_c, proj_c)


def _merge_kernel(hm_ref, hd_ref, hs_ref, g0_ref, g1_ref, g2_ref, w_ref, o_ref, wb):
    H = N_HEADS

    @pl.when(pl.program_id(1) == 0)
    def _():
        wb[...] = w_ref[...].astype(BF16)

    acc = None
    ns = o_ref.shape[1] // LANES
    for n, (br, gr) in enumerate(((hm_ref, g0_ref), (hd_ref, g1_ref), (hs_ref, g2_ref))):
        a = jnp.concatenate([br[h] for h in range(H)], axis=1)
        up = _dot(a, wb[n])
        gate = jnp.concatenate([gr[s] for s in range(ns)], axis=1).astype(F32)
        term = _sigmoid(gate) * up
        acc = term if acc is None else acc + term
    o_ref[...] = acc.astype(o_ref.dtype)


def _merge(hm, hd, hs, proj_c, w_branch, layer, tm=1024, tn=512):
    H, n, d = hm.shape
    tm = min(tm, n)
    D = w_branch.shape[3]
    ns = tn // LANES
    br = pl.BlockSpec((H, tm, d), lambda j, i: (0, i, 0))

    def gate(b):
        first = (C_GATE + b * (D // LANES)) // ns
        return pl.BlockSpec((ns, tm, LANES), lambda j, i: (first + j, i, 0))

    return pl.pallas_call(
        _merge_kernel,
        grid=(D // tn, n // tm),
        in_specs=[br, br, br, gate(0), gate(1), gate(2),
                  pl.BlockSpec((None, N_BRANCHES, H * d, tn), lambda j, i: (layer, 0, 0, j))],
        out_specs=pl.BlockSpec((tm, tn), lambda j, i: (i, j)),
        out_shape=jax.ShapeDtypeStruct((n, D), BF16),
        scratch_shapes=[pltpu.VMEM((N_BRANCHES, H * d, tn), BF16)],
        compiler_params=_params(("parallel", "arbitrary")),
        name="branch_merge",
    )(hm, hd, hs, proj_c, proj_c, proj_c, w_branch)


def _mm_res_kernel(a_ref, w_ref, x_ref, o_ref, wb):
    @pl.when(pl.program_id(1) == 0)
    def _():
        wb[...] = w_ref[...].astype(BF16)

    o_ref[...] = x_ref[...] + _dot(a_ref[...], wb[...])


def _matmul_residual(a, w, layer, x, tm=1024, tn=512):
    m, k = a.shape
    tm = min(tm, m)
    n = w.shape[2]
    return pl.pallas_call(
        _mm_res_kernel,
        grid=(n // tn, m // tm),
        in_specs=[pl.BlockSpec((tm, k), lambda j, i: (i, 0)),
                  pl.BlockSpec((None, k, tn), lambda j, i: (layer, 0, j)),
                  pl.BlockSpec((tm, tn), lambda j, i: (i, j))],
        out_specs=pl.BlockSpec((tm, tn), lambda j, i: (i, j)),
        out_shape=jax.ShapeDtypeStruct((m, n), F32),
        scratch_shapes=[pltpu.VMEM((k, tn), BF16)],
        compiler_params=_params(("parallel", "arbitrary")),
        name="out_proj",
    )(a, w, x)


def _router_kernel(x_ref, g_ref, wr_ref, br_ref, xn_ref, ids_ref, wts_ref, cnt_ref, carry):
    i = pl.program_id(0)
    tm, D = x_ref.shape

    @pl.when(i == 0)
    def _():
        carry[...] = jnp.zeros(carry.shape, F32)

    x = x_ref[...]
    xn = x * lax.rsqrt(jnp.mean(x * x, axis=-1, keepdims=True) + EPS) * g_ref[...]
    xn_ref[...] = xn
    logits = jnp.dot(xn, wr_ref[...], preferred_element_type=F32, precision=lax.Precision.HIGHEST)
    biased = logits + br_ref[...]
    lane = lax.broadcasted_iota(I32, (tm, LANES), 1)
    lanef = lane.astype(F32)
    big = float(LANES)

    def first_lane(mask):
        return jnp.min(jnp.where(mask, lanef, big), axis=1, keepdims=True)

    gmask = lane < N_GROUPS
    gmax = jnp.max(jnp.where(gmask, biased, NEG), axis=1, keepdims=True)
    g_sel = first_lane(gmask & (biased == gmax))
    gm = jnp.max(jnp.where(gmask, logits, NEG), axis=1, keepdims=True)
    ge = jnp.where(gmask, jnp.exp(logits - gm), 0.0)
    p_group = jnp.sum(jnp.where(lanef == g_sel, ge, 0.0), axis=1, keepdims=True) / jnp.sum(ge, axis=1, keepdims=True)
    lo = N_GROUPS + EXPERTS_PER_GROUP * g_sel
    emask = (lanef >= lo) & (lanef < lo + EXPERTS_PER_GROUP)
    eb = jnp.where(emask, biased, NEG)
    e1 = first_lane(emask & (eb == jnp.max(eb, axis=1, keepdims=True)))
    emask2 = emask & (lanef != e1)
    eb2 = jnp.where(emask2, biased, NEG)
    e2 = first_lane(emask2 & (eb2 == jnp.max(eb2, axis=1, keepdims=True)))
    em = jnp.max(jnp.where(emask, logits, NEG), axis=1, keepdims=True)
    ee = jnp.where(emask, jnp.exp(logits - em), 0.0)
    s1 = jnp.sum(jnp.where(lanef == e1, ee, 0.0), axis=1, keepdims=True)
    s2 = jnp.sum(jnp.where(lanef == e2, ee, 0.0), axis=1, keepdims=True)
    se = jnp.sum(ee, axis=1, keepdims=True)
    w1 = s1 / se
    w2 = s2 / se
    wsum = w1 + w2
    w1 = p_group * w1 / wsum
    w2 = p_group * w2 / wsum
    x1 = e1 - N_GROUPS
    x2 = e2 - N_GROUPS

    onehot = jnp.where((lanef == x1) | (lanef == x2), 1.0, 0.0)
    r = lax.broadcasted_iota(I32, (tm, tm), 0)
    c = lax.broadcasted_iota(I32, (tm, tm), 1)
    lower = jnp.where(c < r, 1.0, 0.0).astype(BF16)
    prefix = _dot(lower, onehot.astype(BF16)) + carry[...]
    r1 = jnp.sum(jnp.where(lanef == x1, prefix, 0.0), axis=1, keepdims=True)
    r2 = jnp.sum(jnp.where(lanef == x2, prefix, 0.0), axis=1, keepdims=True)
    carry[...] = carry[...] + jnp.sum(onehot, axis=0, keepdims=True)
    cnt_ref[...] = carry[...].astype(I32)

    idsf = jnp.where(lane == 0, x1, jnp.where(lane == 1, x2, jnp.where(lane == 2, r1, jnp.where(lane == 3, r2, 0.0))))
    ids_ref[...] = idsf.astype(I32)
    wts_ref[...] = jnp.where(lane == 0, w1, jnp.where(lane == 1, w2, 0.0))


def _router(x2d, g, w_router, b_router, tm=512):
    n, D = x2d.shape
    return pl.pallas_call(
        _router_kernel,
        grid=(n // tm,),
        in_specs=[pl.BlockSpec((tm, D), lambda i: (i, 0)), pl.BlockSpec((1, D), lambda i: (0, 0)),
                  pl.BlockSpec((D, LANES), lambda i: (0, 0)), pl.BlockSpec((1, LANES), lambda i: (0, 0))],
        out_specs=[pl.BlockSpec((tm, D), lambda i: (i, 0)),
                   pl.BlockSpec((tm, LANES), lambda i: (i, 0)),
                   pl.BlockSpec((tm, LANES), lambda i: (i, 0)),
                   pl.BlockSpec((1, LANES), lambda i: (0, 0))],
        out_shape=[jax.ShapeDtypeStruct((n, D), F32),
                   jax.ShapeDtypeStruct((n, LANES), I32),
                   jax.ShapeDtypeStruct((n, LANES), F32),
                   jax.ShapeDtypeStruct((1, LANES), I32)],
        scratch_shapes=[pltpu.VMEM((1, LANES), F32)],
        compiler_params=_params(("arbitrary",)),
        name="moe_router",
    )(x2d, g.reshape(1, D), w_router, b_router)


def _dispatch_kernel(p1_ref, p2_ref, xn_ref, xs_in_ref, xs_ref, sem, *, tb):
    del xs_in_ref
    base = pl.program_id(0) * tb

    def copies(t):
        src = xn_ref.at[pl.ds(t, 1), :]
        return (pltpu.make_async_copy(src, xs_ref.at[pl.ds(p1_ref[base + t], 1), :], sem),
                pltpu.make_async_copy(src, xs_ref.at[pl.ds(p2_ref[base + t], 1), :], sem))

    def issue(t, _):
        for cp in copies(t):
            cp.start()
        return 0

    lax.fori_loop(0, tb, issue, 0, unroll=DMA_UNROLL)

    def drain(t, _):
        for cp in copies(t):
            cp.wait()
        return 0

    lax.fori_loop(0, tb, drain, 0, unroll=DMA_UNROLL)


def _dispatch(pos1, pos2, xn, xs0, tb=512):
    n, D = xn.shape
    n_rows = xs0.shape[0]
    tb = min(tb, n)
    grid_spec = pltpu.PrefetchScalarGridSpec(
        num_scalar_prefetch=2,
        grid=(n // tb,),
        in_specs=[pl.BlockSpec((tb, D), lambda i, p1, p2: (i, 0)),
                  pl.BlockSpec(memory_space=pl.ANY)],
        out_specs=pl.BlockSpec(memory_space=pl.ANY),
        scratch_shapes=[pltpu.SemaphoreType.DMA(())],
    )
    return pl.pallas_call(
        functools.partial(_dispatch_kernel, tb=tb),
        grid_spec=grid_spec,
        out_shape=jax.ShapeDtypeStruct((n_rows, D), F32),
        input_output_aliases={3: 0},
        compiler_params=pltpu.CompilerParams(dimension_semantics=("arbitrary",), has_side_effects=True),
        name="moe_dispatch",
    )(pos1, pos2, xn, xs0)


def _expert_kernel(te_ref, nv_ref, ne_ref, grp_ref, xs_ref, wg_ref, wu_ref, wd_ref, ys_ref,
                   wgf, wuf, wdf, wgb, wub, wdb, sem, *, layer):
    i = pl.program_id(0)
    prev = te_ref[jnp.maximum(i - 1, 0)]

    def fetch(e, slot):
        return (pltpu.make_async_copy(wg_ref.at[layer, e], wgf.at[slot], sem.at[slot]),
                pltpu.make_async_copy(wu_ref.at[layer, e], wuf.at[slot], sem.at[slot]),
                pltpu.make_async_copy(wd_ref.at[layer, e], wdf.at[slot], sem.at[slot]))

    @pl.when(i < nv_ref[0])
    def _():
        @pl.when(i == 0)
        def _():
            for cp in fetch(te_ref[0], 0):
                cp.start()

        @pl.when((i == 0) | (te_ref[i] != prev))
        def _():
            slot = grp_ref[i] % 2
            for cp in fetch(te_ref[i], slot):
                cp.wait()

            @pl.when(ne_ref[i] >= 0)
            def _():
                for cp in fetch(ne_ref[i], 1 - slot):
                    cp.start()

            wgb[...] = wgf[slot].astype(BF16)
            wub[...] = wuf[slot].astype(BF16)
            wdb[...] = wdf[slot].astype(BF16)

        x = xs_ref[...].astype(BF16)
        g = _dot(x, wgb[...])
        u = _dot(x, wub[...])
        hcur = (g * _sigmoid(g) * u).astype(BF16)
        ys_ref[...] = _dot(hcur, wdb[...])

    @pl.when(i >= nv_ref[0])
    def _():
        ys_ref[...] = jnp.zeros(ys_ref.shape, F32)


def _experts(tile_expert, n_valid, xs, w_gate, w_up, w_down, layer):
    n_rows = xs.shape[0]
    _, E, D, Fe = w_gate.shape
    tm = MOE_TILE
    n_tiles = n_rows // tm
    tiles = jnp.arange(n_tiles, dtype=I32)
    first = (tiles < n_valid[0]) & ((tiles == 0) | (tile_expert != jnp.roll(tile_expert, 1)))
    group = (jnp.cumsum(first.astype(I32)) - 1).astype(I32)
    later = lax.cummin(jnp.where(first, tiles, n_tiles), reverse=True)
    nxt = jnp.concatenate([later[1:], jnp.full((1,), n_tiles, I32)])
    next_expert = jnp.where(nxt < n_tiles, tile_expert[jnp.minimum(nxt, n_tiles - 1)], -1).astype(I32)

    def row_map(i, te, nv, ne, grp):
        return (jnp.minimum(i, jnp.maximum(nv[0] - 1, 0)), 0)

    grid_spec = pltpu.PrefetchScalarGridSpec(
        num_scalar_prefetch=4,
        grid=(n_tiles,),
        in_specs=[pl.BlockSpec((tm, D), row_map),
                  pl.BlockSpec(memory_space=pl.ANY), pl.BlockSpec(memory_space=pl.ANY),
                  pl.BlockSpec(memory_space=pl.ANY)],
        out_specs=pl.BlockSpec((tm, D), lambda i, te, nv, ne, grp: (i, 0)),
        scratch_shapes=[pltpu.VMEM((2, D, Fe), F32), pltpu.VMEM((2, D, Fe), F32), pltpu.VMEM((2, Fe, D), F32),
                        pltpu.VMEM((D, Fe), BF16), pltpu.VMEM((D, Fe), BF16), pltpu.VMEM((Fe, D), BF16),
                        pltpu.SemaphoreType.DMA((2,))],
    )
    return pl.pallas_call(
        functools.partial(_expert_kernel, layer=layer),
        grid_spec=grid_spec,
        out_shape=jax.ShapeDtypeStruct((n_rows, D), F32),
        compiler_params=_params(("arbitrary",)),
        name="moe_experts",
    )(tile_expert, n_valid, next_expert, group, xs, w_gate, w_up, w_down)


def _combine_kernel(p1_ref, p2_ref, ys_ref, x_ref, w_ref, o_ref, buf, sem):
    i = pl.program_id(0)
    tc = x_ref.shape[0]
    slot = i % 2

    def copies(tile, slot, t):
        tok = tile * tc + t
        return (pltpu.make_async_copy(ys_ref.at[pl.ds(p1_ref[tok], 1), :], buf.at[slot, 0, pl.ds(t, 1), :],
                                      sem.at[slot]),
                pltpu.make_async_copy(ys_ref.at[pl.ds(p2_ref[tok], 1), :], buf.at[slot, 1, pl.ds(t, 1), :],
                                      sem.at[slot]))

    def issue(tile, slot):
        def body(t, _):
            for cp in copies(tile, slot, t):
                cp.start()
            return 0

        lax.fori_loop(0, tc, body, 0, unroll=DMA_UNROLL)

    @pl.when(i == 0)
    def _():
        issue(0, 0)

    @pl.when(i + 1 < pl.num_programs(0))
    def _():
        issue(i + 1, 1 - slot)

    def drain(t, _):
        for cp in copies(i, slot, t):
            cp.wait()
        return 0

    lax.fori_loop(0, tc, drain, 0, unroll=DMA_UNROLL)
    w = w_ref[...]
    o_ref[...] = x_ref[...] + w[:, 0:1] * buf[slot, 0] + w[:, 1:2] * buf[slot, 1]


def _combine(pos1, pos2, ys, x2d, wts, tc=256):
    n, D = x2d.shape
    grid_spec = pltpu.PrefetchScalarGridSpec(
        num_scalar_prefetch=2,
        grid=(n // tc,),
        in_specs=[pl.BlockSpec(memory_space=pl.ANY),
                  pl.BlockSpec((tc, D), lambda i, p1, p2: (i, 0)),
                  pl.BlockSpec((tc, LANES), lambda i, p1, p2: (i, 0))],
        out_specs=pl.BlockSpec((tc, D), lambda i, p1, p2: (i, 0)),
        scratch_shapes=[pltpu.VMEM((2, 2, tc, D), F32), pltpu.SemaphoreType.DMA((2,))],
    )
    return pl.pallas_call(
        _combine_kernel,
        grid_spec=grid_spec,
        out_shape=jax.ShapeDtypeStruct((n, D), F32),
        compiler_params=_params(("arbitrary",)),
        name="moe_combine",
    )(pos1, pos2, ys, x2d, wts)


def _pos_kernel(ids_ref, cnt_ref, pos_ref):
    tm = ids_ref.shape[0]
    tiles = jnp.floor((cnt_ref[...].astype(F32) + (MOE_TILE - 1)) / MOE_TILE)
    r = lax.broadcasted_iota(I32, (LANES, LANES), 0)
    c = lax.broadcasted_iota(I32, (LANES, LANES), 1)
    before = jnp.where(r < c, 1.0, 0.0).astype(BF16)
    first_tile = _dot(jnp.broadcast_to(tiles, (8, LANES)).astype(BF16), before)
    offs = first_tile[0:1, :] * MOE_TILE
    ids = ids_ref[...].astype(F32)
    lane = lax.broadcasted_iota(I32, (tm, LANES), 1)
    lanef = lane.astype(F32)
    p1 = jnp.sum(jnp.where(lanef == ids[:, 0:1], offs, 0.0), axis=1, keepdims=True) + ids[:, 2:3]
    p2 = jnp.sum(jnp.where(lanef == ids[:, 1:2], offs, 0.0), axis=1, keepdims=True) + ids[:, 3:4]
    packed = jnp.where(lane == 0, p1, jnp.where(lane == 1, p2, 0.0))
    pos_ref[...] = packed.T[0:8, :].astype(I32)


def _positions(ids, counts, tm=512):
    n = ids.shape[0]
    assert MOE_TILE & (MOE_TILE - 1) == 0
    return pl.pallas_call(
        _pos_kernel,
        grid=(n // tm,),
        in_specs=[pl.BlockSpec((tm, LANES), lambda i: (i, 0)), pl.BlockSpec((1, LANES), lambda i: (0, 0))],
        out_specs=pl.BlockSpec((8, tm), lambda i: (0, i)),
        out_shape=jax.ShapeDtypeStruct((8, n), I32),
        compiler_params=_params(("parallel",)),
        name="moe_positions",
    )(ids, counts)


def _hier_moe(x2d, norm_g, w_rg, b_rg, w_re, b_re, w_gate, w_up, w_down, layer, xs_buf):
    n, D = x2d.shape
    tm = MOE_TILE
    pad = LANES - N_GROUPS - N_EXPERTS
    w_router = jnp.concatenate([w_rg, w_re, jnp.zeros((D, pad), F32)], axis=1)
    b_router = jnp.concatenate([b_rg, b_re.reshape(-1), jnp.zeros((pad,), F32)]).reshape(1, LANES)
    xn3, ids, wts, counts = _router(x2d, norm_g, w_router, b_router)
    cnt = counts[0, :N_EXPERTS]
    padded = ((cnt + tm - 1) // tm) * tm
    ends = jnp.cumsum(padded)
    n_tiles = xs_buf.shape[0] // tm
    tile_start = jnp.arange(n_tiles, dtype=I32) * tm
    tile_expert = jnp.minimum(jnp.sum(tile_start[:, None] >= ends[None, :], axis=1), N_EXPERTS - 1).astype(I32)
    n_valid = (ends[-1] // tm).astype(I32).reshape(1)
    last_e = tile_expert[jnp.maximum(n_valid[0] - 1, 0)]
    tile_expert = jnp.where(jnp.arange(n_tiles) < n_valid[0], tile_expert, last_e)
    pos = _positions(ids, counts)
    pos1, pos2 = pos[0], pos[1]
    xs = _dispatch(pos1, pos2, xn3, xs_buf)
    ys = _experts(tile_expert, n_valid, xs, w_gate, w_up, w_down, layer)
    return _combine(pos1, pos2, ys, x2d, wts), xs


def _in_proj_regions(D):
    bw = BRANCH_WIDTH
    sizes = (bw, bw, bw, bw, N_HEADS, N_HEADS, bw, bw, bw, IDX_HEADS * IDX_DIM, IDX_DIM, IDX_HEADS,
             bw, bw, bw, N_BRANCHES * D)
    offs = np.concatenate([[0], np.cumsum(sizes)]).tolist()
    region_a = (offs[0], offs[4] - offs[0])
    region_b = (offs[6], offs[10] - offs[6])
    region_c = (offs[12], offs[16] - offs[12])
    small = dict(mi=offs[4], mf=offs[5], ik=offs[10], iw=offs[11])
    return region_a, region_b, region_c, small


def _token_mixer(x2d, B, S, layer, norm_g, w_in, conv_w, b_i, b_f, mlstm_norm_g, q_norm_g, k_norm_g,
                 w_branch, w_out, rel_bias):
    n, D = x2d.shape
    L = MLSTM_CHUNK
    H = N_HEADS
    tm = min(n, PROJ_TM)
    xn = _rmsnorm(x2d, norm_g)
    ra, rb, rc, small = _in_proj_regions(D)
    w_t = jnp.swapaxes(w_in, 1, 2)
    proj_a = _in_proj(xn, w_t, layer, ra[0], ra[1], tm)
    proj_b = _in_proj(xn, w_t, layer, rb[0], rb[1], tm)
    proj_c = _in_proj(xn, w_t, layer, rc[0], rc[1], tm)
    tail = _tail_proj(xn, w_t, layer, small, tm)
    g = tail[1][:, IDX_HEADS:IDX_HEADS + 2 * H].reshape(B, S // L, L, 2, H)
    gates_t = jnp.transpose(g, (0, 4, 1, 3, 2))
    hm = _mlstm(proj_a, gates_t, jnp.stack([b_i, b_f]), conv_w, mlstm_norm_g, B, S)
    hd = _dsa(proj_b, tail, q_norm_g, k_norm_g, rel_bias, B, S)
    hs = _stick_breaking(proj_c, B, S)
    merged = _merge(hm, hd, hs, proj_c, w_branch, layer)
    return _matmul_residual(merged, w_out, layer, x2d)


def kernel(x, norm1_g, w_in, conv_w, b_i, b_f, mlstm_norm_g, q_norm_g, k_norm_g, w_branch, w_out, norm2_g,
           w_router_g, b_router_g, w_router_e, b_router_e, w_gate, w_up, w_down, rel_bias):
    B, S, D = x.shape
    x2d = x.reshape(B * S, D)
    xs_buf = jnp.zeros((2 * B * S + N_EXPERTS * MOE_TILE, D), F32)
    for l in range(w_in.shape[0]):
        x2d = _token_mixer(x2d, B, S, l, norm1_g[l], w_in, conv_w[l], b_i[l], b_f[l], mlstm_norm_g[l],
                           q_norm_g[l], k_norm_g[l], w_branch, w_out, rel_bias)
        x2d, xs_buf = _hier_moe(x2d, norm2_g[l], w_router_g[l], b_router_g[l], w_router_e[l], b_router_e[l],
                                w_gate, w_up, w_down, l, xs_buf)
    return x2d.reshape(B, S, D)
```

```python
import functools
import math

import numpy as np
import jax
import jax.numpy as jnp
from jax import lax
from jax.experimental import pallas as pl
from jax.experimental.pallas import tpu as pltpu

F32 = jnp.float32
BF16 = jnp.bfloat16
I32 = jnp.int32

LANES = 128
HEAD_DIM = 128
N_HEADS = 8
BRANCH_WIDTH = N_HEADS * HEAD_DIM
N_BRANCHES = 3
CONV_WIDTH = 4
IDX_HEADS = 16
IDX_DIM = 64
TOPK_MAX = 256
N_BUCKETS = 32
MAX_DISTANCE = 128
N_GROUPS = 4
EXPERTS_PER_GROUP = 8
N_EXPERTS = N_GROUPS * EXPERTS_PER_GROUP
EPS = 1e-6
NEG = -1e30
INT_MIN = -(2 ** 31)

MLSTM_CHUNK = 256
DSA_BLOCK = 128
DSA_BANDS = 4
DSA_QBLOCKS = 2
SB_BLOCK = 512
SB_ROWS = 512
SB_PIECES = 1
SB_HEADS = 4
HEADS_PER_STEP = 4
MOE_TILE = 256
WEIGHT_CHUNKS = 4
DMA_UNROLL = 8
PROJ_TN = 512
PROJ_TM = 2048
VMEM_LIMIT = 56 * 1024 * 1024

A_MQ, A_MK, A_MV, A_MO = 0, 8, 16, 24
B_DQ, B_DK, B_DV, B_IQ = 0, 8, 16, 24
C_SQ, C_SK, C_SV, C_GATE = 0, 8, 16, 24


def _params(sem):
    return pltpu.CompilerParams(dimension_semantics=sem, vmem_limit_bytes=VMEM_LIMIT)


def _dot(a, b):
    return jnp.dot(a, b, preferred_element_type=F32)


def _dot_nt(a, b):
    return lax.dot_general(a, b, (((1,), (1,)), ((), ())), preferred_element_type=F32)


def _sigmoid(z):
    return 1.0 / (1.0 + jnp.exp(-z))


def _rmsnorm_kernel(x_ref, g_ref, o_ref):
    x = x_ref[...]
    ms = jnp.mean(x * x, axis=-1, keepdims=True)
    o_ref[...] = (x * lax.rsqrt(ms + EPS) * g_ref[...]).astype(o_ref.dtype)


def _rmsnorm(x2d, g, tm=512):
    n, d = x2d.shape
    return pl.pallas_call(
        _rmsnorm_kernel,
        grid=(n // tm,),
        in_specs=[pl.BlockSpec((tm, d), lambda i: (i, 0)), pl.BlockSpec((1, d), lambda i: (0, 0))],
        out_specs=pl.BlockSpec((tm, d), lambda i: (i, 0)),
        out_shape=jax.ShapeDtypeStruct((n, d), BF16),
        compiler_params=_params(("parallel",)),
        name="rmsnorm",
    )(x2d, g.reshape(1, d))


def _in_proj_kernel(x_ref, *rest, shift, nblk):
    w_refs, o_ref, wb = rest[:nblk], rest[nblk], rest[nblk + 1]
    tn = wb.shape[1]

    @pl.when(pl.program_id(1) == 0)
    def _():
        w = jnp.concatenate([r[...] for r in w_refs], axis=0)
        wb[...] = w[shift:shift + tn, :].T.astype(BF16)

    acc = _dot(x_ref[...], wb[...])
    for j in range(o_ref.shape[0]):
        o_ref[j] = acc[:, j * LANES:(j + 1) * LANES].astype(o_ref.dtype)


def _in_proj(xn, w_t, layer, col0, ncols, tm):
    m, k = xn.shape
    tn = PROJ_TN
    base, shift = divmod(col0, LANES)
    assert shift % 8 == 0
    nblk = tn // LANES + (1 if shift else 0)
    per = tn // LANES

    def wspec(r):
        return pl.BlockSpec((None, LANES, k), lambda j, i: (layer, base + per * j + r, 0))

    return pl.pallas_call(
        functools.partial(_in_proj_kernel, shift=shift, nblk=nblk),
        grid=(ncols // tn, m // tm),
        in_specs=[pl.BlockSpec((tm, k), lambda j, i: (i, 0))] + [wspec(r) for r in range(nblk)],
        out_specs=pl.BlockSpec((per, tm, LANES), lambda j, i: (j, i, 0)),
        out_shape=jax.ShapeDtypeStruct((ncols // LANES, m, LANES), BF16),
        scratch_shapes=[pltpu.VMEM((k, tn), BF16)],
        compiler_params=_params(("parallel", "arbitrary")),
        name="in_proj",
    )(xn, *([w_t] * nblk))


def _tail_kernel(x_ref, wg_ref, wi_ref, o_ref, wt, *, g_lane, ik_lane, iw_lane):
    @pl.when(pl.program_id(0) == 0)
    def _():
        wg = wg_ref[...]
        wi = wi_ref[...]
        ik = wi[ik_lane:ik_lane + IDX_DIM, :]
        iw = wi[iw_lane:iw_lane + IDX_HEADS, :]
        gates = wg[g_lane:g_lane + 2 * N_HEADS, :]
        pad = jnp.zeros((LANES - IDX_HEADS - 2 * N_HEADS, wg.shape[1]), F32)
        wt[...] = jnp.concatenate([ik, ik, iw, gates, pad], axis=0).T.astype(BF16)

    acc = _dot(x_ref[...], wt[...])
    o_ref[0] = acc[:, :LANES]
    o_ref[1] = acc[:, LANES:]


def _tail_proj(xn, w_t, layer, small, tm):
    m, k = xn.shape
    g_blk, g_lane = divmod(small["mi"], LANES)
    i_blk, ik_lane = divmod(small["ik"], LANES)
    iw_lane = small["iw"] - i_blk * LANES
    assert small["mf"] == small["mi"] + N_HEADS and g_lane + 2 * N_HEADS <= LANES
    assert ik_lane + IDX_DIM <= LANES and 0 <= iw_lane and iw_lane + IDX_HEADS <= LANES
    assert g_lane % 8 == 0 and ik_lane % 8 == 0 and iw_lane % 8 == 0
    return pl.pallas_call(
        functools.partial(_tail_kernel, g_lane=g_lane, ik_lane=ik_lane, iw_lane=iw_lane),
        grid=(m // tm,),
        in_specs=[pl.BlockSpec((tm, k), lambda i: (i, 0)),
                  pl.BlockSpec((None, LANES, k), lambda i: (layer, g_blk, 0)),
                  pl.BlockSpec((None, LANES, k), lambda i: (layer, i_blk, 0))],
        out_specs=pl.BlockSpec((2, tm, LANES), lambda i: (0, i, 0)),
        out_shape=jax.ShapeDtypeStruct((2, m, LANES), F32),
        scratch_shapes=[pltpu.VMEM((k, 2 * LANES), BF16)],
        compiler_params=_params(("arbitrary",)),
        name="tail_proj",
    )(xn, w_t, w_t)


def _mlstm_kernel(bias_ref, q_ref, k_ref, v_ref, o_ref, g_ref, cwq_ref, cwk_ref, ng_ref, out_ref,
                  qf, kf, qc, kc, st):
    hp = pl.program_id(1)
    U = HEADS_PER_STEP
    S = q_ref.shape[1]
    L = MLSTM_CHUNK
    nc = S // L
    d = HEAD_DIM
    PAD = 8

    R = min(S, 256)
    for u in range(U):
        qf[u, 0:PAD, :] = jnp.zeros((PAD, d), F32)
        kf[u, 0:PAD, :] = jnp.zeros((PAD, d), F32)
        qf[u, PAD:PAD + S, :] = q_ref[u].astype(F32)
        kf[u, PAD:PAD + S, :] = k_ref[u].astype(F32)
        ls = slice(u * d, (u + 1) * d)
        for r0 in range(0, S, R):
            aq = jnp.zeros((R, d), F32)
            ak = jnp.zeros((R, d), F32)
            for t in range(CONV_WIDTH):
                off = PAD - (CONV_WIDTH - 1) + t + r0
                aq = aq + cwq_ref[t:t + 1, ls] * qf[u, off:off + R, :]
                ak = ak + cwk_ref[t:t + 1, ls] * kf[u, off:off + R, :]
            qc[u, r0:r0 + R, :] = (aq * _sigmoid(aq) * (d ** -0.5)).astype(BF16)
            kc[u, r0:r0 + R, :] = ak * _sigmoid(ak)

    st[...] = jnp.zeros(st.shape, F32)
    row = lax.broadcasted_iota(I32, (L, L), 0)
    col = lax.broadcasted_iota(I32, (L, L), 1)
    causal = col <= row
    eye = col == row
    lane = lax.broadcasted_iota(I32, (L, d), 1)
    ones_col = jnp.where(lane == 0, 1.0, 0.0).astype(BF16)
    ng = ng_ref[...]

    def chunk_one(u, c, r, m):
        q = qc[u, pl.ds(r, L), :]
        kT = kc[u, pl.ds(r, L), :].T
        v = v_ref[u, pl.ds(r, L), :]
        vaug = jnp.concatenate([v, ones_col], axis=1)
        gates = g_ref[u, c]
        i_row = gates[0:1, :] + bias_ref[0, hp * U + u]
        f_row = gates[1:2, :] + bias_ref[1, hp * U + u]
        lf_row = jnp.minimum(f_row, 0.0) - jnp.log1p(jnp.exp(-jnp.abs(f_row)))
        b_col = jnp.sum(jnp.where(causal, lf_row, 0.0), axis=1, keepdims=True)
        b_row = jnp.sum(jnp.where(eye, b_col, 0.0), axis=0, keepdims=True)
        dlog = jnp.where(causal, b_col - b_row + i_row, NEG)
        inter = b_col + m
        m_t = jnp.maximum(inter, jnp.max(dlog, axis=1, keepdims=True))
        w_intra = jnp.exp(dlog - m_t)
        w_inter = jnp.exp(inter - m_t)
        s = _dot(q, kT.astype(BF16)) * w_intra
        res = w_inter * _dot(q, st[u].astype(BF16)) + _dot(s.astype(BF16), vaug)
        num = res[:, :d]
        den = res[:, d:d + 1]
        hh = num / jnp.maximum(jnp.abs(den), jnp.exp(-m_t))
        hn = hh * lax.rsqrt(jnp.mean(hh * hh, axis=1, keepdims=True) + EPS) * ng[:, u * d:(u + 1) * d]
        og = o_ref[u, pl.ds(r, L), :].astype(F32)
        out_ref[u, pl.ds(r, L), :] = (hn * _sigmoid(og)).astype(out_ref.dtype)
        ws_row = w_intra[L - 1:L, :]
        decay = w_inter[L - 1:L, :]
        st[u] = decay * st[u] + _dot((kT * ws_row).astype(BF16), vaug)
        return m_t[L - 1:L, :]

    def chunk(c, ms):
        r = pl.multiple_of(c * L, L)
        return tuple(chunk_one(u, c, r, ms[u]) for u in range(U))

    lax.fori_loop(0, nc, chunk, tuple(jnp.zeros((1, 1), F32) for _ in range(U)))


def _mlstm(proj_a, gates_t, bias_if, conv_w, norm_g, B, S):
    L = MLSTM_CHUNK
    nc = S // L
    d = HEAD_DIM
    H = N_HEADS
    U = HEADS_PER_STEP

    def slab(off):
        return pl.BlockSpec((U, S, d), lambda b, h: (off // U + h, b, 0))

    return pl.pallas_call(
        _mlstm_kernel,
        grid=(B, H // U),
        in_specs=[
            pl.BlockSpec(memory_space=pltpu.SMEM),
            slab(A_MQ), slab(A_MK), slab(A_MV), slab(A_MO),
            pl.BlockSpec((None, U, nc, 2, L), lambda b, h: (b, h, 0, 0, 0)),
            pl.BlockSpec((CONV_WIDTH, U * d), lambda b, h: (0, h)),
            pl.BlockSpec((CONV_WIDTH, U * d), lambda b, h: (0, H // U + h)),
            pl.BlockSpec((1, U * d), lambda b, h: (0, h)),
        ],
        out_specs=pl.BlockSpec((U, S, d), lambda b, h: (h, b, 0)),
        out_shape=jax.ShapeDtypeStruct((H, B * S, d), BF16),
        scratch_shapes=[
            pltpu.VMEM((U, S + 8, d), F32), pltpu.VMEM((U, S + 8, d), F32),
            pltpu.VMEM((U, S, d), BF16), pltpu.VMEM((U, S, d), F32),
            pltpu.VMEM((U, d, 2 * d), F32),
        ],
        compiler_params=_params(("parallel", "parallel")),
        name="mlstm",
    )(bias_if, proj_a, proj_a, proj_a, proj_a, gates_t, conv_w, conv_w, norm_g.reshape(1, H * d))


def _t5_thresholds():
    max_exact = N_BUCKETS // 2
    n = np.arange(0, 2 * MAX_DISTANCE)
    nf = np.maximum(n, 1).astype(np.float64)
    val = np.log(nf / max_exact) / math.log(MAX_DISTANCE / max_exact) * (N_BUCKETS - max_exact)
    frac = np.abs(val - np.round(val))
    frac_ok = (frac > 1e-4) | (n <= max_exact) | (n >= MAX_DISTANCE)
    assert frac_ok.all()
    large = np.minimum(max_exact + np.trunc(val).astype(np.int64), N_BUCKETS - 1)
    bucket = np.where(n < max_exact, n, large)
    assert (np.diff(bucket) >= 0).all() and bucket[MAX_DISTANCE] == N_BUCKETS - 1
    return [int(np.argmax(bucket >= j)) for j in range(1, N_BUCKETS)]


_T5_THR = _t5_thresholds()


def _dsa_kernel(rb_ref, q_ref, k_ref, v_ref, iq_ref, ik_ref, wt_ref, gq_ref, gk_ref, prev_ref, out_ref,
                kn, ikk, wb, key_ref, mask_ref, bias_ref, *, topk, q0, nb, nqb):
    del prev_ref
    qis = [q0 + pl.program_id(1) * nqb + qb for qb in range(nqb)]
    T = DSA_BLOCK
    d = HEAD_DIM
    H = N_HEADS
    HALF = T // 2
    row = lax.broadcasted_iota(I32, (T, T), 0)
    col = lax.broadcasted_iota(I32, (T, T), 1)

    @pl.when(pl.program_id(1) == 0)
    def _per_batch():
        gk = gk_ref[...]

        def norm_k(h, _):
            kh = k_ref[h].astype(F32)
            kn[h] = (kh * lax.rsqrt(jnp.mean(kh * kh, axis=1, keepdims=True) + EPS) * gk).astype(BF16)
            for o in range(2):
                n = o * T + row - col
                val = jnp.full((T, T), rb_ref[0, h], F32)
                for j, thr in enumerate(_T5_THR):
                    val = jnp.where(n >= thr, rb_ref[j + 1, h], val)
                bias_ref[h, o] = val
            bias_ref[h, 2] = jnp.full((T, T), rb_ref[N_BUCKETS - 1, h], F32)
            return 0

        lax.fori_loop(0, H, norm_k, 0)
        ik = ik_ref[...]
        klane = lax.broadcasted_iota(I32, ik.shape, 1)
        ikk[0] = jnp.where(klane < IDX_DIM, ik, 0.0).astype(BF16)
        ikk[1] = jnp.where(klane >= IDX_DIM, ik, 0.0).astype(BF16)

    wsc = wt_ref[...] * (IDX_HEADS ** -0.5 * IDX_DIM ** -0.5)
    for qb in range(nqb):
        for h16 in range(IDX_HEADS):
            wb[qb * IDX_HEADS + h16] = jnp.broadcast_to(wsc[qb * T:(qb + 1) * T, h16:h16 + 1], (T, LANES))
    q_pairs = iq_ref[...].reshape((IDX_HEADS // 2) * nqb * T, LANES)

    def score_block(j, _):
        r = pl.multiple_of(j * T, T)
        kk = jnp.concatenate([ikk[0, pl.ds(r, T), :], ikk[1, pl.ds(r, T), :]], axis=0)
        dots = _dot_nt(q_pairs, kk)
        for qb in range(nqb):
            sc = jnp.zeros((T, T), F32)
            for h16 in range(IDX_HEADS):
                hp, odd = divmod(h16, 2)
                r0 = (hp * nqb + qb) * T
                sc = sc + wb[qb * IDX_HEADS + h16] * jnp.maximum(dots[r0:r0 + T, odd * T:(odd + 1) * T], 0.0)
            bits = lax.bitcast_convert_type(sc, I32)
            key = jnp.where(bits < 0, bits ^ jnp.int32(0x7FFFFFFF), bits)
            key = jnp.where(sc == 0.0, 0, key)
            key = jnp.where(j * T + col <= qis[qb] * T + row, key, INT_MIN)
            key_ref[qb, j] = key
        return 0

    lax.fori_loop(0, nb, score_block, 0, unroll=2)

    kf = float(topk)
    ones = jnp.ones((LANES, LANES), BF16)

    def count(chain, pred):
        qb, lo = chain
        acc = jnp.zeros((HALF, LANES), F32)
        for jb in range(nb):
            acc = acc + jnp.where(pred(key_ref[qb, jb, lo:lo + HALF, :]), 1.0, 0.0)
        return _dot(acc.astype(BF16), ones)

    chains = [(qb, lo) for qb in range(nqb) for lo in (0, HALF)]
    zero_i = jnp.zeros((HALF, LANES), I32)
    thr0 = tuple(jnp.where(count(ch, lambda k: k >= zero_i) >= kf, jnp.int32(0), jnp.int32(INT_MIN))
                 for ch in chains)

    def bisect_bit(thrs, bit):
        out = []
        for ch, t in zip(chains, thrs):
            cand = t | bit
            out.append(jnp.where(count(ch, lambda k, c=cand: k >= c) >= kf, cand, t))
        return tuple(out)

    def bisect_pair(it, thrs):
        hi_bit = jnp.left_shift(jnp.int32(1), 29 - 2 * it)
        lo_bit = jnp.left_shift(jnp.int32(1), 28 - 2 * it)
        out = []
        for ch, t in zip(chains, thrs):
            c01, c10, c11 = t | lo_bit, t | hi_bit, t | hi_bit | lo_bit
            n01, n10, n11 = (count(ch, lambda k, c=c: k >= c) >= kf for c in (c01, c10, c11))
            out.append(jnp.where(n11, c11, jnp.where(n10, c10, jnp.where(n01, c01, t))))
        return tuple(out)

    thrs = bisect_bit(thr0, jnp.int32(1 << 30))
    thrs = lax.fori_loop(0, 15, bisect_pair, thrs)
    needs = [kf - count(ch, lambda k, t=t: k > t) for ch, t in zip(chains, thrs)]

    tri = jnp.where(row <= col, 1.0, 0.0).astype(BF16)
    for qb in range(nqb):
        need = jnp.concatenate(needs[2 * qb:2 * qb + 2], axis=0)
        thrb = jnp.concatenate(thrs[2 * qb:2 * qb + 2], axis=0)
        seen = jnp.zeros((T, LANES), F32)
        for jb in range(nb):
            key = key_ref[qb, jb]
            tie = key == thrb
            tie16 = jnp.where(tie, 1.0, 0.0).astype(BF16)
            pre = _dot(tie16, tri) + seen
            m = jnp.where(key > thrb, 0.0, jnp.where(tie, jnp.where(pre <= need, 0.0, NEG), NEG))
            mask_ref[qb, jb] = jnp.where(key == INT_MIN, NEG, m)
            seen = seen + _dot(tie16, ones)

    gq = gq_ref[...]

    def head(h):
        qh = q_ref[h].astype(F32)
        qn = (qh * lax.rsqrt(jnp.mean(qh * qh, axis=1, keepdims=True) + EPS) * gq * (d ** -0.5)).astype(BF16)
        lgs = _dot_nt(qn, kn[h])
        p_rows, sums = [], []
        for qb in range(nqb):
            mx = None
            lg_blocks = []
            for jb in range(nb):
                lg = (lgs[qb * T:(qb + 1) * T, jb * T:(jb + 1) * T]
                      + bias_ref[h, jnp.clip(qis[qb] - jb, 0, 2)] + mask_ref[qb, jb])
                lg_blocks.append(lg)
                mx = lg if mx is None else jnp.maximum(mx, lg)
            rowmax = jnp.max(mx, axis=1, keepdims=True)
            l = jnp.zeros((T, T), F32)
            ps = []
            for jb in range(nb):
                p = jnp.exp(lg_blocks[jb] - rowmax)
                l = l + p
                ps.append(p.astype(BF16))
            p_rows.append(jnp.concatenate(ps, axis=1))
            sums.append(jnp.sum(l, axis=1, keepdims=True))
        acc = _dot(jnp.concatenate(p_rows, axis=0), v_ref[h])
        out_ref[h] = (acc / jnp.concatenate(sums, axis=0)).astype(out_ref.dtype)

    def head_group(g, _):
        for u in range(HEADS_PER_STEP):
            head(g * HEADS_PER_STEP + u)
        return 0

    lax.fori_loop(0, H // HEADS_PER_STEP, head_group, 0)


def _dsa(proj_b, tail, q_norm_g, k_norm_g, rel_bias, B, S):
    T = DSA_BLOCK
    nq = S // T
    d = HEAD_DIM
    H = N_HEADS
    topk = min(TOPK_MAX, S // 4)
    per = nq // DSA_BANDS
    nqb = math.gcd(per, DSA_QBLOCKS)
    TQ = nqb * T
    pb = proj_b.reshape(proj_b.shape[0], B, S, d)
    tl = tail.reshape(tail.shape[0], B, S, LANES)
    out = jnp.zeros((H, B, S, d), BF16)
    for band in range(DSA_BANDS):
        q0 = band * per
        nb = q0 + per
        W = nb * T
        in_specs = [
            pl.BlockSpec(memory_space=pltpu.SMEM),
            pl.BlockSpec((H, None, TQ, d), lambda b, i: (B_DQ // H, b, q0 // nqb + i, 0)),
            pl.BlockSpec((H, None, W, d), lambda b, i: (B_DK // H, b, 0, 0)),
            pl.BlockSpec((H, None, W, d), lambda b, i: (B_DV // H, b, 0, 0)),
            pl.BlockSpec((H, None, TQ, d), lambda b, i: (B_IQ // H, b, q0 // nqb + i, 0)),
            pl.BlockSpec((None, None, W, LANES), lambda b, i: (0, b, 0, 0)),
            pl.BlockSpec((None, None, TQ, LANES), lambda b, i: (1, b, q0 // nqb + i, 0)),
            pl.BlockSpec((1, d), lambda b, i: (0, 0)),
            pl.BlockSpec((1, d), lambda b, i: (0, 0)),
            pl.BlockSpec(memory_space=pl.ANY),
        ]
        args = [rel_bias, pb, pb, pb, pb, tl, tl, q_norm_g.reshape(1, d), k_norm_g.reshape(1, d), out]
        out = pl.pallas_call(
            functools.partial(_dsa_kernel, topk=topk, q0=q0, nb=nb, nqb=nqb),
            grid=(B, per // nqb),
            in_specs=in_specs,
            out_specs=pl.BlockSpec((H, None, TQ, d), lambda b, i: (0, b, q0 // nqb + i, 0)),
            out_shape=jax.ShapeDtypeStruct((H, B, S, d), BF16),
            scratch_shapes=[
                pltpu.VMEM((H, W, d), BF16),
                pltpu.VMEM((2, W, LANES), BF16),
                pltpu.VMEM((nqb * IDX_HEADS, T, LANES), F32),
                pltpu.VMEM((nqb, nb, T, T), I32),
                pltpu.VMEM((nqb, nb, T, T), F32),
                pltpu.VMEM((H, 3, T, T), F32),
            ],
            input_output_aliases={len(args) - 1: 0},
            compiler_params=_params(("parallel", "arbitrary")),
            name=f"dsa_band{band}",
        )(*args)
    return out.reshape(H, B * S, d)


def _sb_kernel(q_ref, k_ref, v_ref, out_ref):
    qi = pl.program_id(2)
    U = SB_HEADS
    T = SB_BLOCK
    d = HEAD_DIM
    R = SB_ROWS
    ns = T // R
    row = lax.broadcasted_iota(I32, (T, T), 0)
    col = lax.broadcasted_iota(I32, (T, T), 1)
    upper = jnp.where(row > col, 1.0, 0.0).astype(BF16)
    srow = lax.broadcasted_iota(I32, (R, T), 0)
    scol = lax.broadcasted_iota(I32, (R, T), 1)
    units = [(u, s) for u in range(U) for s in range(ns)]
    qs = [(q_ref[u, s * R:(s + 1) * R, :].astype(F32) * (d ** -0.5)).astype(BF16) for u, s in units]

    def block(n, j, diag, run, acc):
        u, s = units[n]
        r = pl.multiple_of(j * T, T)
        z = _dot_nt(qs[n], k_ref[u, pl.ds(r, T), :])
        sp = jnp.maximum(z, 0.0) + jnp.log(1.0 + jnp.exp(-jnp.abs(z)))
        strict = scol < srow + s * R
        ln = jnp.where(strict, -sp, 0.0) if diag else -sp
        parts = [ln.astype(BF16)]
        for _ in range(SB_PIECES - 1):
            parts.append((ln - sum(p.astype(F32) for p in parts)).astype(BF16))
        sums = _dot(jnp.concatenate(parts, axis=0), upper)
        suf = sum(sums[p * R:(p + 1) * R] for p in range(SB_PIECES))
        a = jnp.exp(z - sp + suf + run)
        if diag:
            a = jnp.where(strict, a, 0.0)
        acc = acc + _dot(a.astype(BF16), v_ref[u, pl.ds(r, T), :])
        return run + suf[:, 0:1] + ln[:, 0:1], acc

    state = []
    for n in range(len(units)):
        state.extend(block(n, qi, True, jnp.zeros((R, 1), F32), jnp.zeros((R, d), F32)))

    def earlier(i, st):
        out = []
        for n in range(len(units)):
            out.extend(block(n, qi - i, False, st[2 * n], st[2 * n + 1]))
        return tuple(out)

    state = lax.fori_loop(1, qi + 1, earlier, tuple(state))
    for n, (u, s) in enumerate(units):
        out_ref[u, s * R:(s + 1) * R, :] = state[2 * n + 1].astype(out_ref.dtype)


def _stick_breaking(proj_c, B, S):
    T = SB_BLOCK
    nq = S // T
    d = HEAD_DIM
    H = N_HEADS
    U = SB_HEADS
    return pl.pallas_call(
        _sb_kernel,
        grid=(B, H // U, nq),
        in_specs=[
            pl.BlockSpec((U, T, d), lambda b, h, i: (C_SQ // U + h, b * nq + i, 0)),
            pl.BlockSpec((U, S, d), lambda b, h, i: (C_SK // U + h, b, 0)),
            pl.BlockSpec((U, S, d), lambda b, h, i: (C_SV // U + h, b, 0)),
        ],
        out_specs=pl.BlockSpec((U, T, d), lambda b, h, i: (h, b * nq + i, 0)),
        out_shape=jax.ShapeDtypeStruct((H, B * S, d), BF16),
        compiler_params=_params(("parallel", "parallel", "parallel")),
        name="stick_breaking",
    )(proj_c, proj_c, proj_c)


def _merge_kernel(hm_ref, hd_ref, hs_ref, g0_ref, g1_ref, g2_ref, w_ref, o_ref, wb):
    H = N_HEADS

    @pl.when(pl.program_id(1) == 0)
    def _():
        wb[...] = w_ref[...].astype(BF16)

    acc = None
    ns = o_ref.shape[1] // LANES
    for n, (br, gr) in enumerate(((hm_ref, g0_ref), (hd_ref, g1_ref), (hs_ref, g2_ref))):
        a = jnp.concatenate([br[h] for h in range(H)], axis=1)
        up = _dot(a, wb[n])
        gate = jnp.concatenate([gr[s] for s in range(ns)], axis=1).astype(F32)
        term = _sigmoid(gate) * up
        acc = term if acc is None else acc + term
    o_ref[...] = acc.astype(o_ref.dtype)


def _merge(hm, hd, hs, proj_c, w_branch, layer, tm=1024, tn=512):
    H, n, d = hm.shape
    tm = min(tm, n)
    D = w_branch.shape[3]
    ns = tn // LANES
    br = pl.BlockSpec((H, tm, d), lambda j, i: (0, i, 0))

    def gate(b):
        first = (C_GATE + b * (D // LANES)) // ns
        return pl.BlockSpec((ns, tm, LANES), lambda j, i: (first + j, i, 0))

    return pl.pallas_call(
        _merge_kernel,
        grid=(D // tn, n // tm),
        in_specs=[br, br, br, gate(0), gate(1), gate(2),
                  pl.BlockSpec((None, N_BRANCHES, H * d, tn), lambda j, i: (layer, 0, 0, j))],
        out_specs=pl.BlockSpec((tm, tn), lambda j, i: (i, j)),
        out_shape=jax.ShapeDtypeStruct((n, D), BF16),
        scratch_shapes=[pltpu.VMEM((N_BRANCHES, H * d, tn), BF16)],
        compiler_params=_params(("parallel", "arbitrary")),
        name="branch_merge",
    )(hm, hd, hs, proj_c, proj_c, proj_c, w_branch)


def _mm_res_kernel(a_ref, w_ref, x_ref, o_ref, wb):
    @pl.when(pl.program_id(1) == 0)
    def _():
        wb[...] = w_ref[...].astype(BF16)

    o_ref[...] = x_ref[...] + _dot(a_ref[...], wb[...])


def _matmul_residual(a, w, layer, x, tm=1024, tn=512):
    m, k = a.shape
    tm = min(tm, m)
    n = w.shape[2]
    return pl.pallas_call(
        _mm_res_kernel,
        grid=(n // tn, m // tm),
        in_specs=[pl.BlockSpec((tm, k), lambda j, i: (i, 0)),
                  pl.BlockSpec((None, k, tn), lambda j, i: (layer, 0, j)),
                  pl.BlockSpec((tm, tn), lambda j, i: (i, j))],
        out_specs=pl.BlockSpec((tm, tn), lambda j, i: (i, j)),
        out_shape=jax.ShapeDtypeStruct((m, n), F32),
        scratch_shapes=[pltpu.VMEM((k, tn), BF16)],
        compiler_params=_params(("parallel", "arbitrary")),
        name="out_proj",
    )(a, w, x)


def _router_kernel(x_ref, g_ref, wr_ref, br_ref, xn_ref, ids_ref, wts_ref, cnt_ref, carry):
    i = pl.program_id(0)
    tm, D = x_ref.shape

    @pl.when(i == 0)
    def _():
        carry[...] = jnp.zeros(carry.shape, F32)

    x = x_ref[...]
    xn = x * lax.rsqrt(jnp.mean(x * x, axis=-1, keepdims=True) + EPS) * g_ref[...]
    xn_ref[...] = xn
    logits = jnp.dot(xn, wr_ref[...], preferred_element_type=F32, precision=lax.Precision.HIGHEST)
    biased = logits + br_ref[...]
    lane = lax.broadcasted_iota(I32, (tm, LANES), 1)
    lanef = lane.astype(F32)
    big = float(LANES)

    def first_lane(mask):
        return jnp.min(jnp.where(mask, lanef, big), axis=1, keepdims=True)

    gmask = lane < N_GROUPS
    gmax = jnp.max(jnp.where(gmask, biased, NEG), axis=1, keepdims=True)
    g_sel = first_lane(gmask & (biased == gmax))
    gm = jnp.max(jnp.where(gmask, logits, NEG), axis=1, keepdims=True)
    ge = jnp.where(gmask, jnp.exp(logits - gm), 0.0)
    p_group = jnp.sum(jnp.where(lanef == g_sel, ge, 0.0), axis=1, keepdims=True) / jnp.sum(ge, axis=1, keepdims=True)
    lo = N_GROUPS + EXPERTS_PER_GROUP * g_sel
    emask = (lanef >= lo) & (lanef < lo + EXPERTS_PER_GROUP)
    eb = jnp.where(emask, biased, NEG)
    e1 = first_lane(emask & (eb == jnp.max(eb, axis=1, keepdims=True)))
    emask2 = emask & (lanef != e1)
    eb2 = jnp.where(emask2, biased, NEG)
    e2 = first_lane(emask2 & (eb2 == jnp.max(eb2, axis=1, keepdims=True)))
    em = jnp.max(jnp.where(emask, logits, NEG), axis=1, keepdims=True)
    ee = jnp.where(emask, jnp.exp(logits - em), 0.0)
    s1 = jnp.sum(jnp.where(lanef == e1, ee, 0.0), axis=1, keepdims=True)
    s2 = jnp.sum(jnp.where(lanef == e2, ee, 0.0), axis=1, keepdims=True)
    se = jnp.sum(ee, axis=1, keepdims=True)
    w1 = s1 / se
    w2 = s2 / se
    wsum = w1 + w2
    w1 = p_group * w1 / wsum
    w2 = p_group * w2 / wsum
    x1 = e1 - N_GROUPS
    x2 = e2 - N_GROUPS

    onehot = jnp.where((lanef == x1) | (lanef == x2), 1.0, 0.0)
    r = lax.broadcasted_iota(I32, (tm, tm), 0)
    c = lax.broadcasted_iota(I32, (tm, tm), 1)
    lower = jnp.where(c < r, 1.0, 0.0).astype(BF16)
    prefix = _dot(lower, onehot.astype(BF16)) + carry[...]
    r1 = jnp.sum(jnp.where(lanef == x1, prefix, 0.0), axis=1, keepdims=True)
    r2 = jnp.sum(jnp.where(lanef == x2, prefix, 0.0), axis=1, keepdims=True)
    carry[...] = carry[...] + jnp.sum(onehot, axis=0, keepdims=True)
    cnt_ref[...] = carry[...].astype(I32)

    idsf = jnp.where(lane == 0, x1, jnp.where(lane == 1, x2, jnp.where(lane == 2, r1, jnp.where(lane == 3, r2, 0.0))))
    ids_ref[...] = idsf.astype(I32)
    wts_ref[...] = jnp.where(lane == 0, w1, jnp.where(lane == 1, w2, 0.0))


def _router(x2d, g, w_router, b_router, tm=512):
    n, D = x2d.shape
    return pl.pallas_call(
        _router_kernel,
        grid=(n // tm,),
        in_specs=[pl.BlockSpec((tm, D), lambda i: (i, 0)), pl.BlockSpec((1, D), lambda i: (0, 0)),
                  pl.BlockSpec((D, LANES), lambda i: (0, 0)), pl.BlockSpec((1, LANES), lambda i: (0, 0))],
        out_specs=[pl.BlockSpec((tm, D), lambda i: (i, 0)),
                   pl.BlockSpec((tm, LANES), lambda i: (i, 0)),
                   pl.BlockSpec((tm, LANES), lambda i: (i, 0)),
                   pl.BlockSpec((1, LANES), lambda i: (0, 0))],
        out_shape=[jax.ShapeDtypeStruct((n, D), F32),
                   jax.ShapeDtypeStruct((n, LANES), I32),
                   jax.ShapeDtypeStruct((n, LANES), F32),
                   jax.ShapeDtypeStruct((1, LANES), I32)],
        scratch_shapes=[pltpu.VMEM((1, LANES), F32)],
        compiler_params=_params(("arbitrary",)),
        name="moe_router",
    )(x2d, g.reshape(1, D), w_router, b_router)


def _dispatch_kernel(p1_ref, p2_ref, xn_ref, xs_in_ref, xs_ref, sem, *, tb):
    del xs_in_ref
    base = pl.program_id(0) * tb

    def copies(t):
        src = xn_ref.at[pl.ds(t, 1), :]
        return (pltpu.make_async_copy(src, xs_ref.at[pl.ds(p1_ref[base + t], 1), :], sem),
                pltpu.make_async_copy(src, xs_ref.at[pl.ds(p2_ref[base + t], 1), :], sem))

    def issue(t, _):
        for cp in copies(t):
            cp.start()
        return 0

    lax.fori_loop(0, tb, issue, 0, unroll=DMA_UNROLL)

    def drain(t, _):
        for cp in copies(t):
            cp.wait()
        return 0

    lax.fori_loop(0, tb, drain, 0, unroll=DMA_UNROLL)


def _dispatch(pos1, pos2, xn, xs0, tb=512):
    n, D = xn.shape
    n_rows = xs0.shape[0]
    tb = min(tb, n)
    grid_spec = pltpu.PrefetchScalarGridSpec(
        num_scalar_prefetch=2,
        grid=(n // tb,),
        in_specs=[pl.BlockSpec((tb, D), lambda i, p1, p2: (i, 0)),
                  pl.BlockSpec(memory_space=pl.ANY)],
        out_specs=pl.BlockSpec(memory_space=pl.ANY),
        scratch_shapes=[pltpu.SemaphoreType.DMA(())],
    )
    return pl.pallas_call(
        functools.partial(_dispatch_kernel, tb=tb),
        grid_spec=grid_spec,
        out_shape=jax.ShapeDtypeStruct((n_rows, D), F32),
        input_output_aliases={3: 0},
        compiler_params=pltpu.CompilerParams(dimension_semantics=("arbitrary",), has_side_effects=True),
        name="moe_dispatch",
    )(pos1, pos2, xn, xs0)


def _expert_kernel(te_ref, nv_ref, ne_ref, grp_ref, xs_ref, wg_ref, wu_ref, wd_ref, ys_ref,
                   wgf, wuf, wdf, wgb, wub, wdb, sem, *, layer):
    i = pl.program_id(0)
    prev = te_ref[jnp.maximum(i - 1, 0)]

    def fetch(e, slot):
        cps = []
        for src, dst in ((wg_ref, wgf), (wu_ref, wuf), (wd_ref, wdf)):
            rows = dst.shape[1] // WEIGHT_CHUNKS
            for c in range(WEIGHT_CHUNKS):
                cps.append(pltpu.make_async_copy(src.at[layer, e, pl.ds(c * rows, rows), :],
                                                 dst.at[slot, pl.ds(c * rows, rows), :], sem.at[slot]))
        return cps

    @pl.when(i < nv_ref[0])
    def _():
        @pl.when(i == 0)
        def _():
            for cp in fetch(te_ref[0], 0):
                cp.start()

        @pl.when((i == 0) | (te_ref[i] != prev))
        def _():
            slot = grp_ref[i] % 2
            for cp in fetch(te_ref[i], slot):
                cp.wait()

            @pl.when(ne_ref[i] >= 0)
            def _():
                for cp in fetch(ne_ref[i], 1 - slot):
                    cp.start()

            wgb[...] = wgf[slot].astype(BF16)
            wub[...] = wuf[slot].astype(BF16)
            wdb[...] = wdf[slot].astype(BF16)

        x = xs_ref[...].astype(BF16)
        g = _dot(x, wgb[...])
        u = _dot(x, wub[...])
        hcur = (g * _sigmoid(g) * u).astype(BF16)
        ys_ref[...] = _dot(hcur, wdb[...])

    @pl.when(i >= nv_ref[0])
    def _():
        ys_ref[...] = jnp.zeros(ys_ref.shape, F32)


def _experts(tile_expert, n_valid, xs, w_gate, w_up, w_down, layer):
    n_rows = xs.shape[0]
    _, E, D, Fe = w_gate.shape
    tm = MOE_TILE
    n_tiles = n_rows // tm
    tiles = jnp.arange(n_tiles, dtype=I32)
    first = (tiles < n_valid[0]) & ((tiles == 0) | (tile_expert != jnp.roll(tile_expert, 1)))
    group = (jnp.cumsum(first.astype(I32)) - 1).astype(I32)
    later = lax.cummin(jnp.where(first, tiles, n_tiles), reverse=True)
    nxt = jnp.concatenate([later[1:], jnp.full((1,), n_tiles, I32)])
    next_expert = jnp.where(nxt < n_tiles, tile_expert[jnp.minimum(nxt, n_tiles - 1)], -1).astype(I32)

    def row_map(i, te, nv, ne, grp):
        return (jnp.minimum(i, jnp.maximum(nv[0] - 1, 0)), 0)

    grid_spec = pltpu.PrefetchScalarGridSpec(
        num_scalar_prefetch=4,
        grid=(n_tiles,),
        in_specs=[pl.BlockSpec((tm, D), row_map),
                  pl.BlockSpec(memory_space=pl.ANY), pl.BlockSpec(memory_space=pl.ANY),
                  pl.BlockSpec(memory_space=pl.ANY)],
        out_specs=pl.BlockSpec((tm, D), lambda i, te, nv, ne, grp: (i, 0)),
        scratch_shapes=[pltpu.VMEM((2, D, Fe), F32), pltpu.VMEM((2, D, Fe), F32), pltpu.VMEM((2, Fe, D), F32),
                        pltpu.VMEM((D, Fe), BF16), pltpu.VMEM((D, Fe), BF16), pltpu.VMEM((Fe, D), BF16),
                        pltpu.SemaphoreType.DMA((2,))],
    )
    return pl.pallas_call(
        functools.partial(_expert_kernel, layer=layer),
        grid_spec=grid_spec,
        out_shape=jax.ShapeDtypeStruct((n_rows, D), F32),
        compiler_params=_params(("arbitrary",)),
        name="moe_experts",
    )(tile_expert, n_valid, next_expert, group, xs, w_gate, w_up, w_down)


def _combine_kernel(p1_ref, p2_ref, ys_ref, x_ref, w_ref, o_ref, buf, sem):
    i = pl.program_id(0)
    tc = x_ref.shape[0]
    slot = i % 2

    def copies(tile, slot, t):
        tok = tile * tc + t
        return (pltpu.make_async_copy(ys_ref.at[pl.ds(p1_ref[tok], 1), :], buf.at[slot, 0, pl.ds(t, 1), :],
                                      sem.at[slot]),
                pltpu.make_async_copy(ys_ref.at[pl.ds(p2_ref[tok], 1), :], buf.at[slot, 1, pl.ds(t, 1), :],
                                      sem.at[slot]))

    def issue(tile, slot):
        def body(t, _):
            for cp in copies(tile, slot, t):
                cp.start()
            return 0

        lax.fori_loop(0, tc, body, 0, unroll=DMA_UNROLL)

    @pl.when(i == 0)
    def _():
        issue(0, 0)

    @pl.when(i + 1 < pl.num_programs(0))
    def _():
        issue(i + 1, 1 - slot)

    def drain(t, _):
        for cp in copies(i, slot, t):
            cp.wait()
        return 0

    lax.fori_loop(0, tc, drain, 0, unroll=DMA_UNROLL)
    w = w_ref[...]
    o_ref[...] = x_ref[...] + w[:, 0:1] * buf[slot, 0] + w[:, 1:2] * buf[slot, 1]


def _combine(pos1, pos2, ys, x2d, wts, tc=256):
    n, D = x2d.shape
    grid_spec = pltpu.PrefetchScalarGridSpec(
        num_scalar_prefetch=2,
        grid=(n // tc,),
        in_specs=[pl.BlockSpec(memory_space=pl.ANY),
                  pl.BlockSpec((tc, D), lambda i, p1, p2: (i, 0)),
                  pl.BlockSpec((tc, LANES), lambda i, p1, p2: (i, 0))],
        out_specs=pl.BlockSpec((tc, D), lambda i, p1, p2: (i, 0)),
        scratch_shapes=[pltpu.VMEM((2, 2, tc, D), F32), pltpu.SemaphoreType.DMA((2,))],
    )
    return pl.pallas_call(
        _combine_kernel,
        grid_spec=grid_spec,
        out_shape=jax.ShapeDtypeStruct((n, D), F32),
        compiler_params=_params(("arbitrary",)),
        name="moe_combine",
    )(pos1, pos2, ys, x2d, wts)


def _pos_kernel(ids_ref, cnt_ref, pos_ref):
    tm = ids_ref.shape[0]
    tiles = jnp.floor((cnt_ref[...].astype(F32) + (MOE_TILE - 1)) / MOE_TILE)
    r = lax.broadcasted_iota(I32, (LANES, LANES), 0)
    c = lax.broadcasted_iota(I32, (LANES, LANES), 1)
    before = jnp.where(r < c, 1.0, 0.0).astype(BF16)
    first_tile = _dot(jnp.broadcast_to(tiles, (8, LANES)).astype(BF16), before)
    offs = first_tile[0:1, :] * MOE_TILE
    ids = ids_ref[...].astype(F32)
    lane = lax.broadcasted_iota(I32, (tm, LANES), 1)
    lanef = lane.astype(F32)
    p1 = jnp.sum(jnp.where(lanef == ids[:, 0:1], offs, 0.0), axis=1, keepdims=True) + ids[:, 2:3]
    p2 = jnp.sum(jnp.where(lanef == ids[:, 1:2], offs, 0.0), axis=1, keepdims=True) + ids[:, 3:4]
    packed = jnp.where(lane == 0, p1, jnp.where(lane == 1, p2, 0.0))
    pos_ref[...] = packed.T[0:8, :].astype(I32)


def _positions(ids, counts, tm=512):
    n = ids.shape[0]
    assert MOE_TILE & (MOE_TILE - 1) == 0
    return pl.pallas_call(
        _pos_kernel,
        grid=(n // tm,),
        in_specs=[pl.BlockSpec((tm, LANES), lambda i: (i, 0)), pl.BlockSpec((1, LANES), lambda i: (0, 0))],
        out_specs=pl.BlockSpec((8, tm), lambda i: (0, i)),
        out_shape=jax.ShapeDtypeStruct((8, n), I32),
        compiler_params=_params(("parallel",)),
        name="moe_positions",
    )(ids, counts)


def _hier_moe(x2d, norm_g, w_rg, b_rg, w_re, b_re, w_gate, w_up, w_down, layer, xs_buf):
    n, D = x2d.shape
    tm = MOE_TILE
    pad = LANES - N_GROUPS - N_EXPERTS
    w_router = jnp.concatenate([w_rg, w_re, jnp.zeros((D, pad), F32)], axis=1)
    b_router = jnp.concatenate([b_rg, b_re.reshape(-1), jnp.zeros((pad,), F32)]).reshape(1, LANES)
    xn3, ids, wts, counts = _router(x2d, norm_g, w_router, b_router)
    cnt = counts[0, :N_EXPERTS]
    padded = ((cnt + tm - 1) // tm) * tm
    ends = jnp.cumsum(padded)
    n_tiles = xs_buf.shape[0] // tm
    tile_start = jnp.arange(n_tiles, dtype=I32) * tm
    tile_expert = jnp.minimum(jnp.sum(tile_start[:, None] >= ends[None, :], axis=1), N_EXPERTS - 1).astype(I32)
    n_valid = (ends[-1] // tm).astype(I32).reshape(1)
    last_e = tile_expert[jnp.maximum(n_valid[0] - 1, 0)]
    tile_expert = jnp.where(jnp.arange(n_tiles) < n_valid[0], tile_expert, last_e)
    pos = _positions(ids, counts)
    pos1, pos2 = pos[0], pos[1]
    xs = _dispatch(pos1, pos2, xn3, xs_buf)
    ys = _experts(tile_expert, n_valid, xs, w_gate, w_up, w_down, layer)
    return _combine(pos1, pos2, ys, x2d, wts), xs


def _in_proj_regions(D):
    bw = BRANCH_WIDTH
    sizes = (bw, bw, bw, bw, N_HEADS, N_HEADS, bw, bw, bw, IDX_HEADS * IDX_DIM, IDX_DIM, IDX_HEADS,
             bw, bw, bw, N_BRANCHES * D)
    offs = np.concatenate([[0], np.cumsum(sizes)]).tolist()
    region_a = (offs[0], offs[4] - offs[0])
    region_b = (offs[6], offs[10] - offs[6])
    region_c = (offs[12], offs[16] - offs[12])
    small = dict(mi=offs[4], mf=offs[5], ik=offs[10], iw=offs[11])
    return region_a, region_b, region_c, small


def _token_mixer(x2d, B, S, layer, norm_g, w_in, conv_w, b_i, b_f, mlstm_norm_g, q_norm_g, k_norm_g,
                 w_branch, w_out, rel_bias):
    n, D = x2d.shape
    L = MLSTM_CHUNK
    H = N_HEADS
    tm = min(n, PROJ_TM)
    xn = _rmsnorm(x2d, norm_g)
    ra, rb, rc, small = _in_proj_regions(D)
    w_t = jnp.swapaxes(w_in, 1, 2)
    proj_a = _in_proj(xn, w_t, layer, ra[0], ra[1], tm)
    proj_b = _in_proj(xn, w_t, layer, rb[0], rb[1], tm)
    proj_c = _in_proj(xn, w_t, layer, rc[0], rc[1], tm)
    tail = _tail_proj(xn, w_t, layer, small, tm)
    g = tail[1][:, IDX_HEADS:IDX_HEADS + 2 * H].reshape(B, S // L, L, 2, H)
    gates_t = jnp.transpose(g, (0, 4, 1, 3, 2))
    hm = _mlstm(proj_a, gates_t, jnp.stack([b_i, b_f]), conv_w, mlstm_norm_g, B, S)
    hd = _dsa(proj_b, tail, q_norm_g, k_norm_g, rel_bias, B, S)
    hs = _stick_breaking(proj_c, B, S)
    merged = _merge(hm, hd, hs, proj_c, w_branch, layer)
    return _matmul_residual(merged, w_out, layer, x2d)


def kernel(x, norm1_g, w_in, conv_w, b_i, b_f, mlstm_norm_g, q_norm_g, k_norm_g, w_branch, w_out, norm2_g,
           w_router_g, b_router_g, w_router_e, b_router_e, w_gate, w_up, w_down, rel_bias):
    B, S, D = x.shape
    x2d = x.reshape(B * S, D)
    xs_buf = jnp.zeros((2 * B * S + N_EXPERTS * MOE_TILE, D), F32)
    for l in range(w_in.shape[0]):
        x2d = _token_mixer(x2d, B, S, l, norm1_g[l], w_in, conv_w[l], b_i[l], b_f[l], mlstm_norm_g[l],
                           q_norm_g[l], k_norm_g[l], w_branch, w_out, rel_bias)
        x2d, xs_buf = _hier_moe(x2d, norm2_g[l], w_router_g[l], b_router_g[l], w_router_e[l], b_router_e[l],
                                w_gate, w_up, w_down, l, xs_buf)
    return x2d.reshape(B, S, D)
```

```python
import functools
import math

import numpy as np
import jax
import jax.numpy as jnp
from jax import lax
from jax.experimental import pallas as pl
from jax.experimental.pallas import tpu as pltpu

F32 = jnp.float32
BF16 = jnp.bfloat16
I32 = jnp.int32

LANES = 128
HEAD_DIM = 128
N_HEADS = 8
BRANCH_WIDTH = N_HEADS * HEAD_DIM
N_BRANCHES = 3
CONV_WIDTH = 4
IDX_HEADS = 16
IDX_DIM = 64
TOPK_MAX = 256
N_BUCKETS = 32
MAX_DISTANCE = 128
N_GROUPS = 4
EXPERTS_PER_GROUP = 8
N_EXPERTS = N_GROUPS * EXPERTS_PER_GROUP
EPS = 1e-6
NEG = -1e30
INT_MIN = -(2 ** 31)

MLSTM_CHUNK = 256
DSA_BLOCK = 128
DSA_BANDS = 4
DSA_QBLOCKS = 2
SB_BLOCK = 512
SB_ROWS = 512
SB_PIECES = 1
SB_HEADS = 4
HEADS_PER_STEP = 4
MOE_TILE = 256
DMA_UNROLL = 8
PROJ_TN = 512
PROJ_TM = 2048
VMEM_LIMIT = 56 * 1024 * 1024

A_MQ, A_MK, A_MV, A_MO = 0, 8, 16, 24
B_DQ, B_DK, B_DV, B_IQ = 0, 8, 16, 24
C_SQ, C_SK, C_SV, C_GATE = 0, 8, 16, 24


def _params(sem):
    return pltpu.CompilerParams(dimension_semantics=sem, vmem_limit_bytes=VMEM_LIMIT)


def _dot(a, b):
    return jnp.dot(a, b, preferred_element_type=F32)


def _dot_nt(a, b):
    return lax.dot_general(a, b, (((1,), (1,)), ((), ())), preferred_element_type=F32)


def _sigmoid(z):
    return 1.0 / (1.0 + jnp.exp(-z))


def _rmsnorm_kernel(x_ref, g_ref, o_ref):
    x = x_ref[...]
    ms = jnp.mean(x * x, axis=-1, keepdims=True)
    o_ref[...] = (x * lax.rsqrt(ms + EPS) * g_ref[...]).astype(o_ref.dtype)


def _rmsnorm(x2d, g, tm=512):
    n, d = x2d.shape
    return pl.pallas_call(
        _rmsnorm_kernel,
        grid=(n // tm,),
        in_specs=[pl.BlockSpec((tm, d), lambda i: (i, 0)), pl.BlockSpec((1, d), lambda i: (0, 0))],
        out_specs=pl.BlockSpec((tm, d), lambda i: (i, 0)),
        out_shape=jax.ShapeDtypeStruct((n, d), BF16),
        compiler_params=_params(("parallel",)),
        name="rmsnorm",
    )(x2d, g.reshape(1, d))


def _in_proj_kernel(x_ref, *rest, shift, nblk):
    w_refs, o_ref, wb = rest[:nblk], rest[nblk], rest[nblk + 1]
    tn = wb.shape[1]

    @pl.when(pl.program_id(1) == 0)
    def _():
        w = jnp.concatenate([r[...] for r in w_refs], axis=0)
        wb[...] = w[shift:shift + tn, :].T.astype(BF16)

    acc = _dot(x_ref[...], wb[...])
    for j in range(o_ref.shape[0]):
        o_ref[j] = acc[:, j * LANES:(j + 1) * LANES].astype(o_ref.dtype)


def _in_proj(xn, w_t, layer, col0, ncols, tm):
    m, k = xn.shape
    tn = PROJ_TN
    base, shift = divmod(col0, LANES)
    assert shift % 8 == 0
    nblk = tn // LANES + (1 if shift else 0)
    per = tn // LANES

    def wspec(r):
        return pl.BlockSpec((None, LANES, k), lambda j, i: (layer, base + per * j + r, 0))

    return pl.pallas_call(
        functools.partial(_in_proj_kernel, shift=shift, nblk=nblk),
        grid=(ncols // tn, m // tm),
        in_specs=[pl.BlockSpec((tm, k), lambda j, i: (i, 0))] + [wspec(r) for r in range(nblk)],
        out_specs=pl.BlockSpec((per, tm, LANES), lambda j, i: (j, i, 0)),
        out_shape=jax.ShapeDtypeStruct((ncols // LANES, m, LANES), BF16),
        scratch_shapes=[pltpu.VMEM((k, tn), BF16)],
        compiler_params=_params(("parallel", "arbitrary")),
        name="in_proj",
    )(xn, *([w_t] * nblk))


def _tail_kernel(x_ref, wg_ref, wi_ref, o_ref, wt, *, g_lane, ik_lane, iw_lane):
    @pl.when(pl.program_id(0) == 0)
    def _():
        wg = wg_ref[...]
        wi = wi_ref[...]
        ik = wi[ik_lane:ik_lane + IDX_DIM, :]
        iw = wi[iw_lane:iw_lane + IDX_HEADS, :]
        gates = wg[g_lane:g_lane + 2 * N_HEADS, :]
        pad = jnp.zeros((LANES - IDX_HEADS - 2 * N_HEADS, wg.shape[1]), F32)
        wt[...] = jnp.concatenate([ik, ik, iw, gates, pad], axis=0).T.astype(BF16)

    acc = _dot(x_ref[...], wt[...])
    o_ref[0] = acc[:, :LANES]
    o_ref[1] = acc[:, LANES:]


def _tail_proj(xn, w_t, layer, small, tm):
    m, k = xn.shape
    g_blk, g_lane = divmod(small["mi"], LANES)
    i_blk, ik_lane = divmod(small["ik"], LANES)
    iw_lane = small["iw"] - i_blk * LANES
    assert small["mf"] == small["mi"] + N_HEADS and g_lane + 2 * N_HEADS <= LANES
    assert ik_lane + IDX_DIM <= LANES and 0 <= iw_lane and iw_lane + IDX_HEADS <= LANES
    assert g_lane % 8 == 0 and ik_lane % 8 == 0 and iw_lane % 8 == 0
    return pl.pallas_call(
        functools.partial(_tail_kernel, g_lane=g_lane, ik_lane=ik_lane, iw_lane=iw_lane),
        grid=(m // tm,),
        in_specs=[pl.BlockSpec((tm, k), lambda i: (i, 0)),
                  pl.BlockSpec((None, LANES, k), lambda i: (layer, g_blk, 0)),
                  pl.BlockSpec((None, LANES, k), lambda i: (layer, i_blk, 0))],
        out_specs=pl.BlockSpec((2, tm, LANES), lambda i: (0, i, 0)),
        out_shape=jax.ShapeDtypeStruct((2, m, LANES), F32),
        scratch_shapes=[pltpu.VMEM((k, 2 * LANES), BF16)],
        compiler_params=_params(("arbitrary",)),
        name="tail_proj",
    )(xn, w_t, w_t)


def _mlstm_kernel(bias_ref, q_ref, k_ref, v_ref, o_ref, g_ref, cwq_ref, cwk_ref, ng_ref, out_ref,
                  qf, kf, qc, kc, st):
    hp = pl.program_id(1)
    U = HEADS_PER_STEP
    S = q_ref.shape[1]
    L = MLSTM_CHUNK
    nc = S // L
    d = HEAD_DIM
    PAD = 8

    R = min(S, 256)
    for u in range(U):
        qf[u, 0:PAD, :] = jnp.zeros((PAD, d), F32)
        kf[u, 0:PAD, :] = jnp.zeros((PAD, d), F32)
        qf[u, PAD:PAD + S, :] = q_ref[u].astype(F32)
        kf[u, PAD:PAD + S, :] = k_ref[u].astype(F32)
        ls = slice(u * d, (u + 1) * d)
        for r0 in range(0, S, R):
            aq = jnp.zeros((R, d), F32)
            ak = jnp.zeros((R, d), F32)
            for t in range(CONV_WIDTH):
                off = PAD - (CONV_WIDTH - 1) + t + r0
                aq = aq + cwq_ref[t:t + 1, ls] * qf[u, off:off + R, :]
                ak = ak + cwk_ref[t:t + 1, ls] * kf[u, off:off + R, :]
            qc[u, r0:r0 + R, :] = (aq * _sigmoid(aq) * (d ** -0.5)).astype(BF16)
            kc[u, r0:r0 + R, :] = ak * _sigmoid(ak)

    st[...] = jnp.zeros(st.shape, F32)
    row = lax.broadcasted_iota(I32, (L, L), 0)
    col = lax.broadcasted_iota(I32, (L, L), 1)
    causal = col <= row
    eye = col == row
    lane = lax.broadcasted_iota(I32, (L, d), 1)
    ones_col = jnp.where(lane == 0, 1.0, 0.0).astype(BF16)
    ng = ng_ref[...]

    def chunk_one(u, c, r, m):
        q = qc[u, pl.ds(r, L), :]
        kT = kc[u, pl.ds(r, L), :].T
        v = v_ref[u, pl.ds(r, L), :]
        vaug = jnp.concatenate([v, ones_col], axis=1)
        gates = g_ref[u, c]
        i_row = gates[0:1, :] + bias_ref[0, hp * U + u]
        f_row = gates[1:2, :] + bias_ref[1, hp * U + u]
        lf_row = jnp.minimum(f_row, 0.0) - jnp.log1p(jnp.exp(-jnp.abs(f_row)))
        b_col = jnp.sum(jnp.where(causal, lf_row, 0.0), axis=1, keepdims=True)
        b_row = jnp.sum(jnp.where(eye, b_col, 0.0), axis=0, keepdims=True)
        dlog = jnp.where(causal, b_col - b_row + i_row, NEG)
        inter = b_col + m
        m_t = jnp.maximum(inter, jnp.max(dlog, axis=1, keepdims=True))
        w_intra = jnp.exp(dlog - m_t)
        w_inter = jnp.exp(inter - m_t)
        s = _dot(q, kT.astype(BF16)) * w_intra
        res = w_inter * _dot(q, st[u].astype(BF16)) + _dot(s.astype(BF16), vaug)
        num = res[:, :d]
        den = res[:, d:d + 1]
        hh = num / jnp.maximum(jnp.abs(den), jnp.exp(-m_t))
        hn = hh * lax.rsqrt(jnp.mean(hh * hh, axis=1, keepdims=True) + EPS) * ng[:, u * d:(u + 1) * d]
        og = o_ref[u, pl.ds(r, L), :].astype(F32)
        out_ref[u, pl.ds(r, L), :] = (hn * _sigmoid(og)).astype(out_ref.dtype)
        ws_row = w_intra[L - 1:L, :]
        decay = w_inter[L - 1:L, :]
        st[u] = decay * st[u] + _dot((kT * ws_row).astype(BF16), vaug)
        return m_t[L - 1:L, :]

    def chunk(c, ms):
        r = pl.multiple_of(c * L, L)
        return tuple(chunk_one(u, c, r, ms[u]) for u in range(U))

    lax.fori_loop(0, nc, chunk, tuple(jnp.zeros((1, 1), F32) for _ in range(U)))


def _mlstm(proj_a, gates_t, bias_if, conv_w, norm_g, B, S):
    L = MLSTM_CHUNK
    nc = S // L
    d = HEAD_DIM
    H = N_HEADS
    U = HEADS_PER_STEP

    def slab(off):
        return pl.BlockSpec((U, S, d), lambda b, h: (off // U + h, b, 0))

    return pl.pallas_call(
        _mlstm_kernel,
        grid=(B, H // U),
        in_specs=[
            pl.BlockSpec(memory_space=pltpu.SMEM),
            slab(A_MQ), slab(A_MK), slab(A_MV), slab(A_MO),
            pl.BlockSpec((None, U, nc, 2, L), lambda b, h: (b, h, 0, 0, 0)),
            pl.BlockSpec((CONV_WIDTH, U * d), lambda b, h: (0, h)),
            pl.BlockSpec((CONV_WIDTH, U * d), lambda b, h: (0, H // U + h)),
            pl.BlockSpec((1, U * d), lambda b, h: (0, h)),
        ],
        out_specs=pl.BlockSpec((U, S, d), lambda b, h: (h, b, 0)),
        out_shape=jax.ShapeDtypeStruct((H, B * S, d), BF16),
        scratch_shapes=[
            pltpu.VMEM((U, S + 8, d), F32), pltpu.VMEM((U, S + 8, d), F32),
            pltpu.VMEM((U, S, d), BF16), pltpu.VMEM((U, S, d), F32),
            pltpu.VMEM((U, d, 2 * d), F32),
        ],
        compiler_params=_params(("parallel", "parallel")),
        name="mlstm",
    )(bias_if, proj_a, proj_a, proj_a, proj_a, gates_t, conv_w, conv_w, norm_g.reshape(1, H * d))


def _t5_thresholds():
    max_exact = N_BUCKETS // 2
    n = np.arange(0, 2 * MAX_DISTANCE)
    nf = np.maximum(n, 1).astype(np.float64)
    val = np.log(nf / max_exact) / math.log(MAX_DISTANCE / max_exact) * (N_BUCKETS - max_exact)
    frac = np.abs(val - np.round(val))
    frac_ok = (frac > 1e-4) | (n <= max_exact) | (n >= MAX_DISTANCE)
    assert frac_ok.all()
    large = np.minimum(max_exact + np.trunc(val).astype(np.int64), N_BUCKETS - 1)
    bucket = np.where(n < max_exact, n, large)
    assert (np.diff(bucket) >= 0).all() and bucket[MAX_DISTANCE] == N_BUCKETS - 1
    return [int(np.argmax(bucket >= j)) for j in range(1, N_BUCKETS)]


_T5_THR = _t5_thresholds()


def _dsa_kernel(rb_ref, q_ref, k_ref, v_ref, iq_ref, ik_ref, wt_ref, gq_ref, gk_ref, prev_ref, out_ref,
                kn, ikk, wb, key_ref, mask_ref, bias_ref, *, topk, q0, nb, nqb):
    del prev_ref
    qis = [q0 + pl.program_id(1) * nqb + qb for qb in range(nqb)]
    T = DSA_BLOCK
    d = HEAD_DIM
    H = N_HEADS
    HALF = T // 2
    row = lax.broadcasted_iota(I32, (T, T), 0)
    col = lax.broadcasted_iota(I32, (T, T), 1)

    @pl.when(pl.program_id(1) == 0)
    def _per_batch():
        gk = gk_ref[...]

        def norm_k(h, _):
            kh = k_ref[h].astype(F32)
            kn[h] = (kh * lax.rsqrt(jnp.mean(kh * kh, axis=1, keepdims=True) + EPS) * gk).astype(BF16)
            for o in range(2):
                n = o * T + row - col
                val = jnp.full((T, T), rb_ref[0, h], F32)
                for j, thr in enumerate(_T5_THR):
                    val = jnp.where(n >= thr, rb_ref[j + 1, h], val)
                bias_ref[h, o] = val
            bias_ref[h, 2] = jnp.full((T, T), rb_ref[N_BUCKETS - 1, h], F32)
            return 0

        lax.fori_loop(0, H, norm_k, 0)
        ik = ik_ref[...]
        klane = lax.broadcasted_iota(I32, ik.shape, 1)
        ikk[0] = jnp.where(klane < IDX_DIM, ik, 0.0).astype(BF16)
        ikk[1] = jnp.where(klane >= IDX_DIM, ik, 0.0).astype(BF16)

    wsc = wt_ref[...] * (IDX_HEADS ** -0.5 * IDX_DIM ** -0.5)
    for qb in range(nqb):
        for h16 in range(IDX_HEADS):
            wb[qb * IDX_HEADS + h16] = jnp.broadcast_to(wsc[qb * T:(qb + 1) * T, h16:h16 + 1], (T, LANES))
    q_pairs = iq_ref[...].reshape((IDX_HEADS // 2) * nqb * T, LANES)

    def score_block(j, _):
        r = pl.multiple_of(j * T, T)
        kk = jnp.concatenate([ikk[0, pl.ds(r, T), :], ikk[1, pl.ds(r, T), :]], axis=0)
        dots = _dot_nt(q_pairs, kk)
        for qb in range(nqb):
            sc = jnp.zeros((T, T), F32)
            for h16 in range(IDX_HEADS):
                hp, odd = divmod(h16, 2)
                r0 = (hp * nqb + qb) * T
                sc = sc + wb[qb * IDX_HEADS + h16] * jnp.maximum(dots[r0:r0 + T, odd * T:(odd + 1) * T], 0.0)
            bits = lax.bitcast_convert_type(sc, I32)
            key = jnp.where(bits < 0, bits ^ jnp.int32(0x7FFFFFFF), bits)
            key = jnp.where(sc == 0.0, 0, key)
            key = jnp.where(j * T + col <= qis[qb] * T + row, key, INT_MIN)
            key_ref[qb, j] = key
        return 0

    lax.fori_loop(0, nb, score_block, 0, unroll=2)

    kf = float(topk)
    ones = jnp.ones((LANES, LANES), BF16)

    def count(chain, pred):
        qb, lo = chain
        acc = jnp.zeros((HALF, LANES), F32)
        for jb in range(nb):
            acc = acc + jnp.where(pred(key_ref[qb, jb, lo:lo + HALF, :]), 1.0, 0.0)
        return _dot(acc.astype(BF16), ones)

    chains = [(qb, lo) for qb in range(nqb) for lo in (0, HALF)]
    zero_i = jnp.zeros((HALF, LANES), I32)
    thr0 = tuple(jnp.where(count(ch, lambda k: k >= zero_i) >= kf, jnp.int32(0), jnp.int32(INT_MIN))
                 for ch in chains)

    def bisect_bit(thrs, bit):
        out = []
        for ch, t in zip(chains, thrs):
            cand = t | bit
            out.append(jnp.where(count(ch, lambda k, c=cand: k >= c) >= kf, cand, t))
        return tuple(out)

    def bisect_pair(it, thrs):
        hi_bit = jnp.left_shift(jnp.int32(1), 29 - 2 * it)
        lo_bit = jnp.left_shift(jnp.int32(1), 28 - 2 * it)
        out = []
        for ch, t in zip(chains, thrs):
            c01, c10, c11 = t | lo_bit, t | hi_bit, t | hi_bit | lo_bit
            n01, n10, n11 = (count(ch, lambda k, c=c: k >= c) >= kf for c in (c01, c10, c11))
            out.append(jnp.where(n11, c11, jnp.where(n10, c10, jnp.where(n01, c01, t))))
        return tuple(out)

    thrs = bisect_bit(thr0, jnp.int32(1 << 30))
    thrs = lax.fori_loop(0, 15, bisect_pair, thrs)
    needs = [kf - count(ch, lambda k, t=t: k > t) for ch, t in zip(chains, thrs)]

    tri = jnp.where(row <= col, 1.0, 0.0).astype(BF16)
    for qb in range(nqb):
        need = jnp.concatenate(needs[2 * qb:2 * qb + 2], axis=0)
        thrb = jnp.concatenate(thrs[2 * qb:2 * qb + 2], axis=0)
        seen = jnp.zeros((T, LANES), F32)
        for jb in range(nb):
            key = key_ref[qb, jb]
            tie = key == thrb
            tie16 = jnp.where(tie, 1.0, 0.0).astype(BF16)
            pre = _dot(tie16, tri) + seen
            m = jnp.where(key > thrb, 0.0, jnp.where(tie, jnp.where(pre <= need, 0.0, NEG), NEG))
            mask_ref[qb, jb] = jnp.where(key == INT_MIN, NEG, m)
            seen = seen + _dot(tie16, ones)

    gq = gq_ref[...]

    def head(h):
        qh = q_ref[h].astype(F32)
        qn = (qh * lax.rsqrt(jnp.mean(qh * qh, axis=1, keepdims=True) + EPS) * gq * (d ** -0.5)).astype(BF16)
        lgs = _dot_nt(qn, kn[h])
        p_rows, sums = [], []
        for qb in range(nqb):
            mx = None
            lg_blocks = []
            for jb in range(nb):
                lg = (lgs[qb * T:(qb + 1) * T, jb * T:(jb + 1) * T]
                      + bias_ref[h, jnp.clip(qis[qb] - jb, 0, 2)] + mask_ref[qb, jb])
                lg_blocks.append(lg)
                mx = lg if mx is None else jnp.maximum(mx, lg)
            rowmax = jnp.max(mx, axis=1, keepdims=True)
            l = jnp.zeros((T, T), F32)
            ps = []
            for jb in range(nb):
                p = jnp.exp(lg_blocks[jb] - rowmax)
                l = l + p
                ps.append(p.astype(BF16))
            p_rows.append(jnp.concatenate(ps, axis=1))
            sums.append(jnp.sum(l, axis=1, keepdims=True))
        acc = _dot(jnp.concatenate(p_rows, axis=0), v_ref[h])
        out_ref[h] = (acc / jnp.concatenate(sums, axis=0)).astype(out_ref.dtype)

    def head_group(g, _):
        for u in range(HEADS_PER_STEP):
            head(g * HEADS_PER_STEP + u)
        return 0

    lax.fori_loop(0, H // HEADS_PER_STEP, head_group, 0)


def _dsa(proj_b, tail, q_norm_g, k_norm_g, rel_bias, B, S):
    T = DSA_BLOCK
    nq = S // T
    d = HEAD_DIM
    H = N_HEADS
    topk = min(TOPK_MAX, S // 4)
    per = nq // DSA_BANDS
    nqb = math.gcd(per, DSA_QBLOCKS)
    TQ = nqb * T
    pb = proj_b.reshape(proj_b.shape[0], B, S, d)
    tl = tail.reshape(tail.shape[0], B, S, LANES)
    out = jnp.zeros((H, B, S, d), BF16)
    for band in range(DSA_BANDS):
        q0 = band * per
        nb = q0 + per
        W = nb * T
        in_specs = [
            pl.BlockSpec(memory_space=pltpu.SMEM),
            pl.BlockSpec((H, None, TQ, d), lambda b, i: (B_DQ // H, b, q0 // nqb + i, 0)),
            pl.BlockSpec((H, None, W, d), lambda b, i: (B_DK // H, b, 0, 0)),
            pl.BlockSpec((H, None, W, d), lambda b, i: (B_DV // H, b, 0, 0)),
            pl.BlockSpec((H, None, TQ, d), lambda b, i: (B_IQ // H, b, q0 // nqb + i, 0)),
            pl.BlockSpec((None, None, W, LANES), lambda b, i: (0, b, 0, 0)),
            pl.BlockSpec((None, None, TQ, LANES), lambda b, i: (1, b, q0 // nqb + i, 0)),
            pl.BlockSpec((1, d), lambda b, i: (0, 0)),
            pl.BlockSpec((1, d), lambda b, i: (0, 0)),
            pl.BlockSpec(memory_space=pl.ANY),
        ]
        args = [rel_bias, pb, pb, pb, pb, tl, tl, q_norm_g.reshape(1, d), k_norm_g.reshape(1, d), out]
        out = pl.pallas_call(
            functools.partial(_dsa_kernel, topk=topk, q0=q0, nb=nb, nqb=nqb),
            grid=(B, per // nqb),
            in_specs=in_specs,
            out_specs=pl.BlockSpec((H, None, TQ, d), lambda b, i: (0, b, q0 // nqb + i, 0)),
            out_shape=jax.ShapeDtypeStruct((H, B, S, d), BF16),
            scratch_shapes=[
                pltpu.VMEM((H, W, d), BF16),
                pltpu.VMEM((2, W, LANES), BF16),
                pltpu.VMEM((nqb * IDX_HEADS, T, LANES), F32),
                pltpu.VMEM((nqb, nb, T, T), I32),
                pltpu.VMEM((nqb, nb, T, T), F32),
                pltpu.VMEM((H, 3, T, T), F32),
            ],
            input_output_aliases={len(args) - 1: 0},
            compiler_params=_params(("parallel", "arbitrary")),
            name=f"dsa_band{band}",
        )(*args)
    return out.reshape(H, B * S, d)


def _sb_kernel(q_ref, k_ref, v_ref, out_ref):
    qi = pl.program_id(2)
    U = SB_HEADS
    T = SB_BLOCK
    d = HEAD_DIM
    R = SB_ROWS
    ns = T // R
    row = lax.broadcasted_iota(I32, (T, T), 0)
    col = lax.broadcasted_iota(I32, (T, T), 1)
    upper = jnp.where(row > col, 1.0, 0.0).astype(BF16)
    srow = lax.broadcasted_iota(I32, (R, T), 0)
    scol = lax.broadcasted_iota(I32, (R, T), 1)
    units = [(u, s) for u in range(U) for s in range(ns)]
    qs = [(q_ref[u, s * R:(s + 1) * R, :].astype(F32) * (d ** -0.5)).astype(BF16) for u, s in units]

    def block(n, j, diag, run, acc):
        u, s = units[n]
        r = pl.multiple_of(j * T, T)
        z = _dot_nt(qs[n], k_ref[u, pl.ds(r, T), :])
        sp = jnp.maximum(z, 0.0) + jnp.log(1.0 + jnp.exp(-jnp.abs(z)))
        strict = scol < srow + s * R
        ln = jnp.where(strict, -sp, 0.0) if diag else -sp
        parts = [ln.astype(BF16)]
        for _ in range(SB_PIECES - 1):
            parts.append((ln - sum(p.astype(F32) for p in parts)).astype(BF16))
        sums = _dot(jnp.concatenate(parts, axis=0), upper)
        suf = sum(sums[p * R:(p + 1) * R] for p in range(SB_PIECES))
        a = jnp.exp(z - sp + suf + run)
        if diag:
            a = jnp.where(strict, a, 0.0)
        acc = acc + _dot(a.astype(BF16), v_ref[u, pl.ds(r, T), :])
        return run + suf[:, 0:1] + ln[:, 0:1], acc

    state = []
    for n in range(len(units)):
        state.extend(block(n, qi, True, jnp.zeros((R, 1), F32), jnp.zeros((R, d), F32)))

    def earlier(i, st):
        out = []
        for n in range(len(units)):
            out.extend(block(n, qi - i, False, st[2 * n], st[2 * n + 1]))
        return tuple(out)

    state = lax.fori_loop(1, qi + 1, earlier, tuple(state))
    for n, (u, s) in enumerate(units):
        out_ref[u, s * R:(s + 1) * R, :] = state[2 * n + 1].astype(out_ref.dtype)


def _stick_breaking(proj_c, B, S):
    T = SB_BLOCK
    nq = S // T
    d = HEAD_DIM
    H = N_HEADS
    U = SB_HEADS
    return pl.pallas_call(
        _sb_kernel,
        grid=(B, H // U, nq),
        in_specs=[
            pl.BlockSpec((U, T, d), lambda b, h, i: (C_SQ // U + h, b * nq + i, 0)),
            pl.BlockSpec((U, S, d), lambda b, h, i: (C_SK // U + h, b, 0)),
            pl.BlockSpec((U, S, d), lambda b, h, i: (C_SV // U + h, b, 0)),
        ],
        out_specs=pl.BlockSpec((U, T, d), lambda b, h, i: (h, b * nq + i, 0)),
        out_shape=jax.ShapeDtypeStruct((H, B * S, d), BF16),
        compiler_params=_params(("parallel", "parallel", "parallel")),
        name="stick_breaking",
    )(proj_c, proj_c, proj_c)


def _merge_kernel(hm_ref, hd_ref, hs_ref, g0_ref, g1_ref, g2_ref, w_ref, o_ref, wb):
    H = N_HEADS

    @pl.when(pl.program_id(1) == 0)
    def _():
        wb[...] = w_ref[...].astype(BF16)

    acc = None
    ns = o_ref.shape[1] // LANES
    for n, (br, gr) in enumerate(((hm_ref, g0_ref), (hd_ref, g1_ref), (hs_ref, g2_ref))):
        a = jnp.concatenate([br[h] for h in range(H)], axis=1)
        up = _dot(a, wb[n])
        gate = jnp.concatenate([gr[s] for s in range(ns)], axis=1).astype(F32)
        term = _sigmoid(gate) * up
        acc = term if acc is None else acc + term
    o_ref[...] = acc.astype(o_ref.dtype)


def _merge(hm, hd, hs, proj_c, w_branch, layer, tm=1024, tn=512):
    H, n, d = hm.shape
    tm = min(tm, n)
    D = w_branch.shape[3]
    ns = tn // LANES
    br = pl.BlockSpec((H, tm, d), lambda j, i: (0, i, 0))

    def gate(b):
        first = (C_GATE + b * (D // LANES)) // ns
        return pl.BlockSpec((ns, tm, LANES), lambda j, i: (first + j, i, 0))

    return pl.pallas_call(
        _merge_kernel,
        grid=(D // tn, n // tm),
        in_specs=[br, br, br, gate(0), gate(1), gate(2),
                  pl.BlockSpec((None, N_BRANCHES, H * d, tn), lambda j, i: (layer, 0, 0, j))],
        out_specs=pl.BlockSpec((tm, tn), lambda j, i: (i, j)),
        out_shape=jax.ShapeDtypeStruct((n, D), BF16),
        scratch_shapes=[pltpu.VMEM((N_BRANCHES, H * d, tn), BF16)],
        compiler_params=_params(("parallel", "arbitrary")),
        name="branch_merge",
    )(hm, hd, hs, proj_c, proj_c, proj_c, w_branch)


def _mm_res_kernel(a_ref, w_ref, x_ref, o_ref, wb):
    @pl.when(pl.program_id(1) == 0)
    def _():
        wb[...] = w_ref[...].astype(BF16)

    o_ref[...] = x_ref[...] + _dot(a_ref[...], wb[...])


def _matmul_residual(a, w, layer, x, tm=1024, tn=512):
    m, k = a.shape
    tm = min(tm, m)
    n = w.shape[2]
    return pl.pallas_call(
        _mm_res_kernel,
        grid=(n // tn, m // tm),
        in_specs=[pl.BlockSpec((tm, k), lambda j, i: (i, 0)),
                  pl.BlockSpec((None, k, tn), lambda j, i: (layer, 0, j)),
                  pl.BlockSpec((tm, tn), lambda j, i: (i, j))],
        out_specs=pl.BlockSpec((tm, tn), lambda j, i: (i, j)),
        out_shape=jax.ShapeDtypeStruct((m, n), F32),
        scratch_shapes=[pltpu.VMEM((k, tn), BF16)],
        compiler_params=_params(("parallel", "arbitrary")),
        name="out_proj",
    )(a, w, x)


def _router_kernel(x_ref, g_ref, wr_ref, br_ref, xn_ref, ids_ref, wts_ref, cnt_ref, carry, lower):
    i = pl.program_id(0)
    tm, D = x_ref.shape

    @pl.when(i == 0)
    def _():
        carry[...] = jnp.zeros(carry.shape, F32)
        r = lax.broadcasted_iota(I32, (tm, tm), 0)
        c = lax.broadcasted_iota(I32, (tm, tm), 1)
        lower[...] = jnp.where(c < r, 1.0, 0.0).astype(BF16)

    x = x_ref[...]
    xn = x * lax.rsqrt(jnp.mean(x * x, axis=-1, keepdims=True) + EPS) * g_ref[...]
    xn_ref[...] = xn
    logits = jnp.dot(xn, wr_ref[...], preferred_element_type=F32, precision=lax.Precision.HIGHEST)
    biased = logits + br_ref[...]
    lane = lax.broadcasted_iota(I32, (tm, LANES), 1)
    lanef = lane.astype(F32)
    big = float(LANES)

    def first_lane(mask):
        return jnp.min(jnp.where(mask, lanef, big), axis=1, keepdims=True)

    gmask = lane < N_GROUPS
    gmax = jnp.max(jnp.where(gmask, biased, NEG), axis=1, keepdims=True)
    g_sel = first_lane(gmask & (biased == gmax))
    gm = jnp.max(jnp.where(gmask, logits, NEG), axis=1, keepdims=True)
    ge = jnp.where(gmask, jnp.exp(logits - gm), 0.0)
    p_group = jnp.sum(jnp.where(lanef == g_sel, ge, 0.0), axis=1, keepdims=True) / jnp.sum(ge, axis=1, keepdims=True)
    lo = N_GROUPS + EXPERTS_PER_GROUP * g_sel
    emask = (lanef >= lo) & (lanef < lo + EXPERTS_PER_GROUP)
    eb = jnp.where(emask, biased, NEG)
    e1 = first_lane(emask & (eb == jnp.max(eb, axis=1, keepdims=True)))
    emask2 = emask & (lanef != e1)
    eb2 = jnp.where(emask2, biased, NEG)
    e2 = first_lane(emask2 & (eb2 == jnp.max(eb2, axis=1, keepdims=True)))
    em = jnp.max(jnp.where(emask, logits, NEG), axis=1, keepdims=True)
    ee = jnp.where(emask, jnp.exp(logits - em), 0.0)
    s1 = jnp.sum(jnp.where(lanef == e1, ee, 0.0), axis=1, keepdims=True)
    s2 = jnp.sum(jnp.where(lanef == e2, ee, 0.0), axis=1, keepdims=True)
    se = jnp.sum(ee, axis=1, keepdims=True)
    w1 = s1 / se
    w2 = s2 / se
    wsum = w1 + w2
    w1 = p_group * w1 / wsum
    w2 = p_group * w2 / wsum
    x1 = e1 - N_GROUPS
    x2 = e2 - N_GROUPS

    onehot = jnp.where((lanef == x1) | (lanef == x2), 1.0, 0.0)
    prefix = _dot(lower[...], onehot.astype(BF16)) + carry[...]
    r1 = jnp.sum(jnp.where(lanef == x1, prefix, 0.0), axis=1, keepdims=True)
    r2 = jnp.sum(jnp.where(lanef == x2, prefix, 0.0), axis=1, keepdims=True)
    carry[...] = carry[...] + jnp.sum(onehot, axis=0, keepdims=True)
    cnt_ref[...] = carry[...].astype(I32)

    idsf = jnp.where(lane == 0, x1, jnp.where(lane == 1, x2, jnp.where(lane == 2, r1, jnp.where(lane == 3, r2, 0.0))))
    ids_ref[...] = idsf.astype(I32)
    wts_ref[...] = jnp.where(lane == 0, w1, jnp.where(lane == 1, w2, 0.0))


def _router(x2d, g, w_router, b_router, tm=512):
    n, D = x2d.shape
    return pl.pallas_call(
        _router_kernel,
        grid=(n // tm,),
        in_specs=[pl.BlockSpec((tm, D), lambda i: (i, 0)), pl.BlockSpec((1, D), lambda i: (0, 0)),
                  pl.BlockSpec((D, LANES), lambda i: (0, 0)), pl.BlockSpec((1, LANES), lambda i: (0, 0))],
        out_specs=[pl.BlockSpec((tm, D), lambda i: (i, 0)),
                   pl.BlockSpec((tm, LANES), lambda i: (i, 0)),
                   pl.BlockSpec((tm, LANES), lambda i: (i, 0)),
                   pl.BlockSpec((1, LANES), lambda i: (0, 0))],
        out_shape=[jax.ShapeDtypeStruct((n, D), F32),
                   jax.ShapeDtypeStruct((n, LANES), I32),
                   jax.ShapeDtypeStruct((n, LANES), F32),
                   jax.ShapeDtypeStruct((1, LANES), I32)],
        scratch_shapes=[pltpu.VMEM((1, LANES), F32), pltpu.VMEM((tm, tm), BF16)],
        compiler_params=_params(("arbitrary",)),
        name="moe_router",
    )(x2d, g.reshape(1, D), w_router, b_router)


def _dispatch_kernel(p1_ref, p2_ref, xn_ref, xs_in_ref, xs_ref, sem, *, tb):
    del xs_in_ref
    base = pl.program_id(0) * tb

    def copies(t):
        src = xn_ref.at[pl.ds(t, 1), :]
        return (pltpu.make_async_copy(src, xs_ref.at[pl.ds(p1_ref[base + t], 1), :], sem),
                pltpu.make_async_copy(src, xs_ref.at[pl.ds(p2_ref[base + t], 1), :], sem))

    def issue(t, _):
        for cp in copies(t):
            cp.start()
        return 0

    lax.fori_loop(0, tb, issue, 0, unroll=DMA_UNROLL)

    def drain(t, _):
        for cp in copies(t):
            cp.wait()
        return 0

    lax.fori_loop(0, tb, drain, 0, unroll=DMA_UNROLL)


def _dispatch(pos1, pos2, xn, xs0, tb=512):
    n, D = xn.shape
    n_rows = xs0.shape[0]
    tb = min(tb, n)
    grid_spec = pltpu.PrefetchScalarGridSpec(
        num_scalar_prefetch=2,
        grid=(n // tb,),
        in_specs=[pl.BlockSpec((tb, D), lambda i, p1, p2: (i, 0)),
                  pl.BlockSpec(memory_space=pl.ANY)],
        out_specs=pl.BlockSpec(memory_space=pl.ANY),
        scratch_shapes=[pltpu.SemaphoreType.DMA(())],
    )
    return pl.pallas_call(
        functools.partial(_dispatch_kernel, tb=tb),
        grid_spec=grid_spec,
        out_shape=jax.ShapeDtypeStruct((n_rows, D), F32),
        input_output_aliases={3: 0},
        compiler_params=pltpu.CompilerParams(dimension_semantics=("arbitrary",), has_side_effects=True),
        name="moe_dispatch",
    )(pos1, pos2, xn, xs0)


def _expert_kernel(te_ref, nv_ref, ne_ref, grp_ref, xs_ref, wg_ref, wu_ref, wd_ref, ys_ref,
                   wgf, wuf, wdf, wgb, wub, wdb, sem, *, layer):
    i = pl.program_id(0)
    prev = te_ref[jnp.maximum(i - 1, 0)]

    def fetch(e, slot):
        return (pltpu.make_async_copy(wg_ref.at[layer, e], wgf.at[slot], sem.at[slot]),
                pltpu.make_async_copy(wu_ref.at[layer, e], wuf.at[slot], sem.at[slot]),
                pltpu.make_async_copy(wd_ref.at[layer, e], wdf.at[slot], sem.at[slot]))

    @pl.when(i < nv_ref[0])
    def _():
        @pl.when(i == 0)
        def _():
            for cp in fetch(te_ref[0], 0):
                cp.start()

        @pl.when((i == 0) | (te_ref[i] != prev))
        def _():
            slot = grp_ref[i] % 2
            for cp in fetch(te_ref[i], slot):
                cp.wait()

            @pl.when(ne_ref[i] >= 0)
            def _():
                for cp in fetch(ne_ref[i], 1 - slot):
                    cp.start(priority=1)

            wgb[...] = wgf[slot].astype(BF16)
            wub[...] = wuf[slot].astype(BF16)
            wdb[...] = wdf[slot].astype(BF16)

        x = xs_ref[...].astype(BF16)
        g = _dot(x, wgb[...])
        u = _dot(x, wub[...])
        hcur = (g * _sigmoid(g) * u).astype(BF16)
        ys_ref[...] = _dot(hcur, wdb[...])

    @pl.when(i >= nv_ref[0])
    def _():
        ys_ref[...] = jnp.zeros(ys_ref.shape, F32)


def _experts(tile_expert, n_valid, xs, w_gate, w_up, w_down, layer):
    n_rows = xs.shape[0]
    _, E, D, Fe = w_gate.shape
    tm = MOE_TILE
    n_tiles = n_rows // tm
    tiles = jnp.arange(n_tiles, dtype=I32)
    first = (tiles < n_valid[0]) & ((tiles == 0) | (tile_expert != jnp.roll(tile_expert, 1)))
    group = (jnp.cumsum(first.astype(I32)) - 1).astype(I32)
    later = lax.cummin(jnp.where(first, tiles, n_tiles), reverse=True)
    nxt = jnp.concatenate([later[1:], jnp.full((1,), n_tiles, I32)])
    next_expert = jnp.where(nxt < n_tiles, tile_expert[jnp.minimum(nxt, n_tiles - 1)], -1).astype(I32)

    def row_map(i, te, nv, ne, grp):
        return (jnp.minimum(i, jnp.maximum(nv[0] - 1, 0)), 0)

    grid_spec = pltpu.PrefetchScalarGridSpec(
        num_scalar_prefetch=4,
        grid=(n_tiles,),
        in_specs=[pl.BlockSpec((tm, D), row_map),
                  pl.BlockSpec(memory_space=pl.ANY), pl.BlockSpec(memory_space=pl.ANY),
                  pl.BlockSpec(memory_space=pl.ANY)],
        out_specs=pl.BlockSpec((tm, D), lambda i, te, nv, ne, grp: (i, 0)),
        scratch_shapes=[pltpu.VMEM((2, D, Fe), F32), pltpu.VMEM((2, D, Fe), F32), pltpu.VMEM((2, Fe, D), F32),
                        pltpu.VMEM((D, Fe), BF16), pltpu.VMEM((D, Fe), BF16), pltpu.VMEM((Fe, D), BF16),
                        pltpu.SemaphoreType.DMA((2,))],
    )
    return pl.pallas_call(
        functools.partial(_expert_kernel, layer=layer),
        grid_spec=grid_spec,
        out_shape=jax.ShapeDtypeStruct((n_rows, D), F32),
        compiler_params=_params(("arbitrary",)),
        name="moe_experts",
    )(tile_expert, n_valid, next_expert, group, xs, w_gate, w_up, w_down)


def _combine_kernel(p1_ref, p2_ref, ys_ref, x_ref, w_ref, o_ref, buf, sem):
    i = pl.program_id(0)
    tc = x_ref.shape[0]
    slot = i % 2

    def copies(tile, slot, t):
        tok = tile * tc + t
        return (pltpu.make_async_copy(ys_ref.at[pl.ds(p1_ref[tok], 1), :], buf.at[slot, 0, pl.ds(t, 1), :],
                                      sem.at[slot]),
                pltpu.make_async_copy(ys_ref.at[pl.ds(p2_ref[tok], 1), :], buf.at[slot, 1, pl.ds(t, 1), :],
                                      sem.at[slot]))

    def issue(tile, slot):
        def body(t, _):
            for cp in copies(tile, slot, t):
                cp.start()
            return 0

        lax.fori_loop(0, tc, body, 0, unroll=DMA_UNROLL)

    @pl.when(i == 0)
    def _():
        issue(0, 0)

    @pl.when(i + 1 < pl.num_programs(0))
    def _():
        issue(i + 1, 1 - slot)

    def drain(t, _):
        for cp in copies(i, slot, t):
            cp.wait()
        return 0

    lax.fori_loop(0, tc, drain, 0, unroll=DMA_UNROLL)
    w = w_ref[...]
    o_ref[...] = x_ref[...] + w[:, 0:1] * buf[slot, 0] + w[:, 1:2] * buf[slot, 1]


def _combine(pos1, pos2, ys, x2d, wts, tc=256):
    n, D = x2d.shape
    grid_spec = pltpu.PrefetchScalarGridSpec(
        num_scalar_prefetch=2,
        grid=(n // tc,),
        in_specs=[pl.BlockSpec(memory_space=pl.ANY),
                  pl.BlockSpec((tc, D), lambda i, p1, p2: (i, 0)),
                  pl.BlockSpec((tc, LANES), lambda i, p1, p2: (i, 0))],
        out_specs=pl.BlockSpec((tc, D), lambda i, p1, p2: (i, 0)),
        scratch_shapes=[pltpu.VMEM((2, 2, tc, D), F32), pltpu.SemaphoreType.DMA((2,))],
    )
    return pl.pallas_call(
        _combine_kernel,
        grid_spec=grid_spec,
        out_shape=jax.ShapeDtypeStruct((n, D), F32),
        compiler_params=_params(("arbitrary",)),
        name="moe_combine",
    )(pos1, pos2, ys, x2d, wts)


def _pos_kernel(ids_ref, cnt_ref, pos_ref):
    tm = ids_ref.shape[0]
    tiles = jnp.floor((cnt_ref[...].astype(F32) + (MOE_TILE - 1)) / MOE_TILE)
    r = lax.broadcasted_iota(I32, (LANES, LANES), 0)
    c = lax.broadcasted_iota(I32, (LANES, LANES), 1)
    before = jnp.where(r < c, 1.0, 0.0).astype(BF16)
    first_tile = _dot(jnp.broadcast_to(tiles, (8, LANES)).astype(BF16), before)
    offs = first_tile[0:1, :] * MOE_TILE
    ids = ids_ref[...].astype(F32)
    lane = lax.broadcasted_iota(I32, (tm, LANES), 1)
    lanef = lane.astype(F32)
    p1 = jnp.sum(jnp.where(lanef == ids[:, 0:1], offs, 0.0), axis=1, keepdims=True) + ids[:, 2:3]
    p2 = jnp.sum(jnp.where(lanef == ids[:, 1:2], offs, 0.0), axis=1, keepdims=True) + ids[:, 3:4]
    packed = jnp.where(lane == 0, p1, jnp.where(lane == 1, p2, 0.0))
    pos_ref[...] = packed.T[0:8, :].astype(I32)


def _positions(ids, counts, tm=512):
    n = ids.shape[0]
    assert MOE_TILE & (MOE_TILE - 1) == 0
    return pl.pallas_call(
        _pos_kernel,
        grid=(n // tm,),
        in_specs=[pl.BlockSpec((tm, LANES), lambda i: (i, 0)), pl.BlockSpec((1, LANES), lambda i: (0, 0))],
        out_specs=pl.BlockSpec((8, tm), lambda i: (0, i)),
        out_shape=jax.ShapeDtypeStruct((8, n), I32),
        compiler_params=_params(("parallel",)),
        name="moe_positions",
    )(ids, counts)


def _hier_moe(x2d, norm_g, w_rg, b_rg, w_re, b_re, w_gate, w_up, w_down, layer, xs_buf):
    n, D = x2d.shape
    tm = MOE_TILE
    pad = LANES - N_GROUPS - N_EXPERTS
    w_router = jnp.concatenate([w_rg, w_re, jnp.zeros((D, pad), F32)], axis=1)
    b_router = jnp.concatenate([b_rg, b_re.reshape(-1), jnp.zeros((pad,), F32)]).reshape(1, LANES)
    xn3, ids, wts, counts = _router(x2d, norm_g, w_router, b_router)
    cnt = counts[0, :N_EXPERTS]
    padded = ((cnt + tm - 1) // tm) * tm
    ends = jnp.cumsum(padded)
    n_tiles = xs_buf.shape[0] // tm
    tile_start = jnp.arange(n_tiles, dtype=I32) * tm
    tile_expert = jnp.minimum(jnp.sum(tile_start[:, None] >= ends[None, :], axis=1), N_EXPERTS - 1).astype(I32)
    n_valid = (ends[-1] // tm).astype(I32).reshape(1)
    last_e = tile_expert[jnp.maximum(n_valid[0] - 1, 0)]
    tile_expert = jnp.where(jnp.arange(n_tiles) < n_valid[0], tile_expert, last_e)
    pos = _positions(ids, counts)
    pos1, pos2 = pos[0], pos[1]
    xs = _dispatch(pos1, pos2, xn3, xs_buf)
    ys = _experts(tile_expert, n_valid, xs, w_gate, w_up, w_down, layer)
    return _combine(pos1, pos2, ys, x2d, wts), xs


def _in_proj_regions(D):
    bw = BRANCH_WIDTH
    sizes = (bw, bw, bw, bw, N_HEADS, N_HEADS, bw, bw, bw, IDX_HEADS * IDX_DIM, IDX_DIM, IDX_HEADS,
             bw, bw, bw, N_BRANCHES * D)
    offs = np.concatenate([[0], np.cumsum(sizes)]).tolist()
    region_a = (offs[0], offs[4] - offs[0])
    region_b = (offs[6], offs[10] - offs[6])
    region_c = (offs[12], offs[16] - offs[12])
    small = dict(mi=offs[4], mf=offs[5], ik=offs[10], iw=offs[11])
    return region_a, region_b, region_c, small


def _token_mixer(x2d, B, S, layer, norm_g, w_in, conv_w, b_i, b_f, mlstm_norm_g, q_norm_g, k_norm_g,
                 w_branch, w_out, rel_bias):
    n, D = x2d.shape
    L = MLSTM_CHUNK
    H = N_HEADS
    tm = min(n, PROJ_TM)
    xn = _rmsnorm(x2d, norm_g)
    ra, rb, rc, small = _in_proj_regions(D)
    w_t = jnp.swapaxes(w_in, 1, 2)
    proj_a = _in_proj(xn, w_t, layer, ra[0], ra[1], tm)
    proj_b = _in_proj(xn, w_t, layer, rb[0], rb[1], tm)
    proj_c = _in_proj(xn, w_t, layer, rc[0], rc[1], tm)
    tail = _tail_proj(xn, w_t, layer, small, tm)
    g = tail[1][:, IDX_HEADS:IDX_HEADS + 2 * H].reshape(B, S // L, L, 2, H)
    gates_t = jnp.transpose(g, (0, 4, 1, 3, 2))
    hm = _mlstm(proj_a, gates_t, jnp.stack([b_i, b_f]), conv_w, mlstm_norm_g, B, S)
    hd = _dsa(proj_b, tail, q_norm_g, k_norm_g, rel_bias, B, S)
    hs = _stick_breaking(proj_c, B, S)
    merged = _merge(hm, hd, hs, proj_c, w_branch, layer)
    return _matmul_residual(merged, w_out, layer, x2d)


def kernel(x, norm1_g, w_in, conv_w, b_i, b_f, mlstm_norm_g, q_norm_g, k_norm_g, w_branch, w_out, norm2_g,
           w_router_g, b_router_g, w_router_e, b_router_e, w_gate, w_up, w_down, rel_bias):
    B, S, D = x.shape
    x2d = x.reshape(B * S, D)
    xs_buf = jnp.zeros((2 * B * S + N_EXPERTS * MOE_TILE, D), F32)
    for l in range(w_in.shape[0]):
        x2d = _token_mixer(x2d, B, S, l, norm1_g[l], w_in, conv_w[l], b_i[l], b_f[l], mlstm_norm_g[l],
                           q_norm_g[l], k_norm_g[l], w_branch, w_out, rel_bias)
        x2d, xs_buf = _hier_moe(x2d, norm2_g[l], w_router_g[l], b_router_g[l], w_router_e[l], b_router_e[l],
                                w_gate, w_up, w_down, l, xs_buf)
    return x2d.reshape(B, S, D)
```

```python
import functools
import math

import numpy as np
import jax
import jax.numpy as jnp
from jax import lax
from jax.experimental import pallas as pl
from jax.experimental.pallas import tpu as pltpu

F32 = jnp.float32
BF16 = jnp.bfloat16
I32 = jnp.int32

LANES = 128
HEAD_DIM = 128
N_HEADS = 8
BRANCH_WIDTH = N_HEADS * HEAD_DIM
N_BRANCHES = 3
CONV_WIDTH = 4
IDX_HEADS = 16
IDX_DIM = 64
TOPK_MAX = 256
N_BUCKETS = 32
MAX_DISTANCE = 128
N_GROUPS = 4
EXPERTS_PER_GROUP = 8
N_EXPERTS = N_GROUPS * EXPERTS_PER_GROUP
EPS = 1e-6
NEG = -1e30
INT_MIN = -(2 ** 31)

MLSTM_CHUNK = 256
DSA_BLOCK = 128
DSA_BANDS = 4
DSA_QBLOCKS = 2
SB_BLOCK = 512
SB_ROWS = 512
SB_PIECES = 1
SB_HEADS = 4
HEADS_PER_STEP = 4
MOE_TILE = 256
DMA_UNROLL = 8
PROJ_TN = 512
PROJ_TM = 2048
VMEM_LIMIT = 56 * 1024 * 1024

A_MQ, A_MK, A_MV, A_MO = 0, 8, 16, 24
B_DQ, B_DK, B_DV, B_IQ = 0, 8, 16, 24
C_SQ, C_SK, C_SV, C_GATE = 0, 8, 16, 24


def _params(sem):
    return pltpu.CompilerParams(dimension_semantics=sem, vmem_limit_bytes=VMEM_LIMIT)


def _dot(a, b):
    return jnp.dot(a, b, preferred_element_type=F32)


def _dot_nt(a, b):
    return lax.dot_general(a, b, (((1,), (1,)), ((), ())), preferred_element_type=F32)


def _sigmoid(z):
    return 1.0 / (1.0 + jnp.exp(-z))


def _rmsnorm_kernel(x_ref, g_ref, o_ref):
    x = x_ref[...]
    ms = jnp.mean(x * x, axis=-1, keepdims=True)
    o_ref[...] = (x * lax.rsqrt(ms + EPS) * g_ref[...]).astype(o_ref.dtype)


def _rmsnorm(x2d, g, tm=512):
    n, d = x2d.shape
    return pl.pallas_call(
        _rmsnorm_kernel,
        grid=(n // tm,),
        in_specs=[pl.BlockSpec((tm, d), lambda i: (i, 0)), pl.BlockSpec((1, d), lambda i: (0, 0))],
        out_specs=pl.BlockSpec((tm, d), lambda i: (i, 0)),
        out_shape=jax.ShapeDtypeStruct((n, d), BF16),
        compiler_params=_params(("parallel",)),
        name="rmsnorm",
    )(x2d, g.reshape(1, d))


def _in_proj_kernel(x_ref, *rest, shift, nblk):
    w_refs, o_ref, wb = rest[:nblk], rest[nblk], rest[nblk + 1]
    tn = wb.shape[1]

    @pl.when(pl.program_id(1) == 0)
    def _():
        w = jnp.concatenate([r[...] for r in w_refs], axis=0)
        wb[...] = w[shift:shift + tn, :].T.astype(BF16)

    acc = _dot(x_ref[...], wb[...])
    for j in range(o_ref.shape[0]):
        o_ref[j] = acc[:, j * LANES:(j + 1) * LANES].astype(o_ref.dtype)


def _in_proj(xn, w_t, layer, col0, ncols, tm):
    m, k = xn.shape
    tn = PROJ_TN
    base, shift = divmod(col0, LANES)
    assert shift % 8 == 0
    nblk = tn // LANES + (1 if shift else 0)
    per = tn // LANES

    def wspec(r):
        return pl.BlockSpec((None, LANES, k), lambda j, i: (layer, base + per * j + r, 0))

    return pl.pallas_call(
        functools.partial(_in_proj_kernel, shift=shift, nblk=nblk),
        grid=(ncols // tn, m // tm),
        in_specs=[pl.BlockSpec((tm, k), lambda j, i: (i, 0))] + [wspec(r) for r in range(nblk)],
        out_specs=pl.BlockSpec((per, tm, LANES), lambda j, i: (j, i, 0)),
        out_shape=jax.ShapeDtypeStruct((ncols // LANES, m, LANES), BF16),
        scratch_shapes=[pltpu.VMEM((k, tn), BF16)],
        compiler_params=_params(("parallel", "arbitrary")),
        name="in_proj",
    )(xn, *([w_t] * nblk))


def _tail_kernel(x_ref, wg_ref, wi_ref, o_ref, wt, *, g_lane, ik_lane, iw_lane):
    @pl.when(pl.program_id(0) == 0)
    def _():
        wg = wg_ref[...]
        wi = wi_ref[...]
        ik = wi[ik_lane:ik_lane + IDX_DIM, :]
        iw = wi[iw_lane:iw_lane + IDX_HEADS, :]
        gates = wg[g_lane:g_lane + 2 * N_HEADS, :]
        pad = jnp.zeros((LANES - IDX_HEADS - 2 * N_HEADS, wg.shape[1]), F32)
        wt[...] = jnp.concatenate([ik, ik, iw, gates, pad], axis=0).T.astype(BF16)

    acc = _dot(x_ref[...], wt[...])
    o_ref[0] = acc[:, :LANES]
    o_ref[1] = acc[:, LANES:]


def _tail_proj(xn, w_t, layer, small, tm):
    m, k = xn.shape
    g_blk, g_lane = divmod(small["mi"], LANES)
    i_blk, ik_lane = divmod(small["ik"], LANES)
    iw_lane = small["iw"] - i_blk * LANES
    assert small["mf"] == small["mi"] + N_HEADS and g_lane + 2 * N_HEADS <= LANES
    assert ik_lane + IDX_DIM <= LANES and 0 <= iw_lane and iw_lane + IDX_HEADS <= LANES
    assert g_lane % 8 == 0 and ik_lane % 8 == 0 and iw_lane % 8 == 0
    return pl.pallas_call(
        functools.partial(_tail_kernel, g_lane=g_lane, ik_lane=ik_lane, iw_lane=iw_lane),
        grid=(m // tm,),
        in_specs=[pl.BlockSpec((tm, k), lambda i: (i, 0)),
                  pl.BlockSpec((None, LANES, k), lambda i: (layer, g_blk, 0)),
                  pl.BlockSpec((None, LANES, k), lambda i: (layer, i_blk, 0))],
        out_specs=pl.BlockSpec((2, tm, LANES), lambda i: (0, i, 0)),
        out_shape=jax.ShapeDtypeStruct((2, m, LANES), F32),
        scratch_shapes=[pltpu.VMEM((k, 2 * LANES), BF16)],
        compiler_params=_params(("arbitrary",)),
        name="tail_proj",
    )(xn, w_t, w_t)


def _mlstm_kernel(bias_ref, q_ref, k_ref, v_ref, o_ref, g_ref, cwq_ref, cwk_ref, ng_ref, out_ref,
                  qf, kf, qc, kc, st):
    hp = pl.program_id(1)
    U = HEADS_PER_STEP
    S = q_ref.shape[1]
    L = MLSTM_CHUNK
    nc = S // L
    d = HEAD_DIM
    PAD = 8

    R = min(S, 256)
    for u in range(U):
        qf[u, 0:PAD, :] = jnp.zeros((PAD, d), F32)
        kf[u, 0:PAD, :] = jnp.zeros((PAD, d), F32)
        qf[u, PAD:PAD + S, :] = q_ref[u].astype(F32)
        kf[u, PAD:PAD + S, :] = k_ref[u].astype(F32)
        ls = slice(u * d, (u + 1) * d)
        for r0 in range(0, S, R):
            aq = jnp.zeros((R, d), F32)
            ak = jnp.zeros((R, d), F32)
            for t in range(CONV_WIDTH):
                off = PAD - (CONV_WIDTH - 1) + t + r0
                aq = aq + cwq_ref[t:t + 1, ls] * qf[u, off:off + R, :]
                ak = ak + cwk_ref[t:t + 1, ls] * kf[u, off:off + R, :]
            qc[u, r0:r0 + R, :] = (aq * _sigmoid(aq) * (d ** -0.5)).astype(BF16)
            kc[u, r0:r0 + R, :] = ak * _sigmoid(ak)

    st[...] = jnp.zeros(st.shape, F32)
    row = lax.broadcasted_iota(I32, (L, L), 0)
    col = lax.broadcasted_iota(I32, (L, L), 1)
    causal = col <= row
    eye = col == row
    lane = lax.broadcasted_iota(I32, (L, d), 1)
    ones_col = jnp.where(lane == 0, 1.0, 0.0).astype(BF16)
    ng = ng_ref[...]

    def chunk_one(u, c, r, m):
        q = qc[u, pl.ds(r, L), :]
        kT = kc[u, pl.ds(r, L), :].T
        v = v_ref[u, pl.ds(r, L), :]
        vaug = jnp.concatenate([v, ones_col], axis=1)
        gates = g_ref[u, c]
        i_row = gates[0:1, :] + bias_ref[0, hp * U + u]
        f_row = gates[1:2, :] + bias_ref[1, hp * U + u]
        lf_row = jnp.minimum(f_row, 0.0) - jnp.log1p(jnp.exp(-jnp.abs(f_row)))
        b_col = jnp.sum(jnp.where(causal, lf_row, 0.0), axis=1, keepdims=True)
        b_row = jnp.sum(jnp.where(eye, b_col, 0.0), axis=0, keepdims=True)
        dlog = jnp.where(causal, b_col - b_row + i_row, NEG)
        inter = b_col + m
        m_t = jnp.maximum(inter, jnp.max(dlog, axis=1, keepdims=True))
        w_intra = jnp.exp(dlog - m_t)
        w_inter = jnp.exp(inter - m_t)
        s = _dot(q, kT.astype(BF16)) * w_intra
        res = w_inter * _dot(q, st[u].astype(BF16)) + _dot(s.astype(BF16), vaug)
        num = res[:, :d]
        den = res[:, d:d + 1]
        hh = num / jnp.maximum(jnp.abs(den), jnp.exp(-m_t))
        hn = hh * lax.rsqrt(jnp.mean(hh * hh, axis=1, keepdims=True) + EPS) * ng[:, u * d:(u + 1) * d]
        og = o_ref[u, pl.ds(r, L), :].astype(F32)
        out_ref[u, pl.ds(r, L), :] = (hn * _sigmoid(og)).astype(out_ref.dtype)
        ws_row = w_intra[L - 1:L, :]
        decay = w_inter[L - 1:L, :]
        st[u] = decay * st[u] + _dot((kT * ws_row).astype(BF16), vaug)
        return m_t[L - 1:L, :]

    def chunk(c, ms):
        r = pl.multiple_of(c * L, L)
        return tuple(chunk_one(u, c, r, ms[u]) for u in range(U))

    lax.fori_loop(0, nc, chunk, tuple(jnp.zeros((1, 1), F32) for _ in range(U)))


def _mlstm(proj_a, gates_t, bias_if, conv_w, norm_g, B, S):
    L = MLSTM_CHUNK
    nc = S // L
    d = HEAD_DIM
    H = N_HEADS
    U = HEADS_PER_STEP

    def slab(off):
        return pl.BlockSpec((U, S, d), lambda b, h: (off // U + h, b, 0))

    return pl.pallas_call(
        _mlstm_kernel,
        grid=(B, H // U),
        in_specs=[
            pl.BlockSpec(memory_space=pltpu.SMEM),
            slab(A_MQ), slab(A_MK), slab(A_MV), slab(A_MO),
            pl.BlockSpec((None, U, nc, 2, L), lambda b, h: (b, h, 0, 0, 0)),
            pl.BlockSpec((CONV_WIDTH, U * d), lambda b, h: (0, h)),
            pl.BlockSpec((CONV_WIDTH, U * d), lambda b, h: (0, H // U + h)),
            pl.BlockSpec((1, U * d), lambda b, h: (0, h)),
        ],
        out_specs=pl.BlockSpec((U, S, d), lambda b, h: (h, b, 0)),
        out_shape=jax.ShapeDtypeStruct((H, B * S, d), BF16),
        scratch_shapes=[
            pltpu.VMEM((U, S + 8, d), F32), pltpu.VMEM((U, S + 8, d), F32),
            pltpu.VMEM((U, S, d), BF16), pltpu.VMEM((U, S, d), F32),
            pltpu.VMEM((U, d, 2 * d), F32),
        ],
        compiler_params=_params(("parallel", "parallel")),
        name="mlstm",
    )(bias_if, proj_a, proj_a, proj_a, proj_a, gates_t, conv_w, conv_w, norm_g.reshape(1, H * d))


def _t5_thresholds():
    max_exact = N_BUCKETS // 2
    n = np.arange(0, 2 * MAX_DISTANCE)
    nf = np.maximum(n, 1).astype(np.float64)
    val = np.log(nf / max_exact) / math.log(MAX_DISTANCE / max_exact) * (N_BUCKETS - max_exact)
    frac = np.abs(val - np.round(val))
    frac_ok = (frac > 1e-4) | (n <= max_exact) | (n >= MAX_DISTANCE)
    assert frac_ok.all()
    large = np.minimum(max_exact + np.trunc(val).astype(np.int64), N_BUCKETS - 1)
    bucket = np.where(n < max_exact, n, large)
    assert (np.diff(bucket) >= 0).all() and bucket[MAX_DISTANCE] == N_BUCKETS - 1
    return [int(np.argmax(bucket >= j)) for j in range(1, N_BUCKETS)]


_T5_THR = _t5_thresholds()


def _dsa_kernel(rb_ref, q_ref, k_ref, v_ref, iq_ref, ik_ref, wt_ref, gq_ref, gk_ref, prev_ref, out_ref,
                kn, ikk, wb, key_ref, mask_ref, bias_ref, *, topk, q0, nb, nqb):
    del prev_ref
    qis = [q0 + pl.program_id(1) * nqb + qb for qb in range(nqb)]
    T = DSA_BLOCK
    d = HEAD_DIM
    H = N_HEADS
    HALF = T // 2
    row = lax.broadcasted_iota(I32, (T, T), 0)
    col = lax.broadcasted_iota(I32, (T, T), 1)

    @pl.when(pl.program_id(1) == 0)
    def _per_batch():
        gk = gk_ref[...]

        def norm_k(h, _):
            kh = k_ref[h].astype(F32)
            kn[h] = (kh * lax.rsqrt(jnp.mean(kh * kh, axis=1, keepdims=True) + EPS) * gk).astype(BF16)
            for o in range(2):
                n = o * T + row - col
                val = jnp.full((T, T), rb_ref[0, h], F32)
                for j, thr in enumerate(_T5_THR):
                    val = jnp.where(n >= thr, rb_ref[j + 1, h], val)
                bias_ref[h, o] = val
            bias_ref[h, 2] = jnp.full((T, T), rb_ref[N_BUCKETS - 1, h], F32)
            return 0

        lax.fori_loop(0, H, norm_k, 0)
        ik = ik_ref[...]
        klane = lax.broadcasted_iota(I32, ik.shape, 1)
        ikk[0] = jnp.where(klane < IDX_DIM, ik, 0.0).astype(BF16)
        ikk[1] = jnp.where(klane >= IDX_DIM, ik, 0.0).astype(BF16)

    wsc = wt_ref[...] * (IDX_HEADS ** -0.5 * IDX_DIM ** -0.5)
    for qb in range(nqb):
        for h16 in range(IDX_HEADS):
            wb[qb * IDX_HEADS + h16] = jnp.broadcast_to(wsc[qb * T:(qb + 1) * T, h16:h16 + 1], (T, LANES))
    q_pairs = iq_ref[...].reshape((IDX_HEADS // 2) * nqb * T, LANES)

    def score_block(j, _):
        r = pl.multiple_of(j * T, T)
        kk = jnp.concatenate([ikk[0, pl.ds(r, T), :], ikk[1, pl.ds(r, T), :]], axis=0)
        dots = _dot_nt(q_pairs, kk)
        for qb in range(nqb):
            sc = jnp.zeros((T, T), F32)
            for h16 in range(IDX_HEADS):
                hp, odd = divmod(h16, 2)
                r0 = (hp * nqb + qb) * T
                sc = sc + wb[qb * IDX_HEADS + h16] * jnp.maximum(dots[r0:r0 + T, odd * T:(odd + 1) * T], 0.0)
            bits = lax.bitcast_convert_type(sc, I32)
            key = jnp.where(bits < 0, bits ^ jnp.int32(0x7FFFFFFF), bits)
            key = jnp.where(sc == 0.0, 0, key)
            key = jnp.where(j * T + col <= qis[qb] * T + row, key, INT_MIN)
            key_ref[qb, j] = key
        return 0

    lax.fori_loop(0, nb, score_block, 0, unroll=2)

    kf = float(topk)
    ones = jnp.ones((LANES, LANES), BF16)

    def count(chain, pred):
        qb, lo = chain
        acc = jnp.zeros((HALF, LANES), F32)
        for jb in range(nb):
            acc = acc + jnp.where(pred(key_ref[qb, jb, lo:lo + HALF, :]), 1.0, 0.0)
        return _dot(acc.astype(BF16), ones)

    chains = [(qb, lo) for qb in range(nqb) for lo in (0, HALF)]
    zero_i = jnp.zeros((HALF, LANES), I32)
    thr0 = tuple(jnp.where(count(ch, lambda k: k >= zero_i) >= kf, jnp.int32(0), jnp.int32(INT_MIN))
                 for ch in chains)

    def bisect_bit(thrs, bit):
        out = []
        for ch, t in zip(chains, thrs):
            cand = t | bit
            out.append(jnp.where(count(ch, lambda k, c=cand: k >= c) >= kf, cand, t))
        return tuple(out)

    def bisect_pair(it, thrs):
        hi_bit = jnp.left_shift(jnp.int32(1), 29 - 2 * it)
        lo_bit = jnp.left_shift(jnp.int32(1), 28 - 2 * it)
        out = []
        for ch, t in zip(chains, thrs):
            c01, c10, c11 = t | lo_bit, t | hi_bit, t | hi_bit | lo_bit
            n01, n10, n11 = (count(ch, lambda k, c=c: k >= c) >= kf for c in (c01, c10, c11))
            out.append(jnp.where(n11, c11, jnp.where(n10, c10, jnp.where(n01, c01, t))))
        return tuple(out)

    thrs = bisect_bit(thr0, jnp.int32(1 << 30))
    thrs = lax.fori_loop(0, 15, bisect_pair, thrs)
    needs = [kf - count(ch, lambda k, t=t: k > t) for ch, t in zip(chains, thrs)]

    tri = jnp.where(row <= col, 1.0, 0.0).astype(BF16)
    tri_ones = jnp.concatenate([tri, ones], axis=1)
    for qb in range(nqb):
        need = jnp.concatenate(needs[2 * qb:2 * qb + 2], axis=0)
        thrb = jnp.concatenate(thrs[2 * qb:2 * qb + 2], axis=0)
        seen = jnp.zeros((T, LANES), F32)
        for jb in range(nb):
            key = key_ref[qb, jb]
            tie = key == thrb
            tie16 = jnp.where(tie, 1.0, 0.0).astype(BF16)
            counts = _dot(tie16, tri_ones)
            pre = counts[:, :T] + seen
            m = jnp.where(key > thrb, 0.0, jnp.where(tie, jnp.where(pre <= need, 0.0, NEG), NEG))
            mask_ref[qb, jb] = jnp.where(key == INT_MIN, NEG, m)
            seen = seen + counts[:, T:]

    gq = gq_ref[...]

    def head(h):
        qh = q_ref[h].astype(F32)
        qn = (qh * lax.rsqrt(jnp.mean(qh * qh, axis=1, keepdims=True) + EPS) * gq * (d ** -0.5)).astype(BF16)
        lgs = _dot_nt(qn, kn[h])
        p_rows, sums = [], []
        for qb in range(nqb):
            mx = None
            lg_blocks = []
            for jb in range(nb):
                lg = (lgs[qb * T:(qb + 1) * T, jb * T:(jb + 1) * T]
                      + bias_ref[h, jnp.clip(qis[qb] - jb, 0, 2)] + mask_ref[qb, jb])
                lg_blocks.append(lg)
                mx = lg if mx is None else jnp.maximum(mx, lg)
            rowmax = jnp.max(mx, axis=1, keepdims=True)
            l = jnp.zeros((T, T), F32)
            ps = []
            for jb in range(nb):
                p = jnp.exp(lg_blocks[jb] - rowmax)
                l = l + p
                ps.append(p.astype(BF16))
            p_rows.append(jnp.concatenate(ps, axis=1))
            sums.append(jnp.sum(l, axis=1, keepdims=True))
        acc = _dot(jnp.concatenate(p_rows, axis=0), v_ref[h])
        out_ref[h] = (acc / jnp.concatenate(sums, axis=0)).astype(out_ref.dtype)

    def head_group(g, _):
        for u in range(HEADS_PER_STEP):
            head(g * HEADS_PER_STEP + u)
        return 0

    lax.fori_loop(0, H // HEADS_PER_STEP, head_group, 0)


def _dsa(proj_b, tail, q_norm_g, k_norm_g, rel_bias, B, S):
    T = DSA_BLOCK
    nq = S // T
    d = HEAD_DIM
    H = N_HEADS
    topk = min(TOPK_MAX, S // 4)
    per = nq // DSA_BANDS
    nqb = math.gcd(per, DSA_QBLOCKS)
    TQ = nqb * T
    pb = proj_b.reshape(proj_b.shape[0], B, S, d)
    tl = tail.reshape(tail.shape[0], B, S, LANES)
    out = jnp.zeros((H, B, S, d), BF16)
    for band in range(DSA_BANDS):
        q0 = band * per
        nb = q0 + per
        W = nb * T
        in_specs = [
            pl.BlockSpec(memory_space=pltpu.SMEM),
            pl.BlockSpec((H, None, TQ, d), lambda b, i: (B_DQ // H, b, q0 // nqb + i, 0)),
            pl.BlockSpec((H, None, W, d), lambda b, i: (B_DK // H, b, 0, 0)),
            pl.BlockSpec((H, None, W, d), lambda b, i: (B_DV // H, b, 0, 0)),
            pl.BlockSpec((H, None, TQ, d), lambda b, i: (B_IQ // H, b, q0 // nqb + i, 0)),
            pl.BlockSpec((None, None, W, LANES), lambda b, i: (0, b, 0, 0)),
            pl.BlockSpec((None, None, TQ, LANES), lambda b, i: (1, b, q0 // nqb + i, 0)),
            pl.BlockSpec((1, d), lambda b, i: (0, 0)),
            pl.BlockSpec((1, d), lambda b, i: (0, 0)),
            pl.BlockSpec(memory_space=pl.ANY),
        ]
        args = [rel_bias, pb, pb, pb, pb, tl, tl, q_norm_g.reshape(1, d), k_norm_g.reshape(1, d), out]
        out = pl.pallas_call(
            functools.partial(_dsa_kernel, topk=topk, q0=q0, nb=nb, nqb=nqb),
            grid=(B, per // nqb),
            in_specs=in_specs,
            out_specs=pl.BlockSpec((H, None, TQ, d), lambda b, i: (0, b, q0 // nqb + i, 0)),
            out_shape=jax.ShapeDtypeStruct((H, B, S, d), BF16),
            scratch_shapes=[
                pltpu.VMEM((H, W, d), BF16),
                pltpu.VMEM((2, W, LANES), BF16),
                pltpu.VMEM((nqb * IDX_HEADS, T, LANES), F32),
                pltpu.VMEM((nqb, nb, T, T), I32),
                pltpu.VMEM((nqb, nb, T, T), F32),
                pltpu.VMEM((H, 3, T, T), F32),
            ],
            input_output_aliases={len(args) - 1: 0},
            compiler_params=_params(("parallel", "arbitrary")),
            name=f"dsa_band{band}",
        )(*args)
    return out.reshape(H, B * S, d)


def _sb_kernel(q_ref, k_ref, v_ref, out_ref):
    qi = pl.program_id(2)
    U = SB_HEADS
    T = SB_BLOCK
    d = HEAD_DIM
    R = SB_ROWS
    ns = T // R
    row = lax.broadcasted_iota(I32, (T, T), 0)
    col = lax.broadcasted_iota(I32, (T, T), 1)
    upper = jnp.where(row > col, 1.0, 0.0).astype(BF16)
    srow = lax.broadcasted_iota(I32, (R, T), 0)
    scol = lax.broadcasted_iota(I32, (R, T), 1)
    units = [(u, s) for u in range(U) for s in range(ns)]
    qs = [(q_ref[u, s * R:(s + 1) * R, :].astype(F32) * (d ** -0.5)).astype(BF16) for u, s in units]

    def block(n, j, diag, run, acc):
        u, s = units[n]
        r = pl.multiple_of(j * T, T)
        z = _dot_nt(qs[n], k_ref[u, pl.ds(r, T), :])
        sp = jnp.maximum(z, 0.0) + jnp.log(1.0 + jnp.exp(-jnp.abs(z)))
        strict = scol < srow + s * R
        ln = jnp.where(strict, -sp, 0.0) if diag else -sp
        parts = [ln.astype(BF16)]
        for _ in range(SB_PIECES - 1):
            parts.append((ln - sum(p.astype(F32) for p in parts)).astype(BF16))
        sums = _dot(jnp.concatenate(parts, axis=0), upper)
        suf = sum(sums[p * R:(p + 1) * R] for p in range(SB_PIECES))
        a = jnp.exp(z - sp + suf + run)
        if diag:
            a = jnp.where(strict, a, 0.0)
        acc = acc + _dot(a.astype(BF16), v_ref[u, pl.ds(r, T), :])
        return run + suf[:, 0:1] + ln[:, 0:1], acc

    state = []
    for n in range(len(units)):
        state.extend(block(n, qi, True, jnp.zeros((R, 1), F32), jnp.zeros((R, d), F32)))

    def earlier(i, st):
        out = []
        for n in range(len(units)):
            out.extend(block(n, qi - i, False, st[2 * n], st[2 * n + 1]))
        return tuple(out)

    state = lax.fori_loop(1, qi + 1, earlier, tuple(state))
    for n, (u, s) in enumerate(units):
        out_ref[u, s * R:(s + 1) * R, :] = state[2 * n + 1].astype(out_ref.dtype)


def _stick_breaking(proj_c, B, S):
    T = SB_BLOCK
    nq = S // T
    d = HEAD_DIM
    H = N_HEADS
    U = SB_HEADS
    return pl.pallas_call(
        _sb_kernel,
        grid=(B, H // U, nq),
        in_specs=[
            pl.BlockSpec((U, T, d), lambda b, h, i: (C_SQ // U + h, b * nq + i, 0)),
            pl.BlockSpec((U, S, d), lambda b, h, i: (C_SK // U + h, b, 0)),
            pl.BlockSpec((U, S, d), lambda b, h, i: (C_SV // U + h, b, 0)),
        ],
        out_specs=pl.BlockSpec((U, T, d), lambda b, h, i: (h, b * nq + i, 0)),
        out_shape=jax.ShapeDtypeStruct((H, B * S, d), BF16),
        compiler_params=_params(("parallel", "parallel", "parallel")),
        name="stick_breaking",
    )(proj_c, proj_c, proj_c)


def _merge_kernel(hm_ref, hd_ref, hs_ref, g0_ref, g1_ref, g2_ref, w_ref, o_ref, wb):
    H = N_HEADS

    @pl.when(pl.program_id(1) == 0)
    def _():
        wb[...] = w_ref[...].astype(BF16)

    acc = None
    ns = o_ref.shape[1] // LANES
    for n, (br, gr) in enumerate(((hm_ref, g0_ref), (hd_ref, g1_ref), (hs_ref, g2_ref))):
        a = jnp.concatenate([br[h] for h in range(H)], axis=1)
        up = _dot(a, wb[n])
        gate = jnp.concatenate([gr[s] for s in range(ns)], axis=1).astype(F32)
        term = _sigmoid(gate) * up
        acc = term if acc is None else acc + term
    o_ref[...] = acc.astype(o_ref.dtype)


def _merge(hm, hd, hs, proj_c, w_branch, layer, tm=1024, tn=512):
    H, n, d = hm.shape
    tm = min(tm, n)
    D = w_branch.shape[3]
    ns = tn // LANES
    br = pl.BlockSpec((H, tm, d), lambda j, i: (0, i, 0))

    def gate(b):
        first = (C_GATE + b * (D // LANES)) // ns
        return pl.BlockSpec((ns, tm, LANES), lambda j, i: (first + j, i, 0))

    return pl.pallas_call(
        _merge_kernel,
        grid=(D // tn, n // tm),
        in_specs=[br, br, br, gate(0), gate(1), gate(2),
                  pl.BlockSpec((None, N_BRANCHES, H * d, tn), lambda j, i: (layer, 0, 0, j))],
        out_specs=pl.BlockSpec((tm, tn), lambda j, i: (i, j)),
        out_shape=jax.ShapeDtypeStruct((n, D), BF16),
        scratch_shapes=[pltpu.VMEM((N_BRANCHES, H * d, tn), BF16)],
        compiler_params=_params(("parallel", "arbitrary")),
        name="branch_merge",
    )(hm, hd, hs, proj_c, proj_c, proj_c, w_branch)


def _mm_res_kernel(a_ref, w_ref, x_ref, o_ref, wb):
    @pl.when(pl.program_id(1) == 0)
    def _():
        wb[...] = w_ref[...].astype(BF16)

    o_ref[...] = x_ref[...] + _dot(a_ref[...], wb[...])


def _matmul_residual(a, w, layer, x, tm=1024, tn=512):
    m, k = a.shape
    tm = min(tm, m)
    n = w.shape[2]
    return pl.pallas_call(
        _mm_res_kernel,
        grid=(n // tn, m // tm),
        in_specs=[pl.BlockSpec((tm, k), lambda j, i: (i, 0)),
                  pl.BlockSpec((None, k, tn), lambda j, i: (layer, 0, j)),
                  pl.BlockSpec((tm, tn), lambda j, i: (i, j))],
        out_specs=pl.BlockSpec((tm, tn), lambda j, i: (i, j)),
        out_shape=jax.ShapeDtypeStruct((m, n), F32),
        scratch_shapes=[pltpu.VMEM((k, tn), BF16)],
        compiler_params=_params(("parallel", "arbitrary")),
        name="out_proj",
    )(a, w, x)


def _router_kernel(x_ref, g_ref, wr_ref, br_ref, xn_ref, ids_ref, wts_ref, cnt_ref, carry, lower):
    i = pl.program_id(0)
    tm, D = x_ref.shape

    @pl.when(i == 0)
    def _():
        carry[...] = jnp.zeros(carry.shape, F32)
        r = lax.broadcasted_iota(I32, (tm, tm), 0)
        c = lax.broadcasted_iota(I32, (tm, tm), 1)
        lower[...] = jnp.where(c < r, 1.0, 0.0).astype(BF16)

    x = x_ref[...]
    xn = x * lax.rsqrt(jnp.mean(x * x, axis=-1, keepdims=True) + EPS) * g_ref[...]
    xn_ref[...] = xn
    logits = jnp.dot(xn, wr_ref[...], preferred_element_type=F32, precision=lax.Precision.HIGHEST)
    biased = logits + br_ref[...]
    lane = lax.broadcasted_iota(I32, (tm, LANES), 1)
    lanef = lane.astype(F32)
    big = float(LANES)

    def first_lane(mask):
        return jnp.min(jnp.where(mask, lanef, big), axis=1, keepdims=True)

    gmask = lane < N_GROUPS
    gmax = jnp.max(jnp.where(gmask, biased, NEG), axis=1, keepdims=True)
    g_sel = first_lane(gmask & (biased == gmax))
    gm = jnp.max(jnp.where(gmask, logits, NEG), axis=1, keepdims=True)
    ge = jnp.where(gmask, jnp.exp(logits - gm), 0.0)
    p_group = jnp.sum(jnp.where(lanef == g_sel, ge, 0.0), axis=1, keepdims=True) / jnp.sum(ge, axis=1, keepdims=True)
    lo = N_GROUPS + EXPERTS_PER_GROUP * g_sel
    emask = (lanef >= lo) & (lanef < lo + EXPERTS_PER_GROUP)
    eb = jnp.where(emask, biased, NEG)
    e1 = first_lane(emask & (eb == jnp.max(eb, axis=1, keepdims=True)))
    emask2 = emask & (lanef != e1)
    eb2 = jnp.where(emask2, biased, NEG)
    e2 = first_lane(emask2 & (eb2 == jnp.max(eb2, axis=1, keepdims=True)))
    em = jnp.max(jnp.where(emask, logits, NEG), axis=1, keepdims=True)
    ee = jnp.where(emask, jnp.exp(logits - em), 0.0)
    s1 = jnp.sum(jnp.where(lanef == e1, ee, 0.0), axis=1, keepdims=True)
    s2 = jnp.sum(jnp.where(lanef == e2, ee, 0.0), axis=1, keepdims=True)
    se = jnp.sum(ee, axis=1, keepdims=True)
    w1 = s1 / se
    w2 = s2 / se
    wsum = w1 + w2
    w1 = p_group * w1 / wsum
    w2 = p_group * w2 / wsum
    x1 = e1 - N_GROUPS
    x2 = e2 - N_GROUPS

    onehot = jnp.where((lanef == x1) | (lanef == x2), 1.0, 0.0)
    prefix = _dot(lower[...], onehot.astype(BF16)) + carry[...]
    r1 = jnp.sum(jnp.where(lanef == x1, prefix, 0.0), axis=1, keepdims=True)
    r2 = jnp.sum(jnp.where(lanef == x2, prefix, 0.0), axis=1, keepdims=True)
    carry[...] = carry[...] + jnp.sum(onehot, axis=0, keepdims=True)
    cnt_ref[...] = carry[...].astype(I32)

    idsf = jnp.where(lane == 0, x1, jnp.where(lane == 1, x2, jnp.where(lane == 2, r1, jnp.where(lane == 3, r2, 0.0))))
    ids_ref[...] = idsf.astype(I32)
    wts_ref[...] = jnp.where(lane == 0, w1, jnp.where(lane == 1, w2, 0.0))


def _router(x2d, g, w_router, b_router, tm=512):
    n, D = x2d.shape
    return pl.pallas_call(
        _router_kernel,
        grid=(n // tm,),
        in_specs=[pl.BlockSpec((tm, D), lambda i: (i, 0)), pl.BlockSpec((1, D), lambda i: (0, 0)),
                  pl.BlockSpec((D, LANES), lambda i: (0, 0)), pl.BlockSpec((1, LANES), lambda i: (0, 0))],
        out_specs=[pl.BlockSpec((tm, D), lambda i: (i, 0)),
                   pl.BlockSpec((tm, LANES), lambda i: (i, 0)),
                   pl.BlockSpec((tm, LANES), lambda i: (i, 0)),
                   pl.BlockSpec((1, LANES), lambda i: (0, 0))],
        out_shape=[jax.ShapeDtypeStruct((n, D), F32),
                   jax.ShapeDtypeStruct((n, LANES), I32),
                   jax.ShapeDtypeStruct((n, LANES), F32),
                   jax.ShapeDtypeStruct((1, LANES), I32)],
        scratch_shapes=[pltpu.VMEM((1, LANES), F32), pltpu.VMEM((tm, tm), BF16)],
        compiler_params=_params(("arbitrary",)),
        name="moe_router",
    )(x2d, g.reshape(1, D), w_router, b_router)


def _dispatch_kernel(p1_ref, p2_ref, xn_ref, xs_in_ref, xs_ref, sem, *, tb):
    del xs_in_ref
    base = pl.program_id(0) * tb

    def copies(t):
        src = xn_ref.at[pl.ds(t, 1), :]
        return (pltpu.make_async_copy(src, xs_ref.at[pl.ds(p1_ref[base + t], 1), :], sem),
                pltpu.make_async_copy(src, xs_ref.at[pl.ds(p2_ref[base + t], 1), :], sem))

    def issue(t, _):
        for cp in copies(t):
            cp.start()
        return 0

    lax.fori_loop(0, tb, issue, 0, unroll=DMA_UNROLL)

    def drain(t, _):
        for cp in copies(t):
            cp.wait()
        return 0

    lax.fori_loop(0, tb, drain, 0, unroll=DMA_UNROLL)


def _dispatch(pos1, pos2, xn, xs0, tb=512):
    n, D = xn.shape
    n_rows = xs0.shape[0]
    tb = min(tb, n)
    grid_spec = pltpu.PrefetchScalarGridSpec(
        num_scalar_prefetch=2,
        grid=(n // tb,),
        in_specs=[pl.BlockSpec((tb, D), lambda i, p1, p2: (i, 0)),
                  pl.BlockSpec(memory_space=pl.ANY)],
        out_specs=pl.BlockSpec(memory_space=pl.ANY),
        scratch_shapes=[pltpu.SemaphoreType.DMA(())],
    )
    return pl.pallas_call(
        functools.partial(_dispatch_kernel, tb=tb),
        grid_spec=grid_spec,
        out_shape=jax.ShapeDtypeStruct((n_rows, D), F32),
        input_output_aliases={3: 0},
        compiler_params=pltpu.CompilerParams(dimension_semantics=("arbitrary",), has_side_effects=True),
        name="moe_dispatch",
    )(pos1, pos2, xn, xs0)


def _expert_kernel(te_ref, nv_ref, ne_ref, grp_ref, xs_ref, wg_ref, wu_ref, wd_ref, ys_ref,
                   wgf, wuf, wdf, wgb, wub, wdb, sem, *, layer):
    i = pl.program_id(0)
    prev = te_ref[jnp.maximum(i - 1, 0)]

    def fetch(e, slot):
        cps = []
        for src, dst in ((wg_ref, wgf), (wu_ref, wuf), (wd_ref, wdf)):
            half = dst.shape[1] // 2
            for c in range(2):
                cps.append(pltpu.make_async_copy(src.at[layer, e, pl.ds(c * half, half), :],
                                                 dst.at[slot, pl.ds(c * half, half), :], sem.at[slot]))
        return cps

    def start_fetch(e, slot):
        for n, cp in enumerate(fetch(e, slot)):
            cp.start(priority=n % 2)

    @pl.when(i < nv_ref[0])
    def _():
        @pl.when(i == 0)
        def _():
            start_fetch(te_ref[0], 0)

        @pl.when((i == 0) | (te_ref[i] != prev))
        def _():
            slot = grp_ref[i] % 2
            for cp in fetch(te_ref[i], slot):
                cp.wait()

            @pl.when(ne_ref[i] >= 0)
            def _():
                start_fetch(ne_ref[i], 1 - slot)

            wgb[...] = wgf[slot].astype(BF16)
            wub[...] = wuf[slot].astype(BF16)
            wdb[...] = wdf[slot].astype(BF16)

        x = xs_ref[...].astype(BF16)
        g = _dot(x, wgb[...])
        u = _dot(x, wub[...])
        hcur = (g * _sigmoid(g) * u).astype(BF16)
        ys_ref[...] = _dot(hcur, wdb[...])

    @pl.when(i >= nv_ref[0])
    def _():
        ys_ref[...] = jnp.zeros(ys_ref.shape, F32)


def _experts(tile_expert, n_valid, xs, w_gate, w_up, w_down, layer):
    n_rows = xs.shape[0]
    _, E, D, Fe = w_gate.shape
    tm = MOE_TILE
    n_tiles = n_rows // tm
    tiles = jnp.arange(n_tiles, dtype=I32)
    first = (tiles < n_valid[0]) & ((tiles == 0) | (tile_expert != jnp.roll(tile_expert, 1)))
    group = (jnp.cumsum(first.astype(I32)) - 1).astype(I32)
    later = lax.cummin(jnp.where(first, tiles, n_tiles), reverse=True)
    nxt = jnp.concatenate([later[1:], jnp.full((1,), n_tiles, I32)])
    next_expert = jnp.where(nxt < n_tiles, tile_expert[jnp.minimum(nxt, n_tiles - 1)], -1).astype(I32)

    def row_map(i, te, nv, ne, grp):
        return (jnp.minimum(i, jnp.maximum(nv[0] - 1, 0)), 0)

    grid_spec = pltpu.PrefetchScalarGridSpec(
        num_scalar_prefetch=4,
        grid=(n_tiles,),
        in_specs=[pl.BlockSpec((tm, D), row_map),
                  pl.BlockSpec(memory_space=pl.ANY), pl.BlockSpec(memory_space=pl.ANY),
                  pl.BlockSpec(memory_space=pl.ANY)],
        out_specs=pl.BlockSpec((tm, D), lambda i, te, nv, ne, grp: (i, 0)),
        scratch_shapes=[pltpu.VMEM((2, D, Fe), F32), pltpu.VMEM((2, D, Fe), F32), pltpu.VMEM((2, Fe, D), F32),
                        pltpu.VMEM((D, Fe), BF16), pltpu.VMEM((D, Fe), BF16), pltpu.VMEM((Fe, D), BF16),
                        pltpu.SemaphoreType.DMA((2,))],
    )
    return pl.pallas_call(
        functools.partial(_expert_kernel, layer=layer),
        grid_spec=grid_spec,
        out_shape=jax.ShapeDtypeStruct((n_rows, D), F32),
        compiler_params=_params(("arbitrary",)),
        name="moe_experts",
    )(tile_expert, n_valid, next_expert, group, xs, w_gate, w_up, w_down)


def _combine_kernel(p1_ref, p2_ref, ys_ref, x_ref, w_ref, o_ref, buf, sem):
    i = pl.program_id(0)
    tc = x_ref.shape[0]
    slot = i % 2

    def copies(tile, slot, t):
        tok = tile * tc + t
        return (pltpu.make_async_copy(ys_ref.at[pl.ds(p1_ref[tok], 1), :], buf.at[slot, 0, pl.ds(t, 1), :],
                                      sem.at[slot]),
                pltpu.make_async_copy(ys_ref.at[pl.ds(p2_ref[tok], 1), :], buf.at[slot, 1, pl.ds(t, 1), :],
                                      sem.at[slot]))

    def issue(tile, slot):
        def body(t, _):
            for cp in copies(tile, slot, t):
                cp.start()
            return 0

        lax.fori_loop(0, tc, body, 0, unroll=DMA_UNROLL)

    @pl.when(i == 0)
    def _():
        issue(0, 0)

    @pl.when(i + 1 < pl.num_programs(0))
    def _():
        issue(i + 1, 1 - slot)

    def drain(t, _):
        for cp in copies(i, slot, t):
            cp.wait()
        return 0

    lax.fori_loop(0, tc, drain, 0, unroll=DMA_UNROLL)
    w = w_ref[...]
    o_ref[...] = x_ref[...] + w[:, 0:1] * buf[slot, 0] + w[:, 1:2] * buf[slot, 1]


def _combine(pos1, pos2, ys, x2d, wts, tc=256):
    n, D = x2d.shape
    grid_spec = pltpu.PrefetchScalarGridSpec(
        num_scalar_prefetch=2,
        grid=(n // tc,),
        in_specs=[pl.BlockSpec(memory_space=pl.ANY),
                  pl.BlockSpec((tc, D), lambda i, p1, p2: (i, 0)),
                  pl.BlockSpec((tc, LANES), lambda i, p1, p2: (i, 0))],
        out_specs=pl.BlockSpec((tc, D), lambda i, p1, p2: (i, 0)),
        scratch_shapes=[pltpu.VMEM((2, 2, tc, D), F32), pltpu.SemaphoreType.DMA((2,))],
    )
    return pl.pallas_call(
        _combine_kernel,
        grid_spec=grid_spec,
        out_shape=jax.ShapeDtypeStruct((n, D), F32),
        compiler_params=_params(("arbitrary",)),
        name="moe_combine",
    )(pos1, pos2, ys, x2d, wts)


def _pos_kernel(ids_ref, cnt_ref, pos_ref):
    tm = ids_ref.shape[0]
    tiles = jnp.floor((cnt_ref[...].astype(F32) + (MOE_TILE - 1)) / MOE_TILE)
    r = lax.broadcasted_iota(I32, (LANES, LANES), 0)
    c = lax.broadcasted_iota(I32, (LANES, LANES), 1)
    before = jnp.where(r < c, 1.0, 0.0).astype(BF16)
    first_tile = _dot(jnp.broadcast_to(tiles, (8, LANES)).astype(BF16), before)
    offs = first_tile[0:1, :] * MOE_TILE
    ids = ids_ref[...].astype(F32)
    lane = lax.broadcasted_iota(I32, (tm, LANES), 1)
    lanef = lane.astype(F32)
    p1 = jnp.sum(jnp.where(lanef == ids[:, 0:1], offs, 0.0), axis=1, keepdims=True) + ids[:, 2:3]
    p2 = jnp.sum(jnp.where(lanef == ids[:, 1:2], offs, 0.0), axis=1, keepdims=True) + ids[:, 3:4]
    packed = jnp.where(lane == 0, p1, jnp.where(lane == 1, p2, 0.0))
    pos_ref[...] = packed.T[0:8, :].astype(I32)


def _positions(ids, counts, tm=512):
    n = ids.shape[0]
    assert MOE_TILE & (MOE_TILE - 1) == 0
    return pl.pallas_call(
        _pos_kernel,
        grid=(n // tm,),
        in_specs=[pl.BlockSpec((tm, LANES), lambda i: (i, 0)), pl.BlockSpec((1, LANES), lambda i: (0, 0))],
        out_specs=pl.BlockSpec((8, tm), lambda i: (0, i)),
        out_shape=jax.ShapeDtypeStruct((8, n), I32),
        compiler_params=_params(("parallel",)),
        name="moe_positions",
    )(ids, counts)


def _hier_moe(x2d, norm_g, w_rg, b_rg, w_re, b_re, w_gate, w_up, w_down, layer, xs_buf):
    n, D = x2d.shape
    tm = MOE_TILE
    pad = LANES - N_GROUPS - N_EXPERTS
    w_router = jnp.concatenate([w_rg, w_re, jnp.zeros((D, pad), F32)], axis=1)
    b_router = jnp.concatenate([b_rg, b_re.reshape(-1), jnp.zeros((pad,), F32)]).reshape(1, LANES)
    xn3, ids, wts, counts = _router(x2d, norm_g, w_router, b_router)
    cnt = counts[0, :N_EXPERTS]
    padded = ((cnt + tm - 1) // tm) * tm
    ends = jnp.cumsum(padded)
    n_tiles = xs_buf.shape[0] // tm
    tile_start = jnp.arange(n_tiles, dtype=I32) * tm
    tile_expert = jnp.minimum(jnp.sum(tile_start[:, None] >= ends[None, :], axis=1), N_EXPERTS - 1).astype(I32)
    n_valid = (ends[-1] // tm).astype(I32).reshape(1)
    last_e = tile_expert[jnp.maximum(n_valid[0] - 1, 0)]
    tile_expert = jnp.where(jnp.arange(n_tiles) < n_valid[0], tile_expert, last_e)
    pos = _positions(ids, counts)
    pos1, pos2 = pos[0], pos[1]
    xs = _dispatch(pos1, pos2, xn3, xs_buf)
    ys = _experts(tile_expert, n_valid, xs, w_gate, w_up, w_down, layer)
    return _combine(pos1, pos2, ys, x2d, wts), xs


def _in_proj_regions(D):
    bw = BRANCH_WIDTH
    sizes = (bw, bw, bw, bw, N_HEADS, N_HEADS, bw, bw, bw, IDX_HEADS * IDX_DIM, IDX_DIM, IDX_HEADS,
             bw, bw, bw, N_BRANCHES * D)
    offs = np.concatenate([[0], np.cumsum(sizes)]).tolist()
    region_a = (offs[0], offs[4] - offs[0])
    region_b = (offs[6], offs[10] - offs[6])
    region_c = (offs[12], offs[16] - offs[12])
    small = dict(mi=offs[4], mf=offs[5], ik=offs[10], iw=offs[11])
    return region_a, region_b, region_c, small


def _token_mixer(x2d, B, S, layer, norm_g, w_in, conv_w, b_i, b_f, mlstm_norm_g, q_norm_g, k_norm_g,
                 w_branch, w_out, rel_bias):
    n, D = x2d.shape
    L = MLSTM_CHUNK
    H = N_HEADS
    tm = min(n, PROJ_TM)
    xn = _rmsnorm(x2d, norm_g)
    ra, rb, rc, small = _in_proj_regions(D)
    w_t = jnp.swapaxes(w_in, 1, 2)
    proj_a = _in_proj(xn, w_t, layer, ra[0], ra[1], tm)
    proj_b = _in_proj(xn, w_t, layer, rb[0], rb[1], tm)
    proj_c = _in_proj(xn, w_t, layer, rc[0], rc[1], tm)
    tail = _tail_proj(xn, w_t, layer, small, tm)
    g = tail[1][:, IDX_HEADS:IDX_HEADS + 2 * H].reshape(B, S // L, L, 2, H)
    gates_t = jnp.transpose(g, (0, 4, 1, 3, 2))
    hm = _mlstm(proj_a, gates_t, jnp.stack([b_i, b_f]), conv_w, mlstm_norm_g, B, S)
    hd = _dsa(proj_b, tail, q_norm_g, k_norm_g, rel_bias, B, S)
    hs = _stick_breaking(proj_c, B, S)
    merged = _merge(hm, hd, hs, proj_c, w_branch, layer)
    return _matmul_residual(merged, w_out, layer, x2d)


def kernel(x, norm1_g, w_in, conv_w, b_i, b_f, mlstm_norm_g, q_norm_g, k_norm_g, w_branch, w_out, norm2_g,
           w_router_g, b_router_g, w_router_e, b_router_e, w_gate, w_up, w_down, rel_bias):
    B, S, D = x.shape
    x2d = x.reshape(B * S, D)
    xs_buf = jnp.zeros((2 * B * S + N_EXPERTS * MOE_TILE, D), F32)
    for l in range(w_in.shape[0]):
        x2d = _token_mixer(x2d, B, S, l, norm1_g[l], w_in, conv_w[l], b_i[l], b_f[l], mlstm_norm_g[l],
                           q_norm_g[l], k_norm_g[l], w_branch, w_out, rel_bias)
        x2d, xs_buf = _hier_moe(x2d, norm2_g[l], w_router_g[l], b_router_g[l], w_router_e[l], b_router_e[l],
                                w_gate, w_up, w_down, l, xs_buf)
    return x2d.reshape(B, S, D)
```

```python
import functools
import math

import numpy as np
import jax
import jax.numpy as jnp
from jax import lax
from jax.experimental import pallas as pl
from jax.experimental.pallas import tpu as pltpu

F32 = jnp.float32
BF16 = jnp.bfloat16
I32 = jnp.int32

LANES = 128
HEAD_DIM = 128
N_HEADS = 8
BRANCH_WIDTH = N_HEADS * HEAD_DIM
N_BRANCHES = 3
CONV_WIDTH = 4
IDX_HEADS = 16
IDX_DIM = 64
TOPK_MAX = 256
N_BUCKETS = 32
MAX_DISTANCE = 128
N_GROUPS = 4
EXPERTS_PER_GROUP = 8
N_EXPERTS = N_GROUPS * EXPERTS_PER_GROUP
EPS = 1e-6
NEG = -1e30
INT_MIN = -(2 ** 31)

MLSTM_CHUNK = 256
DSA_BLOCK = 128
DSA_BANDS = 4
DSA_QBLOCKS = 2
SB_BLOCK = 512
SB_ROWS = 512
SB_PIECES = 1
SB_HEADS = 4
HEADS_PER_STEP = 4
MOE_TILE = 512
DMA_UNROLL = 8
PROJ_TN = 512
PROJ_TM = 2048
VMEM_LIMIT = 56 * 1024 * 1024

A_MQ, A_MK, A_MV, A_MO = 0, 8, 16, 24
B_DQ, B_DK, B_DV, B_IQ = 0, 8, 16, 24
C_SQ, C_SK, C_SV, C_GATE = 0, 8, 16, 24


def _params(sem):
    return pltpu.CompilerParams(dimension_semantics=sem, vmem_limit_bytes=VMEM_LIMIT)


def _dot(a, b):
    return jnp.dot(a, b, preferred_element_type=F32)


def _dot_nt(a, b):
    return lax.dot_general(a, b, (((1,), (1,)), ((), ())), preferred_element_type=F32)


def _sigmoid(z):
    return 1.0 / (1.0 + jnp.exp(-z))


def _rmsnorm_kernel(x_ref, g_ref, o_ref):
    x = x_ref[...]
    ms = jnp.mean(x * x, axis=-1, keepdims=True)
    o_ref[...] = (x * lax.rsqrt(ms + EPS) * g_ref[...]).astype(o_ref.dtype)


def _rmsnorm(x2d, g, tm=512):
    n, d = x2d.shape
    return pl.pallas_call(
        _rmsnorm_kernel,
        grid=(n // tm,),
        in_specs=[pl.BlockSpec((tm, d), lambda i: (i, 0)), pl.BlockSpec((1, d), lambda i: (0, 0))],
        out_specs=pl.BlockSpec((tm, d), lambda i: (i, 0)),
        out_shape=jax.ShapeDtypeStruct((n, d), BF16),
        compiler_params=_params(("parallel",)),
        name="rmsnorm",
    )(x2d, g.reshape(1, d))


def _in_proj_kernel(x_ref, *rest, shift, nblk):
    w_refs, o_ref, wb = rest[:nblk], rest[nblk], rest[nblk + 1]
    tn = wb.shape[1]

    @pl.when(pl.program_id(1) == 0)
    def _():
        w = jnp.concatenate([r[...] for r in w_refs], axis=0)
        wb[...] = w[shift:shift + tn, :].T.astype(BF16)

    acc = _dot(x_ref[...], wb[...])
    for j in range(o_ref.shape[0]):
        o_ref[j] = acc[:, j * LANES:(j + 1) * LANES].astype(o_ref.dtype)


def _in_proj(xn, w_t, layer, col0, ncols, tm):
    m, k = xn.shape
    tn = PROJ_TN
    base, shift = divmod(col0, LANES)
    assert shift % 8 == 0
    nblk = tn // LANES + (1 if shift else 0)
    per = tn // LANES

    def wspec(r):
        return pl.BlockSpec((None, LANES, k), lambda j, i: (layer, base + per * j + r, 0))

    return pl.pallas_call(
        functools.partial(_in_proj_kernel, shift=shift, nblk=nblk),
        grid=(ncols // tn, m // tm),
        in_specs=[pl.BlockSpec((tm, k), lambda j, i: (i, 0))] + [wspec(r) for r in range(nblk)],
        out_specs=pl.BlockSpec((per, tm, LANES), lambda j, i: (j, i, 0)),
        out_shape=jax.ShapeDtypeStruct((ncols // LANES, m, LANES), BF16),
        scratch_shapes=[pltpu.VMEM((k, tn), BF16)],
        compiler_params=_params(("parallel", "arbitrary")),
        name="in_proj",
    )(xn, *([w_t] * nblk))


def _tail_kernel(x_ref, wg_ref, wi_ref, o_ref, wt, *, g_lane, ik_lane, iw_lane):
    @pl.when(pl.program_id(0) == 0)
    def _():
        wg = wg_ref[...]
        wi = wi_ref[...]
        ik = wi[ik_lane:ik_lane + IDX_DIM, :]
        iw = wi[iw_lane:iw_lane + IDX_HEADS, :]
        gates = wg[g_lane:g_lane + 2 * N_HEADS, :]
        pad = jnp.zeros((LANES - IDX_HEADS - 2 * N_HEADS, wg.shape[1]), F32)
        wt[...] = jnp.concatenate([ik, ik, iw, gates, pad], axis=0).T.astype(BF16)

    acc = _dot(x_ref[...], wt[...])
    o_ref[0] = acc[:, :LANES]
    o_ref[1] = acc[:, LANES:]


def _tail_proj(xn, w_t, layer, small, tm):
    m, k = xn.shape
    g_blk, g_lane = divmod(small["mi"], LANES)
    i_blk, ik_lane = divmod(small["ik"], LANES)
    iw_lane = small["iw"] - i_blk * LANES
    assert small["mf"] == small["mi"] + N_HEADS and g_lane + 2 * N_HEADS <= LANES
    assert ik_lane + IDX_DIM <= LANES and 0 <= iw_lane and iw_lane + IDX_HEADS <= LANES
    assert g_lane % 8 == 0 and ik_lane % 8 == 0 and iw_lane % 8 == 0
    return pl.pallas_call(
        functools.partial(_tail_kernel, g_lane=g_lane, ik_lane=ik_lane, iw_lane=iw_lane),
        grid=(m // tm,),
        in_specs=[pl.BlockSpec((tm, k), lambda i: (i, 0)),
                  pl.BlockSpec((None, LANES, k), lambda i: (layer, g_blk, 0)),
                  pl.BlockSpec((None, LANES, k), lambda i: (layer, i_blk, 0))],
        out_specs=pl.BlockSpec((2, tm, LANES), lambda i: (0, i, 0)),
        out_shape=jax.ShapeDtypeStruct((2, m, LANES), F32),
        scratch_shapes=[pltpu.VMEM((k, 2 * LANES), BF16)],
        compiler_params=_params(("arbitrary",)),
        name="tail_proj",
    )(xn, w_t, w_t)


def _mlstm_kernel(bias_ref, q_ref, k_ref, v_ref, o_ref, g_ref, cwq_ref, cwk_ref, ng_ref, out_ref,
                  qf, kf, qc, kc, st):
    hp = pl.program_id(1)
    U = HEADS_PER_STEP
    S = q_ref.shape[1]
    L = MLSTM_CHUNK
    nc = S // L
    d = HEAD_DIM
    PAD = 8

    R = min(S, 256)
    for u in range(U):
        qf[u, 0:PAD, :] = jnp.zeros((PAD, d), F32)
        kf[u, 0:PAD, :] = jnp.zeros((PAD, d), F32)
        qf[u, PAD:PAD + S, :] = q_ref[u].astype(F32)
        kf[u, PAD:PAD + S, :] = k_ref[u].astype(F32)
        ls = slice(u * d, (u + 1) * d)
        for r0 in range(0, S, R):
            aq = jnp.zeros((R, d), F32)
            ak = jnp.zeros((R, d), F32)
            for t in range(CONV_WIDTH):
                off = PAD - (CONV_WIDTH - 1) + t + r0
                aq = aq + cwq_ref[t:t + 1, ls] * qf[u, off:off + R, :]
                ak = ak + cwk_ref[t:t + 1, ls] * kf[u, off:off + R, :]
            qc[u, r0:r0 + R, :] = (aq * _sigmoid(aq) * (d ** -0.5)).astype(BF16)
            kc[u, r0:r0 + R, :] = ak * _sigmoid(ak)

    st[...] = jnp.zeros(st.shape, F32)
    row = lax.broadcasted_iota(I32, (L, L), 0)
    col = lax.broadcasted_iota(I32, (L, L), 1)
    causal = col <= row
    eye = col == row
    lane = lax.broadcasted_iota(I32, (L, d), 1)
    ones_col = jnp.where(lane == 0, 1.0, 0.0).astype(BF16)
    ng = ng_ref[...]

    def chunk_one(u, c, r, m):
        q = qc[u, pl.ds(r, L), :]
        kT = kc[u, pl.ds(r, L), :].T
        v = v_ref[u, pl.ds(r, L), :]
        vaug = jnp.concatenate([v, ones_col], axis=1)
        gates = g_ref[u, c]
        i_row = gates[0:1, :] + bias_ref[0, hp * U + u]
        f_row = gates[1:2, :] + bias_ref[1, hp * U + u]
        lf_row = jnp.minimum(f_row, 0.0) - jnp.log1p(jnp.exp(-jnp.abs(f_row)))
        b_col = jnp.sum(jnp.where(causal, lf_row, 0.0), axis=1, keepdims=True)
        b_row = jnp.sum(jnp.where(eye, b_col, 0.0), axis=0, keepdims=True)
        dlog = jnp.where(causal, b_col - b_row + i_row, NEG)
        inter = b_col + m
        m_t = jnp.maximum(inter, jnp.max(dlog, axis=1, keepdims=True))
        w_intra = jnp.exp(dlog - m_t)
        w_inter = jnp.exp(inter - m_t)
        s = _dot(q, kT.astype(BF16)) * w_intra
        res = w_inter * _dot(q, st[u].astype(BF16)) + _dot(s.astype(BF16), vaug)
        num = res[:, :d]
        den = res[:, d:d + 1]
        hh = num / jnp.maximum(jnp.abs(den), jnp.exp(-m_t))
        hn = hh * lax.rsqrt(jnp.mean(hh * hh, axis=1, keepdims=True) + EPS) * ng[:, u * d:(u + 1) * d]
        og = o_ref[u, pl.ds(r, L), :].astype(F32)
        out_ref[u, pl.ds(r, L), :] = (hn * _sigmoid(og)).astype(out_ref.dtype)
        ws_row = w_intra[L - 1:L, :]
        decay = w_inter[L - 1:L, :]
        st[u] = decay * st[u] + _dot((kT * ws_row).astype(BF16), vaug)
        return m_t[L - 1:L, :]

    def chunk(c, ms):
        r = pl.multiple_of(c * L, L)
        return tuple(chunk_one(u, c, r, ms[u]) for u in range(U))

    lax.fori_loop(0, nc, chunk, tuple(jnp.zeros((1, 1), F32) for _ in range(U)))


def _mlstm(proj_a, gates_t, bias_if, conv_w, norm_g, B, S):
    L = MLSTM_CHUNK
    nc = S // L
    d = HEAD_DIM
    H = N_HEADS
    U = HEADS_PER_STEP

    def slab(off):
        return pl.BlockSpec((U, S, d), lambda b, h: (off // U + h, b, 0))

    return pl.pallas_call(
        _mlstm_kernel,
        grid=(B, H // U),
        in_specs=[
            pl.BlockSpec(memory_space=pltpu.SMEM),
            slab(A_MQ), slab(A_MK), slab(A_MV), slab(A_MO),
            pl.BlockSpec((None, U, nc, 2, L), lambda b, h: (b, h, 0, 0, 0)),
            pl.BlockSpec((CONV_WIDTH, U * d), lambda b, h: (0, h)),
            pl.BlockSpec((CONV_WIDTH, U * d), lambda b, h: (0, H // U + h)),
            pl.BlockSpec((1, U * d), lambda b, h: (0, h)),
        ],
        out_specs=pl.BlockSpec((U, S, d), lambda b, h: (h, b, 0)),
        out_shape=jax.ShapeDtypeStruct((H, B * S, d), BF16),
        scratch_shapes=[
            pltpu.VMEM((U, S + 8, d), F32), pltpu.VMEM((U, S + 8, d), F32),
            pltpu.VMEM((U, S, d), BF16), pltpu.VMEM((U, S, d), F32),
            pltpu.VMEM((U, d, 2 * d), F32),
        ],
        compiler_params=_params(("parallel", "parallel")),
        name="mlstm",
    )(bias_if, proj_a, proj_a, proj_a, proj_a, gates_t, conv_w, conv_w, norm_g.reshape(1, H * d))


def _t5_thresholds():
    max_exact = N_BUCKETS // 2
    n = np.arange(0, 2 * MAX_DISTANCE)
    nf = np.maximum(n, 1).astype(np.float64)
    val = np.log(nf / max_exact) / math.log(MAX_DISTANCE / max_exact) * (N_BUCKETS - max_exact)
    frac = np.abs(val - np.round(val))
    frac_ok = (frac > 1e-4) | (n <= max_exact) | (n >= MAX_DISTANCE)
    assert frac_ok.all()
    large = np.minimum(max_exact + np.trunc(val).astype(np.int64), N_BUCKETS - 1)
    bucket = np.where(n < max_exact, n, large)
    assert (np.diff(bucket) >= 0).all() and bucket[MAX_DISTANCE] == N_BUCKETS - 1
    return [int(np.argmax(bucket >= j)) for j in range(1, N_BUCKETS)]


_T5_THR = _t5_thresholds()


def _dsa_kernel(rb_ref, q_ref, k_ref, v_ref, iq_ref, ik_ref, wt_ref, gq_ref, gk_ref, prev_ref, out_ref,
                kn, ikk, wb, key_ref, mask_ref, bias_ref, *, topk, q0, nb, nqb):
    del prev_ref
    qis = [q0 + pl.program_id(1) * nqb + qb for qb in range(nqb)]
    T = DSA_BLOCK
    d = HEAD_DIM
    H = N_HEADS
    HALF = T // 2
    row = lax.broadcasted_iota(I32, (T, T), 0)
    col = lax.broadcasted_iota(I32, (T, T), 1)

    @pl.when(pl.program_id(1) == 0)
    def _per_batch():
        gk = gk_ref[...]

        def norm_k(h, _):
            kh = k_ref[h].astype(F32)
            kn[h] = (kh * lax.rsqrt(jnp.mean(kh * kh, axis=1, keepdims=True) + EPS) * gk).astype(BF16)
            for o in range(2):
                n = o * T + row - col
                val = jnp.full((T, T), rb_ref[0, h], F32)
                for j, thr in enumerate(_T5_THR):
                    val = jnp.where(n >= thr, rb_ref[j + 1, h], val)
                bias_ref[h, o] = val
            bias_ref[h, 2] = jnp.full((T, T), rb_ref[N_BUCKETS - 1, h], F32)
            return 0

        lax.fori_loop(0, H, norm_k, 0)
        ik = ik_ref[...]
        klane = lax.broadcasted_iota(I32, ik.shape, 1)
        ikk[0] = jnp.where(klane < IDX_DIM, ik, 0.0).astype(BF16)
        ikk[1] = jnp.where(klane >= IDX_DIM, ik, 0.0).astype(BF16)

    wsc = wt_ref[...] * (IDX_HEADS ** -0.5 * IDX_DIM ** -0.5)
    for qb in range(nqb):
        for h16 in range(IDX_HEADS):
            wb[qb * IDX_HEADS + h16] = jnp.broadcast_to(wsc[qb * T:(qb + 1) * T, h16:h16 + 1], (T, LANES))
    q_pairs = iq_ref[...].reshape((IDX_HEADS // 2) * nqb * T, LANES)

    def score_block(j, _):
        r = pl.multiple_of(j * T, T)
        kk = jnp.concatenate([ikk[0, pl.ds(r, T), :], ikk[1, pl.ds(r, T), :]], axis=0)
        dots = _dot_nt(q_pairs, kk)
        for qb in range(nqb):
            sc = jnp.zeros((T, T), F32)
            for h16 in range(IDX_HEADS):
                hp, odd = divmod(h16, 2)
                r0 = (hp * nqb + qb) * T
                sc = sc + wb[qb * IDX_HEADS + h16] * jnp.maximum(dots[r0:r0 + T, odd * T:(odd + 1) * T], 0.0)
            bits = lax.bitcast_convert_type(sc, I32)
            key = jnp.where(bits < 0, bits ^ jnp.int32(0x7FFFFFFF), bits)
            key = jnp.where(sc == 0.0, 0, key)
            key = jnp.where(j * T + col <= qis[qb] * T + row, key, INT_MIN)
            key_ref[qb, j] = key
        return 0

    lax.fori_loop(0, nb, score_block, 0, unroll=2)

    kf = float(topk)
    ones = jnp.ones((LANES, LANES), BF16)

    def count(chain, pred):
        qb, lo = chain
        acc = jnp.zeros((HALF, LANES), F32)
        for jb in range(nb):
            acc = acc + jnp.where(pred(key_ref[qb, jb, lo:lo + HALF, :]), 1.0, 0.0)
        return _dot(acc.astype(BF16), ones)

    chains = [(qb, lo) for qb in range(nqb) for lo in (0, HALF)]
    zero_i = jnp.zeros((HALF, LANES), I32)
    thr0 = tuple(jnp.where(count(ch, lambda k: k >= zero_i) >= kf, jnp.int32(0), jnp.int32(INT_MIN))
                 for ch in chains)

    def bisect_bit(thrs, bit):
        out = []
        for ch, t in zip(chains, thrs):
            cand = t | bit
            out.append(jnp.where(count(ch, lambda k, c=cand: k >= c) >= kf, cand, t))
        return tuple(out)

    def bisect_pair(it, thrs):
        hi_bit = jnp.left_shift(jnp.int32(1), 29 - 2 * it)
        lo_bit = jnp.left_shift(jnp.int32(1), 28 - 2 * it)
        out = []
        for ch, t in zip(chains, thrs):
            c01, c10, c11 = t | lo_bit, t | hi_bit, t | hi_bit | lo_bit
            n01, n10, n11 = (count(ch, lambda k, c=c: k >= c) >= kf for c in (c01, c10, c11))
            out.append(jnp.where(n11, c11, jnp.where(n10, c10, jnp.where(n01, c01, t))))
        return tuple(out)

    thrs = bisect_bit(thr0, jnp.int32(1 << 30))
    thrs = lax.fori_loop(0, 15, bisect_pair, thrs)
    needs = [kf - count(ch, lambda k, t=t: k > t) for ch, t in zip(chains, thrs)]

    tri = jnp.where(row <= col, 1.0, 0.0).astype(BF16)
    tri_ones = jnp.concatenate([tri, ones], axis=1)
    for qb in range(nqb):
        need = jnp.concatenate(needs[2 * qb:2 * qb + 2], axis=0)
        thrb = jnp.concatenate(thrs[2 * qb:2 * qb + 2], axis=0)
        seen = jnp.zeros((T, LANES), F32)
        for jb in range(nb):
            key = key_ref[qb, jb]
            tie = key == thrb
            tie16 = jnp.where(tie, 1.0, 0.0).astype(BF16)
            counts = _dot(tie16, tri_ones)
            pre = counts[:, :T] + seen
            m = jnp.where(key > thrb, 0.0, jnp.where(tie, jnp.where(pre <= need, 0.0, NEG), NEG))
            mask_ref[qb, jb] = jnp.where(key == INT_MIN, NEG, m)
            seen = seen + counts[:, T:]

    gq = gq_ref[...]

    def head(h):
        qh = q_ref[h].astype(F32)
        qn = (qh * lax.rsqrt(jnp.mean(qh * qh, axis=1, keepdims=True) + EPS) * gq * (d ** -0.5)).astype(BF16)
        lgs = _dot_nt(qn, kn[h])
        p_rows, sums = [], []
        for qb in range(nqb):
            mx = None
            lg_blocks = []
            for jb in range(nb):
                lg = (lgs[qb * T:(qb + 1) * T, jb * T:(jb + 1) * T]
                      + bias_ref[h, jnp.clip(qis[qb] - jb, 0, 2)] + mask_ref[qb, jb])
                lg_blocks.append(lg)
                mx = lg if mx is None else jnp.maximum(mx, lg)
            rowmax = jnp.max(mx, axis=1, keepdims=True)
            l = jnp.zeros((T, T), F32)
            ps = []
            for jb in range(nb):
                p = jnp.exp(lg_blocks[jb] - rowmax)
                l = l + p
                ps.append(p.astype(BF16))
            p_rows.append(jnp.concatenate(ps, axis=1))
            sums.append(jnp.sum(l, axis=1, keepdims=True))
        acc = _dot(jnp.concatenate(p_rows, axis=0), v_ref[h])
        out_ref[h] = (acc / jnp.concatenate(sums, axis=0)).astype(out_ref.dtype)

    def head_group(g, _):
        for u in range(HEADS_PER_STEP):
            head(g * HEADS_PER_STEP + u)
        return 0

    lax.fori_loop(0, H // HEADS_PER_STEP, head_group, 0)


def _dsa(proj_b, tail, q_norm_g, k_norm_g, rel_bias, B, S):
    T = DSA_BLOCK
    nq = S // T
    d = HEAD_DIM
    H = N_HEADS
    topk = min(TOPK_MAX, S // 4)
    per = nq // DSA_BANDS
    nqb = math.gcd(per, DSA_QBLOCKS)
    TQ = nqb * T
    pb = proj_b.reshape(proj_b.shape[0], B, S, d)
    tl = tail.reshape(tail.shape[0], B, S, LANES)
    out = jnp.zeros((H, B, S, d), BF16)
    for band in range(DSA_BANDS):
        q0 = band * per
        nb = q0 + per
        W = nb * T
        in_specs = [
            pl.BlockSpec(memory_space=pltpu.SMEM),
            pl.BlockSpec((H, None, TQ, d), lambda b, i: (B_DQ // H, b, q0 // nqb + i, 0)),
            pl.BlockSpec((H, None, W, d), lambda b, i: (B_DK // H, b, 0, 0)),
            pl.BlockSpec((H, None, W, d), lambda b, i: (B_DV // H, b, 0, 0)),
            pl.BlockSpec((H, None, TQ, d), lambda b, i: (B_IQ // H, b, q0 // nqb + i, 0)),
            pl.BlockSpec((None, None, W, LANES), lambda b, i: (0, b, 0, 0)),
            pl.BlockSpec((None, None, TQ, LANES), lambda b, i: (1, b, q0 // nqb + i, 0)),
            pl.BlockSpec((1, d), lambda b, i: (0, 0)),
            pl.BlockSpec((1, d), lambda b, i: (0, 0)),
            pl.BlockSpec(memory_space=pl.ANY),
        ]
        args = [rel_bias, pb, pb, pb, pb, tl, tl, q_norm_g.reshape(1, d), k_norm_g.reshape(1, d), out]
        out = pl.pallas_call(
            functools.partial(_dsa_kernel, topk=topk, q0=q0, nb=nb, nqb=nqb),
            grid=(B, per // nqb),
            in_specs=in_specs,
            out_specs=pl.BlockSpec((H, None, TQ, d), lambda b, i: (0, b, q0 // nqb + i, 0)),
            out_shape=jax.ShapeDtypeStruct((H, B, S, d), BF16),
            scratch_shapes=[
                pltpu.VMEM((H, W, d), BF16),
                pltpu.VMEM((2, W, LANES), BF16),
                pltpu.VMEM((nqb * IDX_HEADS, T, LANES), F32),
                pltpu.VMEM((nqb, nb, T, T), I32),
                pltpu.VMEM((nqb, nb, T, T), F32),
                pltpu.VMEM((H, 3, T, T), F32),
            ],
            input_output_aliases={len(args) - 1: 0},
            compiler_params=_params(("parallel", "arbitrary")),
            name=f"dsa_band{band}",
        )(*args)
    return out.reshape(H, B * S, d)


def _sb_kernel(q_ref, k_ref, v_ref, out_ref):
    qi = pl.program_id(2)
    U = SB_HEADS
    T = SB_BLOCK
    d = HEAD_DIM
    R = SB_ROWS
    ns = T // R
    row = lax.broadcasted_iota(I32, (T, T), 0)
    col = lax.broadcasted_iota(I32, (T, T), 1)
    upper = jnp.where(row > col, 1.0, 0.0).astype(BF16)
    srow = lax.broadcasted_iota(I32, (R, T), 0)
    scol = lax.broadcasted_iota(I32, (R, T), 1)
    units = [(u, s) for u in range(U) for s in range(ns)]
    qs = [(q_ref[u, s * R:(s + 1) * R, :].astype(F32) * (d ** -0.5)).astype(BF16) for u, s in units]

    def block(n, j, diag, run, acc):
        u, s = units[n]
        r = pl.multiple_of(j * T, T)
        z = _dot_nt(qs[n], k_ref[u, pl.ds(r, T), :])
        sp = jnp.maximum(z, 0.0) + jnp.log(1.0 + jnp.exp(-jnp.abs(z)))
        strict = scol < srow + s * R
        ln = jnp.where(strict, -sp, 0.0) if diag else -sp
        parts = [ln.astype(BF16)]
        for _ in range(SB_PIECES - 1):
            parts.append((ln - sum(p.astype(F32) for p in parts)).astype(BF16))
        sums = _dot(jnp.concatenate(parts, axis=0), upper)
        suf = sum(sums[p * R:(p + 1) * R] for p in range(SB_PIECES))
        a = jnp.exp(z - sp + suf + run)
        if diag:
            a = jnp.where(strict, a, 0.0)
        acc = acc + _dot(a.astype(BF16), v_ref[u, pl.ds(r, T), :])
        return run + suf[:, 0:1] + ln[:, 0:1], acc

    state = []
    for n in range(len(units)):
        state.extend(block(n, qi, True, jnp.zeros((R, 1), F32), jnp.zeros((R, d), F32)))

    def earlier(i, st):
        out = []
        for n in range(len(units)):
            out.extend(block(n, qi - i, False, st[2 * n], st[2 * n + 1]))
        return tuple(out)

    state = lax.fori_loop(1, qi + 1, earlier, tuple(state))
    for n, (u, s) in enumerate(units):
        out_ref[u, s * R:(s + 1) * R, :] = state[2 * n + 1].astype(out_ref.dtype)


def _stick_breaking(proj_c, B, S):
    T = SB_BLOCK
    nq = S // T
    d = HEAD_DIM
    H = N_HEADS
    U = SB_HEADS
    return pl.pallas_call(
        _sb_kernel,
        grid=(B, H // U, nq),
        in_specs=[
            pl.BlockSpec((U, T, d), lambda b, h, i: (C_SQ // U + h, b * nq + i, 0)),
            pl.BlockSpec((U, S, d), lambda b, h, i: (C_SK // U + h, b, 0)),
            pl.BlockSpec((U, S, d), lambda b, h, i: (C_SV // U + h, b, 0)),
        ],
        out_specs=pl.BlockSpec((U, T, d), lambda b, h, i: (h, b * nq + i, 0)),
        out_shape=jax.ShapeDtypeStruct((H, B * S, d), BF16),
        compiler_params=_params(("parallel", "parallel", "parallel")),
        name="stick_breaking",
    )(proj_c, proj_c, proj_c)


def _merge_kernel(hm_ref, hd_ref, hs_ref, g0_ref, g1_ref, g2_ref, w_ref, o_ref, wb):
    H = N_HEADS

    @pl.when(pl.program_id(1) == 0)
    def _():
        wb[...] = w_ref[...].astype(BF16)

    acc = None
    ns = o_ref.shape[1] // LANES
    for n, (br, gr) in enumerate(((hm_ref, g0_ref), (hd_ref, g1_ref), (hs_ref, g2_ref))):
        a = jnp.concatenate([br[h] for h in range(H)], axis=1)
        up = _dot(a, wb[n])
        gate = jnp.concatenate([gr[s] for s in range(ns)], axis=1).astype(F32)
        term = _sigmoid(gate) * up
        acc = term if acc is None else acc + term
    o_ref[...] = acc.astype(o_ref.dtype)


def _merge(hm, hd, hs, proj_c, w_branch, layer, tm=1024, tn=512):
    H, n, d = hm.shape
    tm = min(tm, n)
    D = w_branch.shape[3]
    ns = tn // LANES
    br = pl.BlockSpec((H, tm, d), lambda j, i: (0, i, 0))

    def gate(b):
        first = (C_GATE + b * (D // LANES)) // ns
        return pl.BlockSpec((ns, tm, LANES), lambda j, i: (first + j, i, 0))

    return pl.pallas_call(
        _merge_kernel,
        grid=(D // tn, n // tm),
        in_specs=[br, br, br, gate(0), gate(1), gate(2),
                  pl.BlockSpec((None, N_BRANCHES, H * d, tn), lambda j, i: (layer, 0, 0, j))],
        out_specs=pl.BlockSpec((tm, tn), lambda j, i: (i, j)),
        out_shape=jax.ShapeDtypeStruct((n, D), BF16),
        scratch_shapes=[pltpu.VMEM((N_BRANCHES, H * d, tn), BF16)],
        compiler_params=_params(("parallel", "arbitrary")),
        name="branch_merge",
    )(hm, hd, hs, proj_c, proj_c, proj_c, w_branch)


def _mm_res_kernel(a_ref, w_ref, x_ref, o_ref, wb):
    @pl.when(pl.program_id(1) == 0)
    def _():
        wb[...] = w_ref[...].astype(BF16)

    o_ref[...] = x_ref[...] + _dot(a_ref[...], wb[...])


def _matmul_residual(a, w, layer, x, tm=1024, tn=512):
    m, k = a.shape
    tm = min(tm, m)
    n = w.shape[2]
    return pl.pallas_call(
        _mm_res_kernel,
        grid=(n // tn, m // tm),
        in_specs=[pl.BlockSpec((tm, k), lambda j, i: (i, 0)),
                  pl.BlockSpec((None, k, tn), lambda j, i: (layer, 0, j)),
                  pl.BlockSpec((tm, tn), lambda j, i: (i, j))],
        out_specs=pl.BlockSpec((tm, tn), lambda j, i: (i, j)),
        out_shape=jax.ShapeDtypeStruct((m, n), F32),
        scratch_shapes=[pltpu.VMEM((k, tn), BF16)],
        compiler_params=_params(("parallel", "arbitrary")),
        name="out_proj",
    )(a, w, x)


def _router_kernel(x_ref, g_ref, wr_ref, br_ref, xn_ref, ids_ref, wts_ref, cnt_ref, carry, lower):
    i = pl.program_id(0)
    tm, D = x_ref.shape

    @pl.when(i == 0)
    def _():
        carry[...] = jnp.zeros(carry.shape, F32)
        r = lax.broadcasted_iota(I32, (tm, tm), 0)
        c = lax.broadcasted_iota(I32, (tm, tm), 1)
        lower[...] = jnp.where(c < r, 1.0, 0.0).astype(BF16)

    x = x_ref[...]
    xn = x * lax.rsqrt(jnp.mean(x * x, axis=-1, keepdims=True) + EPS) * g_ref[...]
    xn_ref[...] = xn
    logits = jnp.dot(xn, wr_ref[...], preferred_element_type=F32, precision=lax.Precision.HIGHEST)
    biased = logits + br_ref[...]
    lane = lax.broadcasted_iota(I32, (tm, LANES), 1)
    lanef = lane.astype(F32)
    big = float(LANES)

    def first_lane(mask):
        return jnp.min(jnp.where(mask, lanef, big), axis=1, keepdims=True)

    gmask = lane < N_GROUPS
    gmax = jnp.max(jnp.where(gmask, biased, NEG), axis=1, keepdims=True)
    g_sel = first_lane(gmask & (biased == gmax))
    gm = jnp.max(jnp.where(gmask, logits, NEG), axis=1, keepdims=True)
    ge = jnp.where(gmask, jnp.exp(logits - gm), 0.0)
    p_group = jnp.sum(jnp.where(lanef == g_sel, ge, 0.0), axis=1, keepdims=True) / jnp.sum(ge, axis=1, keepdims=True)
    lo = N_GROUPS + EXPERTS_PER_GROUP * g_sel
    emask = (lanef >= lo) & (lanef < lo + EXPERTS_PER_GROUP)
    eb = jnp.where(emask, biased, NEG)
    e1 = first_lane(emask & (eb == jnp.max(eb, axis=1, keepdims=True)))
    emask2 = emask & (lanef != e1)
    eb2 = jnp.where(emask2, biased, NEG)
    e2 = first_lane(emask2 & (eb2 == jnp.max(eb2, axis=1, keepdims=True)))
    em = jnp.max(jnp.where(emask, logits, NEG), axis=1, keepdims=True)
    ee = jnp.where(emask, jnp.exp(logits - em), 0.0)
    s1 = jnp.sum(jnp.where(lanef == e1, ee, 0.0), axis=1, keepdims=True)
    s2 = jnp.sum(jnp.where(lanef == e2, ee, 0.0), axis=1, keepdims=True)
    se = jnp.sum(ee, axis=1, keepdims=True)
    w1 = s1 / se
    w2 = s2 / se
    wsum = w1 + w2
    w1 = p_group * w1 / wsum
    w2 = p_group * w2 / wsum
    x1 = e1 - N_GROUPS
    x2 = e2 - N_GROUPS

    onehot = jnp.where((lanef == x1) | (lanef == x2), 1.0, 0.0)
    prefix = _dot(lower[...], onehot.astype(BF16)) + carry[...]
    r1 = jnp.sum(jnp.where(lanef == x1, prefix, 0.0), axis=1, keepdims=True)
    r2 = jnp.sum(jnp.where(lanef == x2, prefix, 0.0), axis=1, keepdims=True)
    carry[...] = carry[...] + jnp.sum(onehot, axis=0, keepdims=True)
    cnt_ref[...] = carry[...].astype(I32)

    idsf = jnp.where(lane == 0, x1, jnp.where(lane == 1, x2, jnp.where(lane == 2, r1, jnp.where(lane == 3, r2, 0.0))))
    ids_ref[...] = idsf.astype(I32)
    wts_ref[...] = jnp.where(lane == 0, w1, jnp.where(lane == 1, w2, 0.0))


def _router(x2d, g, w_router, b_router, tm=512):
    n, D = x2d.shape
    return pl.pallas_call(
        _router_kernel,
        grid=(n // tm,),
        in_specs=[pl.BlockSpec((tm, D), lambda i: (i, 0)), pl.BlockSpec((1, D), lambda i: (0, 0)),
                  pl.BlockSpec((D, LANES), lambda i: (0, 0)), pl.BlockSpec((1, LANES), lambda i: (0, 0))],
        out_specs=[pl.BlockSpec((tm, D), lambda i: (i, 0)),
                   pl.BlockSpec((tm, LANES), lambda i: (i, 0)),
                   pl.BlockSpec((tm, LANES), lambda i: (i, 0)),
                   pl.BlockSpec((1, LANES), lambda i: (0, 0))],
        out_shape=[jax.ShapeDtypeStruct((n, D), F32),
                   jax.ShapeDtypeStruct((n, LANES), I32),
                   jax.ShapeDtypeStruct((n, LANES), F32),
                   jax.ShapeDtypeStruct((1, LANES), I32)],
        scratch_shapes=[pltpu.VMEM((1, LANES), F32), pltpu.VMEM((tm, tm), BF16)],
        compiler_params=_params(("arbitrary",)),
        name="moe_router",
    )(x2d, g.reshape(1, D), w_router, b_router)


def _dispatch_kernel(p1_ref, p2_ref, xn_ref, xs_in_ref, xs_ref, sem, *, tb):
    del xs_in_ref
    base = pl.program_id(0) * tb

    def copies(t):
        src = xn_ref.at[pl.ds(t, 1), :]
        return (pltpu.make_async_copy(src, xs_ref.at[pl.ds(p1_ref[base + t], 1), :], sem),
                pltpu.make_async_copy(src, xs_ref.at[pl.ds(p2_ref[base + t], 1), :], sem))

    def issue(t, _):
        for cp in copies(t):
            cp.start()
        return 0

    lax.fori_loop(0, tb, issue, 0, unroll=DMA_UNROLL)

    def drain(t, _):
        for cp in copies(t):
            cp.wait()
        return 0

    lax.fori_loop(0, tb, drain, 0, unroll=DMA_UNROLL)


def _dispatch(pos1, pos2, xn, xs0, tb=512):
    n, D = xn.shape
    n_rows = xs0.shape[0]
    tb = min(tb, n)
    grid_spec = pltpu.PrefetchScalarGridSpec(
        num_scalar_prefetch=2,
        grid=(n // tb,),
        in_specs=[pl.BlockSpec((tb, D), lambda i, p1, p2: (i, 0)),
                  pl.BlockSpec(memory_space=pl.ANY)],
        out_specs=pl.BlockSpec(memory_space=pl.ANY),
        scratch_shapes=[pltpu.SemaphoreType.DMA(())],
    )
    return pl.pallas_call(
        functools.partial(_dispatch_kernel, tb=tb),
        grid_spec=grid_spec,
        out_shape=jax.ShapeDtypeStruct((n_rows, D), F32),
        input_output_aliases={3: 0},
        compiler_params=pltpu.CompilerParams(dimension_semantics=("arbitrary",), has_side_effects=True),
        name="moe_dispatch",
    )(pos1, pos2, xn, xs0)


def _expert_kernel(te_ref, nv_ref, ne_ref, grp_ref, xs_ref, wg_ref, wu_ref, wd_ref, ys_ref,
                   wgf, wuf, wdf, wgb, wub, wdb, sem, *, layer):
    i = pl.program_id(0)
    prev = te_ref[jnp.maximum(i - 1, 0)]

    def fetch(e, slot):
        return (pltpu.make_async_copy(wg_ref.at[layer, e], wgf.at[slot], sem.at[slot]),
                pltpu.make_async_copy(wu_ref.at[layer, e], wuf.at[slot], sem.at[slot]),
                pltpu.make_async_copy(wd_ref.at[layer, e], wdf.at[slot], sem.at[slot]))

    @pl.when(i < nv_ref[0])
    def _():
        @pl.when(i == 0)
        def _():
            for cp in fetch(te_ref[0], 0):
                cp.start()

        @pl.when((i == 0) | (te_ref[i] != prev))
        def _():
            slot = grp_ref[i] % 2
            for cp in fetch(te_ref[i], slot):
                cp.wait()

            @pl.when(ne_ref[i] >= 0)
            def _():
                for cp in fetch(ne_ref[i], 1 - slot):
                    cp.start(priority=1)

            wgb[...] = wgf[slot].astype(BF16)
            wub[...] = wuf[slot].astype(BF16)
            wdb[...] = wdf[slot].astype(BF16)

        x = xs_ref[...].astype(BF16)
        g = _dot(x, wgb[...])
        u = _dot(x, wub[...])
        hcur = (g * _sigmoid(g) * u).astype(BF16)
        ys_ref[...] = _dot(hcur, wdb[...])

    @pl.when(i >= nv_ref[0])
    def _():
        ys_ref[...] = jnp.zeros(ys_ref.shape, F32)


def _experts(tile_expert, n_valid, xs, w_gate, w_up, w_down, layer):
    n_rows = xs.shape[0]
    _, E, D, Fe = w_gate.shape
    tm = MOE_TILE
    n_tiles = n_rows // tm
    tiles = jnp.arange(n_tiles, dtype=I32)
    first = (tiles < n_valid[0]) & ((tiles == 0) | (tile_expert != jnp.roll(tile_expert, 1)))
    group = (jnp.cumsum(first.astype(I32)) - 1).astype(I32)
    later = lax.cummin(jnp.where(first, tiles, n_tiles), reverse=True)
    nxt = jnp.concatenate([later[1:], jnp.full((1,), n_tiles, I32)])
    next_expert = jnp.where(nxt < n_tiles, tile_expert[jnp.minimum(nxt, n_tiles - 1)], -1).astype(I32)

    def row_map(i, te, nv, ne, grp):
        return (jnp.minimum(i, jnp.maximum(nv[0] - 1, 0)), 0)

    grid_spec = pltpu.PrefetchScalarGridSpec(
        num_scalar_prefetch=4,
        grid=(n_tiles,),
        in_specs=[pl.BlockSpec((tm, D), row_map),
                  pl.BlockSpec(memory_space=pl.ANY), pl.BlockSpec(memory_space=pl.ANY),
                  pl.BlockSpec(memory_space=pl.ANY)],
        out_specs=pl.BlockSpec((tm, D), lambda i, te, nv, ne, grp: (i, 0)),
        scratch_shapes=[pltpu.VMEM((2, D, Fe), F32), pltpu.VMEM((2, D, Fe), F32), pltpu.VMEM((2, Fe, D), F32),
                        pltpu.VMEM((D, Fe), BF16), pltpu.VMEM((D, Fe), BF16), pltpu.VMEM((Fe, D), BF16),
                        pltpu.SemaphoreType.DMA((2,))],
    )
    return pl.pallas_call(
        functools.partial(_expert_kernel, layer=layer),
        grid_spec=grid_spec,
        out_shape=jax.ShapeDtypeStruct((n_rows, D), F32),
        compiler_params=_params(("arbitrary",)),
        name="moe_experts",
    )(tile_expert, n_valid, next_expert, group, xs, w_gate, w_up, w_down)


def _combine_kernel(p1_ref, p2_ref, ys_ref, x_ref, w_ref, o_ref, buf, sem):
    i = pl.program_id(0)
    tc = x_ref.shape[0]
    slot = i % 2

    def copies(tile, slot, t):
        tok = tile * tc + t
        return (pltpu.make_async_copy(ys_ref.at[pl.ds(p1_ref[tok], 1), :], buf.at[slot, 0, pl.ds(t, 1), :],
                                      sem.at[slot]),
                pltpu.make_async_copy(ys_ref.at[pl.ds(p2_ref[tok], 1), :], buf.at[slot, 1, pl.ds(t, 1), :],
                                      sem.at[slot]))

    def issue(tile, slot):
        def body(t, _):
            for cp in copies(tile, slot, t):
                cp.start()
            return 0

        lax.fori_loop(0, tc, body, 0, unroll=DMA_UNROLL)

    @pl.when(i == 0)
    def _():
        issue(0, 0)

    @pl.when(i + 1 < pl.num_programs(0))
    def _():
        issue(i + 1, 1 - slot)

    def drain(t, _):
        for cp in copies(i, slot, t):
            cp.wait()
        return 0

    lax.fori_loop(0, tc, drain, 0, unroll=DMA_UNROLL)
    w = w_ref[...]
    o_ref[...] = x_ref[...] + w[:, 0:1] * buf[slot, 0] + w[:, 1:2] * buf[slot, 1]


def _combine(pos1, pos2, ys, x2d, wts, tc=256):
    n, D = x2d.shape
    grid_spec = pltpu.PrefetchScalarGridSpec(
        num_scalar_prefetch=2,
        grid=(n // tc,),
        in_specs=[pl.BlockSpec(memory_space=pl.ANY),
                  pl.BlockSpec((tc, D), lambda i, p1, p2: (i, 0)),
                  pl.BlockSpec((tc, LANES), lambda i, p1, p2: (i, 0))],
        out_specs=pl.BlockSpec((tc, D), lambda i, p1, p2: (i, 0)),
        scratch_shapes=[pltpu.VMEM((2, 2, tc, D), F32), pltpu.SemaphoreType.DMA((2,))],
    )
    return pl.pallas_call(
        _combine_kernel,
        grid_spec=grid_spec,
        out_shape=jax.ShapeDtypeStruct((n, D), F32),
        compiler_params=_params(("arbitrary",)),
        name="moe_combine",
    )(pos1, pos2, ys, x2d, wts)


def _pos_kernel(ids_ref, cnt_ref, pos_ref):
    tm = ids_ref.shape[0]
    tiles = jnp.floor((cnt_ref[...].astype(F32) + (MOE_TILE - 1)) / MOE_TILE)
    r = lax.broadcasted_iota(I32, (LANES, LANES), 0)
    c = lax.broadcasted_iota(I32, (LANES, LANES), 1)
    before = jnp.where(r < c, 1.0, 0.0).astype(BF16)
    first_tile = _dot(jnp.broadcast_to(tiles, (8, LANES)).astype(BF16), before)
    offs = first_tile[0:1, :] * MOE_TILE
    ids = ids_ref[...].astype(F32)
    lane = lax.broadcasted_iota(I32, (tm, LANES), 1)
    lanef = lane.astype(F32)
    p1 = jnp.sum(jnp.where(lanef == ids[:, 0:1], offs, 0.0), axis=1, keepdims=True) + ids[:, 2:3]
    p2 = jnp.sum(jnp.where(lanef == ids[:, 1:2], offs, 0.0), axis=1, keepdims=True) + ids[:, 3:4]
    packed = jnp.where(lane == 0, p1, jnp.where(lane == 1, p2, 0.0))
    pos_ref[...] = packed.T[0:8, :].astype(I32)


def _positions(ids, counts, tm=512):
    n = ids.shape[0]
    assert MOE_TILE & (MOE_TILE - 1) == 0
    return pl.pallas_call(
        _pos_kernel,
        grid=(n // tm,),
        in_specs=[pl.BlockSpec((tm, LANES), lambda i: (i, 0)), pl.BlockSpec((1, LANES), lambda i: (0, 0))],
        out_specs=pl.BlockSpec((8, tm), lambda i: (0, i)),
        out_shape=jax.ShapeDtypeStruct((8, n), I32),
        compiler_params=_params(("parallel",)),
        name="moe_positions",
    )(ids, counts)


def _hier_moe(x2d, norm_g, w_rg, b_rg, w_re, b_re, w_gate, w_up, w_down, layer, xs_buf):
    n, D = x2d.shape
    tm = MOE_TILE
    pad = LANES - N_GROUPS - N_EXPERTS
    w_router = jnp.concatenate([w_rg, w_re, jnp.zeros((D, pad), F32)], axis=1)
    b_router = jnp.concatenate([b_rg, b_re.reshape(-1), jnp.zeros((pad,), F32)]).reshape(1, LANES)
    xn3, ids, wts, counts = _router(x2d, norm_g, w_router, b_router)
    cnt = counts[0, :N_EXPERTS]
    padded = ((cnt + tm - 1) // tm) * tm
    ends = jnp.cumsum(padded)
    n_tiles = xs_buf.shape[0] // tm
    tile_start = jnp.arange(n_tiles, dtype=I32) * tm
    tile_expert = jnp.minimum(jnp.sum(tile_start[:, None] >= ends[None, :], axis=1), N_EXPERTS - 1).astype(I32)
    n_valid = (ends[-1] // tm).astype(I32).reshape(1)
    last_e = tile_expert[jnp.maximum(n_valid[0] - 1, 0)]
    tile_expert = jnp.where(jnp.arange(n_tiles) < n_valid[0], tile_expert, last_e)
    pos = _positions(ids, counts)
    pos1, pos2 = pos[0], pos[1]
    xs = _dispatch(pos1, pos2, xn3, xs_buf)
    ys = _experts(tile_expert, n_valid, xs, w_gate, w_up, w_down, layer)
    return _combine(pos1, pos2, ys, x2d, wts), xs


def _in_proj_regions(D):
    bw = BRANCH_WIDTH
    sizes = (bw, bw, bw, bw, N_HEADS, N_HEADS, bw, bw, bw, IDX_HEADS * IDX_DIM, IDX_DIM, IDX_HEADS,
             bw, bw, bw, N_BRANCHES * D)
    offs = np.concatenate([[0], np.cumsum(sizes)]).tolist()
    region_a = (offs[0], offs[4] - offs[0])
    region_b = (offs[6], offs[10] - offs[6])
    region_c = (offs[12], offs[16] - offs[12])
    small = dict(mi=offs[4], mf=offs[5], ik=offs[10], iw=offs[11])
    return region_a, region_b, region_c, small


def _token_mixer(x2d, B, S, layer, norm_g, w_in, conv_w, b_i, b_f, mlstm_norm_g, q_norm_g, k_norm_g,
                 w_branch, w_out, rel_bias):
    n, D = x2d.shape
    L = MLSTM_CHUNK
    H = N_HEADS
    tm = min(n, PROJ_TM)
    xn = _rmsnorm(x2d, norm_g)
    ra, rb, rc, small = _in_proj_regions(D)
    w_t = jnp.swapaxes(w_in, 1, 2)
    proj_a = _in_proj(xn, w_t, layer, ra[0], ra[1], tm)
    proj_b = _in_proj(xn, w_t, layer, rb[0], rb[1], tm)
    proj_c = _in_proj(xn, w_t, layer, rc[0], rc[1], tm)
    tail = _tail_proj(xn, w_t, layer, small, tm)
    g = tail[1][:, IDX_HEADS:IDX_HEADS + 2 * H].reshape(B, S // L, L, 2, H)
    gates_t = jnp.transpose(g, (0, 4, 1, 3, 2))
    hm = _mlstm(proj_a, gates_t, jnp.stack([b_i, b_f]), conv_w, mlstm_norm_g, B, S)
    hd = _dsa(proj_b, tail, q_norm_g, k_norm_g, rel_bias, B, S)
    hs = _stick_breaking(proj_c, B, S)
    merged = _merge(hm, hd, hs, proj_c, w_branch, layer)
    return _matmul_residual(merged, w_out, layer, x2d)


def kernel(x, norm1_g, w_in, conv_w, b_i, b_f, mlstm_norm_g, q_norm_g, k_norm_g, w_branch, w_out, norm2_g,
           w_router_g, b_router_g, w_router_e, b_router_e, w_gate, w_up, w_down, rel_bias):
    B, S, D = x.shape
    x2d = x.reshape(B * S, D)
    xs_buf = jnp.zeros((2 * B * S + N_EXPERTS * MOE_TILE, D), F32)
    for l in range(w_in.shape[0]):
        x2d = _token_mixer(x2d, B, S, l, norm1_g[l], w_in, conv_w[l], b_i[l], b_f[l], mlstm_norm_g[l],
                           q_norm_g[l], k_norm_g[l], w_branch, w_out, rel_bias)
        x2d, xs_buf = _hier_moe(x2d, norm2_g[l], w_router_g[l], b_router_g[l], w_router_e[l], b_router_e[l],
                                w_gate, w_up, w_down, l, xs_buf)
    return x2d.reshape(B, S, D)
```

```python
import functools
import math

import numpy as np
import jax
import jax.numpy as jnp
from jax import lax
from jax.experimental import pallas as pl
from jax.experimental.pallas import tpu as pltpu

F32 = jnp.float32
BF16 = jnp.bfloat16
I32 = jnp.int32

LANES = 128
HEAD_DIM = 128
N_HEADS = 8
BRANCH_WIDTH = N_HEADS * HEAD_DIM
N_BRANCHES = 3
CONV_WIDTH = 4
IDX_HEADS = 16
IDX_DIM = 64
TOPK_MAX = 256
N_BUCKETS = 32
MAX_DISTANCE = 128
N_GROUPS = 4
EXPERTS_PER_GROUP = 8
N_EXPERTS = N_GROUPS * EXPERTS_PER_GROUP
EPS = 1e-6
NEG = -1e30
INT_MIN = -(2 ** 31)

MLSTM_CHUNK = 256
DSA_BLOCK = 128
DSA_BANDS = 4
DSA_QBLOCKS = 2
SB_BLOCK = 512
SB_ROWS = 512
SB_PIECES = 1
SB_HEADS = 4
HEADS_PER_STEP = 4
MOE_TILE = 256
DMA_UNROLL = 8
PROJ_TN = 512
PROJ_TM = 2048
VMEM_LIMIT = 56 * 1024 * 1024

A_MQ, A_MK, A_MV, A_MO = 0, 8, 16, 24
B_DQ, B_DK, B_DV, B_IQ = 0, 8, 16, 24
C_SQ, C_SK, C_SV, C_GATE = 0, 8, 16, 24


def _params(sem):
    return pltpu.CompilerParams(dimension_semantics=sem, vmem_limit_bytes=VMEM_LIMIT)


def _dot(a, b):
    return jnp.dot(a, b, preferred_element_type=F32)


def _dot_nt(a, b):
    return lax.dot_general(a, b, (((1,), (1,)), ((), ())), preferred_element_type=F32)


def _sigmoid(z):
    return 1.0 / (1.0 + jnp.exp(-z))


def _rmsnorm_kernel(x_ref, g_ref, o_ref):
    x = x_ref[...]
    ms = jnp.mean(x * x, axis=-1, keepdims=True)
    o_ref[...] = (x * lax.rsqrt(ms + EPS) * g_ref[...]).astype(o_ref.dtype)


def _rmsnorm(x2d, g, tm=512):
    n, d = x2d.shape
    return pl.pallas_call(
        _rmsnorm_kernel,
        grid=(n // tm,),
        in_specs=[pl.BlockSpec((tm, d), lambda i: (i, 0)), pl.BlockSpec((1, d), lambda i: (0, 0))],
        out_specs=pl.BlockSpec((tm, d), lambda i: (i, 0)),
        out_shape=jax.ShapeDtypeStruct((n, d), BF16),
        compiler_params=_params(("parallel",)),
        name="rmsnorm",
    )(x2d, g.reshape(1, d))


def _in_proj_kernel(x_ref, *rest, shift, nblk):
    w_refs, o_ref, wb = rest[:nblk], rest[nblk], rest[nblk + 1]
    tn = wb.shape[1]

    @pl.when(pl.program_id(1) == 0)
    def _():
        w = jnp.concatenate([r[...] for r in w_refs], axis=0)
        wb[...] = w[shift:shift + tn, :].T.astype(BF16)

    acc = _dot(x_ref[...], wb[...])
    for j in range(o_ref.shape[0]):
        o_ref[j] = acc[:, j * LANES:(j + 1) * LANES].astype(o_ref.dtype)


def _in_proj(xn, w_t, layer, col0, ncols, tm):
    m, k = xn.shape
    tn = PROJ_TN
    base, shift = divmod(col0, LANES)
    assert shift % 8 == 0
    nblk = tn // LANES + (1 if shift else 0)
    per = tn // LANES

    def wspec(r):
        return pl.BlockSpec((None, LANES, k), lambda j, i: (layer, base + per * j + r, 0))

    return pl.pallas_call(
        functools.partial(_in_proj_kernel, shift=shift, nblk=nblk),
        grid=(ncols // tn, m // tm),
        in_specs=[pl.BlockSpec((tm, k), lambda j, i: (i, 0))] + [wspec(r) for r in range(nblk)],
        out_specs=pl.BlockSpec((per, tm, LANES), lambda j, i: (j, i, 0)),
        out_shape=jax.ShapeDtypeStruct((ncols // LANES, m, LANES), BF16),
        scratch_shapes=[pltpu.VMEM((k, tn), BF16)],
        compiler_params=_params(("parallel", "arbitrary")),
        name="in_proj",
    )(xn, *([w_t] * nblk))


def _tail_kernel(x_ref, wg_ref, wi_ref, o_ref, wt, *, g_lane, ik_lane, iw_lane):
    @pl.when(pl.program_id(0) == 0)
    def _():
        wg = wg_ref[...]
        wi = wi_ref[...]
        ik = wi[ik_lane:ik_lane + IDX_DIM, :]
        iw = wi[iw_lane:iw_lane + IDX_HEADS, :]
        gates = wg[g_lane:g_lane + 2 * N_HEADS, :]
        pad = jnp.zeros((LANES - IDX_HEADS - 2 * N_HEADS, wg.shape[1]), F32)
        wt[...] = jnp.concatenate([ik, ik, iw, gates, pad], axis=0).T.astype(BF16)

    acc = _dot(x_ref[...], wt[...])
    o_ref[0] = acc[:, :LANES]
    o_ref[1] = acc[:, LANES:]


def _tail_proj(xn, w_t, layer, small, tm):
    m, k = xn.shape
    g_blk, g_lane = divmod(small["mi"], LANES)
    i_blk, ik_lane = divmod(small["ik"], LANES)
    iw_lane = small["iw"] - i_blk * LANES
    assert small["mf"] == small["mi"] + N_HEADS and g_lane + 2 * N_HEADS <= LANES
    assert ik_lane + IDX_DIM <= LANES and 0 <= iw_lane and iw_lane + IDX_HEADS <= LANES
    assert g_lane % 8 == 0 and ik_lane % 8 == 0 and iw_lane % 8 == 0
    return pl.pallas_call(
        functools.partial(_tail_kernel, g_lane=g_lane, ik_lane=ik_lane, iw_lane=iw_lane),
        grid=(m // tm,),
        in_specs=[pl.BlockSpec((tm, k), lambda i: (i, 0)),
                  pl.BlockSpec((None, LANES, k), lambda i: (layer, g_blk, 0)),
                  pl.BlockSpec((None, LANES, k), lambda i: (layer, i_blk, 0))],
        out_specs=pl.BlockSpec((2, tm, LANES), lambda i: (0, i, 0)),
        out_shape=jax.ShapeDtypeStruct((2, m, LANES), F32),
        scratch_shapes=[pltpu.VMEM((k, 2 * LANES), BF16)],
        compiler_params=_params(("arbitrary",)),
        name="tail_proj",
    )(xn, w_t, w_t)


def _mlstm_kernel(bias_ref, q_ref, k_ref, v_ref, o_ref, g_ref, cwq_ref, cwk_ref, ng_ref, out_ref,
                  qf, kf, qc, kc, st):
    hp = pl.program_id(1)
    U = HEADS_PER_STEP
    S = q_ref.shape[1]
    L = MLSTM_CHUNK
    nc = S // L
    d = HEAD_DIM
    PAD = 8

    R = min(S, 256)
    for u in range(U):
        qf[u, 0:PAD, :] = jnp.zeros((PAD, d), F32)
        kf[u, 0:PAD, :] = jnp.zeros((PAD, d), F32)
        qf[u, PAD:PAD + S, :] = q_ref[u].astype(F32)
        kf[u, PAD:PAD + S, :] = k_ref[u].astype(F32)
        ls = slice(u * d, (u + 1) * d)
        for r0 in range(0, S, R):
            aq = jnp.zeros((R, d), F32)
            ak = jnp.zeros((R, d), F32)
            for t in range(CONV_WIDTH):
                off = PAD - (CONV_WIDTH - 1) + t + r0
                aq = aq + cwq_ref[t:t + 1, ls] * qf[u, off:off + R, :]
                ak = ak + cwk_ref[t:t + 1, ls] * kf[u, off:off + R, :]
            qc[u, r0:r0 + R, :] = (aq * _sigmoid(aq) * (d ** -0.5)).astype(BF16)
            kc[u, r0:r0 + R, :] = ak * _sigmoid(ak)

    st[...] = jnp.zeros(st.shape, F32)
    row = lax.broadcasted_iota(I32, (L, L), 0)
    col = lax.broadcasted_iota(I32, (L, L), 1)
    causal = col <= row
    eye = col == row
    lane = lax.broadcasted_iota(I32, (L, d), 1)
    ones_col = jnp.where(lane == 0, 1.0, 0.0).astype(BF16)
    ng = ng_ref[...]

    def chunk_one(u, c, r, m):
        q = qc[u, pl.ds(r, L), :]
        kT = kc[u, pl.ds(r, L), :].T
        v = v_ref[u, pl.ds(r, L), :]
        vaug = jnp.concatenate([v, ones_col], axis=1)
        gates = g_ref[u, c]
        i_row = gates[0:1, :] + bias_ref[0, hp * U + u]
        f_row = gates[1:2, :] + bias_ref[1, hp * U + u]
        lf_row = jnp.minimum(f_row, 0.0) - jnp.log1p(jnp.exp(-jnp.abs(f_row)))
        b_col = jnp.sum(jnp.where(causal, lf_row, 0.0), axis=1, keepdims=True)
        b_row = jnp.sum(jnp.where(eye, b_col, 0.0), axis=0, keepdims=True)
        dlog = jnp.where(causal, b_col - b_row + i_row, NEG)
        inter = b_col + m
        m_t = jnp.maximum(inter, jnp.max(dlog, axis=1, keepdims=True))
        w_intra = jnp.exp(dlog - m_t)
        w_inter = jnp.exp(inter - m_t)
        s = _dot(q, kT.astype(BF16)) * w_intra
        res = w_inter * _dot(q, st[u].astype(BF16)) + _dot(s.astype(BF16), vaug)
        num = res[:, :d]
        den = res[:, d:d + 1]
        hh = num / jnp.maximum(jnp.abs(den), jnp.exp(-m_t))
        hn = hh * lax.rsqrt(jnp.mean(hh * hh, axis=1, keepdims=True) + EPS) * ng[:, u * d:(u + 1) * d]
        og = o_ref[u, pl.ds(r, L), :].astype(F32)
        out_ref[u, pl.ds(r, L), :] = (hn * _sigmoid(og)).astype(out_ref.dtype)
        ws_row = w_intra[L - 1:L, :]
        decay = w_inter[L - 1:L, :]
        st[u] = decay * st[u] + _dot((kT * ws_row).astype(BF16), vaug)
        return m_t[L - 1:L, :]

    def chunk(c, ms):
        r = pl.multiple_of(c * L, L)
        return tuple(chunk_one(u, c, r, ms[u]) for u in range(U))

    lax.fori_loop(0, nc, chunk, tuple(jnp.zeros((1, 1), F32) for _ in range(U)))


def _mlstm(proj_a, gates_t, bias_if, conv_w, norm_g, B, S):
    L = MLSTM_CHUNK
    nc = S // L
    d = HEAD_DIM
    H = N_HEADS
    U = HEADS_PER_STEP

    def slab(off):
        return pl.BlockSpec((U, S, d), lambda b, h: (off // U + h, b, 0))

    return pl.pallas_call(
        _mlstm_kernel,
        grid=(B, H // U),
        in_specs=[
            pl.BlockSpec(memory_space=pltpu.SMEM),
            slab(A_MQ), slab(A_MK), slab(A_MV), slab(A_MO),
            pl.BlockSpec((None, U, nc, 2, L), lambda b, h: (b, h, 0, 0, 0)),
            pl.BlockSpec((CONV_WIDTH, U * d), lambda b, h: (0, h)),
            pl.BlockSpec((CONV_WIDTH, U * d), lambda b, h: (0, H // U + h)),
            pl.BlockSpec((1, U * d), lambda b, h: (0, h)),
        ],
        out_specs=pl.BlockSpec((U, S, d), lambda b, h: (h, b, 0)),
        out_shape=jax.ShapeDtypeStruct((H, B * S, d), BF16),
        scratch_shapes=[
            pltpu.VMEM((U, S + 8, d), F32), pltpu.VMEM((U, S + 8, d), F32),
            pltpu.VMEM((U, S, d), BF16), pltpu.VMEM((U, S, d), F32),
            pltpu.VMEM((U, d, 2 * d), F32),
        ],
        compiler_params=_params(("parallel", "parallel")),
        name="mlstm",
    )(bias_if, proj_a, proj_a, proj_a, proj_a, gates_t, conv_w, conv_w, norm_g.reshape(1, H * d))


def _t5_thresholds():
    max_exact = N_BUCKETS // 2
    n = np.arange(0, 2 * MAX_DISTANCE)
    nf = np.maximum(n, 1).astype(np.float64)
    val = np.log(nf / max_exact) / math.log(MAX_DISTANCE / max_exact) * (N_BUCKETS - max_exact)
    frac = np.abs(val - np.round(val))
    frac_ok = (frac > 1e-4) | (n <= max_exact) | (n >= MAX_DISTANCE)
    assert frac_ok.all()
    large = np.minimum(max_exact + np.trunc(val).astype(np.int64), N_BUCKETS - 1)
    bucket = np.where(n < max_exact, n, large)
    assert (np.diff(bucket) >= 0).all() and bucket[MAX_DISTANCE] == N_BUCKETS - 1
    return [int(np.argmax(bucket >= j)) for j in range(1, N_BUCKETS)]


_T5_THR = _t5_thresholds()


def _dsa_prep_kernel(rb_ref, k_ref, ik_ref, gk_ref, kn, ikk, bias_ref):
    T = DSA_BLOCK
    H = N_HEADS
    row = lax.broadcasted_iota(I32, (T, T), 0)
    col = lax.broadcasted_iota(I32, (T, T), 1)
    gk = gk_ref[...]

    def norm_k(h, _):
        kh = k_ref[h].astype(F32)
        kn[h] = (kh * lax.rsqrt(jnp.mean(kh * kh, axis=1, keepdims=True) + EPS) * gk).astype(BF16)
        for o in range(2):
            n = o * T + row - col
            val = jnp.full((T, T), rb_ref[0, h], F32)
            for j, thr in enumerate(_T5_THR):
                val = jnp.where(n >= thr, rb_ref[j + 1, h], val)
            bias_ref[h, o] = val
        bias_ref[h, 2] = jnp.full((T, T), rb_ref[N_BUCKETS - 1, h], F32)
        return 0

    lax.fori_loop(0, H, norm_k, 0)
    ik = ik_ref[...]
    klane = lax.broadcasted_iota(I32, ik.shape, 1)
    ikk[0] = jnp.where(klane < IDX_DIM, ik, 0.0).astype(BF16)
    ikk[1] = jnp.where(klane >= IDX_DIM, ik, 0.0).astype(BF16)


def _dsa_kernel(q_ref, kn, v_ref, iq_ref, ikk, wt_ref, gq_ref, bias_ref, prev_ref, out_ref,
                wb, key_ref, mask_ref, *, topk, q0, nb, nqb):
    del prev_ref
    qis = [q0 + pl.program_id(1) * nqb + qb for qb in range(nqb)]
    T = DSA_BLOCK
    d = HEAD_DIM
    H = N_HEADS
    HALF = T // 2
    row = lax.broadcasted_iota(I32, (T, T), 0)
    col = lax.broadcasted_iota(I32, (T, T), 1)

    wsc = wt_ref[...] * (IDX_HEADS ** -0.5 * IDX_DIM ** -0.5)
    for qb in range(nqb):
        for h16 in range(IDX_HEADS):
            wb[qb * IDX_HEADS + h16] = jnp.broadcast_to(wsc[qb * T:(qb + 1) * T, h16:h16 + 1], (T, LANES))
    q_pairs = iq_ref[...].reshape((IDX_HEADS // 2) * nqb * T, LANES)

    def score_block(j, _):
        r = pl.multiple_of(j * T, T)
        kk = jnp.concatenate([ikk[0, pl.ds(r, T), :], ikk[1, pl.ds(r, T), :]], axis=0)
        dots = _dot_nt(q_pairs, kk)
        for qb in range(nqb):
            sc = jnp.zeros((T, T), F32)
            for h16 in range(IDX_HEADS):
                hp, odd = divmod(h16, 2)
                r0 = (hp * nqb + qb) * T
                sc = sc + wb[qb * IDX_HEADS + h16] * jnp.maximum(dots[r0:r0 + T, odd * T:(odd + 1) * T], 0.0)
            bits = lax.bitcast_convert_type(sc, I32)
            key = jnp.where(bits < 0, bits ^ jnp.int32(0x7FFFFFFF), bits)
            key = jnp.where(sc == 0.0, 0, key)
            key = jnp.where(j * T + col <= qis[qb] * T + row, key, INT_MIN)
            key_ref[qb, j] = key
        return 0

    lax.fori_loop(0, nb, score_block, 0, unroll=2)

    kf = float(topk)
    ones = jnp.ones((LANES, LANES), BF16)

    def count(chain, pred):
        qb, lo = chain
        acc = jnp.zeros((HALF, LANES), F32)
        for jb in range(nb):
            acc = acc + jnp.where(pred(key_ref[qb, jb, lo:lo + HALF, :]), 1.0, 0.0)
        return _dot(acc.astype(BF16), ones)

    chains = [(qb, lo) for qb in range(nqb) for lo in (0, HALF)]
    zero_i = jnp.zeros((HALF, LANES), I32)
    thr0 = tuple(jnp.where(count(ch, lambda k: k >= zero_i) >= kf, jnp.int32(0), jnp.int32(INT_MIN))
                 for ch in chains)

    def bisect_bit(thrs, bit):
        out = []
        for ch, t in zip(chains, thrs):
            cand = t | bit
            out.append(jnp.where(count(ch, lambda k, c=cand: k >= c) >= kf, cand, t))
        return tuple(out)

    def bisect_pair(it, thrs):
        hi_bit = jnp.left_shift(jnp.int32(1), 29 - 2 * it)
        lo_bit = jnp.left_shift(jnp.int32(1), 28 - 2 * it)
        out = []
        for ch, t in zip(chains, thrs):
            c01, c10, c11 = t | lo_bit, t | hi_bit, t | hi_bit | lo_bit
            n01, n10, n11 = (count(ch, lambda k, c=c: k >= c) >= kf for c in (c01, c10, c11))
            out.append(jnp.where(n11, c11, jnp.where(n10, c10, jnp.where(n01, c01, t))))
        return tuple(out)

    thrs = bisect_bit(thr0, jnp.int32(1 << 30))
    thrs = lax.fori_loop(0, 15, bisect_pair, thrs)
    needs = [kf - count(ch, lambda k, t=t: k > t) for ch, t in zip(chains, thrs)]

    tri = jnp.where(row <= col, 1.0, 0.0).astype(BF16)
    for qb in range(nqb):
        need = jnp.concatenate(needs[2 * qb:2 * qb + 2], axis=0)
        thrb = jnp.concatenate(thrs[2 * qb:2 * qb + 2], axis=0)
        seen = jnp.zeros((T, LANES), F32)
        for jb in range(nb):
            key = key_ref[qb, jb]
            tie = key == thrb
            tie16 = jnp.where(tie, 1.0, 0.0).astype(BF16)
            pre = _dot(tie16, tri) + seen
            m = jnp.where(key > thrb, 0.0, jnp.where(tie, jnp.where(pre <= need, 0.0, NEG), NEG))
            mask_ref[qb, jb] = jnp.where(key == INT_MIN, NEG, m)
            seen = seen + _dot(tie16, ones)

    gq = gq_ref[...]

    def head(h):
        qh = q_ref[h].astype(F32)
        qn = (qh * lax.rsqrt(jnp.mean(qh * qh, axis=1, keepdims=True) + EPS) * gq * (d ** -0.5)).astype(BF16)
        lgs = _dot_nt(qn, kn[h])
        p_rows, sums = [], []
        for qb in range(nqb):
            mx = None
            lg_blocks = []
            for jb in range(nb):
                lg = (lgs[qb * T:(qb + 1) * T, jb * T:(jb + 1) * T]
                      + bias_ref[h, jnp.clip(qis[qb] - jb, 0, 2)] + mask_ref[qb, jb])
                lg_blocks.append(lg)
                mx = lg if mx is None else jnp.maximum(mx, lg)
            rowmax = jnp.max(mx, axis=1, keepdims=True)
            l = jnp.zeros((T, T), F32)
            ps = []
            for jb in range(nb):
                p = jnp.exp(lg_blocks[jb] - rowmax)
                l = l + p
                ps.append(p.astype(BF16))
            p_rows.append(jnp.concatenate(ps, axis=1))
            sums.append(jnp.sum(l, axis=1, keepdims=True))
        acc = _dot(jnp.concatenate(p_rows, axis=0), v_ref[h])
        out_ref[h] = (acc / jnp.concatenate(sums, axis=0)).astype(out_ref.dtype)

    def head_group(g, _):
        for u in range(HEADS_PER_STEP):
            head(g * HEADS_PER_STEP + u)
        return 0

    lax.fori_loop(0, H // HEADS_PER_STEP, head_group, 0)


def _dsa(proj_b, tail, q_norm_g, k_norm_g, rel_bias, B, S):
    T = DSA_BLOCK
    nq = S // T
    d = HEAD_DIM
    H = N_HEADS
    topk = min(TOPK_MAX, S // 4)
    per = nq // DSA_BANDS
    nqb = math.gcd(per, DSA_QBLOCKS)
    TQ = nqb * T
    pb = proj_b.reshape(proj_b.shape[0], B, S, d)
    tl = tail.reshape(tail.shape[0], B, S, LANES)
    kn, ikk, bias = pl.pallas_call(
        _dsa_prep_kernel,
        grid=(B,),
        in_specs=[pl.BlockSpec(memory_space=pltpu.SMEM),
                  pl.BlockSpec((H, None, S, d), lambda b: (B_DK // H, b, 0, 0)),
                  pl.BlockSpec((None, None, S, LANES), lambda b: (0, b, 0, 0)),
                  pl.BlockSpec((1, d), lambda b: (0, 0))],
        out_specs=[pl.BlockSpec((H, None, S, d), lambda b: (0, b, 0, 0)),
                   pl.BlockSpec((2, None, S, LANES), lambda b: (0, b, 0, 0)),
                   pl.BlockSpec((H, 3, T, T), lambda b: (0, 0, 0, 0))],
        out_shape=[jax.ShapeDtypeStruct((H, B, S, d), BF16),
                   jax.ShapeDtypeStruct((2, B, S, LANES), BF16),
                   jax.ShapeDtypeStruct((H, 3, T, T), F32)],
        compiler_params=_params(("arbitrary",)),
        name="dsa_prep",
    )(rel_bias, pb, tl, k_norm_g.reshape(1, d))
    out = jnp.zeros((H, B, S, d), BF16)
    for band in range(DSA_BANDS):
        q0 = band * per
        nb = q0 + per
        W = nb * T
        in_specs = [
            pl.BlockSpec((H, None, TQ, d), lambda b, i: (B_DQ // H, b, q0 // nqb + i, 0)),
            pl.BlockSpec((H, None, W, d), lambda b, i: (0, b, 0, 0)),
            pl.BlockSpec((H, None, W, d), lambda b, i: (B_DV // H, b, 0, 0)),
            pl.BlockSpec((H, None, TQ, d), lambda b, i: (B_IQ // H, b, q0 // nqb + i, 0)),
            pl.BlockSpec((2, None, W, LANES), lambda b, i: (0, b, 0, 0)),
            pl.BlockSpec((None, None, TQ, LANES), lambda b, i: (1, b, q0 // nqb + i, 0)),
            pl.BlockSpec((1, d), lambda b, i: (0, 0)),
            pl.BlockSpec((H, 3, T, T), lambda b, i: (0, 0, 0, 0)),
            pl.BlockSpec(memory_space=pl.ANY),
        ]
        args = [pb, kn, pb, pb, ikk, tl, q_norm_g.reshape(1, d), bias, out]
        out = pl.pallas_call(
            functools.partial(_dsa_kernel, topk=topk, q0=q0, nb=nb, nqb=nqb),
            grid=(B, per // nqb),
            in_specs=in_specs,
            out_specs=pl.BlockSpec((H, None, TQ, d), lambda b, i: (0, b, q0 // nqb + i, 0)),
            out_shape=jax.ShapeDtypeStruct((H, B, S, d), BF16),
            scratch_shapes=[
                pltpu.VMEM((nqb * IDX_HEADS, T, LANES), F32),
                pltpu.VMEM((nqb, nb, T, T), I32),
                pltpu.VMEM((nqb, nb, T, T), F32),
            ],
            input_output_aliases={len(args) - 1: 0},
            compiler_params=_params(("parallel", "parallel")),
            name=f"dsa_band{band}",
        )(*args)
    return out.reshape(H, B * S, d)


def _sb_kernel(q_ref, k_ref, v_ref, out_ref):
    qi = pl.program_id(2)
    U = SB_HEADS
    T = SB_BLOCK
    d = HEAD_DIM
    R = SB_ROWS
    ns = T // R
    row = lax.broadcasted_iota(I32, (T, T), 0)
    col = lax.broadcasted_iota(I32, (T, T), 1)
    upper = jnp.where(row > col, 1.0, 0.0).astype(BF16)
    srow = lax.broadcasted_iota(I32, (R, T), 0)
    scol = lax.broadcasted_iota(I32, (R, T), 1)
    units = [(u, s) for u in range(U) for s in range(ns)]
    qs = [(q_ref[u, s * R:(s + 1) * R, :].astype(F32) * (d ** -0.5)).astype(BF16) for u, s in units]

    def block(n, j, diag, run, acc):
        u, s = units[n]
        r = pl.multiple_of(j * T, T)
        z = _dot_nt(qs[n], k_ref[u, pl.ds(r, T), :])
        sp = jnp.maximum(z, 0.0) + jnp.log(1.0 + jnp.exp(-jnp.abs(z)))
        strict = scol < srow + s * R
        ln = jnp.where(strict, -sp, 0.0) if diag else -sp
        parts = [ln.astype(BF16)]
        for _ in range(SB_PIECES - 1):
            parts.append((ln - sum(p.astype(F32) for p in parts)).astype(BF16))
        sums = _dot(jnp.concatenate(parts, axis=0), upper)
        suf = sum(sums[p * R:(p + 1) * R] for p in range(SB_PIECES))
        a = jnp.exp(z - sp + suf + run)
        if diag:
            a = jnp.where(strict, a, 0.0)
        acc = acc + _dot(a.astype(BF16), v_ref[u, pl.ds(r, T), :])
        return run + suf[:, 0:1] + ln[:, 0:1], acc

    state = []
    for n in range(len(units)):
        state.extend(block(n, qi, True, jnp.zeros((R, 1), F32), jnp.zeros((R, d), F32)))

    def earlier(i, st):
        out = []
        for n in range(len(units)):
            out.extend(block(n, qi - i, False, st[2 * n], st[2 * n + 1]))
        return tuple(out)

    state = lax.fori_loop(1, qi + 1, earlier, tuple(state))
    for n, (u, s) in enumerate(units):
        out_ref[u, s * R:(s + 1) * R, :] = state[2 * n + 1].astype(out_ref.dtype)


def _stick_breaking(proj_c, B, S):
    T = SB_BLOCK
    nq = S // T
    d = HEAD_DIM
    H = N_HEADS
    U = SB_HEADS
    return pl.pallas_call(
        _sb_kernel,
        grid=(B, H // U, nq),
        in_specs=[
            pl.BlockSpec((U, T, d), lambda b, h, i: (C_SQ // U + h, b * nq + i, 0)),
            pl.BlockSpec((U, S, d), lambda b, h, i: (C_SK // U + h, b, 0)),
            pl.BlockSpec((U, S, d), lambda b, h, i: (C_SV // U + h, b, 0)),
        ],
        out_specs=pl.BlockSpec((U, T, d), lambda b, h, i: (h, b * nq + i, 0)),
        out_shape=jax.ShapeDtypeStruct((H, B * S, d), BF16),
        compiler_params=_params(("parallel", "parallel", "parallel")),
        name="stick_breaking",
    )(proj_c, proj_c, proj_c)


def _merge_kernel(hm_ref, hd_ref, hs_ref, g0_ref, g1_ref, g2_ref, w_ref, o_ref, wb):
    H = N_HEADS

    @pl.when(pl.program_id(1) == 0)
    def _():
        wb[...] = w_ref[...].astype(BF16)

    acc = None
    ns = o_ref.shape[1] // LANES
    for n, (br, gr) in enumerate(((hm_ref, g0_ref), (hd_ref, g1_ref), (hs_ref, g2_ref))):
        a = jnp.concatenate([br[h] for h in range(H)], axis=1)
        up = _dot(a, wb[n])
        gate = jnp.concatenate([gr[s] for s in range(ns)], axis=1).astype(F32)
        term = _sigmoid(gate) * up
        acc = term if acc is None else acc + term
    o_ref[...] = acc.astype(o_ref.dtype)


def _merge(hm, hd, hs, proj_c, w_branch, layer, tm=1024, tn=512):
    H, n, d = hm.shape
    tm = min(tm, n)
    D = w_branch.shape[3]
    ns = tn // LANES
    br = pl.BlockSpec((H, tm, d), lambda j, i: (0, i, 0))

    def gate(b):
        first = (C_GATE + b * (D // LANES)) // ns
        return pl.BlockSpec((ns, tm, LANES), lambda j, i: (first + j, i, 0))

    return pl.pallas_call(
        _merge_kernel,
        grid=(D // tn, n // tm),
        in_specs=[br, br, br, gate(0), gate(1), gate(2),
                  pl.BlockSpec((None, N_BRANCHES, H * d, tn), lambda j, i: (layer, 0, 0, j))],
        out_specs=pl.BlockSpec((tm, tn), lambda j, i: (i, j)),
        out_shape=jax.ShapeDtypeStruct((n, D), BF16),
        scratch_shapes=[pltpu.VMEM((N_BRANCHES, H * d, tn), BF16)],
        compiler_params=_params(("parallel", "arbitrary")),
        name="branch_merge",
    )(hm, hd, hs, proj_c, proj_c, proj_c, w_branch)


def _mm_res_kernel(a_ref, w_ref, x_ref, o_ref, wb):
    @pl.when(pl.program_id(1) == 0)
    def _():
        wb[...] = w_ref[...].astype(BF16)

    o_ref[...] = x_ref[...] + _dot(a_ref[...], wb[...])


def _matmul_residual(a, w, layer, x, tm=1024, tn=512):
    m, k = a.shape
    tm = min(tm, m)
    n = w.shape[2]
    return pl.pallas_call(
        _mm_res_kernel,
        grid=(n // tn, m // tm),
        in_specs=[pl.BlockSpec((tm, k), lambda j, i: (i, 0)),
                  pl.BlockSpec((None, k, tn), lambda j, i: (layer, 0, j)),
                  pl.BlockSpec((tm, tn), lambda j, i: (i, j))],
        out_specs=pl.BlockSpec((tm, tn), lambda j, i: (i, j)),
        out_shape=jax.ShapeDtypeStruct((m, n), F32),
        scratch_shapes=[pltpu.VMEM((k, tn), BF16)],
        compiler_params=_params(("parallel", "arbitrary")),
        name="out_proj",
    )(a, w, x)


def _router_kernel(x_ref, g_ref, wr_ref, br_ref, xn_ref, ids_ref, wts_ref, cnt_ref, carry, lower):
    i = pl.program_id(0)
    tm, D = x_ref.shape

    @pl.when(i == 0)
    def _():
        carry[...] = jnp.zeros(carry.shape, F32)
        r = lax.broadcasted_iota(I32, (tm, tm), 0)
        c = lax.broadcasted_iota(I32, (tm, tm), 1)
        lower[...] = jnp.where(c < r, 1.0, 0.0).astype(BF16)

    x = x_ref[...]
    xn = x * lax.rsqrt(jnp.mean(x * x, axis=-1, keepdims=True) + EPS) * g_ref[...]
    xn_ref[...] = xn
    logits = jnp.dot(xn, wr_ref[...], preferred_element_type=F32, precision=lax.Precision.HIGHEST)
    biased = logits + br_ref[...]
    lane = lax.broadcasted_iota(I32, (tm, LANES), 1)
    lanef = lane.astype(F32)
    big = float(LANES)

    def first_lane(mask):
        return jnp.min(jnp.where(mask, lanef, big), axis=1, keepdims=True)

    gmask = lane < N_GROUPS
    gmax = jnp.max(jnp.where(gmask, biased, NEG), axis=1, keepdims=True)
    g_sel = first_lane(gmask & (biased == gmax))
    gm = jnp.max(jnp.where(gmask, logits, NEG), axis=1, keepdims=True)
    ge = jnp.where(gmask, jnp.exp(logits - gm), 0.0)
    p_group = jnp.sum(jnp.where(lanef == g_sel, ge, 0.0), axis=1, keepdims=True) / jnp.sum(ge, axis=1, keepdims=True)
    lo = N_GROUPS + EXPERTS_PER_GROUP * g_sel
    emask = (lanef >= lo) & (lanef < lo + EXPERTS_PER_GROUP)
    eb = jnp.where(emask, biased, NEG)
    e1 = first_lane(emask & (eb == jnp.max(eb, axis=1, keepdims=True)))
    emask2 = emask & (lanef != e1)
    eb2 = jnp.where(emask2, biased, NEG)
    e2 = first_lane(emask2 & (eb2 == jnp.max(eb2, axis=1, keepdims=True)))
    em = jnp.max(jnp.where(emask, logits, NEG), axis=1, keepdims=True)
    ee = jnp.where(emask, jnp.exp(logits - em), 0.0)
    s1 = jnp.sum(jnp.where(lanef == e1, ee, 0.0), axis=1, keepdims=True)
    s2 = jnp.sum(jnp.where(lanef == e2, ee, 0.0), axis=1, keepdims=True)
    se = jnp.sum(ee, axis=1, keepdims=True)
    w1 = s1 / se
    w2 = s2 / se
    wsum = w1 + w2
    w1 = p_group * w1 / wsum
    w2 = p_group * w2 / wsum
    x1 = e1 - N_GROUPS
    x2 = e2 - N_GROUPS

    onehot = jnp.where((lanef == x1) | (lanef == x2), 1.0, 0.0)
    prefix = _dot(lower[...], onehot.astype(BF16)) + carry[...]
    r1 = jnp.sum(jnp.where(lanef == x1, prefix, 0.0), axis=1, keepdims=True)
    r2 = jnp.sum(jnp.where(lanef == x2, prefix, 0.0), axis=1, keepdims=True)
    carry[...] = carry[...] + jnp.sum(onehot, axis=0, keepdims=True)
    cnt_ref[...] = carry[...].astype(I32)

    idsf = jnp.where(lane == 0, x1, jnp.where(lane == 1, x2, jnp.where(lane == 2, r1, jnp.where(lane == 3, r2, 0.0))))
    ids_ref[...] = idsf.astype(I32)
    wts_ref[...] = jnp.where(lane == 0, w1, jnp.where(lane == 1, w2, 0.0))


def _router(x2d, g, w_router, b_router, tm=512):
    n, D = x2d.shape
    return pl.pallas_call(
        _router_kernel,
        grid=(n // tm,),
        in_specs=[pl.BlockSpec((tm, D), lambda i: (i, 0)), pl.BlockSpec((1, D), lambda i: (0, 0)),
                  pl.BlockSpec((D, LANES), lambda i: (0, 0)), pl.BlockSpec((1, LANES), lambda i: (0, 0))],
        out_specs=[pl.BlockSpec((tm, D), lambda i: (i, 0)),
                   pl.BlockSpec((tm, LANES), lambda i: (i, 0)),
                   pl.BlockSpec((tm, LANES), lambda i: (i, 0)),
                   pl.BlockSpec((1, LANES), lambda i: (0, 0))],
        out_shape=[jax.ShapeDtypeStruct((n, D), F32),
                   jax.ShapeDtypeStruct((n, LANES), I32),
                   jax.ShapeDtypeStruct((n, LANES), F32),
                   jax.ShapeDtypeStruct((1, LANES), I32)],
        scratch_shapes=[pltpu.VMEM((1, LANES), F32), pltpu.VMEM((tm, tm), BF16)],
        compiler_params=_params(("arbitrary",)),
        name="moe_router",
    )(x2d, g.reshape(1, D), w_router, b_router)


def _dispatch_kernel(p1_ref, p2_ref, xn_ref, xs_in_ref, xs_ref, sem, *, tb):
    del xs_in_ref
    base = pl.program_id(0) * tb

    def copies(t):
        src = xn_ref.at[pl.ds(t, 1), :]
        return (pltpu.make_async_copy(src, xs_ref.at[pl.ds(p1_ref[base + t], 1), :], sem),
                pltpu.make_async_copy(src, xs_ref.at[pl.ds(p2_ref[base + t], 1), :], sem))

    def issue(t, _):
        for cp in copies(t):
            cp.start()
        return 0

    lax.fori_loop(0, tb, issue, 0, unroll=DMA_UNROLL)

    def drain(t, _):
        for cp in copies(t):
            cp.wait()
        return 0

    lax.fori_loop(0, tb, drain, 0, unroll=DMA_UNROLL)


def _dispatch(pos1, pos2, xn, xs0, tb=512):
    n, D = xn.shape
    n_rows = xs0.shape[0]
    tb = min(tb, n)
    grid_spec = pltpu.PrefetchScalarGridSpec(
        num_scalar_prefetch=2,
        grid=(n // tb,),
        in_specs=[pl.BlockSpec((tb, D), lambda i, p1, p2: (i, 0)),
                  pl.BlockSpec(memory_space=pl.ANY)],
        out_specs=pl.BlockSpec(memory_space=pl.ANY),
        scratch_shapes=[pltpu.SemaphoreType.DMA(())],
    )
    return pl.pallas_call(
        functools.partial(_dispatch_kernel, tb=tb),
        grid_spec=grid_spec,
        out_shape=jax.ShapeDtypeStruct((n_rows, D), F32),
        input_output_aliases={3: 0},
        compiler_params=pltpu.CompilerParams(dimension_semantics=("arbitrary",), has_side_effects=True),
        name="moe_dispatch",
    )(pos1, pos2, xn, xs0)


def _expert_kernel(te_ref, nv_ref, ne_ref, grp_ref, xs_ref, wg_ref, wu_ref, wd_ref, ys_ref,
                   wgf, wuf, wdf, wgb, wub, wdb, sem, *, layer):
    i = pl.program_id(0)
    prev = te_ref[jnp.maximum(i - 1, 0)]

    def fetch(e, slot):
        return (pltpu.make_async_copy(wg_ref.at[layer, e], wgf.at[slot], sem.at[slot]),
                pltpu.make_async_copy(wu_ref.at[layer, e], wuf.at[slot], sem.at[slot]),
                pltpu.make_async_copy(wd_ref.at[layer, e], wdf.at[slot], sem.at[slot]))

    @pl.when(i < nv_ref[0])
    def _():
        @pl.when(i == 0)
        def _():
            for cp in fetch(te_ref[0], 0):
                cp.start()

        @pl.when((i == 0) | (te_ref[i] != prev))
        def _():
            slot = grp_ref[i] % 2
            for cp in fetch(te_ref[i], slot):
                cp.wait()

            @pl.when(ne_ref[i] >= 0)
            def _():
                for cp in fetch(ne_ref[i], 1 - slot):
                    cp.start(priority=1)

            wgb[...] = wgf[slot].astype(BF16)
            wub[...] = wuf[slot].astype(BF16)
            wdb[...] = wdf[slot].astype(BF16)

        x = xs_ref[...].astype(BF16)
        g = _dot(x, wgb[...])
        u = _dot(x, wub[...])
        hcur = (g * _sigmoid(g) * u).astype(BF16)
        ys_ref[...] = _dot(hcur, wdb[...])

    @pl.when(i >= nv_ref[0])
    def _():
        ys_ref[...] = jnp.zeros(ys_ref.shape, F32)


def _experts(tile_expert, n_valid, xs, w_gate, w_up, w_down, layer):
    n_rows = xs.shape[0]
    _, E, D, Fe = w_gate.shape
    tm = MOE_TILE
    n_tiles = n_rows // tm
    tiles = jnp.arange(n_tiles, dtype=I32)
    first = (tiles < n_valid[0]) & ((tiles == 0) | (tile_expert != jnp.roll(tile_expert, 1)))
    group = (jnp.cumsum(first.astype(I32)) - 1).astype(I32)
    later = lax.cummin(jnp.where(first, tiles, n_tiles), reverse=True)
    nxt = jnp.concatenate([later[1:], jnp.full((1,), n_tiles, I32)])
    next_expert = jnp.where(nxt < n_tiles, tile_expert[jnp.minimum(nxt, n_tiles - 1)], -1).astype(I32)

    def row_map(i, te, nv, ne, grp):
        return (jnp.minimum(i, jnp.maximum(nv[0] - 1, 0)), 0)

    grid_spec = pltpu.PrefetchScalarGridSpec(
        num_scalar_prefetch=4,
        grid=(n_tiles,),
        in_specs=[pl.BlockSpec((tm, D), row_map),
                  pl.BlockSpec(memory_space=pl.ANY), pl.BlockSpec(memory_space=pl.ANY),
                  pl.BlockSpec(memory_space=pl.ANY)],
        out_specs=pl.BlockSpec((tm, D), lambda i, te, nv, ne, grp: (i, 0)),
        scratch_shapes=[pltpu.VMEM((2, D, Fe), F32), pltpu.VMEM((2, D, Fe), F32), pltpu.VMEM((2, Fe, D), F32),
                        pltpu.VMEM((D, Fe), BF16), pltpu.VMEM((D, Fe), BF16), pltpu.VMEM((Fe, D), BF16),
                        pltpu.SemaphoreType.DMA((2,))],
    )
    return pl.pallas_call(
        functools.partial(_expert_kernel, layer=layer),
        grid_spec=grid_spec,
        out_shape=jax.ShapeDtypeStruct((n_rows, D), F32),
        compiler_params=_params(("arbitrary",)),
        name="moe_experts",
    )(tile_expert, n_valid, next_expert, group, xs, w_gate, w_up, w_down)


def _combine_kernel(p1_ref, p2_ref, ys_ref, x_ref, w_ref, o_ref, buf, sem):
    i = pl.program_id(0)
    tc = x_ref.shape[0]
    slot = i % 2

    def copies(tile, slot, t):
        tok = tile * tc + t
        return (pltpu.make_async_copy(ys_ref.at[pl.ds(p1_ref[tok], 1), :], buf.at[slot, 0, pl.ds(t, 1), :],
                                      sem.at[slot]),
                pltpu.make_async_copy(ys_ref.at[pl.ds(p2_ref[tok], 1), :], buf.at[slot, 1, pl.ds(t, 1), :],
                                      sem.at[slot]))

    def issue(tile, slot):
        def body(t, _):
            for cp in copies(tile, slot, t):
                cp.start()
            return 0

        lax.fori_loop(0, tc, body, 0, unroll=DMA_UNROLL)

    @pl.when(i == 0)
    def _():
        issue(0, 0)

    @pl.when(i + 1 < pl.num_programs(0))
    def _():
        issue(i + 1, 1 - slot)

    def drain(t, _):
        for cp in copies(i, slot, t):
            cp.wait()
        return 0

    lax.fori_loop(0, tc, drain, 0, unroll=DMA_UNROLL)
    w = w_ref[...]
    o_ref[...] = x_ref[...] + w[:, 0:1] * buf[slot, 0] + w[:, 1:2] * buf[slot, 1]


def _combine(pos1, pos2, ys, x2d, wts, tc=256):
    n, D = x2d.shape
    grid_spec = pltpu.PrefetchScalarGridSpec(
        num_scalar_prefetch=2,
        grid=(n // tc,),
        in_specs=[pl.BlockSpec(memory_space=pl.ANY),
                  pl.BlockSpec((tc, D), lambda i, p1, p2: (i, 0)),
                  pl.BlockSpec((tc, LANES), lambda i, p1, p2: (i, 0))],
        out_specs=pl.BlockSpec((tc, D), lambda i, p1, p2: (i, 0)),
        scratch_shapes=[pltpu.VMEM((2, 2, tc, D), F32), pltpu.SemaphoreType.DMA((2,))],
    )
    return pl.pallas_call(
        _combine_kernel,
        grid_spec=grid_spec,
        out_shape=jax.ShapeDtypeStruct((n, D), F32),
        compiler_params=_params(("arbitrary",)),
        name="moe_combine",
    )(pos1, pos2, ys, x2d, wts)


def _pos_kernel(ids_ref, cnt_ref, pos_ref):
    tm = ids_ref.shape[0]
    tiles = jnp.floor((cnt_ref[...].astype(F32) + (MOE_TILE - 1)) / MOE_TILE)
    r = lax.broadcasted_iota(I32, (LANES, LANES), 0)
    c = lax.broadcasted_iota(I32, (LANES, LANES), 1)
    before = jnp.where(r < c, 1.0, 0.0).astype(BF16)
    first_tile = _dot(jnp.broadcast_to(tiles, (8, LANES)).astype(BF16), before)
    offs = first_tile[0:1, :] * MOE_TILE
    ids = ids_ref[...].astype(F32)
    lane = lax.broadcasted_iota(I32, (tm, LANES), 1)
    lanef = lane.astype(F32)
    p1 = jnp.sum(jnp.where(lanef == ids[:, 0:1], offs, 0.0), axis=1, keepdims=True) + ids[:, 2:3]
    p2 = jnp.sum(jnp.where(lanef == ids[:, 1:2], offs, 0.0), axis=1, keepdims=True) + ids[:, 3:4]
    packed = jnp.where(lane == 0, p1, jnp.where(lane == 1, p2, 0.0))
    pos_ref[...] = packed.T[0:8, :].astype(I32)


def _positions(ids, counts, tm=512):
    n = ids.shape[0]
    assert MOE_TILE & (MOE_TILE - 1) == 0
    return pl.pallas_call(
        _pos_kernel,
        grid=(n // tm,),
        in_specs=[pl.BlockSpec((tm, LANES), lambda i: (i, 0)), pl.BlockSpec((1, LANES), lambda i: (0, 0))],
        out_specs=pl.BlockSpec((8, tm), lambda i: (0, i)),
        out_shape=jax.ShapeDtypeStruct((8, n), I32),
        compiler_params=_params(("parallel",)),
        name="moe_positions",
    )(ids, counts)


def _hier_moe(x2d, norm_g, w_rg, b_rg, w_re, b_re, w_gate, w_up, w_down, layer, xs_buf):
    n, D = x2d.shape
    tm = MOE_TILE
    pad = LANES - N_GROUPS - N_EXPERTS
    w_router = jnp.concatenate([w_rg, w_re, jnp.zeros((D, pad), F32)], axis=1)
    b_router = jnp.concatenate([b_rg, b_re.reshape(-1), jnp.zeros((pad,), F32)]).reshape(1, LANES)
    xn3, ids, wts, counts = _router(x2d, norm_g, w_router, b_router)
    cnt = counts[0, :N_EXPERTS]
    padded = ((cnt + tm - 1) // tm) * tm
    ends = jnp.cumsum(padded)
    n_tiles = xs_buf.shape[0] // tm
    tile_start = jnp.arange(n_tiles, dtype=I32) * tm
    tile_expert = jnp.minimum(jnp.sum(tile_start[:, None] >= ends[None, :], axis=1), N_EXPERTS - 1).astype(I32)
    n_valid = (ends[-1] // tm).astype(I32).reshape(1)
    last_e = tile_expert[jnp.maximum(n_valid[0] - 1, 0)]
    tile_expert = jnp.where(jnp.arange(n_tiles) < n_valid[0], tile_expert, last_e)
    pos = _positions(ids, counts)
    pos1, pos2 = pos[0], pos[1]
    xs = _dispatch(pos1, pos2, xn3, xs_buf)
    ys = _experts(tile_expert, n_valid, xs, w_gate, w_up, w_down, layer)
    return _combine(pos1, pos2, ys, x2d, wts), xs


def _in_proj_regions(D):
    bw = BRANCH_WIDTH
    sizes = (bw, bw, bw, bw, N_HEADS, N_HEADS, bw, bw, bw, IDX_HEADS * IDX_DIM, IDX_DIM, IDX_HEADS,
             bw, bw, bw, N_BRANCHES * D)
    offs = np.concatenate([[0], np.cumsum(sizes)]).tolist()
    region_a = (offs[0], offs[4] - offs[0])
    region_b = (offs[6], offs[10] - offs[6])
    region_c = (offs[12], offs[16] - offs[12])
    small = dict(mi=offs[4], mf=offs[5], ik=offs[10], iw=offs[11])
    return region_a, region_b, region_c, small


def _token_mixer(x2d, B, S, layer, norm_g, w_in, conv_w, b_i, b_f, mlstm_norm_g, q_norm_g, k_norm_g,
                 w_branch, w_out, rel_bias):
    n, D = x2d.shape
    L = MLSTM_CHUNK
    H = N_HEADS
    tm = min(n, PROJ_TM)
    xn = _rmsnorm(x2d, norm_g)
    ra, rb, rc, small = _in_proj_regions(D)
    w_t = jnp.swapaxes(w_in, 1, 2)
    proj_a = _in_proj(xn, w_t, layer, ra[0], ra[1], tm)
    proj_b = _in_proj(xn, w_t, layer, rb[0], rb[1], tm)
    proj_c = _in_proj(xn, w_t, layer, rc[0], rc[1], tm)
    tail = _tail_proj(xn, w_t, layer, small, tm)
    g = tail[1][:, IDX_HEADS:IDX_HEADS + 2 * H].reshape(B, S // L, L, 2, H)
    gates_t = jnp.transpose(g, (0, 4, 1, 3, 2))
    hm = _mlstm(proj_a, gates_t, jnp.stack([b_i, b_f]), conv_w, mlstm_norm_g, B, S)
    hd = _dsa(proj_b, tail, q_norm_g, k_norm_g, rel_bias, B, S)
    hs = _stick_breaking(proj_c, B, S)
    merged = _merge(hm, hd, hs, proj_c, w_branch, layer)
    return _matmul_residual(merged, w_out, layer, x2d)


def kernel(x, norm1_g, w_in, conv_w, b_i, b_f, mlstm_norm_g, q_norm_g, k_norm_g, w_branch, w_out, norm2_g,
           w_router_g, b_router_g, w_router_e, b_router_e, w_gate, w_up, w_down, rel_bias):
    B, S, D = x.shape
    x2d = x.reshape(B * S, D)
    xs_buf = jnp.zeros((2 * B * S + N_EXPERTS * MOE_TILE, D), F32)
    for l in range(w_in.shape[0]):
        x2d = _token_mixer(x2d, B, S, l, norm1_g[l], w_in, conv_w[l], b_i[l], b_f[l], mlstm_norm_g[l],
                           q_norm_g[l], k_norm_g[l], w_branch, w_out, rel_bias)
        x2d, xs_buf = _hier_moe(x2d, norm2_g[l], w_router_g[l], b_router_g[l], w_router_e[l], b_router_e[l],
                                w_gate, w_up, w_down, l, xs_buf)
    return x2d.reshape(B, S, D)
```

```python
import functools
import math

import numpy as np
import jax
import jax.numpy as jnp
from jax import lax
from jax.experimental import pallas as pl
from jax.experimental.pallas import tpu as pltpu

F32 = jnp.float32
BF16 = jnp.bfloat16
I32 = jnp.int32

LANES = 128
HEAD_DIM = 128
N_HEADS = 8
BRANCH_WIDTH = N_HEADS * HEAD_DIM
N_BRANCHES = 3
CONV_WIDTH = 4
IDX_HEADS = 16
IDX_DIM = 64
TOPK_MAX = 256
N_BUCKETS = 32
MAX_DISTANCE = 128
N_GROUPS = 4
EXPERTS_PER_GROUP = 8
N_EXPERTS = N_GROUPS * EXPERTS_PER_GROUP
EPS = 1e-6
NEG = -1e30
INT_MIN = -(2 ** 31)

MLSTM_CHUNK = 256
DSA_BLOCK = 128
DSA_BANDS = 4
DSA_QBLOCKS = 2
SB_BLOCK = 512
SB_ROWS = 512
SB_PIECES = 1
SB_HEADS = 4
HEADS_PER_STEP = 4
MOE_TILE = 256
DMA_UNROLL = 8
PROJ_TN = 512
PROJ_TM = 2048
VMEM_LIMIT = 56 * 1024 * 1024

A_MQ, A_MK, A_MV, A_MO = 0, 8, 16, 24
B_DQ, B_DK, B_DV, B_IQ = 0, 8, 16, 24
C_SQ, C_SK, C_SV, C_GATE = 0, 8, 16, 24


def _params(sem):
    return pltpu.CompilerParams(dimension_semantics=sem, vmem_limit_bytes=VMEM_LIMIT)


def _dot(a, b):
    return jnp.dot(a, b, preferred_element_type=F32)


def _dot_nt(a, b):
    return lax.dot_general(a, b, (((1,), (1,)), ((), ())), preferred_element_type=F32)


def _sigmoid(z):
    return 1.0 / (1.0 + jnp.exp(-z))


def _rmsnorm_kernel(x_ref, g_ref, o_ref):
    x = x_ref[...]
    ms = jnp.mean(x * x, axis=-1, keepdims=True)
    o_ref[...] = (x * lax.rsqrt(ms + EPS) * g_ref[...]).astype(o_ref.dtype)


def _rmsnorm(x2d, g, tm=512):
    n, d = x2d.shape
    return pl.pallas_call(
        _rmsnorm_kernel,
        grid=(n // tm,),
        in_specs=[pl.BlockSpec((tm, d), lambda i: (i, 0)), pl.BlockSpec((1, d), lambda i: (0, 0))],
        out_specs=pl.BlockSpec((tm, d), lambda i: (i, 0)),
        out_shape=jax.ShapeDtypeStruct((n, d), BF16),
        compiler_params=_params(("parallel",)),
        name="rmsnorm",
    )(x2d, g.reshape(1, d))


def _in_proj_kernel(x_ref, *rest, shift, nblk):
    w_refs, o_ref, wb = rest[:nblk], rest[nblk], rest[nblk + 1]
    tn = wb.shape[1]

    @pl.when(pl.program_id(1) == 0)
    def _():
        w = jnp.concatenate([r[...] for r in w_refs], axis=0)
        wb[...] = w[shift:shift + tn, :].T.astype(BF16)

    acc = _dot(x_ref[...], wb[...])
    for j in range(o_ref.shape[0]):
        o_ref[j] = acc[:, j * LANES:(j + 1) * LANES].astype(o_ref.dtype)


def _in_proj(xn, w_t, layer, col0, ncols, tm):
    m, k = xn.shape
    tn = PROJ_TN
    base, shift = divmod(col0, LANES)
    assert shift % 8 == 0
    nblk = tn // LANES + (1 if shift else 0)
    per = tn // LANES

    def wspec(r):
        return pl.BlockSpec((None, LANES, k), lambda j, i: (layer, base + per * j + r, 0))

    return pl.pallas_call(
        functools.partial(_in_proj_kernel, shift=shift, nblk=nblk),
        grid=(ncols // tn, m // tm),
        in_specs=[pl.BlockSpec((tm, k), lambda j, i: (i, 0))] + [wspec(r) for r in range(nblk)],
        out_specs=pl.BlockSpec((per, tm, LANES), lambda j, i: (j, i, 0)),
        out_shape=jax.ShapeDtypeStruct((ncols // LANES, m, LANES), BF16),
        scratch_shapes=[pltpu.VMEM((k, tn), BF16)],
        compiler_params=_params(("parallel", "arbitrary")),
        name="in_proj",
    )(xn, *([w_t] * nblk))


def _tail_kernel(x_ref, wg_ref, wi_ref, o_ref, wt, *, g_lane, ik_lane, iw_lane):
    @pl.when(pl.program_id(0) == 0)
    def _():
        wg = wg_ref[...]
        wi = wi_ref[...]
        ik = wi[ik_lane:ik_lane + IDX_DIM, :]
        iw = wi[iw_lane:iw_lane + IDX_HEADS, :]
        gates = wg[g_lane:g_lane + 2 * N_HEADS, :]
        pad = jnp.zeros((LANES - IDX_HEADS - 2 * N_HEADS, wg.shape[1]), F32)
        wt[...] = jnp.concatenate([ik, ik, iw, gates, pad], axis=0).T.astype(BF16)

    acc = _dot(x_ref[...], wt[...])
    o_ref[0] = acc[:, :LANES]
    o_ref[1] = acc[:, LANES:]


def _tail_proj(xn, w_t, layer, small, tm):
    m, k = xn.shape
    g_blk, g_lane = divmod(small["mi"], LANES)
    i_blk, ik_lane = divmod(small["ik"], LANES)
    iw_lane = small["iw"] - i_blk * LANES
    assert small["mf"] == small["mi"] + N_HEADS and g_lane + 2 * N_HEADS <= LANES
    assert ik_lane + IDX_DIM <= LANES and 0 <= iw_lane and iw_lane + IDX_HEADS <= LANES
    assert g_lane % 8 == 0 and ik_lane % 8 == 0 and iw_lane % 8 == 0
    return pl.pallas_call(
        functools.partial(_tail_kernel, g_lane=g_lane, ik_lane=ik_lane, iw_lane=iw_lane),
        grid=(m // tm,),
        in_specs=[pl.BlockSpec((tm, k), lambda i: (i, 0)),
                  pl.BlockSpec((None, LANES, k), lambda i: (layer, g_blk, 0)),
                  pl.BlockSpec((None, LANES, k), lambda i: (layer, i_blk, 0))],
        out_specs=pl.BlockSpec((2, tm, LANES), lambda i: (0, i, 0)),
        out_shape=jax.ShapeDtypeStruct((2, m, LANES), F32),
        scratch_shapes=[pltpu.VMEM((k, 2 * LANES), BF16)],
        compiler_params=_params(("arbitrary",)),
        name="tail_proj",
    )(xn, w_t, w_t)


def _mlstm_kernel(bias_ref, q_ref, k_ref, v_ref, o_ref, g_ref, cwq_ref, cwk_ref, ng_ref, out_ref,
                  qf, kf, qc, kc, st):
    hp = pl.program_id(1)
    U = HEADS_PER_STEP
    S = q_ref.shape[1]
    L = MLSTM_CHUNK
    nc = S // L
    d = HEAD_DIM
    PAD = 8

    R = min(S, 256)
    for u in range(U):
        qf[u, 0:PAD, :] = jnp.zeros((PAD, d), F32)
        kf[u, 0:PAD, :] = jnp.zeros((PAD, d), F32)
        qf[u, PAD:PAD + S, :] = q_ref[u].astype(F32)
        kf[u, PAD:PAD + S, :] = k_ref[u].astype(F32)
        ls = slice(u * d, (u + 1) * d)
        for r0 in range(0, S, R):
            aq = jnp.zeros((R, d), F32)
            ak = jnp.zeros((R, d), F32)
            for t in range(CONV_WIDTH):
                off = PAD - (CONV_WIDTH - 1) + t + r0
                aq = aq + cwq_ref[t:t + 1, ls] * qf[u, off:off + R, :]
                ak = ak + cwk_ref[t:t + 1, ls] * kf[u, off:off + R, :]
            qc[u, r0:r0 + R, :] = (aq * _sigmoid(aq) * (d ** -0.5)).astype(BF16)
            kc[u, r0:r0 + R, :] = ak * _sigmoid(ak)

    st[...] = jnp.zeros(st.shape, F32)
    row = lax.broadcasted_iota(I32, (L, L), 0)
    col = lax.broadcasted_iota(I32, (L, L), 1)
    causal = col <= row
    eye = col == row
    lane = lax.broadcasted_iota(I32, (L, d), 1)
    ones_col = jnp.where(lane == 0, 1.0, 0.0).astype(BF16)
    ng = ng_ref[...]

    def chunk_one(u, c, r, m):
        q = qc[u, pl.ds(r, L), :]
        kT = kc[u, pl.ds(r, L), :].T
        v = v_ref[u, pl.ds(r, L), :]
        vaug = jnp.concatenate([v, ones_col], axis=1)
        gates = g_ref[u, c]
        i_row = gates[0:1, :] + bias_ref[0, hp * U + u]
        f_row = gates[1:2, :] + bias_ref[1, hp * U + u]
        lf_row = jnp.minimum(f_row, 0.0) - jnp.log1p(jnp.exp(-jnp.abs(f_row)))
        b_col = jnp.sum(jnp.where(causal, lf_row, 0.0), axis=1, keepdims=True)
        b_row = jnp.sum(jnp.where(eye, b_col, 0.0), axis=0, keepdims=True)
        dlog = jnp.where(causal, b_col - b_row + i_row, NEG)
        inter = b_col + m
        m_t = jnp.maximum(inter, jnp.max(dlog, axis=1, keepdims=True))
        w_intra = jnp.exp(dlog - m_t)
        w_inter = jnp.exp(inter - m_t)
        s = _dot(q, kT.astype(BF16)) * w_intra
        res = w_inter * _dot(q, st[u].astype(BF16)) + _dot(s.astype(BF16), vaug)
        num = res[:, :d]
        den = res[:, d:d + 1]
        hh = num / jnp.maximum(jnp.abs(den), jnp.exp(-m_t))
        hn = hh * lax.rsqrt(jnp.mean(hh * hh, axis=1, keepdims=True) + EPS) * ng[:, u * d:(u + 1) * d]
        og = o_ref[u, pl.ds(r, L), :].astype(F32)
        out_ref[u, pl.ds(r, L), :] = (hn * _sigmoid(og)).astype(out_ref.dtype)
        ws_row = w_intra[L - 1:L, :]
        decay = w_inter[L - 1:L, :]
        st[u] = decay * st[u] + _dot((kT * ws_row).astype(BF16), vaug)
        return m_t[L - 1:L, :]

    def chunk(c, ms):
        r = pl.multiple_of(c * L, L)
        return tuple(chunk_one(u, c, r, ms[u]) for u in range(U))

    lax.fori_loop(0, nc, chunk, tuple(jnp.zeros((1, 1), F32) for _ in range(U)))


def _mlstm(proj_a, gates_t, bias_if, conv_w, norm_g, B, S):
    L = MLSTM_CHUNK
    nc = S // L
    d = HEAD_DIM
    H = N_HEADS
    U = HEADS_PER_STEP

    def slab(off):
        return pl.BlockSpec((U, S, d), lambda b, h: (off // U + h, b, 0))

    return pl.pallas_call(
        _mlstm_kernel,
        grid=(B, H // U),
        in_specs=[
            pl.BlockSpec(memory_space=pltpu.SMEM),
            slab(A_MQ), slab(A_MK), slab(A_MV), slab(A_MO),
            pl.BlockSpec((None, U, nc, 2, L), lambda b, h: (b, h, 0, 0, 0)),
            pl.BlockSpec((CONV_WIDTH, U * d), lambda b, h: (0, h)),
            pl.BlockSpec((CONV_WIDTH, U * d), lambda b, h: (0, H // U + h)),
            pl.BlockSpec((1, U * d), lambda b, h: (0, h)),
        ],
        out_specs=pl.BlockSpec((U, S, d), lambda b, h: (h, b, 0)),
        out_shape=jax.ShapeDtypeStruct((H, B * S, d), BF16),
        scratch_shapes=[
            pltpu.VMEM((U, S + 8, d), F32), pltpu.VMEM((U, S + 8, d), F32),
            pltpu.VMEM((U, S, d), BF16), pltpu.VMEM((U, S, d), F32),
            pltpu.VMEM((U, d, 2 * d), F32),
        ],
        compiler_params=_params(("parallel", "parallel")),
        name="mlstm",
    )(bias_if, proj_a, proj_a, proj_a, proj_a, gates_t, conv_w, conv_w, norm_g.reshape(1, H * d))


def _t5_thresholds():
    max_exact = N_BUCKETS // 2
    n = np.arange(0, 2 * MAX_DISTANCE)
    nf = np.maximum(n, 1).astype(np.float64)
    val = np.log(nf / max_exact) / math.log(MAX_DISTANCE / max_exact) * (N_BUCKETS - max_exact)
    frac = np.abs(val - np.round(val))
    frac_ok = (frac > 1e-4) | (n <= max_exact) | (n >= MAX_DISTANCE)
    assert frac_ok.all()
    large = np.minimum(max_exact + np.trunc(val).astype(np.int64), N_BUCKETS - 1)
    bucket = np.where(n < max_exact, n, large)
    assert (np.diff(bucket) >= 0).all() and bucket[MAX_DISTANCE] == N_BUCKETS - 1
    return [int(np.argmax(bucket >= j)) for j in range(1, N_BUCKETS)]


_T5_THR = _t5_thresholds()


def _dsa_prep_kernel(rb_ref, k_ref, ik_ref, gk_ref, kn, ikk, bias_ref):
    T = DSA_BLOCK
    H = N_HEADS
    row = lax.broadcasted_iota(I32, (T, T), 0)
    col = lax.broadcasted_iota(I32, (T, T), 1)
    gk = gk_ref[...]

    def norm_k(h, _):
        kh = k_ref[h].astype(F32)
        kn[h] = (kh * lax.rsqrt(jnp.mean(kh * kh, axis=1, keepdims=True) + EPS) * gk).astype(BF16)
        for o in range(2):
            n = o * T + row - col
            val = jnp.full((T, T), rb_ref[0, h], F32)
            for j, thr in enumerate(_T5_THR):
                val = jnp.where(n >= thr, rb_ref[j + 1, h], val)
            bias_ref[h, o] = val
        bias_ref[h, 2] = jnp.full((T, T), rb_ref[N_BUCKETS - 1, h], F32)
        return 0

    lax.fori_loop(0, H, norm_k, 0)
    ik = ik_ref[...]
    klane = lax.broadcasted_iota(I32, ik.shape, 1)
    ikk[0] = jnp.where(klane < IDX_DIM, ik, 0.0).astype(BF16)
    ikk[1] = jnp.where(klane >= IDX_DIM, ik, 0.0).astype(BF16)


def _dsa_kernel(q_ref, kn, v_ref, iq_ref, ikk, wt_ref, gq_ref, bias_ref, prev_ref, out_ref,
                wb, key_ref, mask_ref, *, topk, q0, nb, nqb):
    del prev_ref
    qis = [q0 + pl.program_id(1) * nqb + qb for qb in range(nqb)]
    T = DSA_BLOCK
    d = HEAD_DIM
    H = N_HEADS
    HALF = T // 2
    row = lax.broadcasted_iota(I32, (T, T), 0)
    col = lax.broadcasted_iota(I32, (T, T), 1)

    wsc = wt_ref[...] * (IDX_HEADS ** -0.5 * IDX_DIM ** -0.5)
    for qb in range(nqb):
        for h16 in range(IDX_HEADS):
            wb[qb * IDX_HEADS + h16] = jnp.broadcast_to(wsc[qb * T:(qb + 1) * T, h16:h16 + 1], (T, LANES))
    q_pairs = iq_ref[...].reshape((IDX_HEADS // 2) * nqb * T, LANES)

    def score_block(j, _):
        r = pl.multiple_of(j * T, T)
        kk = jnp.concatenate([ikk[0, pl.ds(r, T), :], ikk[1, pl.ds(r, T), :]], axis=0)
        dots = _dot_nt(q_pairs, kk)
        for qb in range(nqb):
            sc = jnp.zeros((T, T), F32)
            for h16 in range(IDX_HEADS):
                hp, odd = divmod(h16, 2)
                r0 = (hp * nqb + qb) * T
                sc = sc + wb[qb * IDX_HEADS + h16] * jnp.maximum(dots[r0:r0 + T, odd * T:(odd + 1) * T], 0.0)
            bits = lax.bitcast_convert_type(sc, I32)
            key = jnp.where(bits < 0, bits ^ jnp.int32(0x7FFFFFFF), bits)
            key = jnp.where(sc == 0.0, 0, key)
            key = jnp.where(j * T + col <= qis[qb] * T + row, key, INT_MIN)
            key_ref[qb, j] = key
        return 0

    lax.fori_loop(0, nb, score_block, 0, unroll=2)

    kf = float(topk)
    ones = jnp.ones((LANES, LANES), BF16)

    def count(chain, pred):
        qb, lo = chain
        acc = jnp.zeros((HALF, LANES), F32)
        for jb in range(nb):
            acc = acc + jnp.where(pred(key_ref[qb, jb, lo:lo + HALF, :]), 1.0, 0.0)
        return _dot(acc.astype(BF16), ones)

    chains = [(qb, lo) for qb in range(nqb) for lo in (0, HALF)]
    zero_i = jnp.zeros((HALF, LANES), I32)
    thr0 = tuple(jnp.where(count(ch, lambda k: k >= zero_i) >= kf, jnp.int32(0), jnp.int32(INT_MIN))
                 for ch in chains)

    def bisect_bit(thrs, bit):
        out = []
        for ch, t in zip(chains, thrs):
            cand = t | bit
            out.append(jnp.where(count(ch, lambda k, c=cand: k >= c) >= kf, cand, t))
        return tuple(out)

    def bisect_pair(it, thrs):
        hi_bit = jnp.left_shift(jnp.int32(1), 29 - 2 * it)
        lo_bit = jnp.left_shift(jnp.int32(1), 28 - 2 * it)
        out = []
        for ch, t in zip(chains, thrs):
            c01, c10, c11 = t | lo_bit, t | hi_bit, t | hi_bit | lo_bit
            n01, n10, n11 = (count(ch, lambda k, c=c: k >= c) >= kf for c in (c01, c10, c11))
            out.append(jnp.where(n11, c11, jnp.where(n10, c10, jnp.where(n01, c01, t))))
        return tuple(out)

    thrs = bisect_bit(thr0, jnp.int32(1 << 30))
    thrs = lax.fori_loop(0, 15, bisect_pair, thrs)
    needs = [kf - count(ch, lambda k, t=t: k > t) for ch, t in zip(chains, thrs)]

    tri = jnp.where(row <= col, 1.0, 0.0).astype(BF16)
    for qb in range(nqb):
        need = jnp.concatenate(needs[2 * qb:2 * qb + 2], axis=0)
        thrb = jnp.concatenate(thrs[2 * qb:2 * qb + 2], axis=0)
        seen = jnp.zeros((T, LANES), F32)
        for jb in range(nb):
            key = key_ref[qb, jb]
            tie = key == thrb
            tie16 = jnp.where(tie, 1.0, 0.0).astype(BF16)
            pre = _dot(tie16, tri) + seen
            m = jnp.where(key > thrb, 0.0, jnp.where(tie, jnp.where(pre <= need, 0.0, NEG), NEG))
            mask_ref[qb, jb] = jnp.where(key == INT_MIN, NEG, m)
            seen = seen + _dot(tie16, ones)

    gq = gq_ref[...]

    def head(h):
        qh = q_ref[h].astype(F32)
        qn = (qh * lax.rsqrt(jnp.mean(qh * qh, axis=1, keepdims=True) + EPS) * gq * (d ** -0.5)).astype(BF16)
        lgs = _dot_nt(qn, kn[h])
        p_rows, sums = [], []
        for qb in range(nqb):
            mx = None
            lg_blocks = []
            for jb in range(nb):
                lg = (lgs[qb * T:(qb + 1) * T, jb * T:(jb + 1) * T]
                      + bias_ref[h, jnp.clip(qis[qb] - jb, 0, 2)] + mask_ref[qb, jb])
                lg_blocks.append(lg)
                mx = lg if mx is None else jnp.maximum(mx, lg)
            rowmax = jnp.max(mx, axis=1, keepdims=True)
            l = jnp.zeros((T, T), F32)
            ps = []
            for jb in range(nb):
                p = jnp.exp(lg_blocks[jb] - rowmax)
                l = l + p
                ps.append(p.astype(BF16))
            p_rows.append(jnp.concatenate(ps, axis=1))
            sums.append(jnp.sum(l, axis=1, keepdims=True))
        acc = _dot(jnp.concatenate(p_rows, axis=0), v_ref[h])
        out_ref[h] = (acc / jnp.concatenate(sums, axis=0)).astype(out_ref.dtype)

    def head_group(g, _):
        for u in range(HEADS_PER_STEP):
            head(g * HEADS_PER_STEP + u)
        return 0

    lax.fori_loop(0, H // HEADS_PER_STEP, head_group, 0)


def _dsa(proj_b, tail, q_norm_g, k_norm_g, rel_bias, B, S):
    T = DSA_BLOCK
    nq = S // T
    d = HEAD_DIM
    H = N_HEADS
    topk = min(TOPK_MAX, S // 4)
    per = nq // DSA_BANDS
    nqb = math.gcd(per, DSA_QBLOCKS)
    TQ = nqb * T
    pb = proj_b.reshape(proj_b.shape[0], B, S, d)
    tl = tail.reshape(tail.shape[0], B, S, LANES)
    kn, ikk, bias = pl.pallas_call(
        _dsa_prep_kernel,
        grid=(B,),
        in_specs=[pl.BlockSpec(memory_space=pltpu.SMEM),
                  pl.BlockSpec((H, None, S, d), lambda b: (B_DK // H, b, 0, 0)),
                  pl.BlockSpec((None, None, S, LANES), lambda b: (0, b, 0, 0)),
                  pl.BlockSpec((1, d), lambda b: (0, 0))],
        out_specs=[pl.BlockSpec((H, None, S, d), lambda b: (0, b, 0, 0)),
                   pl.BlockSpec((2, None, S, LANES), lambda b: (0, b, 0, 0)),
                   pl.BlockSpec((H, 3, T, T), lambda b: (0, 0, 0, 0))],
        out_shape=[jax.ShapeDtypeStruct((H, B, S, d), BF16),
                   jax.ShapeDtypeStruct((2, B, S, LANES), BF16),
                   jax.ShapeDtypeStruct((H, 3, T, T), F32)],
        compiler_params=_params(("arbitrary",)),
        name="dsa_prep",
    )(rel_bias, pb, tl, k_norm_g.reshape(1, d))
    out = jnp.zeros((H, B, S, d), BF16)
    for band in range(DSA_BANDS):
        q0 = band * per
        nb = q0 + per
        W = nb * T
        in_specs = [
            pl.BlockSpec((H, None, TQ, d), lambda b, i: (B_DQ // H, b, q0 // nqb + i, 0)),
            pl.BlockSpec((H, None, W, d), lambda b, i: (0, b, 0, 0)),
            pl.BlockSpec((H, None, W, d), lambda b, i: (B_DV // H, b, 0, 0)),
            pl.BlockSpec((H, None, TQ, d), lambda b, i: (B_IQ // H, b, q0 // nqb + i, 0)),
            pl.BlockSpec((2, None, W, LANES), lambda b, i: (0, b, 0, 0)),
            pl.BlockSpec((None, None, TQ, LANES), lambda b, i: (1, b, q0 // nqb + i, 0)),
            pl.BlockSpec((1, d), lambda b, i: (0, 0)),
            pl.BlockSpec((H, 3, T, T), lambda b, i: (0, 0, 0, 0)),
            pl.BlockSpec(memory_space=pl.ANY),
        ]
        args = [pb, kn, pb, pb, ikk, tl, q_norm_g.reshape(1, d), bias, out]
        out = pl.pallas_call(
            functools.partial(_dsa_kernel, topk=topk, q0=q0, nb=nb, nqb=nqb),
            grid=(B, per // nqb),
            in_specs=in_specs,
            out_specs=pl.BlockSpec((H, None, TQ, d), lambda b, i: (0, b, q0 // nqb + i, 0)),
            out_shape=jax.ShapeDtypeStruct((H, B, S, d), BF16),
            scratch_shapes=[
                pltpu.VMEM((nqb * IDX_HEADS, T, LANES), F32),
                pltpu.VMEM((nqb, nb, T, T), I32),
                pltpu.VMEM((nqb, nb, T, T), F32),
            ],
            input_output_aliases={len(args) - 1: 0},
            compiler_params=_params(("parallel", "parallel")),
            name=f"dsa_band{band}",
        )(*args)
    return out.reshape(H, B * S, d)


def _sb_kernel(q_ref, k_ref, v_ref, out_ref):
    qi = pl.program_id(2)
    U = SB_HEADS
    T = SB_BLOCK
    d = HEAD_DIM
    R = SB_ROWS
    ns = T // R
    row = lax.broadcasted_iota(I32, (T, T), 0)
    col = lax.broadcasted_iota(I32, (T, T), 1)
    upper = jnp.where(row > col, 1.0, 0.0).astype(BF16)
    srow = lax.broadcasted_iota(I32, (R, T), 0)
    scol = lax.broadcasted_iota(I32, (R, T), 1)
    units = [(u, s) for u in range(U) for s in range(ns)]
    qs = [(q_ref[u, s * R:(s + 1) * R, :].astype(F32) * (d ** -0.5)).astype(BF16) for u, s in units]

    def block(n, j, diag, run, acc):
        u, s = units[n]
        r = pl.multiple_of(j * T, T)
        z = _dot_nt(qs[n], k_ref[u, pl.ds(r, T), :])
        sp = jnp.maximum(z, 0.0) + jnp.log(1.0 + jnp.exp(-jnp.abs(z)))
        strict = scol < srow + s * R
        ln = jnp.where(strict, -sp, 0.0) if diag else -sp
        parts = [ln.astype(BF16)]
        for _ in range(SB_PIECES - 1):
            parts.append((ln - sum(p.astype(F32) for p in parts)).astype(BF16))
        sums = _dot(jnp.concatenate(parts, axis=0), upper)
        suf = sum(sums[p * R:(p + 1) * R] for p in range(SB_PIECES))
        a = jnp.exp(z - sp + suf + run)
        if diag:
            a = jnp.where(strict, a, 0.0)
        acc = acc + _dot(a.astype(BF16), v_ref[u, pl.ds(r, T), :])
        return run + suf[:, 0:1] + ln[:, 0:1], acc

    state = []
    for n in range(len(units)):
        state.extend(block(n, qi, True, jnp.zeros((R, 1), F32), jnp.zeros((R, d), F32)))

    def earlier(i, st):
        out = []
        for n in range(len(units)):
            out.extend(block(n, qi - i, False, st[2 * n], st[2 * n + 1]))
        return tuple(out)

    state = lax.fori_loop(1, qi + 1, earlier, tuple(state))
    for n, (u, s) in enumerate(units):
        out_ref[u, s * R:(s + 1) * R, :] = state[2 * n + 1].astype(out_ref.dtype)


def _stick_breaking(proj_c, B, S):
    T = SB_BLOCK
    nq = S // T
    d = HEAD_DIM
    H = N_HEADS
    U = SB_HEADS
    return pl.pallas_call(
        _sb_kernel,
        grid=(B, H // U, nq),
        in_specs=[
            pl.BlockSpec((U, T, d), lambda b, h, i: (C_SQ // U + h, b * nq + i, 0)),
            pl.BlockSpec((U, S, d), lambda b, h, i: (C_SK // U + h, b, 0)),
            pl.BlockSpec((U, S, d), lambda b, h, i: (C_SV // U + h, b, 0)),
        ],
        out_specs=pl.BlockSpec((U, T, d), lambda b, h, i: (h, b * nq + i, 0)),
        out_shape=jax.ShapeDtypeStruct((H, B * S, d), BF16),
        compiler_params=_params(("parallel", "parallel", "parallel")),
        name="stick_breaking",
    )(proj_c, proj_c, proj_c)


def _merge_kernel(hm_ref, hd_ref, hs_ref, g0_ref, g1_ref, g2_ref, w_ref, o_ref, wb):
    H = N_HEADS

    @pl.when(pl.program_id(1) == 0)
    def _():
        wb[...] = w_ref[...].astype(BF16)

    acc = None
    ns = o_ref.shape[1] // LANES
    for n, (br, gr) in enumerate(((hm_ref, g0_ref), (hd_ref, g1_ref), (hs_ref, g2_ref))):
        a = jnp.concatenate([br[h] for h in range(H)], axis=1)
        up = _dot(a, wb[n])
        gate = jnp.concatenate([gr[s] for s in range(ns)], axis=1).astype(F32)
        term = _sigmoid(gate) * up
        acc = term if acc is None else acc + term
    o_ref[...] = acc.astype(o_ref.dtype)


def _merge(hm, hd, hs, proj_c, w_branch, layer, tm=1024, tn=512):
    H, n, d = hm.shape
    tm = min(tm, n)
    D = w_branch.shape[3]
    ns = tn // LANES
    br = pl.BlockSpec((H, tm, d), lambda j, i: (0, i, 0))

    def gate(b):
        first = (C_GATE + b * (D // LANES)) // ns
        return pl.BlockSpec((ns, tm, LANES), lambda j, i: (first + j, i, 0))

    return pl.pallas_call(
        _merge_kernel,
        grid=(D // tn, n // tm),
        in_specs=[br, br, br, gate(0), gate(1), gate(2),
                  pl.BlockSpec((None, N_BRANCHES, H * d, tn), lambda j, i: (layer, 0, 0, j))],
        out_specs=pl.BlockSpec((tm, tn), lambda j, i: (i, j)),
        out_shape=jax.ShapeDtypeStruct((n, D), BF16),
        scratch_shapes=[pltpu.VMEM((N_BRANCHES, H * d, tn), BF16)],
        compiler_params=_params(("parallel", "arbitrary")),
        name="branch_merge",
    )(hm, hd, hs, proj_c, proj_c, proj_c, w_branch)


def _mm_res_kernel(a_ref, w_ref, x_ref, o_ref, wb):
    @pl.when(pl.program_id(1) == 0)
    def _():
        wb[...] = w_ref[...].astype(BF16)

    o_ref[...] = x_ref[...] + _dot(a_ref[...], wb[...])


def _matmul_residual(a, w, layer, x, tm=1024, tn=512):
    m, k = a.shape
    tm = min(tm, m)
    n = w.shape[2]
    return pl.pallas_call(
        _mm_res_kernel,
        grid=(n // tn, m // tm),
        in_specs=[pl.BlockSpec((tm, k), lambda j, i: (i, 0)),
                  pl.BlockSpec((None, k, tn), lambda j, i: (layer, 0, j)),
                  pl.BlockSpec((tm, tn), lambda j, i: (i, j))],
        out_specs=pl.BlockSpec((tm, tn), lambda j, i: (i, j)),
        out_shape=jax.ShapeDtypeStruct((m, n), F32),
        scratch_shapes=[pltpu.VMEM((k, tn), BF16)],
        compiler_params=_params(("parallel", "arbitrary")),
        name="out_proj",
    )(a, w, x)


def _router_kernel(x_ref, g_ref, wr_ref, br_ref, xn_ref, ids_ref, wts_ref, cnt_ref, carry, lower):
    i = pl.program_id(0)
    tm, D = x_ref.shape

    @pl.when(i == 0)
    def _():
        carry[...] = jnp.zeros(carry.shape, F32)
        r = lax.broadcasted_iota(I32, (tm, tm), 0)
        c = lax.broadcasted_iota(I32, (tm, tm), 1)
        lower[...] = jnp.where(c < r, 1.0, 0.0).astype(BF16)

    x = x_ref[...]
    xn = x * lax.rsqrt(jnp.mean(x * x, axis=-1, keepdims=True) + EPS) * g_ref[...]
    xn_ref[...] = xn
    logits = jnp.dot(xn, wr_ref[...], preferred_element_type=F32, precision=lax.Precision.HIGHEST)
    biased = logits + br_ref[...]
    lane = lax.broadcasted_iota(I32, (tm, LANES), 1)
    lanef = lane.astype(F32)
    big = float(LANES)

    def first_lane(mask):
        return jnp.min(jnp.where(mask, lanef, big), axis=1, keepdims=True)

    gmask = lane < N_GROUPS
    gmax = jnp.max(jnp.where(gmask, biased, NEG), axis=1, keepdims=True)
    g_sel = first_lane(gmask & (biased == gmax))
    gm = jnp.max(jnp.where(gmask, logits, NEG), axis=1, keepdims=True)
    ge = jnp.where(gmask, jnp.exp(logits - gm), 0.0)
    p_group = jnp.sum(jnp.where(lanef == g_sel, ge, 0.0), axis=1, keepdims=True) / jnp.sum(ge, axis=1, keepdims=True)
    lo = N_GROUPS + EXPERTS_PER_GROUP * g_sel
    emask = (lanef >= lo) & (lanef < lo + EXPERTS_PER_GROUP)
    eb = jnp.where(emask, biased, NEG)
    e1 = first_lane(emask & (eb == jnp.max(eb, axis=1, keepdims=True)))
    emask2 = emask & (lanef != e1)
    eb2 = jnp.where(emask2, biased, NEG)
    e2 = first_lane(emask2 & (eb2 == jnp.max(eb2, axis=1, keepdims=True)))
    em = jnp.max(jnp.where(emask, logits, NEG), axis=1, keepdims=True)
    ee = jnp.where(emask, jnp.exp(logits - em), 0.0)
    s1 = jnp.sum(jnp.where(lanef == e1, ee, 0.0), axis=1, keepdims=True)
    s2 = jnp.sum(jnp.where(lanef == e2, ee, 0.0), axis=1, keepdims=True)
    se = jnp.sum(ee, axis=1, keepdims=True)
    w1 = s1 / se
    w2 = s2 / se
    wsum = w1 + w2
    w1 = p_group * w1 / wsum
    w2 = p_group * w2 / wsum
    x1 = e1 - N_GROUPS
    x2 = e2 - N_GROUPS

    onehot = jnp.where((lanef == x1) | (lanef == x2), 1.0, 0.0)
    prefix = _dot(lower[...], onehot.astype(BF16)) + carry[...]
    r1 = jnp.sum(jnp.where(lanef == x1, prefix, 0.0), axis=1, keepdims=True)
    r2 = jnp.sum(jnp.where(lanef == x2, prefix, 0.0), axis=1, keepdims=True)
    carry[...] = carry[...] + jnp.sum(onehot, axis=0, keepdims=True)
    cnt_ref[...] = carry[...].astype(I32)

    idsf = jnp.where(lane == 0, x1, jnp.where(lane == 1, x2, jnp.where(lane == 2, r1, jnp.where(lane == 3, r2, 0.0))))
    ids_ref[...] = idsf.astype(I32)
    wts_ref[...] = jnp.where(lane == 0, w1, jnp.where(lane == 1, w2, 0.0))


def _router(x2d, g, w_router, b_router, tm=512):
    n, D = x2d.shape
    return pl.pallas_call(
        _router_kernel,
        grid=(n // tm,),
        in_specs=[pl.BlockSpec((tm, D), lambda i: (i, 0)), pl.BlockSpec((1, D), lambda i: (0, 0)),
                  pl.BlockSpec((D, LANES), lambda i: (0, 0)), pl.BlockSpec((1, LANES), lambda i: (0, 0))],
        out_specs=[pl.BlockSpec((tm, D), lambda i: (i, 0)),
                   pl.BlockSpec((tm, LANES), lambda i: (i, 0)),
                   pl.BlockSpec((tm, LANES), lambda i: (i, 0)),
                   pl.BlockSpec((1, LANES), lambda i: (0, 0))],
        out_shape=[jax.ShapeDtypeStruct((n, D), F32),
                   jax.ShapeDtypeStruct((n, LANES), I32),
                   jax.ShapeDtypeStruct((n, LANES), F32),
                   jax.ShapeDtypeStruct((1, LANES), I32)],
        scratch_shapes=[pltpu.VMEM((1, LANES), F32), pltpu.VMEM((tm, tm), BF16)],
        compiler_params=_params(("arbitrary",)),
        name="moe_router",
    )(x2d, g.reshape(1, D), w_router, b_router)


def _dispatch_kernel(p1_ref, p2_ref, xn_ref, xs_in_ref, xs_ref, sem, *, tb):
    del xs_in_ref
    base = pl.program_id(0) * tb

    def copies(t):
        src = xn_ref.at[pl.ds(t, 1), :]
        return (pltpu.make_async_copy(src, xs_ref.at[pl.ds(p1_ref[base + t], 1), :], sem),
                pltpu.make_async_copy(src, xs_ref.at[pl.ds(p2_ref[base + t], 1), :], sem))

    def issue(t, _):
        for cp in copies(t):
            cp.start()
        return 0

    lax.fori_loop(0, tb, issue, 0, unroll=DMA_UNROLL)

    def drain(t, _):
        for cp in copies(t):
            cp.wait()
        return 0

    lax.fori_loop(0, tb, drain, 0, unroll=DMA_UNROLL)


def _dispatch(pos1, pos2, xn, xs0, tb=512):
    n, D = xn.shape
    n_rows = xs0.shape[0]
    tb = min(tb, n)
    grid_spec = pltpu.PrefetchScalarGridSpec(
        num_scalar_prefetch=2,
        grid=(n // tb,),
        in_specs=[pl.BlockSpec((tb, D), lambda i, p1, p2: (i, 0)),
                  pl.BlockSpec(memory_space=pl.ANY)],
        out_specs=pl.BlockSpec(memory_space=pl.ANY),
        scratch_shapes=[pltpu.SemaphoreType.DMA(())],
    )
    return pl.pallas_call(
        functools.partial(_dispatch_kernel, tb=tb),
        grid_spec=grid_spec,
        out_shape=jax.ShapeDtypeStruct((n_rows, D), F32),
        input_output_aliases={3: 0},
        compiler_params=pltpu.CompilerParams(dimension_semantics=("arbitrary",), has_side_effects=True),
        name="moe_dispatch",
    )(pos1, pos2, xn, xs0)


def _expert_kernel(te_ref, nv_ref, ne_ref, grp_ref, xs_ref, wg_ref, wu_ref, wd_ref, ys_ref,
                   wgf, wuf, wdf, wgb, wub, wdb, sem, *, layer):
    i = pl.program_id(0)
    prev = te_ref[jnp.maximum(i - 1, 0)]

    def fetch(e, slot):
        return (pltpu.make_async_copy(wg_ref.at[layer, e], wgf.at[slot], sem.at[slot]),
                pltpu.make_async_copy(wu_ref.at[layer, e], wuf.at[slot], sem.at[slot]),
                pltpu.make_async_copy(wd_ref.at[layer, e], wdf.at[slot], sem.at[slot]))

    @pl.when(i < nv_ref[0])
    def _():
        @pl.when(i == 0)
        def _():
            for cp in fetch(te_ref[0], 0):
                cp.start()

        @pl.when((i == 0) | (te_ref[i] != prev))
        def _():
            slot = grp_ref[i] % 2
            for cp in fetch(te_ref[i], slot):
                cp.wait()

            @pl.when(ne_ref[i] >= 0)
            def _():
                for cp in fetch(ne_ref[i], 1 - slot):
                    cp.start(priority=1)

            wgb[...] = wgf[slot].astype(BF16)
            wub[...] = wuf[slot].astype(BF16)
            wdb[...] = wdf[slot].astype(BF16)

        x = xs_ref[...].astype(BF16)
        g = _dot(x, wgb[...])
        u = _dot(x, wub[...])
        hcur = (g * _sigmoid(g) * u).astype(BF16)
        ys_ref[...] = _dot(hcur, wdb[...])

    @pl.when(i >= nv_ref[0])
    def _():
        ys_ref[...] = jnp.zeros(ys_ref.shape, F32)


def _experts(tile_expert, n_valid, xs, w_gate, w_up, w_down, layer):
    n_rows = xs.shape[0]
    _, E, D, Fe = w_gate.shape
    tm = MOE_TILE
    n_tiles = n_rows // tm
    tiles = jnp.arange(n_tiles, dtype=I32)
    first = (tiles < n_valid[0]) & ((tiles == 0) | (tile_expert != jnp.roll(tile_expert, 1)))
    group = (jnp.cumsum(first.astype(I32)) - 1).astype(I32)
    later = lax.cummin(jnp.where(first, tiles, n_tiles), reverse=True)
    nxt = jnp.concatenate([later[1:], jnp.full((1,), n_tiles, I32)])
    next_expert = jnp.where(nxt < n_tiles, tile_expert[jnp.minimum(nxt, n_tiles - 1)], -1).astype(I32)

    def row_map(i, te, nv, ne, grp):
        return (jnp.minimum(i, jnp.maximum(nv[0] - 1, 0)), 0)

    grid_spec = pltpu.PrefetchScalarGridSpec(
        num_scalar_prefetch=4,
        grid=(n_tiles,),
        in_specs=[pl.BlockSpec((tm, D), row_map),
                  pl.BlockSpec(memory_space=pl.ANY), pl.BlockSpec(memory_space=pl.ANY),
                  pl.BlockSpec(memory_space=pl.ANY)],
        out_specs=pl.BlockSpec((tm, D), lambda i, te, nv, ne, grp: (i, 0)),
        scratch_shapes=[pltpu.VMEM((2, D, Fe), F32), pltpu.VMEM((2, D, Fe), F32), pltpu.VMEM((2, Fe, D), F32),
                        pltpu.VMEM((D, Fe), BF16), pltpu.VMEM((D, Fe), BF16), pltpu.VMEM((Fe, D), BF16),
                        pltpu.SemaphoreType.DMA((2,))],
    )
    return pl.pallas_call(
        functools.partial(_expert_kernel, layer=layer),
        grid_spec=grid_spec,
        out_shape=jax.ShapeDtypeStruct((n_rows, D), F32),
        compiler_params=_params(("arbitrary",)),
        name="moe_experts",
    )(tile_expert, n_valid, next_expert, group, xs, w_gate, w_up, w_down)


def _combine_kernel(p1_ref, p2_ref, ys_ref, x_ref, w_ref, *rest, with_norm):
    if with_norm:
        g_ref, o_ref, xn_ref, buf, sem = rest
    else:
        o_ref, buf, sem = rest
    i = pl.program_id(0)
    tc = x_ref.shape[0]
    slot = i % 2

    def copies(tile, slot, t):
        tok = tile * tc + t
        return (pltpu.make_async_copy(ys_ref.at[pl.ds(p1_ref[tok], 1), :], buf.at[slot, 0, pl.ds(t, 1), :],
                                      sem.at[slot]),
                pltpu.make_async_copy(ys_ref.at[pl.ds(p2_ref[tok], 1), :], buf.at[slot, 1, pl.ds(t, 1), :],
                                      sem.at[slot]))

    def issue(tile, slot):
        def body(t, _):
            for cp in copies(tile, slot, t):
                cp.start()
            return 0

        lax.fori_loop(0, tc, body, 0, unroll=DMA_UNROLL)

    @pl.when(i == 0)
    def _():
        issue(0, 0)

    @pl.when(i + 1 < pl.num_programs(0))
    def _():
        issue(i + 1, 1 - slot)

    def drain(t, _):
        for cp in copies(i, slot, t):
            cp.wait()
        return 0

    lax.fori_loop(0, tc, drain, 0, unroll=DMA_UNROLL)
    w = w_ref[...]
    out = x_ref[...] + w[:, 0:1] * buf[slot, 0] + w[:, 1:2] * buf[slot, 1]
    o_ref[...] = out
    if with_norm:
        ms = jnp.mean(out * out, axis=-1, keepdims=True)
        xn_ref[...] = (out * lax.rsqrt(ms + EPS) * g_ref[...]).astype(BF16)


def _combine(pos1, pos2, ys, x2d, wts, next_g, tc=256):
    n, D = x2d.shape
    with_norm = next_g is not None
    row = pl.BlockSpec((tc, D), lambda i, p1, p2: (i, 0))
    in_specs = [pl.BlockSpec(memory_space=pl.ANY), row, pl.BlockSpec((tc, LANES), lambda i, p1, p2: (i, 0))]
    args = [pos1, pos2, ys, x2d, wts]
    out_specs, out_shape = row, jax.ShapeDtypeStruct((n, D), F32)
    if with_norm:
        in_specs.append(pl.BlockSpec((1, D), lambda i, p1, p2: (0, 0)))
        args.append(next_g.reshape(1, D))
        out_specs, out_shape = [row, row], [out_shape, jax.ShapeDtypeStruct((n, D), BF16)]
    grid_spec = pltpu.PrefetchScalarGridSpec(
        num_scalar_prefetch=2,
        grid=(n // tc,),
        in_specs=in_specs,
        out_specs=out_specs,
        scratch_shapes=[pltpu.VMEM((2, 2, tc, D), F32), pltpu.SemaphoreType.DMA((2,))],
    )
    res = pl.pallas_call(
        functools.partial(_combine_kernel, with_norm=with_norm),
        grid_spec=grid_spec,
        out_shape=out_shape,
        compiler_params=_params(("arbitrary",)),
        name="moe_combine",
    )(*args)
    return tuple(res) if with_norm else (res, None)


def _pos_kernel(ids_ref, cnt_ref, pos_ref):
    tm = ids_ref.shape[0]
    tiles = jnp.floor((cnt_ref[...].astype(F32) + (MOE_TILE - 1)) / MOE_TILE)
    r = lax.broadcasted_iota(I32, (LANES, LANES), 0)
    c = lax.broadcasted_iota(I32, (LANES, LANES), 1)
    before = jnp.where(r < c, 1.0, 0.0).astype(BF16)
    first_tile = _dot(jnp.broadcast_to(tiles, (8, LANES)).astype(BF16), before)
    offs = first_tile[0:1, :] * MOE_TILE
    ids = ids_ref[...].astype(F32)
    lane = lax.broadcasted_iota(I32, (tm, LANES), 1)
    lanef = lane.astype(F32)
    p1 = jnp.sum(jnp.where(lanef == ids[:, 0:1], offs, 0.0), axis=1, keepdims=True) + ids[:, 2:3]
    p2 = jnp.sum(jnp.where(lanef == ids[:, 1:2], offs, 0.0), axis=1, keepdims=True) + ids[:, 3:4]
    packed = jnp.where(lane == 0, p1, jnp.where(lane == 1, p2, 0.0))
    pos_ref[...] = packed.T[0:8, :].astype(I32)


def _positions(ids, counts, tm=512):
    n = ids.shape[0]
    assert MOE_TILE & (MOE_TILE - 1) == 0
    return pl.pallas_call(
        _pos_kernel,
        grid=(n // tm,),
        in_specs=[pl.BlockSpec((tm, LANES), lambda i: (i, 0)), pl.BlockSpec((1, LANES), lambda i: (0, 0))],
        out_specs=pl.BlockSpec((8, tm), lambda i: (0, i)),
        out_shape=jax.ShapeDtypeStruct((8, n), I32),
        compiler_params=_params(("parallel",)),
        name="moe_positions",
    )(ids, counts)


def _hier_moe(x2d, norm_g, w_rg, b_rg, w_re, b_re, w_gate, w_up, w_down, layer, xs_buf, next_g):
    n, D = x2d.shape
    tm = MOE_TILE
    pad = LANES - N_GROUPS - N_EXPERTS
    w_router = jnp.concatenate([w_rg, w_re, jnp.zeros((D, pad), F32)], axis=1)
    b_router = jnp.concatenate([b_rg, b_re.reshape(-1), jnp.zeros((pad,), F32)]).reshape(1, LANES)
    xn3, ids, wts, counts = _router(x2d, norm_g, w_router, b_router)
    cnt = counts[0, :N_EXPERTS]
    padded = ((cnt + tm - 1) // tm) * tm
    ends = jnp.cumsum(padded)
    n_tiles = xs_buf.shape[0] // tm
    tile_start = jnp.arange(n_tiles, dtype=I32) * tm
    tile_expert = jnp.minimum(jnp.sum(tile_start[:, None] >= ends[None, :], axis=1), N_EXPERTS - 1).astype(I32)
    n_valid = (ends[-1] // tm).astype(I32).reshape(1)
    last_e = tile_expert[jnp.maximum(n_valid[0] - 1, 0)]
    tile_expert = jnp.where(jnp.arange(n_tiles) < n_valid[0], tile_expert, last_e)
    pos = _positions(ids, counts)
    pos1, pos2 = pos[0], pos[1]
    xs = _dispatch(pos1, pos2, xn3, xs_buf)
    ys = _experts(tile_expert, n_valid, xs, w_gate, w_up, w_down, layer)
    out, xn_next = _combine(pos1, pos2, ys, x2d, wts, next_g)
    return out, xn_next, xs


def _in_proj_regions(D):
    bw = BRANCH_WIDTH
    sizes = (bw, bw, bw, bw, N_HEADS, N_HEADS, bw, bw, bw, IDX_HEADS * IDX_DIM, IDX_DIM, IDX_HEADS,
             bw, bw, bw, N_BRANCHES * D)
    offs = np.concatenate([[0], np.cumsum(sizes)]).tolist()
    region_a = (offs[0], offs[4] - offs[0])
    region_b = (offs[6], offs[10] - offs[6])
    region_c = (offs[12], offs[16] - offs[12])
    small = dict(mi=offs[4], mf=offs[5], ik=offs[10], iw=offs[11])
    return region_a, region_b, region_c, small


def _token_mixer(x2d, xn, B, S, layer, norm_g, w_in, conv_w, b_i, b_f, mlstm_norm_g, q_norm_g, k_norm_g,
                 w_branch, w_out, rel_bias):
    n, D = x2d.shape
    L = MLSTM_CHUNK
    H = N_HEADS
    tm = min(n, PROJ_TM)
    if xn is None:
        xn = _rmsnorm(x2d, norm_g)
    ra, rb, rc, small = _in_proj_regions(D)
    w_t = jnp.swapaxes(w_in, 1, 2)
    proj_a = _in_proj(xn, w_t, layer, ra[0], ra[1], tm)
    proj_b = _in_proj(xn, w_t, layer, rb[0], rb[1], tm)
    proj_c = _in_proj(xn, w_t, layer, rc[0], rc[1], tm)
    tail = _tail_proj(xn, w_t, layer, small, tm)
    g = tail[1][:, IDX_HEADS:IDX_HEADS + 2 * H].reshape(B, S // L, L, 2, H)
    gates_t = jnp.transpose(g, (0, 4, 1, 3, 2))
    hm = _mlstm(proj_a, gates_t, jnp.stack([b_i, b_f]), conv_w, mlstm_norm_g, B, S)
    hd = _dsa(proj_b, tail, q_norm_g, k_norm_g, rel_bias, B, S)
    hs = _stick_breaking(proj_c, B, S)
    merged = _merge(hm, hd, hs, proj_c, w_branch, layer)
    return _matmul_residual(merged, w_out, layer, x2d)


def kernel(x, norm1_g, w_in, conv_w, b_i, b_f, mlstm_norm_g, q_norm_g, k_norm_g, w_branch, w_out, norm2_g,
           w_router_g, b_router_g, w_router_e, b_router_e, w_gate, w_up, w_down, rel_bias):
    B, S, D = x.shape
    x2d = x.reshape(B * S, D)
    xs_buf = jnp.zeros((2 * B * S + N_EXPERTS * MOE_TILE, D), F32)
    depth = w_in.shape[0]
    xn = None
    for l in range(depth):
        x2d = _token_mixer(x2d, xn, B, S, l, norm1_g[l], w_in, conv_w[l], b_i[l], b_f[l], mlstm_norm_g[l],
                           q_norm_g[l], k_norm_g[l], w_branch, w_out, rel_bias)
        next_g = norm1_g[l + 1] if l + 1 < depth else None
        x2d, xn, xs_buf = _hier_moe(x2d, norm2_g[l], w_router_g[l], b_router_g[l], w_router_e[l], b_router_e[l],
                                    w_gate, w_up, w_down, l, xs_buf, next_g)
    return x2d.reshape(B, S, D)
```

```python
import functools
import math

import numpy as np
import jax
import jax.numpy as jnp
from jax import lax
from jax.experimental import pallas as pl
from jax.experimental.pallas import tpu as pltpu

F32 = jnp.float32
BF16 = jnp.bfloat16
I32 = jnp.int32

LANES = 128
HEAD_DIM = 128
N_HEADS = 8
BRANCH_WIDTH = N_HEADS * HEAD_DIM
N_BRANCHES = 3
CONV_WIDTH = 4
IDX_HEADS = 16
IDX_DIM = 64
TOPK_MAX = 256
N_BUCKETS = 32
MAX_DISTANCE = 128
N_GROUPS = 4
EXPERTS_PER_GROUP = 8
N_EXPERTS = N_GROUPS * EXPERTS_PER_GROUP
EPS = 1e-6
NEG = -1e30
INT_MIN = -(2 ** 31)

MLSTM_CHUNK = 256
DSA_BLOCK = 128
DSA_BANDS = 4
DSA_QBLOCKS = 2
SB_BLOCK = 512
SB_ROWS = 512
SB_PIECES = 1
SB_HEADS = 4
HEADS_PER_STEP = 4
MOE_TILE = 256
DMA_UNROLL = 8
PROJ_TN = 512
PROJ_TM = 2048
VMEM_LIMIT = 56 * 1024 * 1024

A_MQ, A_MK, A_MV, A_MO = 0, 8, 16, 24
B_DQ, B_DK, B_DV, B_IQ = 0, 8, 16, 24
C_SQ, C_SK, C_SV, C_GATE = 0, 8, 16, 24


def _params(sem):
    return pltpu.CompilerParams(dimension_semantics=sem, vmem_limit_bytes=VMEM_LIMIT)


def _dot(a, b):
    return jnp.dot(a, b, preferred_element_type=F32)


def _dot_nt(a, b):
    return lax.dot_general(a, b, (((1,), (1,)), ((), ())), preferred_element_type=F32)


def _sigmoid(z):
    return 1.0 / (1.0 + jnp.exp(-z))


def _rmsnorm_kernel(x_ref, g_ref, o_ref):
    x = x_ref[...]
    ms = jnp.mean(x * x, axis=-1, keepdims=True)
    o_ref[...] = (x * lax.rsqrt(ms + EPS) * g_ref[...]).astype(o_ref.dtype)


def _rmsnorm(x2d, g, tm=512):
    n, d = x2d.shape
    return pl.pallas_call(
        _rmsnorm_kernel,
        grid=(n // tm,),
        in_specs=[pl.BlockSpec((tm, d), lambda i: (i, 0)), pl.BlockSpec((1, d), lambda i: (0, 0))],
        out_specs=pl.BlockSpec((tm, d), lambda i: (i, 0)),
        out_shape=jax.ShapeDtypeStruct((n, d), BF16),
        compiler_params=_params(("parallel",)),
        name="rmsnorm",
    )(x2d, g.reshape(1, d))


def _in_proj_kernel(x_ref, *rest, shift, nblk):
    w_refs, o_ref, wb = rest[:nblk], rest[nblk], rest[nblk + 1]
    tn = wb.shape[1]

    @pl.when(pl.program_id(1) == 0)
    def _():
        w = jnp.concatenate([r[...] for r in w_refs], axis=0)
        wb[...] = w[shift:shift + tn, :].T.astype(BF16)

    acc = _dot(x_ref[...], wb[...])
    for j in range(o_ref.shape[0]):
        o_ref[j] = acc[:, j * LANES:(j + 1) * LANES].astype(o_ref.dtype)


def _in_proj(xn, w_t, layer, col0, ncols, tm):
    m, k = xn.shape
    tn = PROJ_TN
    base, shift = divmod(col0, LANES)
    assert shift % 8 == 0
    nblk = tn // LANES + (1 if shift else 0)
    per = tn // LANES

    def wspec(r):
        return pl.BlockSpec((None, LANES, k), lambda j, i: (layer, base + per * j + r, 0))

    return pl.pallas_call(
        functools.partial(_in_proj_kernel, shift=shift, nblk=nblk),
        grid=(ncols // tn, m // tm),
        in_specs=[pl.BlockSpec((tm, k), lambda j, i: (i, 0))] + [wspec(r) for r in range(nblk)],
        out_specs=pl.BlockSpec((per, tm, LANES), lambda j, i: (j, i, 0)),
        out_shape=jax.ShapeDtypeStruct((ncols // LANES, m, LANES), BF16),
        scratch_shapes=[pltpu.VMEM((k, tn), BF16)],
        compiler_params=_params(("parallel", "arbitrary")),
        name="in_proj",
    )(xn, *([w_t] * nblk))


def _tail_kernel(x_ref, wg_ref, wi_ref, o_ref, wt, *, g_lane, ik_lane, iw_lane):
    @pl.when(pl.program_id(0) == 0)
    def _():
        wg = wg_ref[...]
        wi = wi_ref[...]
        ik = wi[ik_lane:ik_lane + IDX_DIM, :]
        iw = wi[iw_lane:iw_lane + IDX_HEADS, :]
        gates = wg[g_lane:g_lane + 2 * N_HEADS, :]
        pad = jnp.zeros((LANES - IDX_HEADS - 2 * N_HEADS, wg.shape[1]), F32)
        wt[...] = jnp.concatenate([ik, ik, iw, gates, pad], axis=0).T.astype(BF16)

    acc = _dot(x_ref[...], wt[...])
    o_ref[0] = acc[:, :LANES]
    o_ref[1] = acc[:, LANES:]


def _tail_proj(xn, w_t, layer, small, tm):
    m, k = xn.shape
    g_blk, g_lane = divmod(small["mi"], LANES)
    i_blk, ik_lane = divmod(small["ik"], LANES)
    iw_lane = small["iw"] - i_blk * LANES
    assert small["mf"] == small["mi"] + N_HEADS and g_lane + 2 * N_HEADS <= LANES
    assert ik_lane + IDX_DIM <= LANES and 0 <= iw_lane and iw_lane + IDX_HEADS <= LANES
    assert g_lane % 8 == 0 and ik_lane % 8 == 0 and iw_lane % 8 == 0
    return pl.pallas_call(
        functools.partial(_tail_kernel, g_lane=g_lane, ik_lane=ik_lane, iw_lane=iw_lane),
        grid=(m // tm,),
        in_specs=[pl.BlockSpec((tm, k), lambda i: (i, 0)),
                  pl.BlockSpec((None, LANES, k), lambda i: (layer, g_blk, 0)),
                  pl.BlockSpec((None, LANES, k), lambda i: (layer, i_blk, 0))],
        out_specs=pl.BlockSpec((2, tm, LANES), lambda i: (0, i, 0)),
        out_shape=jax.ShapeDtypeStruct((2, m, LANES), F32),
        scratch_shapes=[pltpu.VMEM((k, 2 * LANES), BF16)],
        compiler_params=_params(("arbitrary",)),
        name="tail_proj",
    )(xn, w_t, w_t)


def _mlstm_kernel(bias_ref, q_ref, k_ref, v_ref, o_ref, g_ref, cwq_ref, cwk_ref, ng_ref, out_ref,
                  qf, kf, qc, kc, st):
    hp = pl.program_id(1)
    U = HEADS_PER_STEP
    S = q_ref.shape[1]
    L = MLSTM_CHUNK
    nc = S // L
    d = HEAD_DIM
    PAD = 8

    R = min(S, 256)
    for u in range(U):
        qf[u, 0:PAD, :] = jnp.zeros((PAD, d), F32)
        kf[u, 0:PAD, :] = jnp.zeros((PAD, d), F32)
        qf[u, PAD:PAD + S, :] = q_ref[u].astype(F32)
        kf[u, PAD:PAD + S, :] = k_ref[u].astype(F32)
        ls = slice(u * d, (u + 1) * d)
        for r0 in range(0, S, R):
            aq = jnp.zeros((R, d), F32)
            ak = jnp.zeros((R, d), F32)
            for t in range(CONV_WIDTH):
                off = PAD - (CONV_WIDTH - 1) + t + r0
                aq = aq + cwq_ref[t:t + 1, ls] * qf[u, off:off + R, :]
                ak = ak + cwk_ref[t:t + 1, ls] * kf[u, off:off + R, :]
            qc[u, r0:r0 + R, :] = (aq * _sigmoid(aq) * (d ** -0.5)).astype(BF16)
            kc[u, r0:r0 + R, :] = ak * _sigmoid(ak)

    st[...] = jnp.zeros(st.shape, F32)
    row = lax.broadcasted_iota(I32, (L, L), 0)
    col = lax.broadcasted_iota(I32, (L, L), 1)
    causal = col <= row
    eye = col == row
    lane = lax.broadcasted_iota(I32, (L, d), 1)
    ones_col = jnp.where(lane == 0, 1.0, 0.0).astype(BF16)
    ng = ng_ref[...]

    def chunk_one(u, c, r, m):
        q = qc[u, pl.ds(r, L), :]
        kT = kc[u, pl.ds(r, L), :].T
        v = v_ref[u, pl.ds(r, L), :]
        vaug = jnp.concatenate([v, ones_col], axis=1)
        gates = g_ref[u, c]
        i_row = gates[0:1, :] + bias_ref[0, hp * U + u]
        f_row = gates[1:2, :] + bias_ref[1, hp * U + u]
        lf_row = jnp.minimum(f_row, 0.0) - jnp.log1p(jnp.exp(-jnp.abs(f_row)))
        b_col = jnp.sum(jnp.where(causal, lf_row, 0.0), axis=1, keepdims=True)
        b_row = jnp.sum(jnp.where(eye, b_col, 0.0), axis=0, keepdims=True)
        dlog = jnp.where(causal, b_col - b_row + i_row, NEG)
        inter = b_col + m
        m_t = jnp.maximum(inter, jnp.max(dlog, axis=1, keepdims=True))
        w_intra = jnp.exp(dlog - m_t)
        w_inter = jnp.exp(inter - m_t)
        s = _dot(q, kT.astype(BF16)) * w_intra
        res = w_inter * _dot(q, st[u].astype(BF16)) + _dot(s.astype(BF16), vaug)
        num = res[:, :d]
        den = res[:, d:d + 1]
        hh = num / jnp.maximum(jnp.abs(den), jnp.exp(-m_t))
        hn = hh * lax.rsqrt(jnp.mean(hh * hh, axis=1, keepdims=True) + EPS) * ng[:, u * d:(u + 1) * d]
        og = o_ref[u, pl.ds(r, L), :].astype(F32)
        out_ref[u, pl.ds(r, L), :] = (hn * _sigmoid(og)).astype(out_ref.dtype)
        ws_row = w_intra[L - 1:L, :]
        decay = w_inter[L - 1:L, :]
        st[u] = decay * st[u] + _dot((kT * ws_row).astype(BF16), vaug)
        return m_t[L - 1:L, :]

    def chunk(c, ms):
        r = pl.multiple_of(c * L, L)
        return tuple(chunk_one(u, c, r, ms[u]) for u in range(U))

    lax.fori_loop(0, nc, chunk, tuple(jnp.zeros((1, 1), F32) for _ in range(U)))


def _mlstm(proj_a, gates_t, bias_if, conv_w, norm_g, B, S):
    L = MLSTM_CHUNK
    nc = S // L
    d = HEAD_DIM
    H = N_HEADS
    U = HEADS_PER_STEP

    def slab(off):
        return pl.BlockSpec((U, S, d), lambda b, h: (off // U + h, b, 0))

    return pl.pallas_call(
        _mlstm_kernel,
        grid=(B, H // U),
        in_specs=[
            pl.BlockSpec(memory_space=pltpu.SMEM),
            slab(A_MQ), slab(A_MK), slab(A_MV), slab(A_MO),
            pl.BlockSpec((None, U, nc, 2, L), lambda b, h: (b, h, 0, 0, 0)),
            pl.BlockSpec((CONV_WIDTH, U * d), lambda b, h: (0, h)),
            pl.BlockSpec((CONV_WIDTH, U * d), lambda b, h: (0, H // U + h)),
            pl.BlockSpec((1, U * d), lambda b, h: (0, h)),
        ],
        out_specs=pl.BlockSpec((U, S, d), lambda b, h: (h, b, 0)),
        out_shape=jax.ShapeDtypeStruct((H, B * S, d), BF16),
        scratch_shapes=[
            pltpu.VMEM((U, S + 8, d), F32), pltpu.VMEM((U, S + 8, d), F32),
            pltpu.VMEM((U, S, d), BF16), pltpu.VMEM((U, S, d), F32),
            pltpu.VMEM((U, d, 2 * d), F32),
        ],
        compiler_params=_params(("parallel", "parallel")),
        name="mlstm",
    )(bias_if, proj_a, proj_a, proj_a, proj_a, gates_t, conv_w, conv_w, norm_g.reshape(1, H * d))


def _t5_thresholds():
    max_exact = N_BUCKETS // 2
    n = np.arange(0, 2 * MAX_DISTANCE)
    nf = np.maximum(n, 1).astype(np.float64)
    val = np.log(nf / max_exact) / math.log(MAX_DISTANCE / max_exact) * (N_BUCKETS - max_exact)
    frac = np.abs(val - np.round(val))
    frac_ok = (frac > 1e-4) | (n <= max_exact) | (n >= MAX_DISTANCE)
    assert frac_ok.all()
    large = np.minimum(max_exact + np.trunc(val).astype(np.int64), N_BUCKETS - 1)
    bucket = np.where(n < max_exact, n, large)
    assert (np.diff(bucket) >= 0).all() and bucket[MAX_DISTANCE] == N_BUCKETS - 1
    return [int(np.argmax(bucket >= j)) for j in range(1, N_BUCKETS)]


_T5_THR = _t5_thresholds()


def _dsa_prep_kernel(rb_ref, k_ref, ik_ref, gk_ref, kn, ikk, bias_ref):
    T = DSA_BLOCK
    H = N_HEADS
    row = lax.broadcasted_iota(I32, (T, T), 0)
    col = lax.broadcasted_iota(I32, (T, T), 1)
    gk = gk_ref[...]

    def norm_k(h, _):
        kh = k_ref[h].astype(F32)
        kn[h] = (kh * lax.rsqrt(jnp.mean(kh * kh, axis=1, keepdims=True) + EPS) * gk).astype(BF16)
        for o in range(2):
            n = o * T + row - col
            val = jnp.full((T, T), rb_ref[0, h], F32)
            for j, thr in enumerate(_T5_THR):
                val = jnp.where(n >= thr, rb_ref[j + 1, h], val)
            bias_ref[h, o] = val
        bias_ref[h, 2] = jnp.full((T, T), rb_ref[N_BUCKETS - 1, h], F32)
        return 0

    lax.fori_loop(0, H, norm_k, 0)
    ik = ik_ref[...]
    klane = lax.broadcasted_iota(I32, ik.shape, 1)
    ikk[0] = jnp.where(klane < IDX_DIM, ik, 0.0).astype(BF16)
    ikk[1] = jnp.where(klane >= IDX_DIM, ik, 0.0).astype(BF16)


def _dsa_kernel(q_ref, kn, v_ref, iq_ref, ikk, wt_ref, gq_ref, bias_ref, prev_ref, out_ref,
                wb, key_ref, mask_ref, *, topk, q0, nb, nqb):
    del prev_ref
    qis = [q0 + pl.program_id(1) * nqb + qb for qb in range(nqb)]
    T = DSA_BLOCK
    d = HEAD_DIM
    H = N_HEADS
    HALF = T // 2
    row = lax.broadcasted_iota(I32, (T, T), 0)
    col = lax.broadcasted_iota(I32, (T, T), 1)

    wsc = wt_ref[...] * (IDX_HEADS ** -0.5 * IDX_DIM ** -0.5)
    for qb in range(nqb):
        for h16 in range(IDX_HEADS):
            wb[qb * IDX_HEADS + h16] = jnp.broadcast_to(wsc[qb * T:(qb + 1) * T, h16:h16 + 1], (T, LANES))
    q_pairs = iq_ref[...].reshape((IDX_HEADS // 2) * nqb * T, LANES)

    def score_block(j, _):
        r = pl.multiple_of(j * T, T)
        kk = jnp.concatenate([ikk[0, pl.ds(r, T), :], ikk[1, pl.ds(r, T), :]], axis=0)
        dots = _dot_nt(q_pairs, kk)
        for qb in range(nqb):
            sc = jnp.zeros((T, T), F32)
            for h16 in range(IDX_HEADS):
                hp, odd = divmod(h16, 2)
                r0 = (hp * nqb + qb) * T
                sc = sc + wb[qb * IDX_HEADS + h16] * jnp.maximum(dots[r0:r0 + T, odd * T:(odd + 1) * T], 0.0)
            bits = lax.bitcast_convert_type(sc, I32)
            key = jnp.where(bits < 0, bits ^ jnp.int32(0x7FFFFFFF), bits)
            key = jnp.where(sc == 0.0, 0, key)
            key = jnp.where(j * T + col <= qis[qb] * T + row, key, INT_MIN)
            key_ref[qb, j] = key
        return 0

    lax.fori_loop(0, nb, score_block, 0, unroll=2)

    kf = float(topk)
    ones = jnp.ones((LANES, LANES), BF16)

    def count(chain, pred):
        qb, lo = chain
        acc = jnp.zeros((HALF, LANES), F32)
        for jb in range(nb):
            acc = acc + jnp.where(pred(key_ref[qb, jb, lo:lo + HALF, :]), 1.0, 0.0)
        return _dot(acc.astype(BF16), ones)

    chains = [(qb, lo) for qb in range(nqb) for lo in (0, HALF)]
    zero_i = jnp.zeros((HALF, LANES), I32)
    thr0 = tuple(jnp.where(count(ch, lambda k: k >= zero_i) >= kf, jnp.int32(0), jnp.int32(INT_MIN))
                 for ch in chains)

    def bisect_bit(thrs, bit):
        out = []
        for ch, t in zip(chains, thrs):
            cand = t | bit
            out.append(jnp.where(count(ch, lambda k, c=cand: k >= c) >= kf, cand, t))
        return tuple(out)

    def bisect_pair(it, thrs):
        hi_bit = jnp.left_shift(jnp.int32(1), 29 - 2 * it)
        lo_bit = jnp.left_shift(jnp.int32(1), 28 - 2 * it)
        out = []
        for ch, t in zip(chains, thrs):
            c01, c10, c11 = t | lo_bit, t | hi_bit, t | hi_bit | lo_bit
            n01, n10, n11 = (count(ch, lambda k, c=c: k >= c) >= kf for c in (c01, c10, c11))
            out.append(jnp.where(n11, c11, jnp.where(n10, c10, jnp.where(n01, c01, t))))
        return tuple(out)

    thrs = bisect_bit(thr0, jnp.int32(1 << 30))
    thrs = lax.fori_loop(0, 15, bisect_pair, thrs)
    needs = [kf - count(ch, lambda k, t=t: k > t) for ch, t in zip(chains, thrs)]

    tri = jnp.where(row <= col, 1.0, 0.0).astype(BF16)
    for qb in range(nqb):
        need = jnp.concatenate(needs[2 * qb:2 * qb + 2], axis=0)
        thrb = jnp.concatenate(thrs[2 * qb:2 * qb + 2], axis=0)
        seen = jnp.zeros((T, LANES), F32)
        for jb in range(nb):
            key = key_ref[qb, jb]
            tie = key == thrb
            tie16 = jnp.where(tie, 1.0, 0.0).astype(BF16)
            pre = _dot(tie16, tri) + seen
            m = jnp.where(key > thrb, 0.0, jnp.where(tie, jnp.where(pre <= need, 0.0, NEG), NEG))
            mask_ref[qb, jb] = jnp.where(key == INT_MIN, NEG, m)
            seen = seen + _dot(tie16, ones)

    gq = gq_ref[...]

    def head(h):
        qh = q_ref[h].astype(F32)
        qn = (qh * lax.rsqrt(jnp.mean(qh * qh, axis=1, keepdims=True) + EPS) * gq * (d ** -0.5)).astype(BF16)
        lgs = _dot_nt(qn, kn[h])
        p_rows, sums = [], []
        for qb in range(nqb):
            mx = None
            lg_blocks = []
            for jb in range(nb):
                lg = (lgs[qb * T:(qb + 1) * T, jb * T:(jb + 1) * T]
                      + bias_ref[h, jnp.clip(qis[qb] - jb, 0, 2)] + mask_ref[qb, jb])
                lg_blocks.append(lg)
                mx = lg if mx is None else jnp.maximum(mx, lg)
            rowmax = jnp.max(mx, axis=1, keepdims=True)
            l = jnp.zeros((T, T), F32)
            ps = []
            for jb in range(nb):
                p = jnp.exp(lg_blocks[jb] - rowmax)
                l = l + p
                ps.append(p.astype(BF16))
            p_rows.append(jnp.concatenate(ps, axis=1))
            sums.append(jnp.sum(l, axis=1, keepdims=True))
        acc = _dot(jnp.concatenate(p_rows, axis=0), v_ref[h])
        out_ref[h] = (acc / jnp.concatenate(sums, axis=0)).astype(out_ref.dtype)

    def head_group(g, _):
        for u in range(HEADS_PER_STEP):
            head(g * HEADS_PER_STEP + u)
        return 0

    lax.fori_loop(0, H // HEADS_PER_STEP, head_group, 0)


def _dsa(proj_b, tail, q_norm_g, k_norm_g, rel_bias, B, S):
    T = DSA_BLOCK
    nq = S // T
    d = HEAD_DIM
    H = N_HEADS
    topk = min(TOPK_MAX, S // 4)
    per = nq // DSA_BANDS
    nqb = math.gcd(per, DSA_QBLOCKS)
    TQ = nqb * T
    pb = proj_b.reshape(proj_b.shape[0], B, S, d)
    tl = tail.reshape(tail.shape[0], B, S, LANES)
    kn, ikk, bias = pl.pallas_call(
        _dsa_prep_kernel,
        grid=(B,),
        in_specs=[pl.BlockSpec(memory_space=pltpu.SMEM),
                  pl.BlockSpec((H, None, S, d), lambda b: (B_DK // H, b, 0, 0)),
                  pl.BlockSpec((None, None, S, LANES), lambda b: (0, b, 0, 0)),
                  pl.BlockSpec((1, d), lambda b: (0, 0))],
        out_specs=[pl.BlockSpec((H, None, S, d), lambda b: (0, b, 0, 0)),
                   pl.BlockSpec((2, None, S, LANES), lambda b: (0, b, 0, 0)),
                   pl.BlockSpec((H, 3, T, T), lambda b: (0, 0, 0, 0))],
        out_shape=[jax.ShapeDtypeStruct((H, B, S, d), BF16),
                   jax.ShapeDtypeStruct((2, B, S, LANES), BF16),
                   jax.ShapeDtypeStruct((H, 3, T, T), F32)],
        compiler_params=_params(("arbitrary",)),
        name="dsa_prep",
    )(rel_bias, pb, tl, k_norm_g.reshape(1, d))
    out = jnp.zeros((H, B, S, d), BF16)
    for band in range(DSA_BANDS):
        q0 = band * per
        nb = q0 + per
        W = nb * T
        in_specs = [
            pl.BlockSpec((H, None, TQ, d), lambda b, i: (B_DQ // H, b, q0 // nqb + i, 0)),
            pl.BlockSpec((H, None, W, d), lambda b, i: (0, b, 0, 0)),
            pl.BlockSpec((H, None, W, d), lambda b, i: (B_DV // H, b, 0, 0)),
            pl.BlockSpec((H, None, TQ, d), lambda b, i: (B_IQ // H, b, q0 // nqb + i, 0)),
            pl.BlockSpec((2, None, W, LANES), lambda b, i: (0, b, 0, 0)),
            pl.BlockSpec((None, None, TQ, LANES), lambda b, i: (1, b, q0 // nqb + i, 0)),
            pl.BlockSpec((1, d), lambda b, i: (0, 0)),
            pl.BlockSpec((H, 3, T, T), lambda b, i: (0, 0, 0, 0)),
            pl.BlockSpec(memory_space=pl.ANY),
        ]
        args = [pb, kn, pb, pb, ikk, tl, q_norm_g.reshape(1, d), bias, out]
        out = pl.pallas_call(
            functools.partial(_dsa_kernel, topk=topk, q0=q0, nb=nb, nqb=nqb),
            grid=(B, per // nqb),
            in_specs=in_specs,
            out_specs=pl.BlockSpec((H, None, TQ, d), lambda b, i: (0, b, q0 // nqb + i, 0)),
            out_shape=jax.ShapeDtypeStruct((H, B, S, d), BF16),
            scratch_shapes=[
                pltpu.VMEM((nqb * IDX_HEADS, T, LANES), F32),
                pltpu.VMEM((nqb, nb, T, T), I32),
                pltpu.VMEM((nqb, nb, T, T), F32),
            ],
            input_output_aliases={len(args) - 1: 0},
            compiler_params=_params(("parallel", "parallel")),
            name=f"dsa_band{band}",
        )(*args)
    return out.reshape(H, B * S, d)


def _sb_kernel(q_ref, k_ref, v_ref, out_ref):
    qi = pl.program_id(2)
    U = SB_HEADS
    T = SB_BLOCK
    d = HEAD_DIM
    R = SB_ROWS
    ns = T // R
    row = lax.broadcasted_iota(I32, (T, T), 0)
    col = lax.broadcasted_iota(I32, (T, T), 1)
    upper = jnp.where(row > col, 1.0, 0.0).astype(BF16)
    srow = lax.broadcasted_iota(I32, (R, T), 0)
    scol = lax.broadcasted_iota(I32, (R, T), 1)
    units = [(u, s) for u in range(U) for s in range(ns)]
    qs = [(q_ref[u, s * R:(s + 1) * R, :].astype(F32) * (d ** -0.5)).astype(BF16) for u, s in units]

    def block(n, j, diag, run, acc):
        u, s = units[n]
        r = pl.multiple_of(j * T, T)
        z = _dot_nt(qs[n], k_ref[u, pl.ds(r, T), :])
        sp = jnp.maximum(z, 0.0) + jnp.log(1.0 + jnp.exp(-jnp.abs(z)))
        strict = scol < srow + s * R
        ln = jnp.where(strict, -sp, 0.0) if diag else -sp
        parts = [ln.astype(BF16)]
        for _ in range(SB_PIECES - 1):
            parts.append((ln - sum(p.astype(F32) for p in parts)).astype(BF16))
        sums = _dot(jnp.concatenate(parts, axis=0), upper)
        suf = sum(sums[p * R:(p + 1) * R] for p in range(SB_PIECES))
        a = jnp.exp(z - sp + suf + run)
        if diag:
            a = jnp.where(strict, a, 0.0)
        acc = acc + _dot(a.astype(BF16), v_ref[u, pl.ds(r, T), :])
        return run + suf[:, 0:1] + ln[:, 0:1], acc

    state = []
    for n in range(len(units)):
        state.extend(block(n, qi, True, jnp.zeros((R, 1), F32), jnp.zeros((R, d), F32)))

    def earlier(i, st):
        out = []
        for n in range(len(units)):
            out.extend(block(n, qi - i, False, st[2 * n], st[2 * n + 1]))
        return tuple(out)

    state = lax.fori_loop(1, qi + 1, earlier, tuple(state))
    for n, (u, s) in enumerate(units):
        out_ref[u, s * R:(s + 1) * R, :] = state[2 * n + 1].astype(out_ref.dtype)


def _stick_breaking(proj_c, B, S):
    T = SB_BLOCK
    nq = S // T
    d = HEAD_DIM
    H = N_HEADS
    U = SB_HEADS
    return pl.pallas_call(
        _sb_kernel,
        grid=(B, H // U, nq),
        in_specs=[
            pl.BlockSpec((U, T, d), lambda b, h, i: (C_SQ // U + h, b * nq + i, 0)),
            pl.BlockSpec((U, S, d), lambda b, h, i: (C_SK // U + h, b, 0)),
            pl.BlockSpec((U, S, d), lambda b, h, i: (C_SV // U + h, b, 0)),
        ],
        out_specs=pl.BlockSpec((U, T, d), lambda b, h, i: (h, b * nq + i, 0)),
        out_shape=jax.ShapeDtypeStruct((H, B * S, d), BF16),
        compiler_params=_params(("parallel", "parallel", "parallel")),
        name="stick_breaking",
    )(proj_c, proj_c, proj_c)


def _merge_kernel(hm_ref, hd_ref, hs_ref, g0_ref, g1_ref, g2_ref, w_ref, o_ref, wb):
    H = N_HEADS

    @pl.when(pl.program_id(1) == 0)
    def _():
        wb[...] = w_ref[...].astype(BF16)

    acc = None
    ns = o_ref.shape[1] // LANES
    for n, (br, gr) in enumerate(((hm_ref, g0_ref), (hd_ref, g1_ref), (hs_ref, g2_ref))):
        a = jnp.concatenate([br[h] for h in range(H)], axis=1)
        up = _dot(a, wb[n])
        gate = jnp.concatenate([gr[s] for s in range(ns)], axis=1).astype(F32)
        term = _sigmoid(gate) * up
        acc = term if acc is None else acc + term
    o_ref[...] = acc.astype(o_ref.dtype)


def _merge(hm, hd, hs, proj_c, w_branch, layer, tm=1024, tn=512):
    H, n, d = hm.shape
    tm = min(tm, n)
    D = w_branch.shape[3]
    ns = tn // LANES
    br = pl.BlockSpec((H, tm, d), lambda j, i: (0, i, 0))

    def gate(b):
        first = (C_GATE + b * (D // LANES)) // ns
        return pl.BlockSpec((ns, tm, LANES), lambda j, i: (first + j, i, 0))

    return pl.pallas_call(
        _merge_kernel,
        grid=(D // tn, n // tm),
        in_specs=[br, br, br, gate(0), gate(1), gate(2),
                  pl.BlockSpec((None, N_BRANCHES, H * d, tn), lambda j, i: (layer, 0, 0, j))],
        out_specs=pl.BlockSpec((tm, tn), lambda j, i: (i, j)),
        out_shape=jax.ShapeDtypeStruct((n, D), BF16),
        scratch_shapes=[pltpu.VMEM((N_BRANCHES, H * d, tn), BF16)],
        compiler_params=_params(("parallel", "arbitrary")),
        name="branch_merge",
    )(hm, hd, hs, proj_c, proj_c, proj_c, w_branch)


def _mm_res_kernel(a_ref, w_ref, x_ref, o_ref, wb):
    @pl.when(pl.program_id(1) == 0)
    def _():
        wb[...] = w_ref[...].astype(BF16)

    o_ref[...] = x_ref[...] + _dot(a_ref[...], wb[...])


def _matmul_residual(a, w, layer, x, tm=1024, tn=512):
    m, k = a.shape
    tm = min(tm, m)
    n = w.shape[2]
    return pl.pallas_call(
        _mm_res_kernel,
        grid=(n // tn, m // tm),
        in_specs=[pl.BlockSpec((tm, k), lambda j, i: (i, 0)),
                  pl.BlockSpec((None, k, tn), lambda j, i: (layer, 0, j)),
                  pl.BlockSpec((tm, tn), lambda j, i: (i, j))],
        out_specs=pl.BlockSpec((tm, tn), lambda j, i: (i, j)),
        out_shape=jax.ShapeDtypeStruct((m, n), F32),
        scratch_shapes=[pltpu.VMEM((k, tn), BF16)],
        compiler_params=_params(("parallel", "arbitrary")),
        name="out_proj",
    )(a, w, x)


def _router_kernel(x_ref, g_ref, wr_ref, br_ref, xn_ref, ids_ref, wts_ref, cnt_ref, carry, lower):
    i = pl.program_id(0)
    tm, D = x_ref.shape

    @pl.when(i == 0)
    def _():
        carry[...] = jnp.zeros(carry.shape, F32)
        r = lax.broadcasted_iota(I32, (tm, tm), 0)
        c = lax.broadcasted_iota(I32, (tm, tm), 1)
        lower[...] = jnp.where(c < r, 1.0, 0.0).astype(BF16)

    x = x_ref[...]
    xn = x * lax.rsqrt(jnp.mean(x * x, axis=-1, keepdims=True) + EPS) * g_ref[...]
    xn_ref[...] = xn
    logits = jnp.dot(xn, wr_ref[...], preferred_element_type=F32, precision=lax.Precision.HIGHEST)
    biased = logits + br_ref[...]
    lane = lax.broadcasted_iota(I32, (tm, LANES), 1)
    lanef = lane.astype(F32)
    big = float(LANES)

    def first_lane(mask):
        return jnp.min(jnp.where(mask, lanef, big), axis=1, keepdims=True)

    gmask = lane < N_GROUPS
    gmax = jnp.max(jnp.where(gmask, biased, NEG), axis=1, keepdims=True)
    g_sel = first_lane(gmask & (biased == gmax))
    gm = jnp.max(jnp.where(gmask, logits, NEG), axis=1, keepdims=True)
    ge = jnp.where(gmask, jnp.exp(logits - gm), 0.0)
    p_group = jnp.sum(jnp.where(lanef == g_sel, ge, 0.0), axis=1, keepdims=True) / jnp.sum(ge, axis=1, keepdims=True)
    lo = N_GROUPS + EXPERTS_PER_GROUP * g_sel
    emask = (lanef >= lo) & (lanef < lo + EXPERTS_PER_GROUP)
    eb = jnp.where(emask, biased, NEG)
    e1 = first_lane(emask & (eb == jnp.max(eb, axis=1, keepdims=True)))
    emask2 = emask & (lanef != e1)
    eb2 = jnp.where(emask2, biased, NEG)
    e2 = first_lane(emask2 & (eb2 == jnp.max(eb2, axis=1, keepdims=True)))
    em = jnp.max(jnp.where(emask, logits, NEG), axis=1, keepdims=True)
    ee = jnp.where(emask, jnp.exp(logits - em), 0.0)
    s1 = jnp.sum(jnp.where(lanef == e1, ee, 0.0), axis=1, keepdims=True)
    s2 = jnp.sum(jnp.where(lanef == e2, ee, 0.0), axis=1, keepdims=True)
    se = jnp.sum(ee, axis=1, keepdims=True)
    w1 = s1 / se
    w2 = s2 / se
    wsum = w1 + w2
    w1 = p_group * w1 / wsum
    w2 = p_group * w2 / wsum
    x1 = e1 - N_GROUPS
    x2 = e2 - N_GROUPS

    onehot = jnp.where((lanef == x1) | (lanef == x2), 1.0, 0.0)
    prefix = _dot(lower[...], onehot.astype(BF16)) + carry[...]
    r1 = jnp.sum(jnp.where(lanef == x1, prefix, 0.0), axis=1, keepdims=True)
    r2 = jnp.sum(jnp.where(lanef == x2, prefix, 0.0), axis=1, keepdims=True)
    carry[...] = carry[...] + jnp.sum(onehot, axis=0, keepdims=True)
    cnt_ref[...] = carry[...].astype(I32)

    idsf = jnp.where(lane == 0, x1, jnp.where(lane == 1, x2, jnp.where(lane == 2, r1, jnp.where(lane == 3, r2, 0.0))))
    ids_ref[...] = idsf.astype(I32)
    wts_ref[...] = jnp.where(lane == 0, w1, jnp.where(lane == 1, w2, 0.0))


def _router(x2d, g, w_router, b_router, tm=512):
    n, D = x2d.shape
    return pl.pallas_call(
        _router_kernel,
        grid=(n // tm,),
        in_specs=[pl.BlockSpec((tm, D), lambda i: (i, 0)), pl.BlockSpec((1, D), lambda i: (0, 0)),
                  pl.BlockSpec((D, LANES), lambda i: (0, 0)), pl.BlockSpec((1, LANES), lambda i: (0, 0))],
        out_specs=[pl.BlockSpec((tm, D), lambda i: (i, 0)),
                   pl.BlockSpec((tm, LANES), lambda i: (i, 0)),
                   pl.BlockSpec((tm, LANES), lambda i: (i, 0)),
                   pl.BlockSpec((1, LANES), lambda i: (0, 0))],
        out_shape=[jax.ShapeDtypeStruct((n, D), F32),
                   jax.ShapeDtypeStruct((n, LANES), I32),
                   jax.ShapeDtypeStruct((n, LANES), F32),
                   jax.ShapeDtypeStruct((1, LANES), I32)],
        scratch_shapes=[pltpu.VMEM((1, LANES), F32), pltpu.VMEM((tm, tm), BF16)],
        compiler_params=_params(("arbitrary",)),
        name="moe_router",
    )(x2d, g.reshape(1, D), w_router, b_router)


def _dispatch_kernel(p1_ref, p2_ref, xn_ref, xs_in_ref, xs_ref, sem, *, tb):
    del xs_in_ref
    base = pl.program_id(0) * tb

    def copies(t):
        src = xn_ref.at[pl.ds(t, 1), :]
        return (pltpu.make_async_copy(src, xs_ref.at[pl.ds(p1_ref[base + t], 1), :], sem),
                pltpu.make_async_copy(src, xs_ref.at[pl.ds(p2_ref[base + t], 1), :], sem))

    def issue(t, _):
        for n, cp in enumerate(copies(t)):
            cp.start(priority=n)
        return 0

    lax.fori_loop(0, tb, issue, 0, unroll=DMA_UNROLL)

    def drain(t, _):
        for cp in copies(t):
            cp.wait()
        return 0

    lax.fori_loop(0, tb, drain, 0, unroll=DMA_UNROLL)


def _dispatch(pos1, pos2, xn, xs0, tb=512):
    n, D = xn.shape
    n_rows = xs0.shape[0]
    tb = min(tb, n)
    grid_spec = pltpu.PrefetchScalarGridSpec(
        num_scalar_prefetch=2,
        grid=(n // tb,),
        in_specs=[pl.BlockSpec((tb, D), lambda i, p1, p2: (i, 0)),
                  pl.BlockSpec(memory_space=pl.ANY)],
        out_specs=pl.BlockSpec(memory_space=pl.ANY),
        scratch_shapes=[pltpu.SemaphoreType.DMA(())],
    )
    return pl.pallas_call(
        functools.partial(_dispatch_kernel, tb=tb),
        grid_spec=grid_spec,
        out_shape=jax.ShapeDtypeStruct((n_rows, D), F32),
        input_output_aliases={3: 0},
        compiler_params=pltpu.CompilerParams(dimension_semantics=("arbitrary",), has_side_effects=True),
        name="moe_dispatch",
    )(pos1, pos2, xn, xs0)


def _expert_kernel(te_ref, nv_ref, ne_ref, grp_ref, xs_ref, wg_ref, wu_ref, wd_ref, ys_ref,
                   wgf, wuf, wdf, wgb, wub, wdb, sem, *, layer):
    i = pl.program_id(0)
    prev = te_ref[jnp.maximum(i - 1, 0)]

    def fetch(e, slot):
        return (pltpu.make_async_copy(wg_ref.at[layer, e], wgf.at[slot], sem.at[slot]),
                pltpu.make_async_copy(wu_ref.at[layer, e], wuf.at[slot], sem.at[slot]),
                pltpu.make_async_copy(wd_ref.at[layer, e], wdf.at[slot], sem.at[slot]))

    @pl.when(i < nv_ref[0])
    def _():
        @pl.when(i == 0)
        def _():
            for cp in fetch(te_ref[0], 0):
                cp.start()

        @pl.when((i == 0) | (te_ref[i] != prev))
        def _():
            slot = grp_ref[i] % 2
            for cp in fetch(te_ref[i], slot):
                cp.wait()

            @pl.when(ne_ref[i] >= 0)
            def _():
                for cp in fetch(ne_ref[i], 1 - slot):
                    cp.start(priority=1)

            wgb[...] = wgf[slot].astype(BF16)
            wub[...] = wuf[slot].astype(BF16)
            wdb[...] = wdf[slot].astype(BF16)

        x = xs_ref[...].astype(BF16)
        g = _dot(x, wgb[...])
        u = _dot(x, wub[...])
        hcur = (g * _sigmoid(g) * u).astype(BF16)
        ys_ref[...] = _dot(hcur, wdb[...])

    @pl.when(i >= nv_ref[0])
    def _():
        ys_ref[...] = jnp.zeros(ys_ref.shape, F32)


def _experts(tile_expert, n_valid, xs, w_gate, w_up, w_down, layer):
    n_rows = xs.shape[0]
    _, E, D, Fe = w_gate.shape
    tm = MOE_TILE
    n_tiles = n_rows // tm
    tiles = jnp.arange(n_tiles, dtype=I32)
    first = (tiles < n_valid[0]) & ((tiles == 0) | (tile_expert != jnp.roll(tile_expert, 1)))
    group = (jnp.cumsum(first.astype(I32)) - 1).astype(I32)
    later = lax.cummin(jnp.where(first, tiles, n_tiles), reverse=True)
    nxt = jnp.concatenate([later[1:], jnp.full((1,), n_tiles, I32)])
    next_expert = jnp.where(nxt < n_tiles, tile_expert[jnp.minimum(nxt, n_tiles - 1)], -1).astype(I32)

    def row_map(i, te, nv, ne, grp):
        return (jnp.minimum(i, jnp.maximum(nv[0] - 1, 0)), 0)

    grid_spec = pltpu.PrefetchScalarGridSpec(
        num_scalar_prefetch=4,
        grid=(n_tiles,),
        in_specs=[pl.BlockSpec((tm, D), row_map),
                  pl.BlockSpec(memory_space=pl.ANY), pl.BlockSpec(memory_space=pl.ANY),
                  pl.BlockSpec(memory_space=pl.ANY)],
        out_specs=pl.BlockSpec((tm, D), lambda i, te, nv, ne, grp: (i, 0)),
        scratch_shapes=[pltpu.VMEM((2, D, Fe), F32), pltpu.VMEM((2, D, Fe), F32), pltpu.VMEM((2, Fe, D), F32),
                        pltpu.VMEM((D, Fe), BF16), pltpu.VMEM((D, Fe), BF16), pltpu.VMEM((Fe, D), BF16),
                        pltpu.SemaphoreType.DMA((2,))],
    )
    return pl.pallas_call(
        functools.partial(_expert_kernel, layer=layer),
        grid_spec=grid_spec,
        out_shape=jax.ShapeDtypeStruct((n_rows, D), F32),
        compiler_params=_params(("arbitrary",)),
        name="moe_experts",
    )(tile_expert, n_valid, next_expert, group, xs, w_gate, w_up, w_down)


def _combine_kernel(p1_ref, p2_ref, ys_ref, x_ref, w_ref, *rest, with_norm):
    if with_norm:
        g_ref, o_ref, xn_ref, buf, sem = rest
    else:
        o_ref, buf, sem = rest
    i = pl.program_id(0)
    tc = x_ref.shape[0]
    slot = i % 2

    def copies(tile, slot, t):
        tok = tile * tc + t
        return (pltpu.make_async_copy(ys_ref.at[pl.ds(p1_ref[tok], 1), :], buf.at[slot, 0, pl.ds(t, 1), :],
                                      sem.at[slot]),
                pltpu.make_async_copy(ys_ref.at[pl.ds(p2_ref[tok], 1), :], buf.at[slot, 1, pl.ds(t, 1), :],
                                      sem.at[slot]))

    def issue(tile, slot):
        def body(t, _):
            for n, cp in enumerate(copies(tile, slot, t)):
                cp.start(priority=n)
            return 0

        lax.fori_loop(0, tc, body, 0, unroll=DMA_UNROLL)

    @pl.when(i == 0)
    def _():
        issue(0, 0)

    @pl.when(i + 1 < pl.num_programs(0))
    def _():
        issue(i + 1, 1 - slot)

    def drain(t, _):
        for cp in copies(i, slot, t):
            cp.wait()
        return 0

    lax.fori_loop(0, tc, drain, 0, unroll=DMA_UNROLL)
    w = w_ref[...]
    out = x_ref[...] + w[:, 0:1] * buf[slot, 0] + w[:, 1:2] * buf[slot, 1]
    o_ref[...] = out
    if with_norm:
        ms = jnp.mean(out * out, axis=-1, keepdims=True)
        xn_ref[...] = (out * lax.rsqrt(ms + EPS) * g_ref[...]).astype(BF16)


def _combine(pos1, pos2, ys, x2d, wts, next_g, tc=256):
    n, D = x2d.shape
    with_norm = next_g is not None
    row = pl.BlockSpec((tc, D), lambda i, p1, p2: (i, 0))
    in_specs = [pl.BlockSpec(memory_space=pl.ANY), row, pl.BlockSpec((tc, LANES), lambda i, p1, p2: (i, 0))]
    args = [pos1, pos2, ys, x2d, wts]
    out_specs, out_shape = row, jax.ShapeDtypeStruct((n, D), F32)
    if with_norm:
        in_specs.append(pl.BlockSpec((1, D), lambda i, p1, p2: (0, 0)))
        args.append(next_g.reshape(1, D))
        out_specs, out_shape = [row, row], [out_shape, jax.ShapeDtypeStruct((n, D), BF16)]
    grid_spec = pltpu.PrefetchScalarGridSpec(
        num_scalar_prefetch=2,
        grid=(n // tc,),
        in_specs=in_specs,
        out_specs=out_specs,
        scratch_shapes=[pltpu.VMEM((2, 2, tc, D), F32), pltpu.SemaphoreType.DMA((2,))],
    )
    res = pl.pallas_call(
        functools.partial(_combine_kernel, with_norm=with_norm),
        grid_spec=grid_spec,
        out_shape=out_shape,
        compiler_params=_params(("arbitrary",)),
        name="moe_combine",
    )(*args)
    return tuple(res) if with_norm else (res, None)


def _pos_kernel(ids_ref, cnt_ref, pos_ref):
    tm = ids_ref.shape[0]
    tiles = jnp.floor((cnt_ref[...].astype(F32) + (MOE_TILE - 1)) / MOE_TILE)
    r = lax.broadcasted_iota(I32, (LANES, LANES), 0)
    c = lax.broadcasted_iota(I32, (LANES, LANES), 1)
    before = jnp.where(r < c, 1.0, 0.0).astype(BF16)
    first_tile = _dot(jnp.broadcast_to(tiles, (8, LANES)).astype(BF16), before)
    offs = first_tile[0:1, :] * MOE_TILE
    ids = ids_ref[...].astype(F32)
    lane = lax.broadcasted_iota(I32, (tm, LANES), 1)
    lanef = lane.astype(F32)
    p1 = jnp.sum(jnp.where(lanef == ids[:, 0:1], offs, 0.0), axis=1, keepdims=True) + ids[:, 2:3]
    p2 = jnp.sum(jnp.where(lanef == ids[:, 1:2], offs, 0.0), axis=1, keepdims=True) + ids[:, 3:4]
    packed = jnp.where(lane == 0, p1, jnp.where(lane == 1, p2, 0.0))
    pos_ref[...] = packed.T[0:8, :].astype(I32)


def _positions(ids, counts, tm=512):
    n = ids.shape[0]
    assert MOE_TILE & (MOE_TILE - 1) == 0
    return pl.pallas_call(
        _pos_kernel,
        grid=(n // tm,),
        in_specs=[pl.BlockSpec((tm, LANES), lambda i: (i, 0)), pl.BlockSpec((1, LANES), lambda i: (0, 0))],
        out_specs=pl.BlockSpec((8, tm), lambda i: (0, i)),
        out_shape=jax.ShapeDtypeStruct((8, n), I32),
        compiler_params=_params(("parallel",)),
        name="moe_positions",
    )(ids, counts)


def _hier_moe(x2d, norm_g, w_rg, b_rg, w_re, b_re, w_gate, w_up, w_down, layer, xs_buf, next_g):
    n, D = x2d.shape
    tm = MOE_TILE
    pad = LANES - N_GROUPS - N_EXPERTS
    w_router = jnp.concatenate([w_rg, w_re, jnp.zeros((D, pad), F32)], axis=1)
    b_router = jnp.concatenate([b_rg, b_re.reshape(-1), jnp.zeros((pad,), F32)]).reshape(1, LANES)
    xn3, ids, wts, counts = _router(x2d, norm_g, w_router, b_router)
    cnt = counts[0, :N_EXPERTS]
    padded = ((cnt + tm - 1) // tm) * tm
    ends = jnp.cumsum(padded)
    n_tiles = xs_buf.shape[0] // tm
    tile_start = jnp.arange(n_tiles, dtype=I32) * tm
    tile_expert = jnp.minimum(jnp.sum(tile_start[:, None] >= ends[None, :], axis=1), N_EXPERTS - 1).astype(I32)
    n_valid = (ends[-1] // tm).astype(I32).reshape(1)
    last_e = tile_expert[jnp.maximum(n_valid[0] - 1, 0)]
    tile_expert = jnp.where(jnp.arange(n_tiles) < n_valid[0], tile_expert, last_e)
    pos = _positions(ids, counts)
    pos1, pos2 = pos[0], pos[1]
    xs = _dispatch(pos1, pos2, xn3, xs_buf)
    ys = _experts(tile_expert, n_valid, xs, w_gate, w_up, w_down, layer)
    out, xn_next = _combine(pos1, pos2, ys, x2d, wts, next_g)
    return out, xn_next, xs


def _in_proj_regions(D):
    bw = BRANCH_WIDTH
    sizes = (bw, bw, bw, bw, N_HEADS, N_HEADS, bw, bw, bw, IDX_HEADS * IDX_DIM, IDX_DIM, IDX_HEADS,
             bw, bw, bw, N_BRANCHES * D)
    offs = np.concatenate([[0], np.cumsum(sizes)]).tolist()
    region_a = (offs[0], offs[4] - offs[0])
    region_b = (offs[6], offs[10] - offs[6])
    region_c = (offs[12], offs[16] - offs[12])
    small = dict(mi=offs[4], mf=offs[5], ik=offs[10], iw=offs[11])
    return region_a, region_b, region_c, small


def _token_mixer(x2d, xn, B, S, layer, norm_g, w_in, conv_w, b_i, b_f, mlstm_norm_g, q_norm_g, k_norm_g,
                 w_branch, w_out, rel_bias):
    n, D = x2d.shape
    L = MLSTM_CHUNK
    H = N_HEADS
    tm = min(n, PROJ_TM)
    if xn is None:
        xn = _rmsnorm(x2d, norm_g)
    ra, rb, rc, small = _in_proj_regions(D)
    w_t = jnp.swapaxes(w_in, 1, 2)
    proj_a = _in_proj(xn, w_t, layer, ra[0], ra[1], tm)
    proj_b = _in_proj(xn, w_t, layer, rb[0], rb[1], tm)
    proj_c = _in_proj(xn, w_t, layer, rc[0], rc[1], tm)
    tail = _tail_proj(xn, w_t, layer, small, tm)
    g = tail[1][:, IDX_HEADS:IDX_HEADS + 2 * H].reshape(B, S // L, L, 2, H)
    gates_t = jnp.transpose(g, (0, 4, 1, 3, 2))
    hm = _mlstm(proj_a, gates_t, jnp.stack([b_i, b_f]), conv_w, mlstm_norm_g, B, S)
    hd = _dsa(proj_b, tail, q_norm_g, k_norm_g, rel_bias, B, S)
    hs = _stick_breaking(proj_c, B, S)
    merged = _merge(hm, hd, hs, proj_c, w_branch, layer)
    return _matmul_residual(merged, w_out, layer, x2d)


def kernel(x, norm1_g, w_in, conv_w, b_i, b_f, mlstm_norm_g, q_norm_g, k_norm_g, w_branch, w_out, norm2_g,
           w_router_g, b_router_g, w_router_e, b_router_e, w_gate, w_up, w_down, rel_bias):
    B, S, D = x.shape
    x2d = x.reshape(B * S, D)
    xs_buf = jnp.zeros((2 * B * S + N_EXPERTS * MOE_TILE, D), F32)
    depth = w_in.shape[0]
    xn = None
    for l in range(depth):
        x2d = _token_mixer(x2d, xn, B, S, l, norm1_g[l], w_in, conv_w[l], b_i[l], b_f[l], mlstm_norm_g[l],
                           q_norm_g[l], k_norm_g[l], w_branch, w_out, rel_bias)
        next_g = norm1_g[l + 1] if l + 1 < depth else None
        x2d, xn, xs_buf = _hier_moe(x2d, norm2_g[l], w_router_g[l], b_router_g[l], w_router_e[l], b_router_e[l],
                                    w_gate, w_up, w_down, l, xs_buf, next_g)
    return x2d.reshape(B, S, D)
```

```python
import functools
import math

import numpy as np
import jax
import jax.numpy as jnp
from jax import lax
from jax.experimental import pallas as pl
from jax.experimental.pallas import tpu as pltpu

F32 = jnp.float32
BF16 = jnp.bfloat16
I32 = jnp.int32

LANES = 128
HEAD_DIM = 128
N_HEADS = 8
BRANCH_WIDTH = N_HEADS * HEAD_DIM
N_BRANCHES = 3
CONV_WIDTH = 4
IDX_HEADS = 16
IDX_DIM = 64
TOPK_MAX = 256
N_BUCKETS = 32
MAX_DISTANCE = 128
N_GROUPS = 4
EXPERTS_PER_GROUP = 8
N_EXPERTS = N_GROUPS * EXPERTS_PER_GROUP
EPS = 1e-6
NEG = -1e30
INT_MIN = -(2 ** 31)

MLSTM_CHUNK = 256
DSA_BLOCK = 128
DSA_BANDS = 8
DSA_QBLOCKS = 2
SB_BLOCK = 512
SB_ROWS = 512
SB_PIECES = 1
SB_HEADS = 4
HEADS_PER_STEP = 4
MOE_TILE = 256
DMA_UNROLL = 8
PROJ_TN = 512
PROJ_TM = 2048
VMEM_LIMIT = 56 * 1024 * 1024

A_MQ, A_MK, A_MV, A_MO = 0, 8, 16, 24
B_DQ, B_DK, B_DV, B_IQ = 0, 8, 16, 24
C_SQ, C_SK, C_SV, C_GATE = 0, 8, 16, 24


def _params(sem):
    return pltpu.CompilerParams(dimension_semantics=sem, vmem_limit_bytes=VMEM_LIMIT)


def _dot(a, b):
    return jnp.dot(a, b, preferred_element_type=F32)


def _dot_nt(a, b):
    return lax.dot_general(a, b, (((1,), (1,)), ((), ())), preferred_element_type=F32)


def _sigmoid(z):
    return 1.0 / (1.0 + jnp.exp(-z))


def _rmsnorm_kernel(x_ref, g_ref, o_ref):
    x = x_ref[...]
    ms = jnp.mean(x * x, axis=-1, keepdims=True)
    o_ref[...] = (x * lax.rsqrt(ms + EPS) * g_ref[...]).astype(o_ref.dtype)


def _rmsnorm(x2d, g, tm=512):
    n, d = x2d.shape
    return pl.pallas_call(
        _rmsnorm_kernel,
        grid=(n // tm,),
        in_specs=[pl.BlockSpec((tm, d), lambda i: (i, 0)), pl.BlockSpec((1, d), lambda i: (0, 0))],
        out_specs=pl.BlockSpec((tm, d), lambda i: (i, 0)),
        out_shape=jax.ShapeDtypeStruct((n, d), BF16),
        compiler_params=_params(("parallel",)),
        name="rmsnorm",
    )(x2d, g.reshape(1, d))


def _in_proj_kernel(x_ref, *rest, shift, nblk):
    w_refs, o_ref, wb = rest[:nblk], rest[nblk], rest[nblk + 1]
    tn = wb.shape[1]

    @pl.when(pl.program_id(1) == 0)
    def _():
        w = jnp.concatenate([r[...] for r in w_refs], axis=0)
        wb[...] = w[shift:shift + tn, :].T.astype(BF16)

    acc = _dot(x_ref[...], wb[...])
    for j in range(o_ref.shape[0]):
        o_ref[j] = acc[:, j * LANES:(j + 1) * LANES].astype(o_ref.dtype)


def _in_proj(xn, w_t, layer, col0, ncols, tm):
    m, k = xn.shape
    tn = PROJ_TN
    base, shift = divmod(col0, LANES)
    assert shift % 8 == 0
    nblk = tn // LANES + (1 if shift else 0)
    per = tn // LANES

    def wspec(r):
        return pl.BlockSpec((None, LANES, k), lambda j, i: (layer, base + per * j + r, 0))

    return pl.pallas_call(
        functools.partial(_in_proj_kernel, shift=shift, nblk=nblk),
        grid=(ncols // tn, m // tm),
        in_specs=[pl.BlockSpec((tm, k), lambda j, i: (i, 0))] + [wspec(r) for r in range(nblk)],
        out_specs=pl.BlockSpec((per, tm, LANES), lambda j, i: (j, i, 0)),
        out_shape=jax.ShapeDtypeStruct((ncols // LANES, m, LANES), BF16),
        scratch_shapes=[pltpu.VMEM((k, tn), BF16)],
        compiler_params=_params(("parallel", "arbitrary")),
        name="in_proj",
    )(xn, *([w_t] * nblk))


def _tail_kernel(x_ref, wg_ref, wi_ref, o_ref, wt, *, g_lane, ik_lane, iw_lane):
    @pl.when(pl.program_id(0) == 0)
    def _():
        wg = wg_ref[...]
        wi = wi_ref[...]
        ik = wi[ik_lane:ik_lane + IDX_DIM, :]
        iw = wi[iw_lane:iw_lane + IDX_HEADS, :]
        gates = wg[g_lane:g_lane + 2 * N_HEADS, :]
        pad = jnp.zeros((LANES - IDX_HEADS - 2 * N_HEADS, wg.shape[1]), F32)
        wt[...] = jnp.concatenate([ik, ik, iw, gates, pad], axis=0).T.astype(BF16)

    acc = _dot(x_ref[...], wt[...])
    o_ref[0] = acc[:, :LANES]
    o_ref[1] = acc[:, LANES:]


def _tail_proj(xn, w_t, layer, small, tm):
    m, k = xn.shape
    g_blk, g_lane = divmod(small["mi"], LANES)
    i_blk, ik_lane = divmod(small["ik"], LANES)
    iw_lane = small["iw"] - i_blk * LANES
    assert small["mf"] == small["mi"] + N_HEADS and g_lane + 2 * N_HEADS <= LANES
    assert ik_lane + IDX_DIM <= LANES and 0 <= iw_lane and iw_lane + IDX_HEADS <= LANES
    assert g_lane % 8 == 0 and ik_lane % 8 == 0 and iw_lane % 8 == 0
    return pl.pallas_call(
        functools.partial(_tail_kernel, g_lane=g_lane, ik_lane=ik_lane, iw_lane=iw_lane),
        grid=(m // tm,),
        in_specs=[pl.BlockSpec((tm, k), lambda i: (i, 0)),
                  pl.BlockSpec((None, LANES, k), lambda i: (layer, g_blk, 0)),
                  pl.BlockSpec((None, LANES, k), lambda i: (layer, i_blk, 0))],
        out_specs=pl.BlockSpec((2, tm, LANES), lambda i: (0, i, 0)),
        out_shape=jax.ShapeDtypeStruct((2, m, LANES), F32),
        scratch_shapes=[pltpu.VMEM((k, 2 * LANES), BF16)],
        compiler_params=_params(("arbitrary",)),
        name="tail_proj",
    )(xn, w_t, w_t)


def _mlstm_kernel(bias_ref, q_ref, k_ref, v_ref, o_ref, g_ref, cwq_ref, cwk_ref, ng_ref, out_ref,
                  qf, kf, qc, kc, st):
    hp = pl.program_id(1)
    U = HEADS_PER_STEP
    S = q_ref.shape[1]
    L = MLSTM_CHUNK
    nc = S // L
    d = HEAD_DIM
    PAD = 8

    R = min(S, 256)
    for u in range(U):
        qf[u, 0:PAD, :] = jnp.zeros((PAD, d), F32)
        kf[u, 0:PAD, :] = jnp.zeros((PAD, d), F32)
        qf[u, PAD:PAD + S, :] = q_ref[u].astype(F32)
        kf[u, PAD:PAD + S, :] = k_ref[u].astype(F32)
        ls = slice(u * d, (u + 1) * d)
        for r0 in range(0, S, R):
            aq = jnp.zeros((R, d), F32)
            ak = jnp.zeros((R, d), F32)
            for t in range(CONV_WIDTH):
                off = PAD - (CONV_WIDTH - 1) + t + r0
                aq = aq + cwq_ref[t:t + 1, ls] * qf[u, off:off + R, :]
                ak = ak + cwk_ref[t:t + 1, ls] * kf[u, off:off + R, :]
            qc[u, r0:r0 + R, :] = (aq * _sigmoid(aq) * (d ** -0.5)).astype(BF16)
            kc[u, r0:r0 + R, :] = ak * _sigmoid(ak)

    st[...] = jnp.zeros(st.shape, F32)
    row = lax.broadcasted_iota(I32, (L, L), 0)
    col = lax.broadcasted_iota(I32, (L, L), 1)
    causal = col <= row
    eye = col == row
    lane = lax.broadcasted_iota(I32, (L, d), 1)
    ones_col = jnp.where(lane == 0, 1.0, 0.0).astype(BF16)
    ng = ng_ref[...]

    def chunk_one(u, c, r, m):
        q = qc[u, pl.ds(r, L), :]
        kT = kc[u, pl.ds(r, L), :].T
        v = v_ref[u, pl.ds(r, L), :]
        vaug = jnp.concatenate([v, ones_col], axis=1)
        gates = g_ref[u, c]
        i_row = gates[0:1, :] + bias_ref[0, hp * U + u]
        f_row = gates[1:2, :] + bias_ref[1, hp * U + u]
        lf_row = jnp.minimum(f_row, 0.0) - jnp.log1p(jnp.exp(-jnp.abs(f_row)))
        b_col = jnp.sum(jnp.where(causal, lf_row, 0.0), axis=1, keepdims=True)
        b_row = jnp.sum(jnp.where(eye, b_col, 0.0), axis=0, keepdims=True)
        dlog = jnp.where(causal, b_col - b_row + i_row, NEG)
        inter = b_col + m
        m_t = jnp.maximum(inter, jnp.max(dlog, axis=1, keepdims=True))
        w_intra = jnp.exp(dlog - m_t)
        w_inter = jnp.exp(inter - m_t)
        s = _dot(q, kT.astype(BF16)) * w_intra
        res = w_inter * _dot(q, st[u].astype(BF16)) + _dot(s.astype(BF16), vaug)
        num = res[:, :d]
        den = res[:, d:d + 1]
        hh = num / jnp.maximum(jnp.abs(den), jnp.exp(-m_t))
        hn = hh * lax.rsqrt(jnp.mean(hh * hh, axis=1, keepdims=True) + EPS) * ng[:, u * d:(u + 1) * d]
        og = o_ref[u, pl.ds(r, L), :].astype(F32)
        out_ref[u, pl.ds(r, L), :] = (hn * _sigmoid(og)).astype(out_ref.dtype)
        ws_row = w_intra[L - 1:L, :]
        decay = w_inter[L - 1:L, :]
        st[u] = decay * st[u] + _dot((kT * ws_row).astype(BF16), vaug)
        return m_t[L - 1:L, :]

    def chunk(c, ms):
        r = pl.multiple_of(c * L, L)
        return tuple(chunk_one(u, c, r, ms[u]) for u in range(U))

    lax.fori_loop(0, nc, chunk, tuple(jnp.zeros((1, 1), F32) for _ in range(U)))


def _mlstm(proj_a, gates_t, bias_if, conv_w, norm_g, B, S):
    L = MLSTM_CHUNK
    nc = S // L
    d = HEAD_DIM
    H = N_HEADS
    U = HEADS_PER_STEP

    def slab(off):
        return pl.BlockSpec((U, S, d), lambda b, h: (off // U + h, b, 0))

    return pl.pallas_call(
        _mlstm_kernel,
        grid=(B, H // U),
        in_specs=[
            pl.BlockSpec(memory_space=pltpu.SMEM),
            slab(A_MQ), slab(A_MK), slab(A_MV), slab(A_MO),
            pl.BlockSpec((None, U, nc, 2, L), lambda b, h: (b, h, 0, 0, 0)),
            pl.BlockSpec((CONV_WIDTH, U * d), lambda b, h: (0, h)),
            pl.BlockSpec((CONV_WIDTH, U * d), lambda b, h: (0, H // U + h)),
            pl.BlockSpec((1, U * d), lambda b, h: (0, h)),
        ],
        out_specs=pl.BlockSpec((U, S, d), lambda b, h: (h, b, 0)),
        out_shape=jax.ShapeDtypeStruct((H, B * S, d), BF16),
        scratch_shapes=[
            pltpu.VMEM((U, S + 8, d), F32), pltpu.VMEM((U, S + 8, d), F32),
            pltpu.VMEM((U, S, d), BF16), pltpu.VMEM((U, S, d), F32),
            pltpu.VMEM((U, d, 2 * d), F32),
        ],
        compiler_params=_params(("parallel", "parallel")),
        name="mlstm",
    )(bias_if, proj_a, proj_a, proj_a, proj_a, gates_t, conv_w, conv_w, norm_g.reshape(1, H * d))


def _t5_thresholds():
    max_exact = N_BUCKETS // 2
    n = np.arange(0, 2 * MAX_DISTANCE)
    nf = np.maximum(n, 1).astype(np.float64)
    val = np.log(nf / max_exact) / math.log(MAX_DISTANCE / max_exact) * (N_BUCKETS - max_exact)
    frac = np.abs(val - np.round(val))
    frac_ok = (frac > 1e-4) | (n <= max_exact) | (n >= MAX_DISTANCE)
    assert frac_ok.all()
    large = np.minimum(max_exact + np.trunc(val).astype(np.int64), N_BUCKETS - 1)
    bucket = np.where(n < max_exact, n, large)
    assert (np.diff(bucket) >= 0).all() and bucket[MAX_DISTANCE] == N_BUCKETS - 1
    return [int(np.argmax(bucket >= j)) for j in range(1, N_BUCKETS)]


_T5_THR = _t5_thresholds()


def _dsa_prep_kernel(rb_ref, k_ref, ik_ref, gk_ref, kn, ikk, bias_ref):
    T = DSA_BLOCK
    H = N_HEADS
    row = lax.broadcasted_iota(I32, (T, T), 0)
    col = lax.broadcasted_iota(I32, (T, T), 1)
    gk = gk_ref[...]

    def norm_k(h, _):
        kh = k_ref[h].astype(F32)
        kn[h] = (kh * lax.rsqrt(jnp.mean(kh * kh, axis=1, keepdims=True) + EPS) * gk).astype(BF16)
        for o in range(2):
            n = o * T + row - col
            val = jnp.full((T, T), rb_ref[0, h], F32)
            for j, thr in enumerate(_T5_THR):
                val = jnp.where(n >= thr, rb_ref[j + 1, h], val)
            bias_ref[h, o] = val
        bias_ref[h, 2] = jnp.full((T, T), rb_ref[N_BUCKETS - 1, h], F32)
        return 0

    lax.fori_loop(0, H, norm_k, 0)
    ik = ik_ref[...]
    klane = lax.broadcasted_iota(I32, ik.shape, 1)
    ikk[0] = jnp.where(klane < IDX_DIM, ik, 0.0).astype(BF16)
    ikk[1] = jnp.where(klane >= IDX_DIM, ik, 0.0).astype(BF16)


def _dsa_kernel(q_ref, kn, v_ref, iq_ref, ikk, wt_ref, gq_ref, bias_ref, prev_ref, out_ref,
                wb, key_ref, mask_ref, *, topk, q0, nb, nqb):
    del prev_ref
    qis = [q0 + pl.program_id(1) * nqb + qb for qb in range(nqb)]
    T = DSA_BLOCK
    d = HEAD_DIM
    H = N_HEADS
    HALF = T // 2
    row = lax.broadcasted_iota(I32, (T, T), 0)
    col = lax.broadcasted_iota(I32, (T, T), 1)

    wsc = wt_ref[...] * (IDX_HEADS ** -0.5 * IDX_DIM ** -0.5)
    for qb in range(nqb):
        for h16 in range(IDX_HEADS):
            wb[qb * IDX_HEADS + h16] = jnp.broadcast_to(wsc[qb * T:(qb + 1) * T, h16:h16 + 1], (T, LANES))
    q_pairs = iq_ref[...].reshape((IDX_HEADS // 2) * nqb * T, LANES)

    def score_block(j, _):
        r = pl.multiple_of(j * T, T)
        kk = jnp.concatenate([ikk[0, pl.ds(r, T), :], ikk[1, pl.ds(r, T), :]], axis=0)
        dots = _dot_nt(q_pairs, kk)
        for qb in range(nqb):
            sc = jnp.zeros((T, T), F32)
            for h16 in range(IDX_HEADS):
                hp, odd = divmod(h16, 2)
                r0 = (hp * nqb + qb) * T
                sc = sc + wb[qb * IDX_HEADS + h16] * jnp.maximum(dots[r0:r0 + T, odd * T:(odd + 1) * T], 0.0)
            bits = lax.bitcast_convert_type(sc, I32)
            key = jnp.where(bits < 0, bits ^ jnp.int32(0x7FFFFFFF), bits)
            key = jnp.where(sc == 0.0, 0, key)
            key = jnp.where(j * T + col <= qis[qb] * T + row, key, INT_MIN)
            key_ref[qb, j] = key
        return 0

    lax.fori_loop(0, nb, score_block, 0, unroll=2)

    kf = float(topk)
    ones = jnp.ones((LANES, LANES), BF16)

    def count(chain, pred):
        qb, lo = chain
        acc = jnp.zeros((HALF, LANES), F32)
        for jb in range(nb):
            acc = acc + jnp.where(pred(key_ref[qb, jb, lo:lo + HALF, :]), 1.0, 0.0)
        return _dot(acc.astype(BF16), ones)

    chains = [(qb, lo) for qb in range(nqb) for lo in (0, HALF)]
    zero_i = jnp.zeros((HALF, LANES), I32)
    thr0 = tuple(jnp.where(count(ch, lambda k: k >= zero_i) >= kf, jnp.int32(0), jnp.int32(INT_MIN))
                 for ch in chains)

    def bisect_bit(thrs, bit):
        out = []
        for ch, t in zip(chains, thrs):
            cand = t | bit
            out.append(jnp.where(count(ch, lambda k, c=cand: k >= c) >= kf, cand, t))
        return tuple(out)

    def bisect_pair(it, thrs):
        hi_bit = jnp.left_shift(jnp.int32(1), 29 - 2 * it)
        lo_bit = jnp.left_shift(jnp.int32(1), 28 - 2 * it)
        out = []
        for ch, t in zip(chains, thrs):
            c01, c10, c11 = t | lo_bit, t | hi_bit, t | hi_bit | lo_bit
            n01, n10, n11 = (count(ch, lambda k, c=c: k >= c) >= kf for c in (c01, c10, c11))
            out.append(jnp.where(n11, c11, jnp.where(n10, c10, jnp.where(n01, c01, t))))
        return tuple(out)

    thrs = bisect_bit(thr0, jnp.int32(1 << 30))
    thrs = lax.fori_loop(0, 15, bisect_pair, thrs)
    needs = [kf - count(ch, lambda k, t=t: k > t) for ch, t in zip(chains, thrs)]

    tri = jnp.where(row <= col, 1.0, 0.0).astype(BF16)
    for qb in range(nqb):
        need = jnp.concatenate(needs[2 * qb:2 * qb + 2], axis=0)
        thrb = jnp.concatenate(thrs[2 * qb:2 * qb + 2], axis=0)
        seen = jnp.zeros((T, LANES), F32)
        for jb in range(nb):
            key = key_ref[qb, jb]
            tie = key == thrb
            tie16 = jnp.where(tie, 1.0, 0.0).astype(BF16)
            pre = _dot(tie16, tri) + seen
            m = jnp.where(key > thrb, 0.0, jnp.where(tie, jnp.where(pre <= need, 0.0, NEG), NEG))
            mask_ref[qb, jb] = jnp.where(key == INT_MIN, NEG, m)
            seen = seen + _dot(tie16, ones)

    gq = gq_ref[...]

    def head(h):
        qh = q_ref[h].astype(F32)
        qn = (qh * lax.rsqrt(jnp.mean(qh * qh, axis=1, keepdims=True) + EPS) * gq * (d ** -0.5)).astype(BF16)
        lgs = _dot_nt(qn, kn[h])
        p_rows, sums = [], []
        for qb in range(nqb):
            mx = None
            lg_blocks = []
            for jb in range(nb):
                lg = (lgs[qb * T:(qb + 1) * T, jb * T:(jb + 1) * T]
                      + bias_ref[h, jnp.clip(qis[qb] - jb, 0, 2)] + mask_ref[qb, jb])
                lg_blocks.append(lg)
                mx = lg if mx is None else jnp.maximum(mx, lg)
            rowmax = jnp.max(mx, axis=1, keepdims=True)
            l = jnp.zeros((T, T), F32)
            ps = []
            for jb in range(nb):
                p = jnp.exp(lg_blocks[jb] - rowmax)
                l = l + p
                ps.append(p.astype(BF16))
            p_rows.append(jnp.concatenate(ps, axis=1))
            sums.append(jnp.sum(l, axis=1, keepdims=True))
        acc = _dot(jnp.concatenate(p_rows, axis=0), v_ref[h])
        out_ref[h] = (acc / jnp.concatenate(sums, axis=0)).astype(out_ref.dtype)

    def head_group(g, _):
        for u in range(HEADS_PER_STEP):
            head(g * HEADS_PER_STEP + u)
        return 0

    lax.fori_loop(0, H // HEADS_PER_STEP, head_group, 0)


def _dsa(proj_b, tail, q_norm_g, k_norm_g, rel_bias, B, S):
    T = DSA_BLOCK
    nq = S // T
    d = HEAD_DIM
    H = N_HEADS
    topk = min(TOPK_MAX, S // 4)
    per = nq // DSA_BANDS
    nqb = math.gcd(per, DSA_QBLOCKS)
    TQ = nqb * T
    pb = proj_b.reshape(proj_b.shape[0], B, S, d)
    tl = tail.reshape(tail.shape[0], B, S, LANES)
    kn, ikk, bias = pl.pallas_call(
        _dsa_prep_kernel,
        grid=(B,),
        in_specs=[pl.BlockSpec(memory_space=pltpu.SMEM),
                  pl.BlockSpec((H, None, S, d), lambda b: (B_DK // H, b, 0, 0)),
                  pl.BlockSpec((None, None, S, LANES), lambda b: (0, b, 0, 0)),
                  pl.BlockSpec((1, d), lambda b: (0, 0))],
        out_specs=[pl.BlockSpec((H, None, S, d), lambda b: (0, b, 0, 0)),
                   pl.BlockSpec((2, None, S, LANES), lambda b: (0, b, 0, 0)),
                   pl.BlockSpec((H, 3, T, T), lambda b: (0, 0, 0, 0))],
        out_shape=[jax.ShapeDtypeStruct((H, B, S, d), BF16),
                   jax.ShapeDtypeStruct((2, B, S, LANES), BF16),
                   jax.ShapeDtypeStruct((H, 3, T, T), F32)],
        compiler_params=_params(("arbitrary",)),
        name="dsa_prep",
    )(rel_bias, pb, tl, k_norm_g.reshape(1, d))
    out = jnp.zeros((H, B, S, d), BF16)
    for band in range(DSA_BANDS):
        q0 = band * per
        nb = q0 + per
        W = nb * T
        in_specs = [
            pl.BlockSpec((H, None, TQ, d), lambda b, i: (B_DQ // H, b, q0 // nqb + i, 0)),
            pl.BlockSpec((H, None, W, d), lambda b, i: (0, b, 0, 0)),
            pl.BlockSpec((H, None, W, d), lambda b, i: (B_DV // H, b, 0, 0)),
            pl.BlockSpec((H, None, TQ, d), lambda b, i: (B_IQ // H, b, q0 // nqb + i, 0)),
            pl.BlockSpec((2, None, W, LANES), lambda b, i: (0, b, 0, 0)),
            pl.BlockSpec((None, None, TQ, LANES), lambda b, i: (1, b, q0 // nqb + i, 0)),
            pl.BlockSpec((1, d), lambda b, i: (0, 0)),
            pl.BlockSpec((H, 3, T, T), lambda b, i: (0, 0, 0, 0)),
            pl.BlockSpec(memory_space=pl.ANY),
        ]
        args = [pb, kn, pb, pb, ikk, tl, q_norm_g.reshape(1, d), bias, out]
        out = pl.pallas_call(
            functools.partial(_dsa_kernel, topk=topk, q0=q0, nb=nb, nqb=nqb),
            grid=(B, per // nqb),
            in_specs=in_specs,
            out_specs=pl.BlockSpec((H, None, TQ, d), lambda b, i: (0, b, q0 // nqb + i, 0)),
            out_shape=jax.ShapeDtypeStruct((H, B, S, d), BF16),
            scratch_shapes=[
                pltpu.VMEM((nqb * IDX_HEADS, T, LANES), F32),
                pltpu.VMEM((nqb, nb, T, T), I32),
                pltpu.VMEM((nqb, nb, T, T), F32),
            ],
            input_output_aliases={len(args) - 1: 0},
            compiler_params=_params(("parallel", "parallel")),
            name=f"dsa_band{band}",
        )(*args)
    return out.reshape(H, B * S, d)


def _sb_kernel(q_ref, k_ref, v_ref, out_ref):
    qi = pl.program_id(2)
    U = SB_HEADS
    T = SB_BLOCK
    d = HEAD_DIM
    R = SB_ROWS
    ns = T // R
    row = lax.broadcasted_iota(I32, (T, T), 0)
    col = lax.broadcasted_iota(I32, (T, T), 1)
    upper = jnp.where(row > col, 1.0, 0.0).astype(BF16)
    srow = lax.broadcasted_iota(I32, (R, T), 0)
    scol = lax.broadcasted_iota(I32, (R, T), 1)
    units = [(u, s) for u in range(U) for s in range(ns)]
    qs = [(q_ref[u, s * R:(s + 1) * R, :].astype(F32) * (d ** -0.5)).astype(BF16) for u, s in units]

    def block(n, j, diag, run, acc):
        u, s = units[n]
        r = pl.multiple_of(j * T, T)
        z = _dot_nt(qs[n], k_ref[u, pl.ds(r, T), :])
        sp = jnp.maximum(z, 0.0) + jnp.log(1.0 + jnp.exp(-jnp.abs(z)))
        strict = scol < srow + s * R
        ln = jnp.where(strict, -sp, 0.0) if diag else -sp
        parts = [ln.astype(BF16)]
        for _ in range(SB_PIECES - 1):
            parts.append((ln - sum(p.astype(F32) for p in parts)).astype(BF16))
        sums = _dot(jnp.concatenate(parts, axis=0), upper)
        suf = sum(sums[p * R:(p + 1) * R] for p in range(SB_PIECES))
        a = jnp.exp(z - sp + suf + run)
        if diag:
            a = jnp.where(strict, a, 0.0)
        acc = acc + _dot(a.astype(BF16), v_ref[u, pl.ds(r, T), :])
        return run + suf[:, 0:1] + ln[:, 0:1], acc

    state = []
    for n in range(len(units)):
        state.extend(block(n, qi, True, jnp.zeros((R, 1), F32), jnp.zeros((R, d), F32)))

    def earlier(i, st):
        out = []
        for n in range(len(units)):
            out.extend(block(n, qi - i, False, st[2 * n], st[2 * n + 1]))
        return tuple(out)

    state = lax.fori_loop(1, qi + 1, earlier, tuple(state))
    for n, (u, s) in enumerate(units):
        out_ref[u, s * R:(s + 1) * R, :] = state[2 * n + 1].astype(out_ref.dtype)


def _stick_breaking(proj_c, B, S):
    T = SB_BLOCK
    nq = S // T
    d = HEAD_DIM
    H = N_HEADS
    U = SB_HEADS
    return pl.pallas_call(
        _sb_kernel,
        grid=(B, H // U, nq),
        in_specs=[
            pl.BlockSpec((U, T, d), lambda b, h, i: (C_SQ // U + h, b * nq + i, 0)),
            pl.BlockSpec((U, S, d), lambda b, h, i: (C_SK // U + h, b, 0)),
            pl.BlockSpec((U, S, d), lambda b, h, i: (C_SV // U + h, b, 0)),
        ],
        out_specs=pl.BlockSpec((U, T, d), lambda b, h, i: (h, b * nq + i, 0)),
        out_shape=jax.ShapeDtypeStruct((H, B * S, d), BF16),
        compiler_params=_params(("parallel", "parallel", "parallel")),
        name="stick_breaking",
    )(proj_c, proj_c, proj_c)


def _merge_kernel(hm_ref, hd_ref, hs_ref, g0_ref, g1_ref, g2_ref, w_ref, o_ref, wb):
    H = N_HEADS

    @pl.when(pl.program_id(1) == 0)
    def _():
        wb[...] = w_ref[...].astype(BF16)

    acc = None
    ns = o_ref.shape[1] // LANES
    for n, (br, gr) in enumerate(((hm_ref, g0_ref), (hd_ref, g1_ref), (hs_ref, g2_ref))):
        a = jnp.concatenate([br[h] for h in range(H)], axis=1)
        up = _dot(a, wb[n])
        gate = jnp.concatenate([gr[s] for s in range(ns)], axis=1).astype(F32)
        term = _sigmoid(gate) * up
        acc = term if acc is None else acc + term
    o_ref[...] = acc.astype(o_ref.dtype)


def _merge(hm, hd, hs, proj_c, w_branch, layer, tm=1024, tn=512):
    H, n, d = hm.shape
    tm = min(tm, n)
    D = w_branch.shape[3]
    ns = tn // LANES
    br = pl.BlockSpec((H, tm, d), lambda j, i: (0, i, 0))

    def gate(b):
        first = (C_GATE + b * (D // LANES)) // ns
        return pl.BlockSpec((ns, tm, LANES), lambda j, i: (first + j, i, 0))

    return pl.pallas_call(
        _merge_kernel,
        grid=(D // tn, n // tm),
        in_specs=[br, br, br, gate(0), gate(1), gate(2),
                  pl.BlockSpec((None, N_BRANCHES, H * d, tn), lambda j, i: (layer, 0, 0, j))],
        out_specs=pl.BlockSpec((tm, tn), lambda j, i: (i, j)),
        out_shape=jax.ShapeDtypeStruct((n, D), BF16),
        scratch_shapes=[pltpu.VMEM((N_BRANCHES, H * d, tn), BF16)],
        compiler_params=_params(("parallel", "arbitrary")),
        name="branch_merge",
    )(hm, hd, hs, proj_c, proj_c, proj_c, w_branch)


def _mm_res_kernel(a_ref, w_ref, x_ref, o_ref, wb):
    @pl.when(pl.program_id(1) == 0)
    def _():
        wb[...] = w_ref[...].astype(BF16)

    o_ref[...] = x_ref[...] + _dot(a_ref[...], wb[...])


def _matmul_residual(a, w, layer, x, tm=1024, tn=512):
    m, k = a.shape
    tm = min(tm, m)
    n = w.shape[2]
    return pl.pallas_call(
        _mm_res_kernel,
        grid=(n // tn, m // tm),
        in_specs=[pl.BlockSpec((tm, k), lambda j, i: (i, 0)),
                  pl.BlockSpec((None, k, tn), lambda j, i: (layer, 0, j)),
                  pl.BlockSpec((tm, tn), lambda j, i: (i, j))],
        out_specs=pl.BlockSpec((tm, tn), lambda j, i: (i, j)),
        out_shape=jax.ShapeDtypeStruct((m, n), F32),
        scratch_shapes=[pltpu.VMEM((k, tn), BF16)],
        compiler_params=_params(("parallel", "arbitrary")),
        name="out_proj",
    )(a, w, x)


def _router_kernel(x_ref, g_ref, wr_ref, br_ref, xn_ref, ids_ref, wts_ref, cnt_ref, carry, lower):
    i = pl.program_id(0)
    tm, D = x_ref.shape

    @pl.when(i == 0)
    def _():
        carry[...] = jnp.zeros(carry.shape, F32)
        r = lax.broadcasted_iota(I32, (tm, tm), 0)
        c = lax.broadcasted_iota(I32, (tm, tm), 1)
        lower[...] = jnp.where(c < r, 1.0, 0.0).astype(BF16)

    x = x_ref[...]
    xn = x * lax.rsqrt(jnp.mean(x * x, axis=-1, keepdims=True) + EPS) * g_ref[...]
    xn_ref[...] = xn
    logits = jnp.dot(xn, wr_ref[...], preferred_element_type=F32, precision=lax.Precision.HIGHEST)
    biased = logits + br_ref[...]
    lane = lax.broadcasted_iota(I32, (tm, LANES), 1)
    lanef = lane.astype(F32)
    big = float(LANES)

    def first_lane(mask):
        return jnp.min(jnp.where(mask, lanef, big), axis=1, keepdims=True)

    gmask = lane < N_GROUPS
    gmax = jnp.max(jnp.where(gmask, biased, NEG), axis=1, keepdims=True)
    g_sel = first_lane(gmask & (biased == gmax))
    gm = jnp.max(jnp.where(gmask, logits, NEG), axis=1, keepdims=True)
    ge = jnp.where(gmask, jnp.exp(logits - gm), 0.0)
    p_group = jnp.sum(jnp.where(lanef == g_sel, ge, 0.0), axis=1, keepdims=True) / jnp.sum(ge, axis=1, keepdims=True)
    lo = N_GROUPS + EXPERTS_PER_GROUP * g_sel
    emask = (lanef >= lo) & (lanef < lo + EXPERTS_PER_GROUP)
    eb = jnp.where(emask, biased, NEG)
    e1 = first_lane(emask & (eb == jnp.max(eb, axis=1, keepdims=True)))
    emask2 = emask & (lanef != e1)
    eb2 = jnp.where(emask2, biased, NEG)
    e2 = first_lane(emask2 & (eb2 == jnp.max(eb2, axis=1, keepdims=True)))
    em = jnp.max(jnp.where(emask, logits, NEG), axis=1, keepdims=True)
    ee = jnp.where(emask, jnp.exp(logits - em), 0.0)
    s1 = jnp.sum(jnp.where(lanef == e1, ee, 0.0), axis=1, keepdims=True)
    s2 = jnp.sum(jnp.where(lanef == e2, ee, 0.0), axis=1, keepdims=True)
    se = jnp.sum(ee, axis=1, keepdims=True)
    w1 = s1 / se
    w2 = s2 / se
    wsum = w1 + w2
    w1 = p_group * w1 / wsum
    w2 = p_group * w2 / wsum
    x1 = e1 - N_GROUPS
    x2 = e2 - N_GROUPS

    onehot = jnp.where((lanef == x1) | (lanef == x2), 1.0, 0.0)
    prefix = _dot(lower[...], onehot.astype(BF16)) + carry[...]
    r1 = jnp.sum(jnp.where(lanef == x1, prefix, 0.0), axis=1, keepdims=True)
    r2 = jnp.sum(jnp.where(lanef == x2, prefix, 0.0), axis=1, keepdims=True)
    carry[...] = carry[...] + jnp.sum(onehot, axis=0, keepdims=True)
    cnt_ref[...] = carry[...].astype(I32)

    idsf = jnp.where(lane == 0, x1, jnp.where(lane == 1, x2, jnp.where(lane == 2, r1, jnp.where(lane == 3, r2, 0.0))))
    ids_ref[...] = idsf.astype(I32)
    wts_ref[...] = jnp.where(lane == 0, w1, jnp.where(lane == 1, w2, 0.0))


def _router(x2d, g, w_router, b_router, tm=512):
    n, D = x2d.shape
    return pl.pallas_call(
        _router_kernel,
        grid=(n // tm,),
        in_specs=[pl.BlockSpec((tm, D), lambda i: (i, 0)), pl.BlockSpec((1, D), lambda i: (0, 0)),
                  pl.BlockSpec((D, LANES), lambda i: (0, 0)), pl.BlockSpec((1, LANES), lambda i: (0, 0))],
        out_specs=[pl.BlockSpec((tm, D), lambda i: (i, 0)),
                   pl.BlockSpec((tm, LANES), lambda i: (i, 0)),
                   pl.BlockSpec((tm, LANES), lambda i: (i, 0)),
                   pl.BlockSpec((1, LANES), lambda i: (0, 0))],
        out_shape=[jax.ShapeDtypeStruct((n, D), F32),
                   jax.ShapeDtypeStruct((n, LANES), I32),
                   jax.ShapeDtypeStruct((n, LANES), F32),
                   jax.ShapeDtypeStruct((1, LANES), I32)],
        scratch_shapes=[pltpu.VMEM((1, LANES), F32), pltpu.VMEM((tm, tm), BF16)],
        compiler_params=_params(("arbitrary",)),
        name="moe_router",
    )(x2d, g.reshape(1, D), w_router, b_router)


def _dispatch_kernel(p1_ref, p2_ref, xn_ref, xs_in_ref, xs_ref, sem, *, tb):
    del xs_in_ref
    base = pl.program_id(0) * tb

    def copies(t):
        src = xn_ref.at[pl.ds(t, 1), :]
        return (pltpu.make_async_copy(src, xs_ref.at[pl.ds(p1_ref[base + t], 1), :], sem),
                pltpu.make_async_copy(src, xs_ref.at[pl.ds(p2_ref[base + t], 1), :], sem))

    def issue(t, _):
        for cp in copies(t):
            cp.start()
        return 0

    lax.fori_loop(0, tb, issue, 0, unroll=DMA_UNROLL)

    def drain(t, _):
        for cp in copies(t):
            cp.wait()
        return 0

    lax.fori_loop(0, tb, drain, 0, unroll=DMA_UNROLL)


def _dispatch(pos1, pos2, xn, xs0, tb=512):
    n, D = xn.shape
    n_rows = xs0.shape[0]
    tb = min(tb, n)
    grid_spec = pltpu.PrefetchScalarGridSpec(
        num_scalar_prefetch=2,
        grid=(n // tb,),
        in_specs=[pl.BlockSpec((tb, D), lambda i, p1, p2: (i, 0)),
                  pl.BlockSpec(memory_space=pl.ANY)],
        out_specs=pl.BlockSpec(memory_space=pl.ANY),
        scratch_shapes=[pltpu.SemaphoreType.DMA(())],
    )
    return pl.pallas_call(
        functools.partial(_dispatch_kernel, tb=tb),
        grid_spec=grid_spec,
        out_shape=jax.ShapeDtypeStruct((n_rows, D), F32),
        input_output_aliases={3: 0},
        compiler_params=pltpu.CompilerParams(dimension_semantics=("arbitrary",), has_side_effects=True),
        name="moe_dispatch",
    )(pos1, pos2, xn, xs0)


def _expert_kernel(te_ref, nv_ref, ne_ref, grp_ref, xs_ref, wg_ref, wu_ref, wd_ref, ys_ref,
                   wgf, wuf, wdf, wgb, wub, wdb, sem, *, layer):
    i = pl.program_id(0)
    prev = te_ref[jnp.maximum(i - 1, 0)]

    def fetch(e, slot):
        return (pltpu.make_async_copy(wg_ref.at[layer, e], wgf.at[slot], sem.at[slot]),
                pltpu.make_async_copy(wu_ref.at[layer, e], wuf.at[slot], sem.at[slot]),
                pltpu.make_async_copy(wd_ref.at[layer, e], wdf.at[slot], sem.at[slot]))

    @pl.when(i < nv_ref[0])
    def _():
        @pl.when(i == 0)
        def _():
            for cp in fetch(te_ref[0], 0):
                cp.start()

        @pl.when((i == 0) | (te_ref[i] != prev))
        def _():
            slot = grp_ref[i] % 2
            for cp in fetch(te_ref[i], slot):
                cp.wait()

            @pl.when(ne_ref[i] >= 0)
            def _():
                for cp in fetch(ne_ref[i], 1 - slot):
                    cp.start(priority=1)

            wgb[...] = wgf[slot].astype(BF16)
            wub[...] = wuf[slot].astype(BF16)
            wdb[...] = wdf[slot].astype(BF16)

        x = xs_ref[...].astype(BF16)
        g = _dot(x, wgb[...])
        u = _dot(x, wub[...])
        hcur = (g * _sigmoid(g) * u).astype(BF16)
        ys_ref[...] = _dot(hcur, wdb[...])

    @pl.when(i >= nv_ref[0])
    def _():
        ys_ref[...] = jnp.zeros(ys_ref.shape, F32)


def _experts(tile_expert, n_valid, xs, w_gate, w_up, w_down, layer):
    n_rows = xs.shape[0]
    _, E, D, Fe = w_gate.shape
    tm = MOE_TILE
    n_tiles = n_rows // tm
    tiles = jnp.arange(n_tiles, dtype=I32)
    first = (tiles < n_valid[0]) & ((tiles == 0) | (tile_expert != jnp.roll(tile_expert, 1)))
    group = (jnp.cumsum(first.astype(I32)) - 1).astype(I32)
    later = lax.cummin(jnp.where(first, tiles, n_tiles), reverse=True)
    nxt = jnp.concatenate([later[1:], jnp.full((1,), n_tiles, I32)])
    next_expert = jnp.where(nxt < n_tiles, tile_expert[jnp.minimum(nxt, n_tiles - 1)], -1).astype(I32)

    def row_map(i, te, nv, ne, grp):
        return (jnp.minimum(i, jnp.maximum(nv[0] - 1, 0)), 0)

    grid_spec = pltpu.PrefetchScalarGridSpec(
        num_scalar_prefetch=4,
        grid=(n_tiles,),
        in_specs=[pl.BlockSpec((tm, D), row_map),
                  pl.BlockSpec(memory_space=pl.ANY), pl.BlockSpec(memory_space=pl.ANY),
                  pl.BlockSpec(memory_space=pl.ANY)],
        out_specs=pl.BlockSpec((tm, D), lambda i, te, nv, ne, grp: (i, 0)),
        scratch_shapes=[pltpu.VMEM((2, D, Fe), F32), pltpu.VMEM((2, D, Fe), F32), pltpu.VMEM((2, Fe, D), F32),
                        pltpu.VMEM((D, Fe), BF16), pltpu.VMEM((D, Fe), BF16), pltpu.VMEM((Fe, D), BF16),
                        pltpu.SemaphoreType.DMA((2,))],
    )
    return pl.pallas_call(
        functools.partial(_expert_kernel, layer=layer),
        grid_spec=grid_spec,
        out_shape=jax.ShapeDtypeStruct((n_rows, D), F32),
        compiler_params=_params(("arbitrary",)),
        name="moe_experts",
    )(tile_expert, n_valid, next_expert, group, xs, w_gate, w_up, w_down)


def _combine_kernel(p1_ref, p2_ref, ys_ref, x_ref, w_ref, *rest, with_norm):
    if with_norm:
        g_ref, o_ref, xn_ref, buf, sem = rest
    else:
        o_ref, buf, sem = rest
    i = pl.program_id(0)
    tc = x_ref.shape[0]
    slot = i % 2

    def copies(tile, slot, t):
        tok = tile * tc + t
        return (pltpu.make_async_copy(ys_ref.at[pl.ds(p1_ref[tok], 1), :], buf.at[slot, 0, pl.ds(t, 1), :],
                                      sem.at[slot]),
                pltpu.make_async_copy(ys_ref.at[pl.ds(p2_ref[tok], 1), :], buf.at[slot, 1, pl.ds(t, 1), :],
                                      sem.at[slot]))

    def issue(tile, slot):
        def body(t, _):
            for cp in copies(tile, slot, t):
                cp.start()
            return 0

        lax.fori_loop(0, tc, body, 0, unroll=DMA_UNROLL)

    @pl.when(i == 0)
    def _():
        issue(0, 0)

    @pl.when(i + 1 < pl.num_programs(0))
    def _():
        issue(i + 1, 1 - slot)

    def drain(t, _):
        for cp in copies(i, slot, t):
            cp.wait()
        return 0

    lax.fori_loop(0, tc, drain, 0, unroll=DMA_UNROLL)
    w = w_ref[...]
    out = x_ref[...] + w[:, 0:1] * buf[slot, 0] + w[:, 1:2] * buf[slot, 1]
    o_ref[...] = out
    if with_norm:
        ms = jnp.mean(out * out, axis=-1, keepdims=True)
        xn_ref[...] = (out * lax.rsqrt(ms + EPS) * g_ref[...]).astype(BF16)


def _combine(pos1, pos2, ys, x2d, wts, next_g, tc=256):
    n, D = x2d.shape
    with_norm = next_g is not None
    row = pl.BlockSpec((tc, D), lambda i, p1, p2: (i, 0))
    in_specs = [pl.BlockSpec(memory_space=pl.ANY), row, pl.BlockSpec((tc, LANES), lambda i, p1, p2: (i, 0))]
    args = [pos1, pos2, ys, x2d, wts]
    out_specs, out_shape = row, jax.ShapeDtypeStruct((n, D), F32)
    if with_norm:
        in_specs.append(pl.BlockSpec((1, D), lambda i, p1, p2: (0, 0)))
        args.append(next_g.reshape(1, D))
        out_specs, out_shape = [row, row], [out_shape, jax.ShapeDtypeStruct((n, D), BF16)]
    grid_spec = pltpu.PrefetchScalarGridSpec(
        num_scalar_prefetch=2,
        grid=(n // tc,),
        in_specs=in_specs,
        out_specs=out_specs,
        scratch_shapes=[pltpu.VMEM((2, 2, tc, D), F32), pltpu.SemaphoreType.DMA((2,))],
    )
    res = pl.pallas_call(
        functools.partial(_combine_kernel, with_norm=with_norm),
        grid_spec=grid_spec,
        out_shape=out_shape,
        compiler_params=_params(("arbitrary",)),
        name="moe_combine",
    )(*args)
    return tuple(res) if with_norm else (res, None)


def _pos_kernel(ids_ref, cnt_ref, pos_ref):
    tm = ids_ref.shape[0]
    tiles = jnp.floor((cnt_ref[...].astype(F32) + (MOE_TILE - 1)) / MOE_TILE)
    r = lax.broadcasted_iota(I32, (LANES, LANES), 0)
    c = lax.broadcasted_iota(I32, (LANES, LANES), 1)
    before = jnp.where(r < c, 1.0, 0.0).astype(BF16)
    first_tile = _dot(jnp.broadcast_to(tiles, (8, LANES)).astype(BF16), before)
    offs = first_tile[0:1, :] * MOE_TILE
    ids = ids_ref[...].astype(F32)
    lane = lax.broadcasted_iota(I32, (tm, LANES), 1)
    lanef = lane.astype(F32)
    p1 = jnp.sum(jnp.where(lanef == ids[:, 0:1], offs, 0.0), axis=1, keepdims=True) + ids[:, 2:3]
    p2 = jnp.sum(jnp.where(lanef == ids[:, 1:2], offs, 0.0), axis=1, keepdims=True) + ids[:, 3:4]
    packed = jnp.where(lane == 0, p1, jnp.where(lane == 1, p2, 0.0))
    pos_ref[...] = packed.T[0:8, :].astype(I32)


def _positions(ids, counts, tm=512):
    n = ids.shape[0]
    assert MOE_TILE & (MOE_TILE - 1) == 0
    return pl.pallas_call(
        _pos_kernel,
        grid=(n // tm,),
        in_specs=[pl.BlockSpec((tm, LANES), lambda i: (i, 0)), pl.BlockSpec((1, LANES), lambda i: (0, 0))],
        out_specs=pl.BlockSpec((8, tm), lambda i: (0, i)),
        out_shape=jax.ShapeDtypeStruct((8, n), I32),
        compiler_params=_params(("parallel",)),
        name="moe_positions",
    )(ids, counts)


def _hier_moe(x2d, norm_g, w_rg, b_rg, w_re, b_re, w_gate, w_up, w_down, layer, xs_buf, next_g):
    n, D = x2d.shape
    tm = MOE_TILE
    pad = LANES - N_GROUPS - N_EXPERTS
    w_router = jnp.concatenate([w_rg, w_re, jnp.zeros((D, pad), F32)], axis=1)
    b_router = jnp.concatenate([b_rg, b_re.reshape(-1), jnp.zeros((pad,), F32)]).reshape(1, LANES)
    xn3, ids, wts, counts = _router(x2d, norm_g, w_router, b_router)
    cnt = counts[0, :N_EXPERTS]
    padded = ((cnt + tm - 1) // tm) * tm
    ends = jnp.cumsum(padded)
    n_tiles = xs_buf.shape[0] // tm
    tile_start = jnp.arange(n_tiles, dtype=I32) * tm
    tile_expert = jnp.minimum(jnp.sum(tile_start[:, None] >= ends[None, :], axis=1), N_EXPERTS - 1).astype(I32)
    n_valid = (ends[-1] // tm).astype(I32).reshape(1)
    last_e = tile_expert[jnp.maximum(n_valid[0] - 1, 0)]
    tile_expert = jnp.where(jnp.arange(n_tiles) < n_valid[0], tile_expert, last_e)
    pos = _positions(ids, counts)
    pos1, pos2 = pos[0], pos[1]
    xs = _dispatch(pos1, pos2, xn3, xs_buf)
    ys = _experts(tile_expert, n_valid, xs, w_gate, w_up, w_down, layer)
    out, xn_next = _combine(pos1, pos2, ys, x2d, wts, next_g)
    return out, xn_next, xs


def _in_proj_regions(D):
    bw = BRANCH_WIDTH
    sizes = (bw, bw, bw, bw, N_HEADS, N_HEADS, bw, bw, bw, IDX_HEADS * IDX_DIM, IDX_DIM, IDX_HEADS,
             bw, bw, bw, N_BRANCHES * D)
    offs = np.concatenate([[0], np.cumsum(sizes)]).tolist()
    region_a = (offs[0], offs[4] - offs[0])
    region_b = (offs[6], offs[10] - offs[6])
    region_c = (offs[12], offs[16] - offs[12])
    small = dict(mi=offs[4], mf=offs[5], ik=offs[10], iw=offs[11])
    return region_a, region_b, region_c, small


def _token_mixer(x2d, xn, B, S, layer, norm_g, w_in, conv_w, b_i, b_f, mlstm_norm_g, q_norm_g, k_norm_g,
                 w_branch, w_out, rel_bias):
    n, D = x2d.shape
    L = MLSTM_CHUNK
    H = N_HEADS
    tm = min(n, PROJ_TM)
    if xn is None:
        xn = _rmsnorm(x2d, norm_g)
    ra, rb, rc, small = _in_proj_regions(D)
    w_t = jnp.swapaxes(w_in, 1, 2)
    proj_a = _in_proj(xn, w_t, layer, ra[0], ra[1], tm)
    proj_b = _in_proj(xn, w_t, layer, rb[0], rb[1], tm)
    proj_c = _in_proj(xn, w_t, layer, rc[0], rc[1], tm)
    tail = _tail_proj(xn, w_t, layer, small, tm)
    g = tail[1][:, IDX_HEADS:IDX_HEADS + 2 * H].reshape(B, S // L, L, 2, H)
    gates_t = jnp.transpose(g, (0, 4, 1, 3, 2))
    hm = _mlstm(proj_a, gates_t, jnp.stack([b_i, b_f]), conv_w, mlstm_norm_g, B, S)
    hd = _dsa(proj_b, tail, q_norm_g, k_norm_g, rel_bias, B, S)
    hs = _stick_breaking(proj_c, B, S)
    merged = _merge(hm, hd, hs, proj_c, w_branch, layer)
    return _matmul_residual(merged, w_out, layer, x2d)


def kernel(x, norm1_g, w_in, conv_w, b_i, b_f, mlstm_norm_g, q_norm_g, k_norm_g, w_branch, w_out, norm2_g,
           w_router_g, b_router_g, w_router_e, b_router_e, w_gate, w_up, w_down, rel_bias):
    B, S, D = x.shape
    x2d = x.reshape(B * S, D)
    xs_buf = jnp.zeros((2 * B * S + N_EXPERTS * MOE_TILE, D), F32)
    depth = w_in.shape[0]
    xn = None
    for l in range(depth):
        x2d = _token_mixer(x2d, xn, B, S, l, norm1_g[l], w_in, conv_w[l], b_i[l], b_f[l], mlstm_norm_g[l],
                           q_norm_g[l], k_norm_g[l], w_branch, w_out, rel_bias)
        next_g = norm1_g[l + 1] if l + 1 < depth else None
        x2d, xn, xs_buf = _hier_moe(x2d, norm2_g[l], w_router_g[l], b_router_g[l], w_router_e[l], b_router_e[l],
                                    w_gate, w_up, w_down, l, xs_buf, next_g)
    return x2d.reshape(B, S, D)
```
